```python
import math
import jax
import jax.numpy as jnp
from jax import lax
import numpy as np

D_MODEL = 1024
BATCH = 32
SEQ = 256
DEPTH = 4
DEC_BATCH = 2
DEC_SEQ = 2048
PAST_LEN = 512

GRID_W = 64
N_MIXERS = 4
NORM_EPS = 1e-6
Q_BLOCK = 128
ROPE_BASE = 10000.0
NEG_INF = -1e30

POOL_WINDOWS = (2, 4, 8, 16)
POOL_GROUP = D_MODEL // len(POOL_WINDOWS)

NA_HEADS = 16
NA_HEAD_DIM = D_MODEL // NA_HEADS
NA_WIN_R = 8
NA_WIN_C = 16
NA_KEY_COLS = 2 * NA_WIN_C

DIFF_HEADS = 8
DIFF_HEAD_DIM = D_MODEL // (2 * DIFF_HEADS)
DIFF_SUBLN_EPS = 1e-5

MLA_HEADS = 8
MLA_Q_LORA = 512
MLA_KV_LORA = 256
MLA_NOPE = 128
MLA_ROPE = 64
MLA_V = 128

FFN_HIDDEN = 2816
N_EXPERTS = 8
TOP_K = 2
MOE_HIDDEN = 2816

N_POOL_LAYERS = (DEPTH + 3) // 4
N_NA_LAYERS = (DEPTH + 2) // 4
N_DIFF_LAYERS = (DEPTH + 1) // 4
N_MLA_LAYERS = DEPTH // 4
N_DENSE_LAYERS = (DEPTH + 1) // 2
N_MOE_LAYERS = DEPTH // 2

kernel_name = 'hybrid_prefix_diffusion_trunk_step'


def _rmsnorm(x, g, eps=NORM_EPS):
    xf = x.astype(jnp.float32)
    y = xf * lax.rsqrt(jnp.mean(xf * xf, axis=-1, keepdims=True) + eps)
    return (y * g.astype(jnp.float32)).astype(x.dtype)


def _adaln(cond, w, b):
    m = jax.nn.silu(cond) @ w + b
    return jnp.split(m[..., None, :], 6, axis=-1)


def _modulate(x, g, shift, scale):
    return _rmsnorm(x, g) * (1 + scale) + shift


def _axial_rope(t, dim):
    pos = jnp.arange(t)
    rows = (pos // GRID_W).astype(jnp.float32)
    cols = (pos % GRID_W).astype(jnp.float32)
    n_freq = dim // 4
    inv = ROPE_BASE ** (-jnp.arange(n_freq, dtype=jnp.float32) / n_freq)
    ang = jnp.concatenate([rows[:, None] * inv[None, :], cols[:, None] * inv[None, :]], axis=-1)
    return jnp.cos(ang), jnp.sin(ang)


def _apply_rope(x, cos, sin):
    half = x.shape[-1] // 2
    shape = (1, x.shape[1]) + (1,) * (x.ndim - 3) + (half,)
    cos = cos.reshape(shape)
    sin = sin.reshape(shape)
    xf = x.astype(jnp.float32)
    x1, x2 = xf[..., :half], xf[..., half:]
    return jnp.concatenate([x1 * cos - x2 * sin, x1 * sin + x2 * cos], axis=-1).astype(x.dtype)


def _over_query_blocks(fn, *qs):
    b, t = qs[0].shape[:2]
    qb = min(Q_BLOCK, t)
    nb = t // qb
    blocks = tuple(jnp.moveaxis(q.reshape((b, nb, qb) + q.shape[2:]), 1, 0) for q in qs)
    out = lax.map(lambda a: fn(*a), blocks)
    return jnp.moveaxis(out, 0, 1).reshape((b, t) + out.shape[3:])


def _softmax_attention(q, k, v, scale):
    def block(qb):
        s = jnp.einsum('bqhd,bkhd->bhqk', qb, k).astype(jnp.float32) * scale
        p = jax.nn.softmax(s, axis=-1).astype(v.dtype)
        return jnp.einsum('bhqk,bkhd->bqhd', p, v)
    return _over_query_blocks(block, q)


def _diff_attention(q, k, v, lam, scale):
    def block(qb):
        s = jnp.einsum('bqhnd,bkhnd->bnhqk', qb, k).astype(jnp.float32) * scale
        p = jax.nn.softmax(s, axis=-1)
        a = (p[:, 0] - lam * p[:, 1]).astype(v.dtype)
        return jnp.einsum('bhqk,bkhd->bqhd', a, v)
    return _over_query_blocks(block, q)


def _multi_scale_pool(x, w_pool, scale):
    b, t, d = x.shape
    ng = len(POOL_WINDOWS)
    xf = x.astype(jnp.float32).reshape(b, t, ng, POOL_GROUP)
    cs = jnp.concatenate([jnp.zeros((b, 1, ng, POOL_GROUP), jnp.float32), jnp.cumsum(xf, axis=1)], axis=1)
    pos = jnp.arange(t)
    means = []
    for g, w in enumerate(POOL_WINDOWS):
        lo = jnp.clip(pos - w // 2, 0, t)
        hi = jnp.clip(pos + w - w // 2, 0, t)
        cnt = (hi - lo).astype(jnp.float32)
        means.append((cs[:, hi, g] - cs[:, lo, g]) / cnt[None, :, None])
    pooled = jnp.stack(means, axis=2)
    y = jnp.einsum('btgc,gcd->btgd', (pooled - xf).astype(x.dtype), w_pool)
    return y.reshape(b, t, d) * scale


def _neighbourhood_attention(q, k, v, k_ctx, v_ctx, rpb, rows):
    b, t, h, dh = q.shape
    wr = min(NA_WIN_R, rows)
    nqc = GRID_W // NA_WIN_C
    scale = dh ** -0.5
    r = jnp.arange(rows)
    row_idx = jnp.clip(r - wr // 2, 0, rows - wr)[:, None] + jnp.arange(wr)[None, :]
    j = jnp.arange(nqc)
    col_idx = (jnp.clip(j * NA_WIN_C - NA_WIN_C // 2, 0, GRID_W - NA_KEY_COLS)[:, None]
               + jnp.arange(NA_KEY_COLS)[None, :])
    qcol = j[:, None] * NA_WIN_C + jnp.arange(NA_WIN_C)[None, :]
    qstart = jnp.clip(qcol - NA_WIN_C // 2, 0, GRID_W - NA_WIN_C)
    kcol = col_idx[:, None, :]
    valid = (kcol >= qstart[..., None]) & (kcol < qstart[..., None] + NA_WIN_C)
    ir = row_idx - r[:, None] + NA_WIN_R - 1
    ic = jnp.clip(kcol - qcol[..., None] + NA_WIN_C - 1, 0, 2 * NA_WIN_C - 2)
    bias = rpb.astype(jnp.float32)[:, ir[:, None, None, :, None], ic[None, :, :, None, :]]
    bias = jnp.where(valid[None, None, :, :, None, :], bias, NEG_INF)
    bias = jnp.transpose(bias, (1, 2, 0, 3, 4, 5)).reshape(rows, nqc, h, NA_WIN_C, wr * NA_KEY_COLS)
    ri = row_idx[:, None, :, None]
    ci = col_idx[None, :, None, :]
    kb = k.reshape(b, rows, GRID_W, h, dh)[:, ri, ci].reshape(b, rows, nqc, wr * NA_KEY_COLS, h, dh)
    vb = v.reshape(b, rows, GRID_W, h, dh)[:, ri, ci].reshape(b, rows, nqc, wr * NA_KEY_COLS, h, dh)
    qb = q.reshape(b, rows, nqc, NA_WIN_C, h, dh)
    s_loc = jnp.einsum('brjqhd,brjkhd->brjhqk', qb, kb).astype(jnp.float32) * scale + bias[None]
    s_ctx = jnp.einsum('brjqhd,bkhd->brjhqk', qb, k_ctx).astype(jnp.float32) * scale
    lc = k_ctx.shape[1]
    p = jax.nn.softmax(jnp.concatenate([s_ctx, s_loc], axis=-1), axis=-1).astype(v.dtype)
    o = (jnp.einsum('brjhqk,bkhd->brjqhd', p[..., :lc], v_ctx)
         + jnp.einsum('brjhqk,brjkhd->brjqhd', p[..., lc:], vb))
    return o.reshape(b, t, h, dh)


def _na_project(hx, w_qkv):
    b, t, _ = hx.shape
    qkv = (hx @ w_qkv).reshape(b, t, 3, NA_HEADS, NA_HEAD_DIM)
    return qkv[:, :, 0], qkv[:, :, 1], qkv[:, :, 2]


def _diff_project(hx, w_qkv):
    b, t, _ = hx.shape
    q, k, v = jnp.split(hx @ w_qkv, 3, axis=-1)
    return (q.reshape(b, t, DIFF_HEADS, 2, DIFF_HEAD_DIM), k.reshape(b, t, DIFF_HEADS, 2, DIFF_HEAD_DIM),
            v.reshape(b, t, DIFF_HEADS, 2 * DIFF_HEAD_DIM))


def _diff_lambda(lam_p, lam_init):
    lp = lam_p.astype(jnp.float32)
    return jnp.exp(jnp.sum(lp[0] * lp[1])) - jnp.exp(jnp.sum(lp[2] * lp[3])) + lam_init


def _diff_output(o, subln, w_o, lam_init):
    b, t = o.shape[:2]
    o = _rmsnorm(o, subln, DIFF_SUBLN_EPS) * (1.0 - lam_init)
    return o.reshape(b, t, D_MODEL) @ w_o


def _mla_queries(hx, w_qa, g_q, w_qb):
    b, t, _ = hx.shape
    return (_rmsnorm(hx @ w_qa, g_q) @ w_qb).reshape(b, t, MLA_HEADS, MLA_NOPE + MLA_ROPE)


def _mla_compress(hx, w_kva, g_kv):
    kv = hx @ w_kva
    return _rmsnorm(kv[..., :MLA_KV_LORA], g_kv), kv[..., MLA_KV_LORA:]


def _mla_attend(q, ckv, kpe, w_kvb, w_o):
    b, tk, _ = ckv.shape
    kvb = (ckv @ w_kvb).reshape(b, tk, MLA_HEADS, MLA_NOPE + MLA_V)
    k = jnp.concatenate([kvb[..., :MLA_NOPE],
                         jnp.broadcast_to(kpe[:, :, None, :], (b, tk, MLA_HEADS, MLA_ROPE))], axis=-1)
    o = _softmax_attention(q, k, kvb[..., MLA_NOPE:], (MLA_NOPE + MLA_ROPE) ** -0.5)
    return o.reshape(q.shape[0], q.shape[1], MLA_HEADS * MLA_V) @ w_o


def _swiglu(x, wg, wu, wd):
    return (jax.nn.silu(x @ wg) * (x @ wu)) @ wd


def _moe(x, router, wg, wu, wd):
    b, t, d = x.shape
    xt = x.reshape(b * t, d)
    logits = (xt @ router).astype(jnp.float32)
    top_v, top_i = lax.top_k(logits, TOP_K)
    gates = jax.nn.softmax(top_v, axis=-1)
    dense_gate = jnp.sum(jax.nn.one_hot(top_i, N_EXPERTS, dtype=jnp.float32) * gates[..., None], axis=1)
    dense_gate = dense_gate.astype(x.dtype)
    out = jnp.zeros_like(xt)
    for e in range(N_EXPERTS):
        out = out + dense_gate[:, e:e + 1] * _swiglu(xt, wg[e], wu[e], wd[e])
    return out.reshape(b, t, d)


def setup_inputs(seed: int = 0) -> dict:
    key = jax.random.key(seed)
    ks = iter(jax.random.split(key, 48))

    def nrm(shape, scale):
        return jax.random.normal(next(ks), shape, jnp.float32) * scale

    def gain(shape):
        return 1.0 + nrm(shape, 0.05)

    D = D_MODEL
    G = POOL_GROUP
    return {
        'x_prompt': nrm((BATCH, SEQ, D), 1.0),
        'x_sample': nrm((DEC_BATCH, DEC_SEQ, D), 1.0),
        'c': nrm((DEC_BATCH, D), 1.0),
        'c_ctx': nrm((D,), 1.0),
        'cache_na_k': nrm((DEC_BATCH, N_NA_LAYERS, PAST_LEN, NA_HEADS, NA_HEAD_DIM), 1.0),
        'cache_na_v': nrm((DEC_BATCH, N_NA_LAYERS, PAST_LEN, NA_HEADS, NA_HEAD_DIM), 1.0),
        'cache_diff_k': nrm((DEC_BATCH, N_DIFF_LAYERS, PAST_LEN, DIFF_HEADS, 2, DIFF_HEAD_DIM), 1.0),
        'cache_diff_v': nrm((DEC_BATCH, N_DIFF_LAYERS, PAST_LEN, DIFF_HEADS, 2 * DIFF_HEAD_DIM), 1.0),
        'cache_mla_ckv': nrm((DEC_BATCH, N_MLA_LAYERS, PAST_LEN, MLA_KV_LORA), 1.0),
        'cache_mla_kpe': nrm((DEC_BATCH, N_MLA_LAYERS, PAST_LEN, MLA_ROPE), 1.0),
        'ada_w': nrm((DEPTH, D, 6 * D), 0.5 * D ** -0.5),
        'ada_b': nrm((DEPTH, 6 * D), 0.02),
        'norm1': gain((DEPTH, D)),
        'norm2': gain((DEPTH, D)),
        'pool_w': nrm((N_POOL_LAYERS, len(POOL_WINDOWS), G, G), G ** -0.5),
        'pool_scale': gain((N_POOL_LAYERS, D)),
        'na_qkv': nrm((N_NA_LAYERS, D, 3 * D), D ** -0.5),
        'na_rpb': nrm((N_NA_LAYERS, NA_HEADS, 2 * NA_WIN_R - 1, 2 * NA_WIN_C - 1), 0.1),
        'na_out': nrm((N_NA_LAYERS, D, D), D ** -0.5),
        'diff_qkv': nrm((N_DIFF_LAYERS, D, 3 * D), D ** -0.5),
        'diff_lambda': nrm((N_DIFF_LAYERS, 4, DIFF_HEAD_DIM), 0.1),
        'diff_subln': gain((N_DIFF_LAYERS, 2 * DIFF_HEAD_DIM)),
        'diff_out': nrm((N_DIFF_LAYERS, D, D), D ** -0.5),
        'mla_qa': nrm((N_MLA_LAYERS, D, MLA_Q_LORA), D ** -0.5),
        'mla_qnorm': gain((N_MLA_LAYERS, MLA_Q_LORA)),
        'mla_qb': nrm((N_MLA_LAYERS, MLA_Q_LORA, MLA_HEADS * (MLA_NOPE + MLA_ROPE)), MLA_Q_LORA ** -0.5),
        'mla_kva': nrm((N_MLA_LAYERS, D, MLA_KV_LORA + MLA_ROPE), D ** -0.5),
        'mla_kvnorm': gain((N_MLA_LAYERS, MLA_KV_LORA)),
        'mla_kvb': nrm((N_MLA_LAYERS, MLA_KV_LORA, MLA_HEADS * (MLA_NOPE + MLA_V)), MLA_KV_LORA ** -0.5),
        'mla_out': nrm((N_MLA_LAYERS, MLA_HEADS * MLA_V, D), (MLA_HEADS * MLA_V) ** -0.5),
        'ffn_gate': nrm((N_DENSE_LAYERS, D, FFN_HIDDEN), D ** -0.5),
        'ffn_up': nrm((N_DENSE_LAYERS, D, FFN_HIDDEN), D ** -0.5),
        'ffn_down': nrm((N_DENSE_LAYERS, FFN_HIDDEN, D), FFN_HIDDEN ** -0.5),
        'moe_router': nrm((N_MOE_LAYERS, D, N_EXPERTS), D ** -0.5),
        'moe_gate': nrm((N_MOE_LAYERS, N_EXPERTS, D, MOE_HIDDEN), D ** -0.5),
        'moe_up': nrm((N_MOE_LAYERS, N_EXPERTS, D, MOE_HIDDEN), D ** -0.5),
        'moe_down': nrm((N_MOE_LAYERS, N_EXPERTS, MOE_HIDDEN, D), MOE_HIDDEN ** -0.5),
        'final_norm': gain((D,)),
    }


def reference(x_prompt, x_sample, c, c_ctx, cache_na_k, cache_na_v, cache_diff_k, cache_diff_v,
              cache_mla_ckv, cache_mla_kpe, ada_w, ada_b, norm1, norm2, pool_w, pool_scale,
              na_qkv, na_rpb, na_out, diff_qkv, diff_lambda, diff_subln, diff_out,
              mla_qa, mla_qnorm, mla_qb, mla_kva, mla_kvnorm, mla_kvb, mla_out,
              ffn_gate, ffn_up, ffn_down, moe_router, moe_gate, moe_up, moe_down, final_norm):
    t_lat = x_sample.shape[1]
    rows = t_lat // GRID_W
    cos_d, sin_d = _axial_rope(t_lat, DIFF_HEAD_DIM)
    cos_m, sin_m = _axial_rope(t_lat, MLA_ROPE)
    xp, xs = x_prompt, x_sample
    na_k, na_v, df_k, df_v, ml_ckv, ml_kpe = [], [], [], [], [], []
    for l in range(DEPTH):
        kind, i = l % N_MIXERS, l // N_MIXERS
        mp = _adaln(c_ctx, ada_w[l], ada_b[l])
        ms = _adaln(c, ada_w[l], ada_b[l])
        hp = _modulate(xp, norm1[l], mp[0], mp[1])
        hs = _modulate(xs, norm1[l], ms[0], ms[1])
        if kind == 0:
            op = _multi_scale_pool(hp, pool_w[i], pool_scale[i])
            os_ = _multi_scale_pool(hs, pool_w[i], pool_scale[i])
        elif kind == 1:
            q, k, v = _na_project(hp, na_qkv[i])
            op = _softmax_attention(q, k, v, NA_HEAD_DIM ** -0.5).reshape(hp.shape) @ na_out[i]
            na_k.append(k)
            na_v.append(v)
            q, k, v = _na_project(hs, na_qkv[i])
            o = _neighbourhood_attention(q, k, v, cache_na_k[:, i], cache_na_v[:, i], na_rpb[i], rows)
            os_ = o.reshape(hs.shape) @ na_out[i]
        elif kind == 2:
            lam_init = 0.8 - 0.6 * math.exp(-0.3 * l)
            lam = _diff_lambda(diff_lambda[i], lam_init)
            sc = DIFF_HEAD_DIM ** -0.5
            q, k, v = _diff_project(hp, diff_qkv[i])
            op = _diff_output(_diff_attention(q, k, v, lam, sc), diff_subln[i], diff_out[i], lam_init)
            df_k.append(k)
            df_v.append(v)
            q, k, v = _diff_project(hs, diff_qkv[i])
            q = _apply_rope(q, cos_d, sin_d)
            k = _apply_rope(k, cos_d, sin_d)
            k_all = jnp.concatenate([cache_diff_k[:, i], k], axis=1)
            v_all = jnp.concatenate([cache_diff_v[:, i], v], axis=1)
            os_ = _diff_output(_diff_attention(q, k_all, v_all, lam, sc), diff_subln[i], diff_out[i], lam_init)
        else:
            q = _mla_queries(hp, mla_qa[i], mla_qnorm[i], mla_qb[i])
            ckv, kpe = _mla_compress(hp, mla_kva[i], mla_kvnorm[i])
            op = _mla_attend(q, ckv, kpe, mla_kvb[i], mla_out[i])
            ml_ckv.append(ckv)
            ml_kpe.append(kpe)
            q = _mla_queries(hs, mla_qa[i], mla_qnorm[i], mla_qb[i])
            q = jnp.concatenate([q[..., :MLA_NOPE], _apply_rope(q[..., MLA_NOPE:], cos_m, sin_m)], axis=-1)
            ckv, kpe = _mla_compress(hs, mla_kva[i], mla_kvnorm[i])
            kpe = _apply_rope(kpe, cos_m, sin_m)
            ckv_all = jnp.concatenate([cache_mla_ckv[:, i], ckv], axis=1)
            kpe_all = jnp.concatenate([cache_mla_kpe[:, i], kpe], axis=1)
            os_ = _mla_attend(q, ckv_all, kpe_all, mla_kvb[i], mla_out[i])
        xp = xp + mp[2] * op
        xs = xs + ms[2] * os_
        hp = _modulate(xp, norm2[l], mp[3], mp[4])
        hs = _modulate(xs, norm2[l], ms[3], ms[4])
        f = l // 2
        if l % 2 == 0:
            fp = _swiglu(hp, ffn_gate[f], ffn_up[f], ffn_down[f])
            fs = _swiglu(hs, ffn_gate[f], ffn_up[f], ffn_down[f])
        else:
            fp = _moe(hp, moe_router[f], moe_gate[f], moe_up[f], moe_down[f])
            fs = _moe(hs, moe_router[f], moe_gate[f], moe_up[f], moe_down[f])
        xp = xp + mp[5] * fp
        xs = xs + ms[5] * fs
    y_prompt = _rmsnorm(xp, final_norm)
    y_sample = _rmsnorm(xs, final_norm)
    new_na_k = jnp.stack(na_k, axis=1)
    new_na_v = jnp.stack(na_v, axis=1)
    new_diff_k = jnp.stack(df_k, axis=1)
    new_diff_v = jnp.stack(df_v, axis=1)
    new_mla_ckv = jnp.stack(ml_ckv, axis=1)
    new_mla_kpe = jnp.stack(ml_kpe, axis=1)
    return (y_prompt, y_sample, new_na_k, new_na_v, new_diff_k, new_diff_v, new_mla_ckv, new_mla_kpe)
```

```python
import functools
import math

import numpy as np
import jax
import jax.numpy as jnp
from jax import lax
from jax.experimental import pallas as pl
from jax.experimental.pallas import tpu as pltpu

F32 = jnp.float32
BF16 = jnp.bfloat16

D = 1024
N_CTX_SEQ = 32
T_CTX = 256
N_LAT_SEQ = 2
T_LAT = 2048
PAST = 512
M_CTX = N_CTX_SEQ * T_CTX
M_LAT = N_LAT_SEQ * T_LAT
M_TOK = M_CTX + M_LAT
GRID_W = 64
NORM_EPS = 1e-6
NEG_INF = -1e30
ROPE_BASE = 10000.0
POOL_WINDOWS = (2, 4, 8, 16)
NA_WIN_R = 8
NA_WIN_C = 16
DIFF_SUBLN_EPS = 1e-5
MLA_HEADS = 8
MLA_NOPE = 128
MLA_ROPE = 64
MLA_KV_LORA = 256
MLA_Q_LORA = 512
FFN_HIDDEN = 2816
N_EXPERTS = 8

LANE = 128
TM = 1024
TH = 256
SUB = 256
VMEM_LIMIT = 56 * 1024 * 1024

MOE_ROWS = 2 * M_TOK + N_EXPERTS * TM
MOE_TILES = MOE_ROWS // TM


def _cparams(sem):
    return pltpu.CompilerParams(dimension_semantics=sem, vmem_limit_bytes=VMEM_LIMIT)


def _group_of_tile(i, tm):
    n_ctx = M_CTX // tm
    return jnp.where(i < n_ctx, 0, 1 + (i - n_ctx) // (T_LAT // tm))


def _dot(a, b):
    return jnp.dot(a, b, preferred_element_type=F32)


def _dot_nt(a, b):
    return lax.dot_general(a, b, (((1,), (1,)), ((), ())), preferred_element_type=F32)


def _rms(x, g, eps):
    return x * lax.rsqrt(jnp.mean(x * x, axis=-1, keepdims=True) + eps) * g


def _ada_kernel(c_ref, w_ref, b_ref, o_ref):
    c = c_ref[...]
    s = (c * jax.nn.sigmoid(c)).astype(BF16)
    o_ref[...] = _dot(s, w_ref[...].astype(BF16)) + b_ref[...]


def _ada(cond8, ada_w, ada_b):
    depth = ada_w.shape[0]
    tn = 1536
    return pl.pallas_call(
        _ada_kernel,
        grid=(depth, 6 * D // tn),
        in_specs=[
            pl.BlockSpec((8, D), lambda l, j: (0, 0)),
            pl.BlockSpec((None, D, tn), lambda l, j: (l, 0, j)),
            pl.BlockSpec((None, 1, tn), lambda l, j: (l, 0, j)),
        ],
        out_specs=pl.BlockSpec((None, 8, tn), lambda l, j: (l, 0, j)),
        out_shape=jax.ShapeDtypeStruct((depth, 8, 6 * D), F32),
        compiler_params=_cparams(("parallel", "parallel")),
        name="ada",
    )(cond8, ada_w, ada_b.reshape(depth, 1, 6 * D))


def _modrows_kernel(x_ref, g_ref, sh_ref, sc_ref, o_ref, *, eps):
    x = _rms(x_ref[...], g_ref[...], eps)
    o_ref[...] = x * (1.0 + sc_ref[...]) + sh_ref[...]


def _modrows(x, g, mods, sh_c, sc_c, *, rows, row0, width, col_blk=0, tm=TM, eps=NORM_EPS,
             grouped=True):
    r0 = row0 // tm
    if grouped:
        grp = lambda i: _group_of_tile(i + r0, tm)
    else:
        grp = lambda i: 0
    return pl.pallas_call(
        functools.partial(_modrows_kernel, eps=eps),
        grid=(rows // tm,),
        in_specs=[
            pl.BlockSpec((tm, width), lambda i: (i + r0, col_blk)),
            pl.BlockSpec((1, width), lambda i: (0, 0)),
            pl.BlockSpec((None, None, 1, width), lambda i: (grp(i), sh_c, 0, 0)),
            pl.BlockSpec((None, None, 1, width), lambda i: (grp(i), sc_c, 0, 0)),
        ],
        out_specs=pl.BlockSpec((tm, width), lambda i: (i, 0)),
        out_shape=jax.ShapeDtypeStruct((rows, width), F32),
        compiler_params=_cparams(("parallel",)),
        name="modrows",
    )(x, g.reshape(1, width), mods, mods)


def _linear_kernel(*refs, prologue, eps, epilogue, split, tm):
    it = iter(refs)
    x_ref = next(it)
    x_lat_ref = next(it) if split else None
    w_ref = next(it)
    g_ref = next(it) if prologue != "none" else None
    sh_ref = next(it) if prologue == "mod" else None
    sc_ref = next(it) if prologue == "mod" else None
    res_ref = next(it) if epilogue == "resid" else None
    gate_ref = next(it) if epilogue == "resid" else None
    o_ref = next(it)
    xb_ref = next(it)

    def load(src_ref):
        x = src_ref[...]
        if prologue != "none":
            x = _rms(x, g_ref[...], eps)
        if prologue == "mod":
            x = x * (1.0 + sc_ref[...]) + sh_ref[...]
        xb_ref[...] = x.astype(BF16)

    first = pl.program_id(1) == 0
    if split:
        is_ctx = pl.program_id(0) < M_CTX // tm
        pl.when(first & is_ctx)(lambda: load(x_ref))
        pl.when(first & jnp.logical_not(is_ctx))(lambda: load(x_lat_ref))
    else:
        pl.when(first)(lambda: load(x_ref))

    acc = _dot(xb_ref[...], w_ref[...].astype(BF16))
    if epilogue == "resid":
        acc = res_ref[...] + gate_ref[...] * acc
    o_ref[...] = acc


def _linear(x, w, *, tn, k=None, x_col=0, norm_g=None, eps=NORM_EPS, mods=None, sh_c=None, sc_c=None,
            resid=None, gate_c=None, planes=False, tm=TM, name="linear"):
    split = isinstance(x, tuple)
    n = w.shape[1]
    prologue = "none" if norm_g is None else ("mod" if sh_c is not None else "norm")
    epilogue = "none" if resid is None else "resid"
    grp = lambda i: _group_of_tile(i, tm)
    if split:
        x_ctx, x_lat = x
        m = x_ctx.shape[0] + x_lat.shape[0]
        k = x_ctx.shape[1]
        n_ctx = x_ctx.shape[0] // tm
        args = [x_ctx, x_lat]
        in_specs = [
            pl.BlockSpec((tm, k), lambda i, j: (jnp.minimum(i, n_ctx - 1), 0)),
            pl.BlockSpec((tm, k), lambda i, j: (jnp.maximum(i - n_ctx, 0), 0)),
        ]
    else:
        m = x.shape[0]
        k = x.shape[1] if k is None else k
        args = [x]
        in_specs = [pl.BlockSpec((tm, k), lambda i, j: (i, x_col))]
    args.append(w)
    in_specs.append(pl.BlockSpec((k, tn), lambda i, j: (0, j)))
    if prologue != "none":
        args.append(norm_g.reshape(1, k))
        in_specs.append(pl.BlockSpec((1, k), lambda i, j: (0, 0)))
    if prologue == "mod":
        args += [mods, mods]
        in_specs += [
            pl.BlockSpec((None, None, 1, k), lambda i, j: (grp(i), sh_c, 0, 0)),
            pl.BlockSpec((None, None, 1, k), lambda i, j: (grp(i), sc_c, 0, 0)),
        ]
    if epilogue == "resid":
        args += [resid, mods]
        in_specs += [
            pl.BlockSpec((tm, tn), lambda i, j: (i, j)),
            pl.BlockSpec((None, None, 1, tn), lambda i, j: (grp(i), gate_c, 0, j)),
        ]
    if planes:
        out_spec = pl.BlockSpec((None, tm, tn), lambda i, j: (j, i, 0))
        out_shape = jax.ShapeDtypeStruct((n // tn, m, tn), F32)
    else:
        out_spec = pl.BlockSpec((tm, tn), lambda i, j: (i, j))
        out_shape = jax.ShapeDtypeStruct((m, n), F32)
    return pl.pallas_call(
        functools.partial(_linear_kernel, prologue=prologue, eps=eps, epilogue=epilogue, split=split, tm=tm),
        grid=(m // tm, n // tn),
        in_specs=in_specs,
        out_specs=out_spec,
        out_shape=out_shape,
        scratch_shapes=[pltpu.VMEM((tm, k), BF16)],
        compiler_params=_cparams(("parallel", "arbitrary")),
        name=name,
    )(*args)


POOL_ROWS = 2048


def _pool_kernel(h_ref, x_ref, w_ref, ps_ref, gate_ref, o_ref):
    i = pl.program_id(0)
    g = pl.program_id(1)
    seq = jnp.where(i < M_CTX // POOL_ROWS, T_CTX, T_LAT)
    t = lax.broadcasted_iota(jnp.int32, (POOL_ROWS, 1), 0) & (seq - 1)
    for gi, win in enumerate(POOL_WINDOWS):
        @pl.when(g == gi)
        def _(win=win):
            h = h_ref[...]
            acc = jnp.zeros_like(h)
            for d in range(-(win // 2), win - win // 2):
                sh = h if d == 0 else pltpu.roll(h, (-d) % POOL_ROWS, 0)
                ok = (t + d >= 0) & (t + d < seq)
                acc = acc + jnp.where(ok, sh, 0.0)
            cnt = jnp.minimum(t + (win - win // 2), seq) - jnp.maximum(t - win // 2, 0)
            pooled = acc / cnt.astype(F32)
            y = _dot((pooled - h).astype(BF16), w_ref[...].astype(BF16)) * ps_ref[...]
            o_ref[...] = x_ref[...] + gate_ref[...] * y


def _pool(h, x, pool_w, pool_scale, mods, gate_c):
    gw = D // len(POOL_WINDOWS)
    grp = lambda i: _group_of_tile(i, POOL_ROWS)
    return pl.pallas_call(
        _pool_kernel,
        grid=(M_TOK // POOL_ROWS, len(POOL_WINDOWS)),
        in_specs=[
            pl.BlockSpec((POOL_ROWS, gw), lambda i, g: (i, g)),
            pl.BlockSpec((POOL_ROWS, gw), lambda i, g: (i, g)),
            pl.BlockSpec((None, gw, gw), lambda i, g: (g, 0, 0)),
            pl.BlockSpec((1, gw), lambda i, g: (0, g)),
            pl.BlockSpec((None, None, 1, gw), lambda i, g: (grp(i), gate_c, 0, g)),
        ],
        out_specs=pl.BlockSpec((POOL_ROWS, gw), lambda i, g: (i, g)),
        out_shape=jax.ShapeDtypeStruct((M_TOK, D), F32),
        compiler_params=_cparams(("parallel", "parallel")),
        name="pool",
    )(h, x, pool_w, pool_scale.reshape(1, D), mods)


def _swiglu_kernel(*refs, moe, nj):
    if moe:
        te_ref, tv_ref, x_ref, wg_ref, wu_ref, wd_ref, rg_ref, o_ref, xb_ref, acc_ref = refs
    else:
        x_ref, wg_ref, wu_ref, wd_ref, g_ref, sh_ref, sc_ref, gate_ref, o_ref, xb_ref, acc_ref = refs
    j = pl.program_id(1)
    n_valid = tv_ref[pl.program_id(0)] if moe else TM

    @pl.when(j == 0)
    def _():
        x = x_ref[...]
        if not moe:
            x = _rms(x, g_ref[...], NORM_EPS) * (1.0 + sc_ref[...]) + sh_ref[...]
        xb_ref[...] = x.astype(BF16)
        acc_ref[...] = jnp.zeros_like(acc_ref)

    wg = wg_ref[...].astype(BF16)
    wu = wu_ref[...].astype(BF16)
    wd = wd_ref[...].astype(BF16)
    for s in range(TM // SUB):
        rows = pl.ds(s * SUB, SUB)

        def block(rows=rows):
            xb = xb_ref[rows, :]
            a = _dot(xb, wg)
            u = _dot(xb, wu)
            hcat = (a * jax.nn.sigmoid(a) * u).astype(BF16)
            acc_ref[rows, :] += _dot(hcat, wd)

        if moe:
            pl.when(s * SUB < n_valid)(block)
        else:
            block()

    @pl.when(j == nj - 1)
    def _():
        if moe:
            o_ref[...] = acc_ref[...] * rg_ref[...]
        else:
            o_ref[...] = x_ref[...] + gate_ref[...] * acc_ref[...]


def _ffn_dense(x, wg, wu, wd, norm_g, mods, sh_c, sc_c, gate_c):
    nj = FFN_HIDDEN // TH
    grp = lambda i: _group_of_tile(i, TM)
    mod_spec = lambda c: pl.BlockSpec((None, None, 1, D), lambda i, j: (grp(i), c, 0, 0))
    return pl.pallas_call(
        functools.partial(_swiglu_kernel, moe=False, nj=nj),
        grid=(M_TOK // TM, nj),
        in_specs=[
            pl.BlockSpec((TM, D), lambda i, j: (i, 0)),
            pl.BlockSpec((D, TH), lambda i, j: (0, j)),
            pl.BlockSpec((D, TH), lambda i, j: (0, j)),
            pl.BlockSpec((TH, D), lambda i, j: (j, 0)),
            pl.BlockSpec((1, D), lambda i, j: (0, 0)),
            mod_spec(sh_c), mod_spec(sc_c), mod_spec(gate_c),
        ],
        out_specs=pl.BlockSpec((TM, D), lambda i, j: (i, 0)),
        out_shape=jax.ShapeDtypeStruct((M_TOK, D), F32),
        scratch_shapes=[pltpu.VMEM((TM, D), BF16), pltpu.VMEM((TM, D), F32)],
        compiler_params=_cparams(("parallel", "arbitrary")),
        name="ffn_dense",
    )(x, wg, wu, wd, norm_g.reshape(1, D), mods, mods, mods)


def _ffn_moe(xs, wg, wu, wd, row_gate, tile_expert, tile_valid):
    nj = FFN_HIDDEN // TH
    jj = lambda i, j, tv: jnp.where(tv[i] > 0, j, nj - 1)
    return pl.pallas_call(
        functools.partial(_swiglu_kernel, moe=True, nj=nj),
        grid_spec=pltpu.PrefetchScalarGridSpec(
            num_scalar_prefetch=2,
            grid=(MOE_TILES, nj),
            in_specs=[
                pl.BlockSpec((TM, D), lambda i, j, te, tv: (i, 0)),
                pl.BlockSpec((None, D, TH), lambda i, j, te, tv: (te[i], 0, jj(i, j, tv))),
                pl.BlockSpec((None, D, TH), lambda i, j, te, tv: (te[i], 0, jj(i, j, tv))),
                pl.BlockSpec((None, TH, D), lambda i, j, te, tv: (te[i], jj(i, j, tv), 0)),
                pl.BlockSpec((TM, 1), lambda i, j, te, tv: (i, 0)),
            ],
            out_specs=pl.BlockSpec((TM, D), lambda i, j, te, tv: (i, 0)),
            scratch_shapes=[pltpu.VMEM((TM, D), BF16), pltpu.VMEM((TM, D), F32)],
        ),
        out_shape=jax.ShapeDtypeStruct((MOE_ROWS, D), F32),
        compiler_params=_cparams(("parallel", "arbitrary")),
        name="ffn_moe",
    )(tile_expert, tile_valid, xs, wg, wu, wd, row_gate)


def _router_kernel(x_ref, g_ref, sh_ref, sc_ref, r_ref, h_ref, idx_ref, gates_ref):
    h = _rms(x_ref[...], g_ref[...], NORM_EPS) * (1.0 + sc_ref[...]) + sh_ref[...]
    h_ref[...] = h
    w = r_ref[...]
    h_hi = h.astype(BF16)
    h_lo = (h - h_hi.astype(F32)).astype(BF16)
    w_hi = w.astype(BF16)
    w_lo = (w - w_hi.astype(F32)).astype(BF16)
    logits = _dot(h_hi, w_hi) + (_dot(h_lo, w_hi) + _dot(h_hi, w_lo))
    col = lax.broadcasted_iota(jnp.int32, logits.shape, 1)
    colf = col.astype(F32)
    lg = jnp.where(col < N_EXPERTS, logits, -jnp.inf)
    m1 = jnp.max(lg, axis=-1, keepdims=True)
    i1 = jnp.min(jnp.where(lg == m1, colf, float(LANE)), axis=-1, keepdims=True)
    lg2 = jnp.where(colf == i1, -jnp.inf, lg)
    m2 = jnp.max(lg2, axis=-1, keepdims=True)
    i2 = jnp.min(jnp.where(lg2 == m2, colf, float(LANE)), axis=-1, keepdims=True)
    i1 = i1.astype(jnp.int32)
    i2 = i2.astype(jnp.int32)
    e = jnp.exp(m2 - m1)
    g1 = 1.0 / (1.0 + e)
    g2 = e / (1.0 + e)
    idx_ref[...] = jnp.where(col == 0, i1, jnp.where(col == 1, i2, 0))
    gates_ref[...] = jnp.where(col == 0, g1, jnp.where(col == 1, g2, 0.0))


def _router(x, norm_g, mods, sh_c, sc_c, router):
    r_pad = jnp.pad(router, ((0, 0), (0, LANE - N_EXPERTS)))
    grp = lambda i: _group_of_tile(i, TM)
    return pl.pallas_call(
        _router_kernel,
        grid=(M_TOK // TM,),
        in_specs=[
            pl.BlockSpec((TM, D), lambda i: (i, 0)),
            pl.BlockSpec((1, D), lambda i: (0, 0)),
            pl.BlockSpec((None, None, 1, D), lambda i: (grp(i), sh_c, 0, 0)),
            pl.BlockSpec((None, None, 1, D), lambda i: (grp(i), sc_c, 0, 0)),
            pl.BlockSpec((D, LANE), lambda i: (0, 0)),
        ],
        out_specs=[
            pl.BlockSpec((TM, D), lambda i: (i, 0)),
            pl.BlockSpec((TM, LANE), lambda i: (i, 0)),
            pl.BlockSpec((TM, LANE), lambda i: (i, 0)),
        ],
        out_shape=[
            jax.ShapeDtypeStruct((M_TOK, D), F32),
            jax.ShapeDtypeStruct((M_TOK, LANE), jnp.int32),
            jax.ShapeDtypeStruct((M_TOK, LANE), F32),
        ],
        compiler_params=_cparams(("parallel",)),
        name="router",
    )(x, norm_g.reshape(1, D), mods, mods, r_pad)


def _route_plan(e1, e2, g1, g2):
    m = e1.shape[0]
    e = jnp.concatenate([e1, e2])
    oh = (e[:, None] == jnp.arange(N_EXPERTS, dtype=jnp.int32)[None, :]).astype(jnp.int32)
    csum = jnp.cumsum(oh, axis=0)
    rank = jnp.sum((csum - oh) * oh, axis=1)
    counts = csum[-1]
    padded = ((counts + TM - 1) // TM) * TM
    ends = jnp.cumsum(padded)
    starts = ends - padded
    pos = jnp.sum(oh * starts[None, :], axis=1) + rank
    tile_start = jnp.arange(MOE_TILES, dtype=jnp.int32) * TM
    te = jnp.minimum(jnp.sum((tile_start[:, None] >= ends[None, :]).astype(jnp.int32), axis=1),
                     N_EXPERTS - 1)
    used = tile_start < ends[-1]
    valid = jnp.where(used, jnp.clip(counts[te] - (tile_start - starts[te]), 0, TM), 0)
    last_e = te[jnp.maximum(ends[-1] // TM - 1, 0)]
    te = jnp.where(used, te, last_e)
    sub_off = jnp.arange(TM // SUB, dtype=jnp.int32) * SUB
    sub_valid = jnp.clip(valid[:, None] - sub_off[None, :], 0, SUB).reshape(-1)
    tok = jnp.concatenate([jnp.arange(m, dtype=jnp.int32)] * 2)
    src = jnp.zeros((MOE_ROWS,), jnp.int32).at[pos].set(tok, unique_indices=True)
    row_gate = jnp.zeros((MOE_ROWS,), F32).at[pos].set(jnp.concatenate([g1, g2]), unique_indices=True)
    return (pos.astype(jnp.int32), src, row_gate.reshape(MOE_ROWS, 1), te.astype(jnp.int32),
            valid.astype(jnp.int32), sub_valid.astype(jnp.int32))


def _gather_kernel(src_ref, sv_ref, h_hbm, o_ref, sem):
    i = pl.program_id(0)
    nv = sv_ref[i]

    @pl.when(nv == 0)
    def _():
        o_ref[...] = jnp.zeros_like(o_ref)

    @pl.when(nv > 0)
    def _():
        def issue(r, c):
            tok = src_ref[i * SUB + r]
            pltpu.make_async_copy(h_hbm.at[pl.ds(tok, 1)], o_ref.at[pl.ds(r, 1)], sem).start()
            return c

        lax.fori_loop(0, SUB, issue, 0)
        pltpu.make_async_copy(h_hbm.at[pl.ds(0, SUB)], o_ref, sem).wait()


def _moe_gather(h, src, sub_valid):
    return pl.pallas_call(
        _gather_kernel,
        grid_spec=pltpu.PrefetchScalarGridSpec(
            num_scalar_prefetch=2,
            grid=(MOE_ROWS // SUB,),
            in_specs=[pl.BlockSpec(memory_space=pl.ANY)],
            out_specs=pl.BlockSpec((SUB, D), lambda i, s, v: (i, 0)),
            scratch_shapes=[pltpu.SemaphoreType.DMA],
        ),
        out_shape=jax.ShapeDtypeStruct((MOE_ROWS, D), F32),
        compiler_params=_cparams(("arbitrary",)),
        name="moe_gather",
    )(src, sub_valid, h)


def _combine_kernel(pos_ref, y_hbm, x_ref, gate_ref, o_ref, y1_ref, y2_ref, sem):
    i = pl.program_id(0)

    def issue(r, c):
        p1 = pos_ref[i * SUB + r]
        p2 = pos_ref[M_TOK + i * SUB + r]
        pltpu.make_async_copy(y_hbm.at[pl.ds(p1, 1)], y1_ref.at[pl.ds(r, 1)], sem.at[0]).start()
        pltpu.make_async_copy(y_hbm.at[pl.ds(p2, 1)], y2_ref.at[pl.ds(r, 1)], sem.at[1]).start()
        return c

    lax.fori_loop(0, SUB, issue, 0)
    pltpu.make_async_copy(y_hbm.at[pl.ds(0, SUB)], y1_ref, sem.at[0]).wait()
    pltpu.make_async_copy(y_hbm.at[pl.ds(0, SUB)], y2_ref, sem.at[1]).wait()
    o_ref[...] = x_ref[...] + gate_ref[...] * (y1_ref[...] + y2_ref[...])


def _moe_combine(y, pos, x, mods, gate_c):
    grp = lambda i: _group_of_tile(i, SUB)
    return pl.pallas_call(
        _combine_kernel,
        grid_spec=pltpu.PrefetchScalarGridSpec(
            num_scalar_prefetch=1,
            grid=(M_TOK // SUB,),
            in_specs=[
                pl.BlockSpec(memory_space=pl.ANY),
                pl.BlockSpec((SUB, D), lambda i, p: (i, 0)),
                pl.BlockSpec((None, None, 1, D), lambda i, p: (grp(i), gate_c, 0, 0)),
            ],
            out_specs=pl.BlockSpec((SUB, D), lambda i, p: (i, 0)),
            scratch_shapes=[pltpu.VMEM((SUB, D), F32), pltpu.VMEM((SUB, D), F32),
                            pltpu.SemaphoreType.DMA((2,))],
        ),
        out_shape=jax.ShapeDtypeStruct((M_TOK, D), F32),
        compiler_params=_cparams(("arbitrary",)),
        name="moe_combine",
    )(pos, y, x, mods)


def _moe(x, norm_g, mods, sh_c, sc_c, gate_c, router, wg, wu, wd):
    h, idx, gates = _router(x, norm_g, mods, sh_c, sc_c, router)
    pos, src, row_gate, te, tv, sub_valid = _route_plan(idx[:, 0], idx[:, 1], gates[:, 0], gates[:, 1])
    xs = _moe_gather(h, src, sub_valid)
    ys = _ffn_moe(xs, wg, wu, wd, row_gate, te, tv)
    return _moe_combine(ys, pos, x, mods, gate_c)


def _rope_tables(width, lo, hi):
    pos = jnp.arange(T_LAT)
    rows = (pos // GRID_W).astype(F32)
    cols = (pos % GRID_W).astype(F32)
    n_freq = 16
    inv = ROPE_BASE ** (-jnp.arange(n_freq, dtype=F32) / n_freq)
    ang = jnp.concatenate([rows[:, None] * inv[None, :], cols[:, None] * inv[None, :]], axis=-1)
    cos, sin = jnp.cos(ang), jnp.sin(ang)
    lane = np.arange(width)
    active = (lane >= lo) & (lane < hi)
    fi = (lane - lo) % 32
    first = ((lane - lo) % 64) < 32
    c = jnp.where(active[None, :], cos[:, fi], 1.0)
    s = jnp.where(active[None, :], jnp.where(first[None, :], -sin[:, fi], sin[:, fi]), 0.0)
    return c, s


def _rope_kernel(x_ref, c_ref, s_ref, o_ref):
    x = x_ref[...]
    w = x.shape[-1]
    lane = lax.broadcasted_iota(jnp.int32, (1, w), 1)
    first = (lane & 63) < 32
    partner = jnp.where(first, pltpu.roll(x, w - 32, 1), pltpu.roll(x, 32, 1))
    o_ref[...] = x * c_ref[...] + partner * s_ref[...]


def _rope(x, c, s, *, planes, row0, width, col0=0, ncol=1, tr=512):
    r0 = row0 // tr
    nt = T_LAT // tr
    if x.ndim == 3:
        ncol = x.shape[2] // width
        x_spec = pl.BlockSpec((None, tr, width), lambda p, i, j: (p, i + r0, j))
    else:
        x_spec = pl.BlockSpec((tr, width), lambda p, i, j: (i + r0, col0 + j))
    return pl.pallas_call(
        _rope_kernel,
        grid=(planes, M_LAT // tr, ncol),
        in_specs=[
            x_spec,
            pl.BlockSpec((tr, width), lambda p, i, j: (i % nt, 0)),
            pl.BlockSpec((tr, width), lambda p, i, j: (i % nt, 0)),
        ],
        out_specs=pl.BlockSpec((None, tr, width), lambda p, i, j: (p, i, j)),
        out_shape=jax.ShapeDtypeStruct((planes, M_LAT, ncol * width), F32),
        compiler_params=_cparams(("parallel", "parallel", "parallel")),
        name="rope",
    )(x, c, s)


def _attn_kernel(*refs, kind, hb, nseg, scale, lam_init):
    it = iter(refs)
    q_ref = next(it)
    k_refs = [next(it) for _ in range(nseg)]
    kpe_refs = [next(it) for _ in range(nseg)] if kind == "mla" else None
    v_refs = [next(it) for _ in range(nseg)] if kind != "mla" else k_refs
    lam_ref = next(it) if kind == "diff" else None
    subln_ref = next(it) if kind == "diff" else None
    o_ref = next(it)

    lane = lax.broadcasted_iota(jnp.int32, (1, LANE), 1)
    lo_half = lane < 64

    def softmax_parts(s_list):
        m = functools.reduce(jnp.maximum, [jnp.max(s, axis=-1, keepdims=True) for s in s_list])
        p_list = [jnp.exp(s - m) for s in s_list]
        l = functools.reduce(jnp.add, [jnp.sum(p, axis=-1, keepdims=True) for p in p_list])
        return p_list, 1.0 / l

    for hh in range(hb):
        if kind == "mla":
            qn = q_ref[:, hh * 256:hh * 256 + 128].astype(BF16)
            qp = q_ref[:, hh * 256 + 128:hh * 256 + 256].astype(BF16)
            s_list = []
            for kr, pr in zip(k_refs, kpe_refs):
                kn = kr[:, hh * 256:hh * 256 + 128].astype(BF16)
                s_list.append((_dot_nt(qn, kn) + _dot_nt(qp, pr[...].astype(BF16))) * scale)
            p_list, inv_l = softmax_parts(s_list)
            o = functools.reduce(jnp.add, [
                _dot(p.astype(BF16), vr[:, hh * 256 + 128:hh * 256 + 256].astype(BF16))
                for p, vr in zip(p_list, v_refs)])
            o_ref[:, hh * LANE:(hh + 1) * LANE] = o * inv_l
            continue

        sl = slice(hh * LANE, (hh + 1) * LANE)
        q2 = q_ref[:, sl]
        k2 = [kr[:, sl].astype(BF16) for kr in k_refs]
        v2 = [vr[:, sl].astype(BF16) for vr in v_refs]
        parts = []
        for n in range(2):
            qm = jnp.where(lo_half if n == 0 else jnp.logical_not(lo_half), q2, 0.0).astype(BF16)
            parts.append(softmax_parts([_dot_nt(qm, kk) * scale for kk in k2]))
        if kind == "na":
            outs = []
            for p_list, inv_l in parts:
                o = functools.reduce(jnp.add, [_dot(p.astype(BF16), vv) for p, vv in zip(p_list, v2)])
                outs.append(o * inv_l)
            o_ref[:, sl] = jnp.where(lo_half, outs[0], outs[1])
        else:
            lam = lam_ref[...][:, :1]
            (p0, il0), (p1, il1) = parts
            o = functools.reduce(jnp.add, [
                _dot((a * il0 - lam * (b * il1)).astype(BF16), vv) for a, b, vv in zip(p0, p1, v2)])
            o = _rms(o, subln_ref[...], DIFF_SUBLN_EPS) * (1.0 - lam_init)
            o_ref[:, sl] = o


def _attention(kind, *, grid, q, ks, vs, kpes=None, extra=(), rows, out_spec, hb, scale, lam_init=0.0,
               name="attn"):
    ops = [q] + list(ks) + (list(kpes) if kind == "mla" else []) + (list(vs) if kind != "mla" else [])
    ops += list(extra)
    return pl.pallas_call(
        functools.partial(_attn_kernel, kind=kind, hb=hb, nseg=len(ks), scale=scale, lam_init=lam_init),
        grid=grid,
        in_specs=[s for _, s in ops],
        out_specs=out_spec,
        out_shape=jax.ShapeDtypeStruct((rows, D), F32),
        compiler_params=_cparams(("parallel",) * len(grid)),
        name=name,
    )(*[a for a, _ in ops])


NA_QROWS = 4
NA_KROWS = 12
NA_TQ = NA_QROWS * GRID_W
NA_TK = NA_KROWS * GRID_W


def _na_bias_table(rpb):
    h = rpb.shape[0]
    qc = np.arange(GRID_W)[:, None]
    kc = np.arange(GRID_W)[None, :]
    qs = np.clip(qc - NA_WIN_C // 2, 0, GRID_W - NA_WIN_C)
    col_ok = (kc >= qs) & (kc < qs + NA_WIN_C)
    ic = np.clip(kc - qc + NA_WIN_C - 1, 0, 2 * NA_WIN_C - 2)
    tab = jnp.take(rpb.astype(F32), jnp.asarray(ic.reshape(-1)), axis=2)
    tab = tab.reshape(h, 2 * NA_WIN_R - 1, GRID_W, GRID_W)
    n_rows = T_LAT // GRID_W
    ir_idx = np.zeros((3, NA_QROWS, NA_KROWS), np.int32)
    row_ok = np.zeros((3, NA_QROWS, NA_KROWS), bool)
    for pat, g in enumerate((0, 1, n_rows // NA_QROWS - 1)):
        base = int(np.clip(NA_QROWS * g - NA_QROWS, 0, n_rows - NA_KROWS))
        for qr in range(NA_QROWS):
            r = NA_QROWS * g + qr
            r0 = int(np.clip(r - NA_WIN_R // 2, 0, n_rows - NA_WIN_R))
            for kr in range(NA_KROWS):
                ok = r0 <= base + kr < r0 + NA_WIN_R
                row_ok[pat, qr, kr] = ok
                ir_idx[pat, qr, kr] = (base + kr - r + NA_WIN_R - 1) if ok else 0
    b = jnp.take(tab, jnp.asarray(ir_idx.reshape(-1)), axis=1)
    b = b.reshape(h, 3, NA_QROWS, NA_KROWS, GRID_W, GRID_W)
    ok = row_ok[None, :, :, :, None, None] & col_ok[None, None, None, None, :, :]
    b = jnp.where(jnp.asarray(ok), b, NEG_INF)
    b = jnp.transpose(b, (1, 0, 2, 4, 3, 5))
    return b.reshape(3, h, NA_TQ, NA_TK)


def _na_lat_kernel(q_ref, kc_ref, vc_ref, k_ref, v_ref, b_ref, o_ref, *, scale):
    g = pl.program_id(2)
    n_rows = T_LAT // GRID_W
    base = jnp.clip(NA_QROWS * g - NA_QROWS, 0, n_rows - NA_KROWS) * GRID_W
    base = pl.multiple_of(base, GRID_W)
    lane = lax.broadcasted_iota(jnp.int32, (1, LANE), 1)
    lo_half = lane < 64
    q2 = q_ref[...]
    kc = kc_ref[...].astype(BF16)
    vc = vc_ref[...].astype(BF16)
    kl = k_ref[pl.ds(base, NA_TK), :].astype(BF16)
    vl = v_ref[pl.ds(base, NA_TK), :].astype(BF16)
    outs = []
    for n in range(2):
        qm = jnp.where(lo_half if n == 0 else jnp.logical_not(lo_half), q2, 0.0).astype(BF16)
        s_c = _dot_nt(qm, kc) * scale
        s_l = _dot_nt(qm, kl) * scale + b_ref[n]
        m = jnp.maximum(jnp.max(s_c, axis=-1, keepdims=True), jnp.max(s_l, axis=-1, keepdims=True))
        p_c = jnp.exp(s_c - m)
        p_l = jnp.exp(s_l - m)
        l = jnp.sum(p_c, axis=-1, keepdims=True) + jnp.sum(p_l, axis=-1, keepdims=True)
        o = _dot(p_c.astype(BF16), vc) + _dot(p_l.astype(BF16), vl)
        outs.append(o * (1.0 / l))
    o_ref[...] = jnp.where(lo_half, outs[0], outs[1])


def _na_lat(qkv, cache_k, cache_v, bias):
    n_g = T_LAT // NA_TQ
    q_blk0 = M_CTX // NA_TQ
    kv_blk0 = M_CTX // T_LAT
    pat = lambda g: jnp.where(g == 0, 0, jnp.where(g == n_g - 1, 2, 1))
    return pl.pallas_call(
        functools.partial(_na_lat_kernel, scale=0.125),
        grid=(N_LAT_SEQ, D // LANE, n_g),
        in_specs=[
            pl.BlockSpec((None, NA_TQ, LANE), lambda b, h, g: (0, q_blk0 + b * n_g + g, h)),
            pl.BlockSpec((None, PAST, LANE), lambda b, h, g: (b, 0, h)),
            pl.BlockSpec((None, PAST, LANE), lambda b, h, g: (b, 0, h)),
            pl.BlockSpec((None, T_LAT, LANE), lambda b, h, g: (1, kv_blk0 + b, h)),
            pl.BlockSpec((None, T_LAT, LANE), lambda b, h, g: (2, kv_blk0 + b, h)),
            pl.BlockSpec((None, 2, NA_TQ, NA_TK), lambda b, h, g: (pat(g), h, 0, 0)),
        ],
        out_specs=pl.BlockSpec((NA_TQ, LANE), lambda b, h, g: (b * n_g + g, h)),
        out_shape=jax.ShapeDtypeStruct((M_LAT, D), F32),
        compiler_params=_cparams(("parallel", "parallel", "parallel")),
        name="na_latent",
    )(qkv, cache_k, cache_v, qkv, qkv, bias)


def kernel(x_prompt, x_sample, c, c_ctx, cache_na_k, cache_na_v, cache_diff_k, cache_diff_v, cache_mla_ckv, cache_mla_kpe, ada_w, ada_b, norm1, norm2, pool_w, pool_scale, na_qkv, na_rpb, na_out, diff_qkv, diff_lambda, diff_subln, diff_out, mla_qa, mla_qnorm, mla_qb, mla_kva, mla_kvnorm, mla_kvb, mla_out, ffn_gate, ffn_up, ffn_down, moe_router, moe_gate, moe_up, moe_down, final_norm):
    depth = ada_w.shape[0]
    x = jnp.concatenate([x_prompt.reshape(M_CTX, D), x_sample.reshape(M_LAT, D)], axis=0)
    cond8 = jnp.concatenate([c_ctx[None, :], c, jnp.zeros((5, D), F32)], axis=0)
    mods_all = _ada(cond8, ada_w, ada_b)[:, :3].reshape(depth, 3, 6, 1, D)
    outs = {}

    for l in range(depth):
        kind, li = l % 4, l // 4
        mods = mods_all[l]
        if kind == 0:
            h = _modrows(x, norm1[l], mods, 0, 1, rows=M_TOK, row0=0, width=D)
            x = _pool(h, x, pool_w[li], pool_scale[li], mods, 2)
        elif kind == 1:
            qkv = _linear(x, na_qkv[li], tn=D, norm_g=norm1[l], mods=mods, sh_c=0, sc_c=1, planes=True,
                          name="na_qkv")
            outs["na_k"] = qkv[1, :M_CTX].reshape(N_CTX_SEQ, 1, T_CTX, 16, 64)
            outs["na_v"] = qkv[2, :M_CTX].reshape(N_CTX_SEQ, 1, T_CTX, 16, 64)
            seq_spec = lambda p: pl.BlockSpec((None, T_CTX, D), lambda b, p=p: (p, b, 0))
            o_ctx = _attention(
                "na", grid=(N_CTX_SEQ,), q=(qkv, seq_spec(0)), ks=[(qkv, seq_spec(1))], vs=[(qkv, seq_spec(2))],
                rows=M_CTX, out_spec=pl.BlockSpec((T_CTX, D), lambda b: (b, 0)), hb=D // LANE, scale=0.125,
                name="na_ctx")
            bias = _na_bias_table(na_rpb[li])
            o_lat = _na_lat(qkv, cache_na_k[:, li].reshape(N_LAT_SEQ, PAST, D),
                            cache_na_v[:, li].reshape(N_LAT_SEQ, PAST, D), bias)
            x = _linear((o_ctx, o_lat), na_out[li], tn=D, resid=x, mods=mods, gate_c=2, name="na_out")
        elif kind == 2:
            lam_init = 0.8 - 0.6 * math.exp(-0.3 * l)
            lp = diff_lambda[li].astype(F32)
            lam = jnp.exp(jnp.sum(lp[0] * lp[1])) - jnp.exp(jnp.sum(lp[2] * lp[3])) + lam_init
            lam_row = jnp.full((1, LANE), lam, F32)
            subln = diff_subln[li].reshape(1, LANE)
            extra = [(lam_row, pl.BlockSpec((1, LANE), lambda *_: (0, 0))),
                     (subln, pl.BlockSpec((1, LANE), lambda *_: (0, 0)))]
            qkv = _linear(x, diff_qkv[li], tn=D, norm_g=norm1[l], mods=mods, sh_c=0, sc_c=1, planes=True,
                          name="diff_qkv")
            outs["diff_k"] = qkv[1, :M_CTX].reshape(N_CTX_SEQ, 1, T_CTX, 8, 2, 64)
            outs["diff_v"] = qkv[2, :M_CTX].reshape(N_CTX_SEQ, 1, T_CTX, 8, 128)
            seq_spec = lambda p: pl.BlockSpec((None, T_CTX, D), lambda b, p=p: (p, b, 0))
            o_ctx = _attention(
                "diff", grid=(N_CTX_SEQ,), q=(qkv, seq_spec(0)), ks=[(qkv, seq_spec(1))], vs=[(qkv, seq_spec(2))],
                extra=extra, rows=M_CTX, out_spec=pl.BlockSpec((T_CTX, D), lambda b: (b, 0)), hb=D // LANE,
                scale=0.125, lam_init=lam_init, name="diff_ctx")
            rc, rs = _rope_tables(LANE, 0, LANE)
            qk_r = _rope(qkv, rc, rs, planes=2, row0=M_CTX, width=LANE)
            tq = 256
            nq = T_LAT // tq
            ck = cache_diff_k[:, li].reshape(N_LAT_SEQ, PAST, D)
            cv = cache_diff_v[:, li].reshape(N_LAT_SEQ, PAST, D)
            o_lat = _attention(
                "diff", grid=(N_LAT_SEQ, D // LANE, nq),
                q=(qk_r, pl.BlockSpec((None, tq, LANE), lambda b, h, i: (0, b * nq + i, h))),
                ks=[(ck, pl.BlockSpec((None, PAST, LANE), lambda b, h, i: (b, 0, h))),
                    (qk_r, pl.BlockSpec((None, T_LAT, LANE), lambda b, h, i: (1, b, h)))],
                vs=[(cv, pl.BlockSpec((None, PAST, LANE), lambda b, h, i: (b, 0, h))),
                    (qkv, pl.BlockSpec((None, T_LAT, LANE), lambda b, h, i: (2, M_CTX // T_LAT + b, h)))],
                extra=extra,
                rows=M_LAT, out_spec=pl.BlockSpec((tq, LANE), lambda b, h, i: (b * nq + i, h)),
                hb=1, scale=0.125, lam_init=lam_init, name="diff_latent")
            x = _linear((o_ctx, o_lat), diff_out[li], tn=D, resid=x, mods=mods, gate_c=2, name="diff_out")
        else:
            hd = MLA_NOPE + MLA_ROPE
            wqb = mla_qb[li].reshape(MLA_Q_LORA, MLA_HEADS, hd)
            wqb = jnp.pad(wqb, ((0, 0), (0, 0), (0, 256 - hd))).reshape(MLA_Q_LORA, MLA_HEADS * 256)
            w_a = jnp.concatenate(
                [mla_qa[li], jnp.pad(mla_kva[li], ((0, 0), (0, 512 - MLA_KV_LORA - MLA_ROPE)))], axis=1)
            a = _linear(x, w_a, tn=D, norm_g=norm1[l], mods=mods, sh_c=0, sc_c=1, name="mla_a")
            q = _linear(a, wqb, tn=D, k=MLA_Q_LORA, x_col=0, norm_g=mla_qnorm[li], name="mla_q")
            zero_mod = jnp.zeros((1, 2, 1, MLA_KV_LORA), F32)
            ckv = _modrows(a, mla_kvnorm[li], zero_mod, 0, 1, rows=M_TOK, row0=0, width=MLA_KV_LORA, col_blk=2,
                           grouped=False)
            outs["mla_ckv"] = ckv[:M_CTX].reshape(N_CTX_SEQ, 1, T_CTX, MLA_KV_LORA)
            outs["mla_kpe"] = a[:M_CTX, 768:768 + MLA_ROPE].reshape(N_CTX_SEQ, 1, T_CTX, MLA_ROPE)
            ckv_all = jnp.concatenate([ckv, cache_mla_ckv[:, li].reshape(N_LAT_SEQ * PAST, MLA_KV_LORA)], axis=0)
            kvb = _linear(ckv_all, mla_kvb[li], tn=D, tm=512, name="mla_kvb")
            scale = float(hd) ** -0.5
            o_ctx = _attention(
                "mla", grid=(N_CTX_SEQ,),
                q=(q, pl.BlockSpec((T_CTX, 2048), lambda b: (b, 0))),
                ks=[(kvb, pl.BlockSpec((T_CTX, 2048), lambda b: (b, 0)))],
                kpes=[(a, pl.BlockSpec((T_CTX, LANE), lambda b: (b, 6)))],
                vs=None, rows=M_CTX, out_spec=pl.BlockSpec((T_CTX, D), lambda b: (b, 0)), hb=MLA_HEADS,
                scale=scale, name="mla_ctx")
            rc, rs = _rope_tables(256, 128, 192)
            q_r = _rope(q, rc, rs, planes=1, row0=M_CTX, width=256, ncol=MLA_HEADS)[0]
            rc1, rs1 = _rope_tables(LANE, 0, 64)
            kpe_r = _rope(a, rc1, rs1, planes=1, row0=M_CTX, width=LANE, col0=6)[0]
            kpe_c = jnp.pad(cache_mla_kpe[:, li], ((0, 0), (0, 0), (0, LANE - MLA_ROPE)))
            tq = 256
            nq = T_LAT // tq
            o_lat = _attention(
                "mla", grid=(N_LAT_SEQ, MLA_HEADS, nq),
                q=(q_r, pl.BlockSpec((tq, 256), lambda b, h, i: (b * nq + i, h))),
                ks=[(kvb, pl.BlockSpec((PAST, 256), lambda b, h, i: (M_TOK // PAST + b, h))),
                    (kvb, pl.BlockSpec((T_LAT, 256), lambda b, h, i: (M_CTX // T_LAT + b, h)))],
                kpes=[(kpe_c, pl.BlockSpec((None, PAST, LANE), lambda b, h, i: (b, 0, 0))),
                      (kpe_r, pl.BlockSpec((T_LAT, LANE), lambda b, h, i: (b, 0)))],
                vs=None, rows=M_LAT,
                out_spec=pl.BlockSpec((tq, LANE), lambda b, h, i: (b * nq + i, h)),
                hb=1, scale=scale, name="mla_latent")
            x = _linear((o_ctx, o_lat), mla_out[li], tn=D, resid=x, mods=mods, gate_c=2, name="mla_out")

        f = l // 2
        if l % 2 == 0:
            x = _ffn_dense(x, ffn_gate[f], ffn_up[f], ffn_down[f], norm2[l], mods, 3, 4, 5)
        else:
            x = _moe(x, norm2[l], mods, 3, 4, 5, moe_router[f], moe_gate[f], moe_up[f], moe_down[f])

    zero_mod = jnp.zeros((1, 2, 1, D), F32)
    y_prompt = _modrows(x, final_norm, zero_mod, 0, 1, rows=M_CTX, row0=0, width=D, grouped=False)
    y_sample = _modrows(x, final_norm, zero_mod, 0, 1, rows=M_LAT, row0=M_CTX, width=D, grouped=False)
    return (y_prompt.reshape(N_CTX_SEQ, T_CTX, D), y_sample.reshape(N_LAT_SEQ, T_LAT, D),
            outs["na_k"], outs["na_v"], outs["diff_k"], outs["diff_v"], outs["mla_ckv"], outs["mla_kpe"])
```

```python
import functools
import math

import numpy as np
import jax
import jax.numpy as jnp
from jax import lax
from jax.experimental import pallas as pl
from jax.experimental.pallas import tpu as pltpu

F32 = jnp.float32
BF16 = jnp.bfloat16

D = 1024
N_CTX_SEQ = 32
T_CTX = 256
N_LAT_SEQ = 2
T_LAT = 2048
PAST = 512
M_CTX = N_CTX_SEQ * T_CTX
M_LAT = N_LAT_SEQ * T_LAT
M_TOK = M_CTX + M_LAT
GRID_W = 64
NORM_EPS = 1e-6
NEG_INF = -1e30
ROPE_BASE = 10000.0
POOL_WINDOWS = (2, 4, 8, 16)
NA_WIN_R = 8
NA_WIN_C = 16
DIFF_SUBLN_EPS = 1e-5
MLA_HEADS = 8
MLA_NOPE = 128
MLA_ROPE = 64
MLA_KV_LORA = 256
MLA_Q_LORA = 512
FFN_HIDDEN = 2816
N_EXPERTS = 8

LANE = 128
TM = 1024
TH = 256
SUB = 256
VMEM_LIMIT = 56 * 1024 * 1024

MOE_ROWS = 2 * M_TOK + N_EXPERTS * TM
MOE_TILES = MOE_ROWS // TM


def _cparams(sem):
    return pltpu.CompilerParams(dimension_semantics=sem, vmem_limit_bytes=VMEM_LIMIT)


def _group_of_tile(i, tm):
    n_ctx = M_CTX // tm
    return jnp.where(i < n_ctx, 0, 1 + (i - n_ctx) // (T_LAT // tm))


def _dot(a, b):
    return jnp.dot(a, b, preferred_element_type=F32)


def _dot_nt(a, b):
    return lax.dot_general(a, b, (((1,), (1,)), ((), ())), preferred_element_type=F32)


def _rms(x, g, eps):
    return x * lax.rsqrt(jnp.mean(x * x, axis=-1, keepdims=True) + eps) * g


def _ada_kernel(c_ref, w_ref, b_ref, o_ref):
    c = c_ref[...]
    s = (c * jax.nn.sigmoid(c)).astype(BF16)
    o_ref[...] = _dot(s, w_ref[...].astype(BF16)) + b_ref[...]


def _ada(cond8, ada_w, ada_b):
    depth = ada_w.shape[0]
    tn = 1536
    return pl.pallas_call(
        _ada_kernel,
        grid=(depth, 6 * D // tn),
        in_specs=[
            pl.BlockSpec((8, D), lambda l, j: (0, 0)),
            pl.BlockSpec((None, D, tn), lambda l, j: (l, 0, j)),
            pl.BlockSpec((None, 1, tn), lambda l, j: (l, 0, j)),
        ],
        out_specs=pl.BlockSpec((None, 8, tn), lambda l, j: (l, 0, j)),
        out_shape=jax.ShapeDtypeStruct((depth, 8, 6 * D), F32),
        compiler_params=_cparams(("parallel", "parallel")),
        name="ada",
    )(cond8, ada_w, ada_b.reshape(depth, 1, 6 * D))


def _modrows_kernel(x_ref, g_ref, sh_ref, sc_ref, o_ref, *, eps):
    x = _rms(x_ref[...], g_ref[...], eps)
    o_ref[...] = x * (1.0 + sc_ref[...]) + sh_ref[...]


def _modrows(x, g, mods, sh_c, sc_c, *, rows, row0, width, col_blk=0, tm=TM, eps=NORM_EPS,
             grouped=True):
    r0 = row0 // tm
    if grouped:
        grp = lambda i: _group_of_tile(i + r0, tm)
    else:
        grp = lambda i: 0
    return pl.pallas_call(
        functools.partial(_modrows_kernel, eps=eps),
        grid=(rows // tm,),
        in_specs=[
            pl.BlockSpec((tm, width), lambda i: (i + r0, col_blk)),
            pl.BlockSpec((1, width), lambda i: (0, 0)),
            pl.BlockSpec((None, None, 1, width), lambda i: (grp(i), sh_c, 0, 0)),
            pl.BlockSpec((None, None, 1, width), lambda i: (grp(i), sc_c, 0, 0)),
        ],
        out_specs=pl.BlockSpec((tm, width), lambda i: (i, 0)),
        out_shape=jax.ShapeDtypeStruct((rows, width), F32),
        compiler_params=_cparams(("parallel",)),
        name="modrows",
    )(x, g.reshape(1, width), mods, mods)


def _linear_kernel(*refs, prologue, eps, epilogue, split, tm):
    it = iter(refs)
    x_ref = next(it)
    x_lat_ref = next(it) if split else None
    w_ref = next(it)
    g_ref = next(it) if prologue != "none" else None
    sh_ref = next(it) if prologue == "mod" else None
    sc_ref = next(it) if prologue == "mod" else None
    res_ref = next(it) if epilogue == "resid" else None
    gate_ref = next(it) if epilogue == "resid" else None
    o_ref = next(it)
    xb_ref = next(it)

    def load(src_ref):
        x = src_ref[...]
        if prologue != "none":
            x = _rms(x, g_ref[...], eps)
        if prologue == "mod":
            x = x * (1.0 + sc_ref[...]) + sh_ref[...]
        xb_ref[...] = x.astype(BF16)

    first = pl.program_id(1) == 0
    if split:
        is_ctx = pl.program_id(0) < M_CTX // tm
        pl.when(first & is_ctx)(lambda: load(x_ref))
        pl.when(first & jnp.logical_not(is_ctx))(lambda: load(x_lat_ref))
    else:
        pl.when(first)(lambda: load(x_ref))

    acc = _dot(xb_ref[...], w_ref[...].astype(BF16))
    if epilogue == "resid":
        acc = res_ref[...] + gate_ref[...] * acc
    o_ref[...] = acc


def _linear(x, w, *, tn, k=None, x_col=0, norm_g=None, eps=NORM_EPS, mods=None, sh_c=None, sc_c=None,
            resid=None, gate_c=None, planes=False, tm=TM, name="linear"):
    split = isinstance(x, tuple)
    n = w.shape[1]
    prologue = "none" if norm_g is None else ("mod" if sh_c is not None else "norm")
    epilogue = "none" if resid is None else "resid"
    grp = lambda i: _group_of_tile(i, tm)
    if split:
        x_ctx, x_lat = x
        m = x_ctx.shape[0] + x_lat.shape[0]
        k = x_ctx.shape[1]
        n_ctx = x_ctx.shape[0] // tm
        args = [x_ctx, x_lat]
        in_specs = [
            pl.BlockSpec((tm, k), lambda i, j: (jnp.minimum(i, n_ctx - 1), 0)),
            pl.BlockSpec((tm, k), lambda i, j: (jnp.maximum(i - n_ctx, 0), 0)),
        ]
    else:
        m = x.shape[0]
        k = x.shape[1] if k is None else k
        args = [x]
        in_specs = [pl.BlockSpec((tm, k), lambda i, j: (i, x_col))]
    args.append(w)
    in_specs.append(pl.BlockSpec((k, tn), lambda i, j: (0, j)))
    if prologue != "none":
        args.append(norm_g.reshape(1, k))
        in_specs.append(pl.BlockSpec((1, k), lambda i, j: (0, 0)))
    if prologue == "mod":
        args += [mods, mods]
        in_specs += [
            pl.BlockSpec((None, None, 1, k), lambda i, j: (grp(i), sh_c, 0, 0)),
            pl.BlockSpec((None, None, 1, k), lambda i, j: (grp(i), sc_c, 0, 0)),
        ]
    if epilogue == "resid":
        args += [resid, mods]
        in_specs += [
            pl.BlockSpec((tm, tn), lambda i, j: (i, j)),
            pl.BlockSpec((None, None, 1, tn), lambda i, j: (grp(i), gate_c, 0, j)),
        ]
    if planes:
        out_spec = pl.BlockSpec((None, tm, tn), lambda i, j: (j, i, 0))
        out_shape = jax.ShapeDtypeStruct((n // tn, m, tn), F32)
    else:
        out_spec = pl.BlockSpec((tm, tn), lambda i, j: (i, j))
        out_shape = jax.ShapeDtypeStruct((m, n), F32)
    return pl.pallas_call(
        functools.partial(_linear_kernel, prologue=prologue, eps=eps, epilogue=epilogue, split=split, tm=tm),
        grid=(m // tm, n // tn),
        in_specs=in_specs,
        out_specs=out_spec,
        out_shape=out_shape,
        scratch_shapes=[pltpu.VMEM((tm, k), BF16)],
        compiler_params=_cparams(("parallel", "arbitrary")),
        name=name,
    )(*args)


POOL_ROWS = 2048


def _pool_kernel(h_ref, x_ref, w_ref, ps_ref, gate_ref, o_ref):
    i = pl.program_id(0)
    g = pl.program_id(1)
    seq = jnp.where(i < M_CTX // POOL_ROWS, T_CTX, T_LAT)
    t = lax.broadcasted_iota(jnp.int32, (POOL_ROWS, 1), 0) & (seq - 1)
    for gi, win in enumerate(POOL_WINDOWS):
        @pl.when(g == gi)
        def _(win=win):
            h = h_ref[...]
            acc = jnp.zeros_like(h)
            for d in range(-(win // 2), win - win // 2):
                sh = h if d == 0 else pltpu.roll(h, (-d) % POOL_ROWS, 0)
                ok = (t + d >= 0) & (t + d < seq)
                acc = acc + jnp.where(ok, sh, 0.0)
            cnt = jnp.minimum(t + (win - win // 2), seq) - jnp.maximum(t - win // 2, 0)
            pooled = acc / cnt.astype(F32)
            y = _dot((pooled - h).astype(BF16), w_ref[...].astype(BF16)) * ps_ref[...]
            o_ref[...] = x_ref[...] + gate_ref[...] * y


def _pool(h, x, pool_w, pool_scale, mods, gate_c):
    gw = D // len(POOL_WINDOWS)
    grp = lambda i: _group_of_tile(i, POOL_ROWS)
    return pl.pallas_call(
        _pool_kernel,
        grid=(M_TOK // POOL_ROWS, len(POOL_WINDOWS)),
        in_specs=[
            pl.BlockSpec((POOL_ROWS, gw), lambda i, g: (i, g)),
            pl.BlockSpec((POOL_ROWS, gw), lambda i, g: (i, g)),
            pl.BlockSpec((None, gw, gw), lambda i, g: (g, 0, 0)),
            pl.BlockSpec((1, gw), lambda i, g: (0, g)),
            pl.BlockSpec((None, None, 1, gw), lambda i, g: (grp(i), gate_c, 0, g)),
        ],
        out_specs=pl.BlockSpec((POOL_ROWS, gw), lambda i, g: (i, g)),
        out_shape=jax.ShapeDtypeStruct((M_TOK, D), F32),
        compiler_params=_cparams(("parallel", "parallel")),
        name="pool",
    )(h, x, pool_w, pool_scale.reshape(1, D), mods)


NJ = FFN_HIDDEN // TH
TN2 = 256
NN = D // TN2


def _swiglu_kernel(*refs, moe):
    if moe:
        te_ref, tv_ref, x_ref, wg_ref, wu_ref, wd_ref, o_ref, xb_ref, h_ref = refs
    else:
        x_ref, wg_ref, wu_ref, wd_ref, g_ref, sh_ref, sc_ref, res_ref, gate_ref, o_ref, xb_ref, h_ref = refs
    j = pl.program_id(1)
    n_valid = tv_ref[pl.program_id(0)] if moe else TM

    def per_sub_block(fn, fn_skipped=None):
        sub_blocks = [pl.ds(s * SUB, SUB) for s in range(TM // SUB)]

        def all_rows():
            for rows in sub_blocks:
                fn(rows)

        def valid_rows_only():
            for s, rows in enumerate(sub_blocks):
                pl.when(s * SUB < n_valid)(functools.partial(fn, rows))
                if fn_skipped is not None:
                    pl.when(s * SUB >= n_valid)(functools.partial(fn_skipped, rows))

        if moe:
            pl.when(n_valid == TM)(all_rows)
            pl.when(n_valid < TM)(valid_rows_only)
        else:
            all_rows()

    @pl.when(j == 0)
    def _():
        def cast(rows):
            x = x_ref[rows, :]
            if not moe:
                x = _rms(x, g_ref[...], NORM_EPS) * (1.0 + sc_ref[...]) + sh_ref[...]
            xb_ref[rows, :] = x.astype(BF16)

        per_sub_block(cast)

    @pl.when(j < NJ)
    def _():
        wg = wg_ref[...].astype(BF16)
        wu = wu_ref[...].astype(BF16)

        def up(rows):
            xb = xb_ref[rows, :]
            a = _dot(xb, wg)
            u = _dot(xb, wu)
            h_ref[j, rows, :] = (a * jax.nn.sigmoid(a) * u).astype(BF16)

        per_sub_block(up)

    @pl.when(j >= NJ)
    def _():
        wd = wd_ref[...].astype(BF16)

        def down(rows):
            hcat = jnp.concatenate([h_ref[jj, rows, :] for jj in range(NJ)], axis=1)
            y = _dot(hcat, wd)
            if moe:
                o_ref[rows, :] = y
            else:
                o_ref[rows, :] = res_ref[rows, :] + gate_ref[...] * y

        def skipped(rows):
            o_ref[rows, :] = jnp.zeros((SUB, TN2), F32)

        per_sub_block(down, skipped)


_SWIGLU_SCRATCH = [pltpu.VMEM((TM, D), BF16), pltpu.VMEM((NJ, TM, TH), BF16)]


def _ffn_dense(x, wg, wu, wd, f, norm_g, mods, sh_c, sc_c, gate_c):
    grp = lambda i: _group_of_tile(i, TM)
    jh = lambda j: jnp.minimum(j, NJ - 1)
    jn = lambda j: jnp.maximum(j - NJ, 0)
    mod_spec = lambda c: pl.BlockSpec((None, None, 1, D), lambda i, j: (grp(i), c, 0, 0))
    return pl.pallas_call(
        functools.partial(_swiglu_kernel, moe=False),
        grid=(M_TOK // TM, NJ + NN),
        in_specs=[
            pl.BlockSpec((TM, D), lambda i, j: (i, 0)),
            pl.BlockSpec((None, D, TH), lambda i, j: (f, 0, jh(j))),
            pl.BlockSpec((None, D, TH), lambda i, j: (f, 0, jh(j))),
            pl.BlockSpec((None, FFN_HIDDEN, TN2), lambda i, j: (f, 0, jn(j))),
            pl.BlockSpec((1, D), lambda i, j: (0, 0)),
            mod_spec(sh_c), mod_spec(sc_c),
            pl.BlockSpec((TM, TN2), lambda i, j: (i, jn(j))),
            pl.BlockSpec((None, None, 1, TN2), lambda i, j: (grp(i), gate_c, 0, jn(j))),
        ],
        out_specs=pl.BlockSpec((TM, TN2), lambda i, j: (i, jn(j))),
        out_shape=jax.ShapeDtypeStruct((M_TOK, D), F32),
        scratch_shapes=_SWIGLU_SCRATCH,
        compiler_params=_cparams(("parallel", "arbitrary")),
        name="ffn_dense",
    )(x, wg, wu, wd, norm_g.reshape(1, D), mods, mods, x, mods)


def _ffn_moe(xs, wg, wu, wd, f, tile_expert, tile_valid):
    jh = lambda i, j, tv: jnp.where(tv[i] > 0, jnp.minimum(j, NJ - 1), NJ - 1)
    jn = lambda i, j, tv: jnp.where(tv[i] > 0, jnp.maximum(j - NJ, 0), NN - 1)
    return pl.pallas_call(
        functools.partial(_swiglu_kernel, moe=True),
        grid_spec=pltpu.PrefetchScalarGridSpec(
            num_scalar_prefetch=2,
            grid=(MOE_TILES, NJ + NN),
            in_specs=[
                pl.BlockSpec((TM, D), lambda i, j, te, tv: (i, 0)),
                pl.BlockSpec((None, None, D, TH), lambda i, j, te, tv: (f, te[i], 0, jh(i, j, tv))),
                pl.BlockSpec((None, None, D, TH), lambda i, j, te, tv: (f, te[i], 0, jh(i, j, tv))),
                pl.BlockSpec((None, None, FFN_HIDDEN, TN2), lambda i, j, te, tv: (f, te[i], 0, jn(i, j, tv))),
            ],
            out_specs=pl.BlockSpec((TM, TN2), lambda i, j, te, tv: (i, jnp.maximum(j - NJ, 0))),
            scratch_shapes=_SWIGLU_SCRATCH,
        ),
        out_shape=jax.ShapeDtypeStruct((MOE_ROWS, D), F32),
        compiler_params=_cparams(("parallel", "arbitrary")),
        name="ffn_moe",
    )(tile_expert, tile_valid, xs, wg, wu, wd)


def _router_kernel(x_ref, g_ref, sh_ref, sc_ref, r_ref, h_ref, idx_ref, gates_ref):
    h = _rms(x_ref[...], g_ref[...], NORM_EPS) * (1.0 + sc_ref[...]) + sh_ref[...]
    h_ref[...] = h
    w = r_ref[...]
    h_hi = h.astype(BF16)
    h_lo = (h - h_hi.astype(F32)).astype(BF16)
    w_hi = w.astype(BF16)
    w_lo = (w - w_hi.astype(F32)).astype(BF16)
    logits = _dot(h_hi, w_hi) + (_dot(h_lo, w_hi) + _dot(h_hi, w_lo))
    col = lax.broadcasted_iota(jnp.int32, logits.shape, 1)
    colf = col.astype(F32)
    lg = jnp.where(col < N_EXPERTS, logits, -jnp.inf)
    m1 = jnp.max(lg, axis=-1, keepdims=True)
    i1 = jnp.min(jnp.where(lg == m1, colf, float(LANE)), axis=-1, keepdims=True)
    lg2 = jnp.where(colf == i1, -jnp.inf, lg)
    m2 = jnp.max(lg2, axis=-1, keepdims=True)
    i2 = jnp.min(jnp.where(lg2 == m2, colf, float(LANE)), axis=-1, keepdims=True)
    i1 = i1.astype(jnp.int32)
    i2 = i2.astype(jnp.int32)
    e = jnp.exp(m2 - m1)
    g1 = 1.0 / (1.0 + e)
    g2 = e / (1.0 + e)
    idx_ref[...] = jnp.where(col == 0, i1, jnp.where(col == 1, i2, 0))
    gates_ref[...] = jnp.where(col == 0, g1, jnp.where(col == 1, g2, 0.0))


def _router(x, norm_g, mods, sh_c, sc_c, router):
    r_pad = jnp.pad(router, ((0, 0), (0, LANE - N_EXPERTS)))
    grp = lambda i: _group_of_tile(i, TM)
    return pl.pallas_call(
        _router_kernel,
        grid=(M_TOK // TM,),
        in_specs=[
            pl.BlockSpec((TM, D), lambda i: (i, 0)),
            pl.BlockSpec((1, D), lambda i: (0, 0)),
            pl.BlockSpec((None, None, 1, D), lambda i: (grp(i), sh_c, 0, 0)),
            pl.BlockSpec((None, None, 1, D), lambda i: (grp(i), sc_c, 0, 0)),
            pl.BlockSpec((D, LANE), lambda i: (0, 0)),
        ],
        out_specs=[
            pl.BlockSpec((TM, D), lambda i: (i, 0)),
            pl.BlockSpec((TM, LANE), lambda i: (i, 0)),
            pl.BlockSpec((TM, LANE), lambda i: (i, 0)),
        ],
        out_shape=[
            jax.ShapeDtypeStruct((M_TOK, D), F32),
            jax.ShapeDtypeStruct((M_TOK, LANE), jnp.int32),
            jax.ShapeDtypeStruct((M_TOK, LANE), F32),
        ],
        compiler_params=_cparams(("parallel",)),
        name="router",
    )(x, norm_g.reshape(1, D), mods, mods, r_pad)


def _route_plan(e1, e2):
    e = jnp.concatenate([e1, e2])
    oh = (e[:, None] == jnp.arange(N_EXPERTS, dtype=jnp.int32)[None, :]).astype(jnp.int32)
    csum = jnp.cumsum(oh, axis=0)
    rank = jnp.sum((csum - oh) * oh, axis=1)
    counts = csum[-1]
    padded = ((counts + TM - 1) // TM) * TM
    ends = jnp.cumsum(padded)
    starts = ends - padded
    pos = jnp.sum(oh * starts[None, :], axis=1) + rank
    tile_start = jnp.arange(MOE_TILES, dtype=jnp.int32) * TM
    te = jnp.minimum(jnp.sum((tile_start[:, None] >= ends[None, :]).astype(jnp.int32), axis=1),
                     N_EXPERTS - 1)
    used = tile_start < ends[-1]
    valid = jnp.where(used, jnp.clip(counts[te] - (tile_start - starts[te]), 0, TM), 0)
    last_e = te[jnp.maximum(ends[-1] // TM - 1, 0)]
    te = jnp.where(used, te, last_e)
    sub_off = jnp.arange(TM // SUB, dtype=jnp.int32) * SUB
    sub_valid = jnp.clip(valid[:, None] - sub_off[None, :], 0, SUB).reshape(-1)
    return pos.astype(jnp.int32), te.astype(jnp.int32), valid.astype(jnp.int32), sub_valid.astype(jnp.int32)


DMA_UNROLL = 8


def _dispatch_kernel(pos_ref, sv_ref, h_ref, xs_hbm, zero_ref, sem, zsem):
    i = pl.program_id(0)

    @pl.when(i == 0)
    def _():
        zero_ref[...] = jnp.zeros_like(zero_ref)

        def zero_copy(b):
            return pltpu.make_async_copy(zero_ref, xs_hbm.at[pl.ds(pl.multiple_of(b * SUB, SUB), SUB)], zsem)

        def start(b, c):
            pl.when(sv_ref[b] < SUB)(lambda: zero_copy(b).start())
            return c

        def wait(b, c):
            pl.when(sv_ref[b] < SUB)(lambda: zero_copy(b).wait())
            return c

        lax.fori_loop(0, MOE_ROWS // SUB, start, 0)
        lax.fori_loop(0, MOE_ROWS // SUB, wait, 0)

    def issue(r, c):
        p1 = pos_ref[i * SUB + r]
        p2 = pos_ref[M_TOK + i * SUB + r]
        pltpu.make_async_copy(h_ref.at[pl.ds(r, 1)], xs_hbm.at[pl.ds(p1, 1)], sem).start()
        pltpu.make_async_copy(h_ref.at[pl.ds(r, 1)], xs_hbm.at[pl.ds(p2, 1)], sem).start()
        return c

    lax.fori_loop(0, SUB, issue, 0, unroll=DMA_UNROLL)
    pltpu.make_async_copy(h_ref, xs_hbm.at[pl.ds(0, SUB)], sem).wait()
    pltpu.make_async_copy(h_ref, xs_hbm.at[pl.ds(0, SUB)], sem).wait()


def _moe_dispatch(h, pos, sub_valid):
    return pl.pallas_call(
        _dispatch_kernel,
        grid_spec=pltpu.PrefetchScalarGridSpec(
            num_scalar_prefetch=2,
            grid=(M_TOK // SUB,),
            in_specs=[pl.BlockSpec((SUB, D), lambda i, p, z: (i, 0))],
            out_specs=pl.BlockSpec(memory_space=pl.ANY),
            scratch_shapes=[pltpu.VMEM((SUB, D), F32), pltpu.SemaphoreType.DMA, pltpu.SemaphoreType.DMA],
        ),
        out_shape=jax.ShapeDtypeStruct((MOE_ROWS, D), F32),
        compiler_params=_cparams(("arbitrary",)),
        name="moe_dispatch",
    )(pos, sub_valid, h)


def _combine_kernel(pos_ref, y_hbm, x_ref, tg_ref, gate_ref, o_ref, y1_ref, y2_ref, sem):
    i = pl.program_id(0)

    def issue(r, c):
        p1 = pos_ref[i * SUB + r]
        p2 = pos_ref[M_TOK + i * SUB + r]
        pltpu.make_async_copy(y_hbm.at[pl.ds(p1, 1)], y1_ref.at[pl.ds(r, 1)], sem.at[0]).start()
        pltpu.make_async_copy(y_hbm.at[pl.ds(p2, 1)], y2_ref.at[pl.ds(r, 1)], sem.at[1]).start()
        return c

    lax.fori_loop(0, SUB, issue, 0, unroll=DMA_UNROLL)
    pltpu.make_async_copy(y_hbm.at[pl.ds(0, SUB)], y1_ref, sem.at[0]).wait()
    pltpu.make_async_copy(y_hbm.at[pl.ds(0, SUB)], y2_ref, sem.at[1]).wait()
    tg = tg_ref[...]
    moe_out = tg[:, 0:1] * y1_ref[...] + tg[:, 1:2] * y2_ref[...]
    o_ref[...] = x_ref[...] + gate_ref[...] * moe_out


def _moe_combine(y, pos, x, top_gates, mods, gate_c):
    grp = lambda i: _group_of_tile(i, SUB)
    return pl.pallas_call(
        _combine_kernel,
        grid_spec=pltpu.PrefetchScalarGridSpec(
            num_scalar_prefetch=1,
            grid=(M_TOK // SUB,),
            in_specs=[
                pl.BlockSpec(memory_space=pl.ANY),
                pl.BlockSpec((SUB, D), lambda i, p: (i, 0)),
                pl.BlockSpec((SUB, LANE), lambda i, p: (i, 0)),
                pl.BlockSpec((None, None, 1, D), lambda i, p: (grp(i), gate_c, 0, 0)),
            ],
            out_specs=pl.BlockSpec((SUB, D), lambda i, p: (i, 0)),
            scratch_shapes=[pltpu.VMEM((SUB, D), F32), pltpu.VMEM((SUB, D), F32),
                            pltpu.SemaphoreType.DMA((2,))],
        ),
        out_shape=jax.ShapeDtypeStruct((M_TOK, D), F32),
        compiler_params=_cparams(("arbitrary",)),
        name="moe_combine",
    )(pos, y, x, top_gates, mods)


def _moe(x, norm_g, mods, sh_c, sc_c, gate_c, router, wg, wu, wd, f):
    h, idx, gates = _router(x, norm_g, mods, sh_c, sc_c, router)
    pos, te, tv, sub_valid = _route_plan(idx[:, 0], idx[:, 1])
    xs = _moe_dispatch(h, pos, sub_valid)
    ys = _ffn_moe(xs, wg, wu, wd, f, te, tv)
    return _moe_combine(ys, pos, x, gates, mods, gate_c)


def _rope_tables(width, lo, hi):
    pos = jnp.arange(T_LAT)
    rows = (pos // GRID_W).astype(F32)
    cols = (pos % GRID_W).astype(F32)
    n_freq = 16
    inv = ROPE_BASE ** (-jnp.arange(n_freq, dtype=F32) / n_freq)
    ang = jnp.concatenate([rows[:, None] * inv[None, :], cols[:, None] * inv[None, :]], axis=-1)
    cos, sin = jnp.cos(ang), jnp.sin(ang)
    lane = np.arange(width)
    active = (lane >= lo) & (lane < hi)
    reps = width // 64
    c = jnp.tile(jnp.concatenate([cos, cos], axis=-1), (1, reps))
    s = jnp.tile(jnp.concatenate([-sin, sin], axis=-1), (1, reps))
    return jnp.where(active[None, :], c, 1.0), jnp.where(active[None, :], s, 0.0)


def _rope_kernel(x_ref, c_ref, s_ref, o_ref):
    x = x_ref[...]
    w = x.shape[-1]
    lane = lax.broadcasted_iota(jnp.int32, (1, w), 1)
    first = (lane & 63) < 32
    partner = jnp.where(first, pltpu.roll(x, w - 32, 1), pltpu.roll(x, 32, 1))
    o_ref[...] = x * c_ref[...] + partner * s_ref[...]


def _rope(x, c, s, *, planes, row0, width, col0=0, ncol=1, tr=512):
    r0 = row0 // tr
    nt = T_LAT // tr
    if x.ndim == 3:
        ncol = x.shape[2] // width
        x_spec = pl.BlockSpec((None, tr, width), lambda p, i, j: (p, i + r0, j))
    else:
        x_spec = pl.BlockSpec((tr, width), lambda p, i, j: (i + r0, col0 + j))
    return pl.pallas_call(
        _rope_kernel,
        grid=(planes, M_LAT // tr, ncol),
        in_specs=[
            x_spec,
            pl.BlockSpec((tr, width), lambda p, i, j: (i % nt, 0)),
            pl.BlockSpec((tr, width), lambda p, i, j: (i % nt, 0)),
        ],
        out_specs=pl.BlockSpec((None, tr, width), lambda p, i, j: (p, i, j)),
        out_shape=jax.ShapeDtypeStruct((planes, M_LAT, ncol * width), F32),
        compiler_params=_cparams(("parallel", "parallel", "parallel")),
        name="rope",
    )(x, c, s)


def _attn_kernel(*refs, kind, hb, nseg, scale, lam_init):
    it = iter(refs)
    q_ref = next(it)
    k_refs = [next(it) for _ in range(nseg)]
    kpe_refs = [next(it) for _ in range(nseg)] if kind == "mla" else None
    v_refs = [next(it) for _ in range(nseg)] if kind != "mla" else k_refs
    lam_ref = next(it) if kind == "diff" else None
    subln_ref = next(it) if kind == "diff" else None
    o_ref = next(it)

    lane = lax.broadcasted_iota(jnp.int32, (1, LANE), 1)
    lo_half = lane < 64

    def softmax_parts(s_list):
        m = functools.reduce(jnp.maximum, [jnp.max(s, axis=-1, keepdims=True) for s in s_list])
        p_list = [jnp.exp(s - m) for s in s_list]
        l = functools.reduce(jnp.add, [jnp.sum(p, axis=-1, keepdims=True) for p in p_list])
        return p_list, 1.0 / l

    for hh in range(hb):
        if kind == "mla":
            qn = q_ref[:, hh * 256:hh * 256 + 128].astype(BF16)
            qp = q_ref[:, hh * 256 + 128:hh * 256 + 256].astype(BF16)
            s_list = []
            for kr, pr in zip(k_refs, kpe_refs):
                kn = kr[:, hh * 256:hh * 256 + 128].astype(BF16)
                s_list.append((_dot_nt(qn, kn) + _dot_nt(qp, pr[...].astype(BF16))) * scale)
            p_list, inv_l = softmax_parts(s_list)
            o = functools.reduce(jnp.add, [
                _dot(p.astype(BF16), vr[:, hh * 256 + 128:hh * 256 + 256].astype(BF16))
                for p, vr in zip(p_list, v_refs)])
            o_ref[:, hh * LANE:(hh + 1) * LANE] = o * inv_l
            continue

        sl = slice(hh * LANE, (hh + 1) * LANE)
        q2 = q_ref[:, sl]
        k2 = [kr[:, sl].astype(BF16) for kr in k_refs]
        v2 = [vr[:, sl].astype(BF16) for vr in v_refs]
        parts = []
        for n in range(2):
            qm = jnp.where(lo_half if n == 0 else jnp.logical_not(lo_half), q2, 0.0).astype(BF16)
            parts.append(softmax_parts([_dot_nt(qm, kk) * scale for kk in k2]))
        if kind == "na":
            outs = []
            for p_list, inv_l in parts:
                o = functools.reduce(jnp.add, [_dot(p.astype(BF16), vv) for p, vv in zip(p_list, v2)])
                outs.append(o * inv_l)
            o_ref[:, sl] = jnp.where(lo_half, outs[0], outs[1])
        else:
            lam = lam_ref[...][:, :1]
            (p0, il0), (p1, il1) = parts
            o = functools.reduce(jnp.add, [
                _dot((a * il0 - lam * (b * il1)).astype(BF16), vv) for a, b, vv in zip(p0, p1, v2)])
            o = _rms(o, subln_ref[...], DIFF_SUBLN_EPS) * (1.0 - lam_init)
            o_ref[:, sl] = o


def _attention(kind, *, grid, q, ks, vs, kpes=None, extra=(), rows, out_spec, hb, scale, lam_init=0.0,
               name="attn"):
    ops = [q] + list(ks) + (list(kpes) if kind == "mla" else []) + (list(vs) if kind != "mla" else [])
    ops += list(extra)
    return pl.pallas_call(
        functools.partial(_attn_kernel, kind=kind, hb=hb, nseg=len(ks), scale=scale, lam_init=lam_init),
        grid=grid,
        in_specs=[s for _, s in ops],
        out_specs=out_spec,
        out_shape=jax.ShapeDtypeStruct((rows, D), F32),
        compiler_params=_cparams(("parallel",) * len(grid)),
        name=name,
    )(*[a for a, _ in ops])


NA_QROWS = 4
NA_KROWS = 12
NA_TQ = NA_QROWS * GRID_W
NA_TK = NA_KROWS * GRID_W


def _na_bias_table(rpb):
    h = rpb.shape[0]
    qc = np.arange(GRID_W)[:, None]
    kc = np.arange(GRID_W)[None, :]
    qs = np.clip(qc - NA_WIN_C // 2, 0, GRID_W - NA_WIN_C)
    col_ok = (kc >= qs) & (kc < qs + NA_WIN_C)
    ic = np.clip(kc - qc + NA_WIN_C - 1, 0, 2 * NA_WIN_C - 2)
    tab = jnp.take(rpb.astype(F32), jnp.asarray(ic.reshape(-1)), axis=2)
    tab = tab.reshape(h, 2 * NA_WIN_R - 1, GRID_W, GRID_W)
    n_rows = T_LAT // GRID_W
    ir_idx = np.zeros((3, NA_QROWS, NA_KROWS), np.int32)
    row_ok = np.zeros((3, NA_QROWS, NA_KROWS), bool)
    for pat, g in enumerate((0, 1, n_rows // NA_QROWS - 1)):
        base = int(np.clip(NA_QROWS * g - NA_QROWS, 0, n_rows - NA_KROWS))
        for qr in range(NA_QROWS):
            r = NA_QROWS * g + qr
            r0 = int(np.clip(r - NA_WIN_R // 2, 0, n_rows - NA_WIN_R))
            for kr in range(NA_KROWS):
                ok = r0 <= base + kr < r0 + NA_WIN_R
                row_ok[pat, qr, kr] = ok
                ir_idx[pat, qr, kr] = (base + kr - r + NA_WIN_R - 1) if ok else 0
    b = jnp.take(tab, jnp.asarray(ir_idx.reshape(-1)), axis=1)
    b = b.reshape(h, 3, NA_QROWS, NA_KROWS, GRID_W, GRID_W)
    ok = row_ok[None, :, :, :, None, None] & col_ok[None, None, None, None, :, :]
    b = jnp.where(jnp.asarray(ok), b, NEG_INF)
    b = jnp.transpose(b, (1, 0, 2, 4, 3, 5))
    return b.reshape(3, h, NA_TQ, NA_TK)


def _na_lat_kernel(q_ref, kc_ref, vc_ref, k_ref, v_ref, b_ref, o_ref, *, scale):
    g = pl.program_id(2)
    n_rows = T_LAT // GRID_W
    base = jnp.clip(NA_QROWS * g - NA_QROWS, 0, n_rows - NA_KROWS) * GRID_W
    base = pl.multiple_of(base, GRID_W)
    lane = lax.broadcasted_iota(jnp.int32, (1, LANE), 1)
    lo_half = lane < 64
    q2 = q_ref[...]
    kc = kc_ref[...].astype(BF16)
    vc = vc_ref[...].astype(BF16)
    kl = k_ref[pl.ds(base, NA_TK), :].astype(BF16)
    vl = v_ref[pl.ds(base, NA_TK), :].astype(BF16)
    outs = []
    for n in range(2):
        qm = jnp.where(lo_half if n == 0 else jnp.logical_not(lo_half), q2, 0.0).astype(BF16)
        s_c = _dot_nt(qm, kc) * scale
        s_l = _dot_nt(qm, kl) * scale + b_ref[n]
        m = jnp.maximum(jnp.max(s_c, axis=-1, keepdims=True), jnp.max(s_l, axis=-1, keepdims=True))
        p_c = jnp.exp(s_c - m)
        p_l = jnp.exp(s_l - m)
        l = jnp.sum(p_c, axis=-1, keepdims=True) + jnp.sum(p_l, axis=-1, keepdims=True)
        o = _dot(p_c.astype(BF16), vc) + _dot(p_l.astype(BF16), vl)
        outs.append(o * (1.0 / l))
    o_ref[...] = jnp.where(lo_half, outs[0], outs[1])


def _na_lat(qkv, cache_k, cache_v, bias):
    n_g = T_LAT // NA_TQ
    q_blk0 = M_CTX // NA_TQ
    kv_blk0 = M_CTX // T_LAT
    pat = lambda g: jnp.where(g == 0, 0, jnp.where(g == n_g - 1, 2, 1))
    return pl.pallas_call(
        functools.partial(_na_lat_kernel, scale=0.125),
        grid=(N_LAT_SEQ, D // LANE, n_g),
        in_specs=[
            pl.BlockSpec((None, NA_TQ, LANE), lambda b, h, g: (0, q_blk0 + b * n_g + g, h)),
            pl.BlockSpec((None, PAST, LANE), lambda b, h, g: (b, 0, h)),
            pl.BlockSpec((None, PAST, LANE), lambda b, h, g: (b, 0, h)),
            pl.BlockSpec((None, T_LAT, LANE), lambda b, h, g: (1, kv_blk0 + b, h)),
            pl.BlockSpec((None, T_LAT, LANE), lambda b, h, g: (2, kv_blk0 + b, h)),
            pl.BlockSpec((None, 2, NA_TQ, NA_TK), lambda b, h, g: (pat(g), h, 0, 0)),
        ],
        out_specs=pl.BlockSpec((NA_TQ, LANE), lambda b, h, g: (b * n_g + g, h)),
        out_shape=jax.ShapeDtypeStruct((M_LAT, D), F32),
        compiler_params=_cparams(("parallel", "parallel", "parallel")),
        name="na_latent",
    )(qkv, cache_k, cache_v, qkv, qkv, bias)


def _seq_t_kernel(x_ref, o_ref):
    o_ref[...] = x_ref[...].T


def _seq_transpose(qkv, plane):
    return pl.pallas_call(
        _seq_t_kernel,
        grid=(N_CTX_SEQ,),
        in_specs=[pl.BlockSpec((None, T_CTX, D), lambda b: (plane, b, 0))],
        out_specs=pl.BlockSpec((None, D, T_CTX), lambda b: (b, 0, 0)),
        out_shape=jax.ShapeDtypeStruct((N_CTX_SEQ, D, T_CTX), F32),
        compiler_params=_cparams(("parallel",)),
        name="seq_transpose",
    )(qkv)


def kernel(x_prompt, x_sample, c, c_ctx, cache_na_k, cache_na_v, cache_diff_k, cache_diff_v, cache_mla_ckv, cache_mla_kpe, ada_w, ada_b, norm1, norm2, pool_w, pool_scale, na_qkv, na_rpb, na_out, diff_qkv, diff_lambda, diff_subln, diff_out, mla_qa, mla_qnorm, mla_qb, mla_kva, mla_kvnorm, mla_kvb, mla_out, ffn_gate, ffn_up, ffn_down, moe_router, moe_gate, moe_up, moe_down, final_norm):
    depth = ada_w.shape[0]
    x = jnp.concatenate([x_prompt.reshape(M_CTX, D), x_sample.reshape(M_LAT, D)], axis=0)
    cond8 = jnp.concatenate([c_ctx[None, :], c, jnp.zeros((5, D), F32)], axis=0)
    mods_all = _ada(cond8, ada_w, ada_b)[:, :3].reshape(depth, 3, 6, 1, D)
    outs = {}

    for l in range(depth):
        kind, li = l % 4, l // 4
        mods = mods_all[l]
        if kind == 0:
            h = _modrows(x, norm1[l], mods, 0, 1, rows=M_TOK, row0=0, width=D)
            x = _pool(h, x, pool_w[li], pool_scale[li], mods, 2)
        elif kind == 1:
            qkv = _linear(x, na_qkv[li], tn=D, norm_g=norm1[l], mods=mods, sh_c=0, sc_c=1, planes=True,
                          name="na_qkv")
            to_cache = lambda a: jnp.transpose(a.reshape(N_CTX_SEQ, 1, 16, 64, T_CTX), (0, 1, 4, 2, 3))
            outs["na_k"] = to_cache(_seq_transpose(qkv, 1))
            outs["na_v"] = to_cache(_seq_transpose(qkv, 2))
            seq_spec = lambda p: pl.BlockSpec((None, T_CTX, D), lambda b, p=p: (p, b, 0))
            o_ctx = _attention(
                "na", grid=(N_CTX_SEQ,), q=(qkv, seq_spec(0)), ks=[(qkv, seq_spec(1))], vs=[(qkv, seq_spec(2))],
                rows=M_CTX, out_spec=pl.BlockSpec((T_CTX, D), lambda b: (b, 0)), hb=D // LANE, scale=0.125,
                name="na_ctx")
            bias = _na_bias_table(na_rpb[li])
            o_lat = _na_lat(qkv, cache_na_k[:, li].reshape(N_LAT_SEQ, PAST, D),
                            cache_na_v[:, li].reshape(N_LAT_SEQ, PAST, D), bias)
            x = _linear((o_ctx, o_lat), na_out[li], tn=D, resid=x, mods=mods, gate_c=2, name="na_out")
        elif kind == 2:
            lam_init = 0.8 - 0.6 * math.exp(-0.3 * l)
            lp = diff_lambda[li].astype(F32)
            lam = jnp.exp(jnp.sum(lp[0] * lp[1])) - jnp.exp(jnp.sum(lp[2] * lp[3])) + lam_init
            lam_row = jnp.full((1, LANE), lam, F32)
            subln = diff_subln[li].reshape(1, LANE)
            extra = [(lam_row, pl.BlockSpec((1, LANE), lambda *_: (0, 0))),
                     (subln, pl.BlockSpec((1, LANE), lambda *_: (0, 0)))]
            qkv = _linear(x, diff_qkv[li], tn=D, norm_g=norm1[l], mods=mods, sh_c=0, sc_c=1, planes=True,
                          name="diff_qkv")
            outs["diff_k"] = jnp.transpose(
                _seq_transpose(qkv, 1).reshape(N_CTX_SEQ, 1, 8, 2, 64, T_CTX), (0, 1, 5, 2, 3, 4))
            outs["diff_v"] = qkv[2, :M_CTX].reshape(N_CTX_SEQ, 1, T_CTX, 8, 128)
            seq_spec = lambda p: pl.BlockSpec((None, T_CTX, D), lambda b, p=p: (p, b, 0))
            o_ctx = _attention(
                "diff", grid=(N_CTX_SEQ,), q=(qkv, seq_spec(0)), ks=[(qkv, seq_spec(1))], vs=[(qkv, seq_spec(2))],
                extra=extra, rows=M_CTX, out_spec=pl.BlockSpec((T_CTX, D), lambda b: (b, 0)), hb=D // LANE,
                scale=0.125, lam_init=lam_init, name="diff_ctx")
            rc, rs = _rope_tables(LANE, 0, LANE)
            qk_r = _rope(qkv, rc, rs, planes=2, row0=M_CTX, width=LANE)
            tq = 256
            nq = T_LAT // tq
            ck = cache_diff_k[:, li].reshape(N_LAT_SEQ, PAST, D)
            cv = cache_diff_v[:, li].reshape(N_LAT_SEQ, PAST, D)
            o_lat = _attention(
                "diff", grid=(N_LAT_SEQ, D // LANE, nq),
                q=(qk_r, pl.BlockSpec((None, tq, LANE), lambda b, h, i: (0, b * nq + i, h))),
                ks=[(ck, pl.BlockSpec((None, PAST, LANE), lambda b, h, i: (b, 0, h))),
                    (qk_r, pl.BlockSpec((None, T_LAT, LANE), lambda b, h, i: (1, b, h)))],
                vs=[(cv, pl.BlockSpec((None, PAST, LANE), lambda b, h, i: (b, 0, h))),
                    (qkv, pl.BlockSpec((None, T_LAT, LANE), lambda b, h, i: (2, M_CTX // T_LAT + b, h)))],
                extra=extra,
                rows=M_LAT, out_spec=pl.BlockSpec((tq, LANE), lambda b, h, i: (b * nq + i, h)),
                hb=1, scale=0.125, lam_init=lam_init, name="diff_latent")
            x = _linear((o_ctx, o_lat), diff_out[li], tn=D, resid=x, mods=mods, gate_c=2, name="diff_out")
        else:
            hd = MLA_NOPE + MLA_ROPE
            wqb = mla_qb[li].reshape(MLA_Q_LORA, MLA_HEADS, hd)
            wqb = jnp.pad(wqb, ((0, 0), (0, 0), (0, 256 - hd))).reshape(MLA_Q_LORA, MLA_HEADS * 256)
            w_a = jnp.concatenate(
                [mla_qa[li], jnp.pad(mla_kva[li], ((0, 0), (0, 512 - MLA_KV_LORA - MLA_ROPE)))], axis=1)
            a = _linear(x, w_a, tn=D, norm_g=norm1[l], mods=mods, sh_c=0, sc_c=1, name="mla_a")
            q = _linear(a, wqb, tn=D, k=MLA_Q_LORA, x_col=0, norm_g=mla_qnorm[li], name="mla_q")
            zero_mod = jnp.zeros((1, 2, 1, MLA_KV_LORA), F32)
            ckv = _modrows(a, mla_kvnorm[li], zero_mod, 0, 1, rows=M_TOK, row0=0, width=MLA_KV_LORA, col_blk=2,
                           grouped=False)
            outs["mla_ckv"] = ckv[:M_CTX].reshape(N_CTX_SEQ, 1, T_CTX, MLA_KV_LORA)
            outs["mla_kpe"] = a[:M_CTX, 768:768 + MLA_ROPE].reshape(N_CTX_SEQ, 1, T_CTX, MLA_ROPE)
            ckv_all = jnp.concatenate([ckv, cache_mla_ckv[:, li].reshape(N_LAT_SEQ * PAST, MLA_KV_LORA)], axis=0)
            kvb = _linear(ckv_all, mla_kvb[li], tn=D, tm=512, name="mla_kvb")
            scale = float(hd) ** -0.5
            o_ctx = _attention(
                "mla", grid=(N_CTX_SEQ,),
                q=(q, pl.BlockSpec((T_CTX, 2048), lambda b: (b, 0))),
                ks=[(kvb, pl.BlockSpec((T_CTX, 2048), lambda b: (b, 0)))],
                kpes=[(a, pl.BlockSpec((T_CTX, LANE), lambda b: (b, 6)))],
                vs=None, rows=M_CTX, out_spec=pl.BlockSpec((T_CTX, D), lambda b: (b, 0)), hb=MLA_HEADS,
                scale=scale, name="mla_ctx")
            rc, rs = _rope_tables(256, 128, 192)
            q_r = _rope(q, rc, rs, planes=1, row0=M_CTX, width=256, ncol=MLA_HEADS)[0]
            rc1, rs1 = _rope_tables(LANE, 0, 64)
            kpe_r = _rope(a, rc1, rs1, planes=1, row0=M_CTX, width=LANE, col0=6)[0]
            kpe_c = jnp.pad(cache_mla_kpe[:, li], ((0, 0), (0, 0), (0, LANE - MLA_ROPE)))
            tq = 256
            nq = T_LAT // tq
            o_lat = _attention(
                "mla", grid=(N_LAT_SEQ, MLA_HEADS, nq),
                q=(q_r, pl.BlockSpec((tq, 256), lambda b, h, i: (b * nq + i, h))),
                ks=[(kvb, pl.BlockSpec((PAST, 256), lambda b, h, i: (M_TOK // PAST + b, h))),
                    (kvb, pl.BlockSpec((T_LAT, 256), lambda b, h, i: (M_CTX // T_LAT + b, h)))],
                kpes=[(kpe_c, pl.BlockSpec((None, PAST, LANE), lambda b, h, i: (b, 0, 0))),
                      (kpe_r, pl.BlockSpec((T_LAT, LANE), lambda b, h, i: (b, 0)))],
                vs=None, rows=M_LAT,
                out_spec=pl.BlockSpec((tq, LANE), lambda b, h, i: (b * nq + i, h)),
                hb=1, scale=scale, name="mla_latent")
            x = _linear((o_ctx, o_lat), mla_out[li], tn=D, resid=x, mods=mods, gate_c=2, name="mla_out")

        f = l // 2
        if l % 2 == 0:
            x = _ffn_dense(x, ffn_gate, ffn_up, ffn_down, f, norm2[l], mods, 3, 4, 5)
        else:
            x = _moe(x, norm2[l], mods, 3, 4, 5, moe_router[f], moe_gate, moe_up, moe_down, f)

    zero_mod = jnp.zeros((1, 2, 1, D), F32)
    y_prompt = _modrows(x, final_norm, zero_mod, 0, 1, rows=M_CTX, row0=0, width=D, grouped=False)
    y_sample = _modrows(x, final_norm, zero_mod, 0, 1, rows=M_LAT, row0=M_CTX, width=D, grouped=False)
    return (y_prompt.reshape(N_CTX_SEQ, T_CTX, D), y_sample.reshape(N_LAT_SEQ, T_LAT, D),
            outs["na_k"], outs["na_v"], outs["diff_k"], outs["diff_v"], outs["mla_ckv"], outs["mla_kpe"])
```

```python
import functools
import math

import numpy as np
import jax
import jax.numpy as jnp
from jax import lax
from jax.experimental import pallas as pl
from jax.experimental.pallas import tpu as pltpu

F32 = jnp.float32
BF16 = jnp.bfloat16

D = 1024
N_CTX_SEQ = 32
T_CTX = 256
N_LAT_SEQ = 2
T_LAT = 2048
PAST = 512
M_CTX = N_CTX_SEQ * T_CTX
M_LAT = N_LAT_SEQ * T_LAT
M_TOK = M_CTX + M_LAT
GRID_W = 64
NORM_EPS = 1e-6
NEG_INF = -1e30
ROPE_BASE = 10000.0
POOL_WINDOWS = (2, 4, 8, 16)
NA_WIN_R = 8
NA_WIN_C = 16
DIFF_SUBLN_EPS = 1e-5
MLA_HEADS = 8
MLA_NOPE = 128
MLA_ROPE = 64
MLA_KV_LORA = 256
MLA_Q_LORA = 512
FFN_HIDDEN = 2816
N_EXPERTS = 8

LOG2_E = math.log2(math.e)
LANE = 128
TM = 1024
TH = 256
SUB = 256
VMEM_LIMIT = 56 * 1024 * 1024

TMF = 2048
MOE_ROWS = 2 * M_TOK + N_EXPERTS * TMF
MOE_TILES = MOE_ROWS // TMF


def _cparams(sem):
    return pltpu.CompilerParams(dimension_semantics=sem, vmem_limit_bytes=VMEM_LIMIT)


def _group_of_tile(i, tm):
    n_ctx = M_CTX // tm
    return jnp.where(i < n_ctx, 0, 1 + (i - n_ctx) // (T_LAT // tm))


def _dot(a, b):
    return jnp.dot(a, b, preferred_element_type=F32)


def _dot_nt(a, b):
    return lax.dot_general(a, b, (((1,), (1,)), ((), ())), preferred_element_type=F32)


def _rms(x, g, eps):
    return x * lax.rsqrt(jnp.mean(x * x, axis=-1, keepdims=True) + eps) * g


def _ada_kernel(c_ref, w_ref, b_ref, o_ref):
    c = c_ref[...]
    s = (c * jax.nn.sigmoid(c)).astype(BF16)
    o_ref[...] = _dot(s, w_ref[...].astype(BF16)) + b_ref[...]


def _ada(cond8, ada_w, ada_b):
    depth = ada_w.shape[0]
    tn = 1536
    return pl.pallas_call(
        _ada_kernel,
        grid=(depth, 6 * D // tn),
        in_specs=[
            pl.BlockSpec((8, D), lambda l, j: (0, 0)),
            pl.BlockSpec((None, D, tn), lambda l, j: (l, 0, j)),
            pl.BlockSpec((None, 1, tn), lambda l, j: (l, 0, j)),
        ],
        out_specs=pl.BlockSpec((None, 8, tn), lambda l, j: (l, 0, j)),
        out_shape=jax.ShapeDtypeStruct((depth, 8, 6 * D), F32),
        compiler_params=_cparams(("parallel", "parallel")),
        name="ada",
    )(cond8, ada_w, ada_b.reshape(depth, 1, 6 * D))


def _modrows_kernel(x_ref, g_ref, sh_ref, sc_ref, o_ref, *, eps):
    x = _rms(x_ref[...], g_ref[...], eps)
    o_ref[...] = x * (1.0 + sc_ref[...]) + sh_ref[...]


def _modrows(x, g, mods, sh_c, sc_c, *, rows, row0, width, col_blk=0, tm=TM, eps=NORM_EPS,
             grouped=True):
    r0 = row0 // tm
    if grouped:
        grp = lambda i: _group_of_tile(i + r0, tm)
    else:
        grp = lambda i: 0
    return pl.pallas_call(
        functools.partial(_modrows_kernel, eps=eps),
        grid=(rows // tm,),
        in_specs=[
            pl.BlockSpec((tm, width), lambda i: (i + r0, col_blk)),
            pl.BlockSpec((1, width), lambda i: (0, 0)),
            pl.BlockSpec((None, None, 1, width), lambda i: (grp(i), sh_c, 0, 0)),
            pl.BlockSpec((None, None, 1, width), lambda i: (grp(i), sc_c, 0, 0)),
        ],
        out_specs=pl.BlockSpec((tm, width), lambda i: (i, 0)),
        out_shape=jax.ShapeDtypeStruct((rows, width), F32),
        compiler_params=_cparams(("parallel",)),
        name="modrows",
    )(x, g.reshape(1, width), mods, mods)


def _linear_kernel(*refs, prologue, eps, epilogue, split, tm):
    it = iter(refs)
    x_ref = next(it)
    x_lat_ref = next(it) if split else None
    w_ref = next(it)
    g_ref = next(it) if prologue != "none" else None
    sh_ref = next(it) if prologue == "mod" else None
    sc_ref = next(it) if prologue == "mod" else None
    res_ref = next(it) if epilogue == "resid" else None
    gate_ref = next(it) if epilogue == "resid" else None
    o_ref = next(it)
    xb_ref = next(it)

    def load(src_ref):
        x = src_ref[...]
        if prologue != "none":
            x = _rms(x, g_ref[...], eps)
        if prologue == "mod":
            x = x * (1.0 + sc_ref[...]) + sh_ref[...]
        xb_ref[...] = x.astype(BF16)

    first = pl.program_id(1) == 0
    if split:
        is_ctx = pl.program_id(0) < M_CTX // tm
        pl.when(first & is_ctx)(lambda: load(x_ref))
        pl.when(first & jnp.logical_not(is_ctx))(lambda: load(x_lat_ref))
    else:
        pl.when(first)(lambda: load(x_ref))

    acc = _dot(xb_ref[...], w_ref[...].astype(BF16))
    if epilogue == "resid":
        acc = res_ref[...] + gate_ref[...] * acc
    o_ref[...] = acc


def _linear(x, w, *, tn, k=None, x_col=0, norm_g=None, eps=NORM_EPS, mods=None, sh_c=None, sc_c=None,
            resid=None, gate_c=None, planes=False, tm=TM, name="linear"):
    split = isinstance(x, tuple)
    n = w.shape[1]
    prologue = "none" if norm_g is None else ("mod" if sh_c is not None else "norm")
    epilogue = "none" if resid is None else "resid"
    grp = lambda i: _group_of_tile(i, tm)
    if split:
        x_ctx, x_lat = x
        m = x_ctx.shape[0] + x_lat.shape[0]
        k = x_ctx.shape[1]
        n_ctx = x_ctx.shape[0] // tm
        args = [x_ctx, x_lat]
        in_specs = [
            pl.BlockSpec((tm, k), lambda i, j: (jnp.minimum(i, n_ctx - 1), 0)),
            pl.BlockSpec((tm, k), lambda i, j: (jnp.maximum(i - n_ctx, 0), 0)),
        ]
    else:
        m = x.shape[0]
        k = x.shape[1] if k is None else k
        args = [x]
        in_specs = [pl.BlockSpec((tm, k), lambda i, j: (i, x_col))]
    args.append(w)
    in_specs.append(pl.BlockSpec((k, tn), lambda i, j: (0, j)))
    if prologue != "none":
        args.append(norm_g.reshape(1, k))
        in_specs.append(pl.BlockSpec((1, k), lambda i, j: (0, 0)))
    if prologue == "mod":
        args += [mods, mods]
        in_specs += [
            pl.BlockSpec((None, None, 1, k), lambda i, j: (grp(i), sh_c, 0, 0)),
            pl.BlockSpec((None, None, 1, k), lambda i, j: (grp(i), sc_c, 0, 0)),
        ]
    if epilogue == "resid":
        args += [resid, mods]
        in_specs += [
            pl.BlockSpec((tm, tn), lambda i, j: (i, j)),
            pl.BlockSpec((None, None, 1, tn), lambda i, j: (grp(i), gate_c, 0, j)),
        ]
    if planes:
        out_spec = pl.BlockSpec((None, tm, tn), lambda i, j: (j, i, 0))
        out_shape = jax.ShapeDtypeStruct((n // tn, m, tn), F32)
    else:
        out_spec = pl.BlockSpec((tm, tn), lambda i, j: (i, j))
        out_shape = jax.ShapeDtypeStruct((m, n), F32)
    return pl.pallas_call(
        functools.partial(_linear_kernel, prologue=prologue, eps=eps, epilogue=epilogue, split=split, tm=tm),
        grid=(m // tm, n // tn),
        in_specs=in_specs,
        out_specs=out_spec,
        out_shape=out_shape,
        scratch_shapes=[pltpu.VMEM((tm, k), BF16)],
        compiler_params=_cparams(("parallel", "arbitrary")),
        name=name,
    )(*args)


POOL_ROWS = 2048


def _pool_kernel(h_ref, x_ref, w_ref, ps_ref, gate_ref, o_ref):
    i = pl.program_id(0)
    g = pl.program_id(1)
    seq = jnp.where(i < M_CTX // POOL_ROWS, T_CTX, T_LAT)
    t = lax.broadcasted_iota(jnp.int32, (POOL_ROWS, 1), 0) & (seq - 1)
    for gi, win in enumerate(POOL_WINDOWS):
        @pl.when(g == gi)
        def _(win=win):
            h = h_ref[...]
            acc = jnp.zeros_like(h)
            for d in range(-(win // 2), win - win // 2):
                sh = h if d == 0 else pltpu.roll(h, (-d) % POOL_ROWS, 0)
                ok = (t + d >= 0) & (t + d < seq)
                acc = acc + jnp.where(ok, sh, 0.0)
            cnt = jnp.minimum(t + (win - win // 2), seq) - jnp.maximum(t - win // 2, 0)
            pooled = acc / cnt.astype(F32)
            y = _dot((pooled - h).astype(BF16), w_ref[...].astype(BF16)) * ps_ref[...]
            o_ref[...] = x_ref[...] + gate_ref[...] * y


def _pool(h, x, pool_w, pool_scale, mods, gate_c):
    gw = D // len(POOL_WINDOWS)
    grp = lambda i: _group_of_tile(i, POOL_ROWS)
    return pl.pallas_call(
        _pool_kernel,
        grid=(M_TOK // POOL_ROWS, len(POOL_WINDOWS)),
        in_specs=[
            pl.BlockSpec((POOL_ROWS, gw), lambda i, g: (i, g)),
            pl.BlockSpec((POOL_ROWS, gw), lambda i, g: (i, g)),
            pl.BlockSpec((None, gw, gw), lambda i, g: (g, 0, 0)),
            pl.BlockSpec((1, gw), lambda i, g: (0, g)),
            pl.BlockSpec((None, None, 1, gw), lambda i, g: (grp(i), gate_c, 0, g)),
        ],
        out_specs=pl.BlockSpec((POOL_ROWS, gw), lambda i, g: (i, g)),
        out_shape=jax.ShapeDtypeStruct((M_TOK, D), F32),
        compiler_params=_cparams(("parallel", "parallel")),
        name="pool",
    )(h, x, pool_w, pool_scale.reshape(1, D), mods)


NJ = FFN_HIDDEN // TH
TN2 = 256
NN = D // TN2


def _swiglu_kernel(*refs, moe):
    if moe:
        te_ref, tv_ref, x_ref, wg_ref, wu_ref, wd_ref, o_ref, xb_ref, h_ref = refs
    else:
        x_ref, wg_ref, wu_ref, wd_ref, g_ref, sh_ref, sc_ref, res_ref, gate_ref, o_ref, xb_ref, h_ref = refs
    j = pl.program_id(1)
    n_valid = tv_ref[pl.program_id(0)] if moe else TMF

    def per_sub_block(fn, fn_skipped=None):
        sub_blocks = [pl.ds(s * SUB, SUB) for s in range(TMF // SUB)]

        def all_rows():
            for rows in sub_blocks:
                fn(rows)

        def valid_rows_only():
            for s, rows in enumerate(sub_blocks):
                pl.when(s * SUB < n_valid)(functools.partial(fn, rows))
                if fn_skipped is not None:
                    pl.when(s * SUB >= n_valid)(functools.partial(fn_skipped, rows))

        if moe:
            pl.when(n_valid == TMF)(all_rows)
            pl.when(n_valid < TMF)(valid_rows_only)
        else:
            all_rows()

    @pl.when(j == 0)
    def _():
        def cast(rows):
            x = x_ref[rows, :]
            if not moe:
                x = _rms(x, g_ref[...], NORM_EPS) * (1.0 + sc_ref[...]) + sh_ref[...]
            xb_ref[rows, :] = x.astype(BF16)

        per_sub_block(cast)

    @pl.when(j < NJ)
    def _():
        wg = wg_ref[...].astype(BF16)
        wu = wu_ref[...].astype(BF16)

        def up(rows):
            xb = xb_ref[rows, :]
            a = _dot(xb, wg)
            u = _dot(xb, wu)
            h_ref[j, rows, :] = (a * jax.nn.sigmoid(a) * u).astype(BF16)

        per_sub_block(up)

    @pl.when(j >= NJ)
    def _():
        wd = wd_ref[...].astype(BF16)

        def down(rows):
            hcat = jnp.concatenate([h_ref[jj, rows, :] for jj in range(NJ)], axis=1)
            y = _dot(hcat, wd)
            if moe:
                o_ref[rows, :] = y
            else:
                o_ref[rows, :] = res_ref[rows, :] + gate_ref[...] * y

        def skipped(rows):
            o_ref[rows, :] = jnp.zeros((SUB, TN2), F32)

        per_sub_block(down, skipped)


_SWIGLU_SCRATCH = [pltpu.VMEM((TMF, D), BF16), pltpu.VMEM((NJ, TMF, TH), BF16)]


def _ffn_dense(x, wg, wu, wd, f, norm_g, mods, sh_c, sc_c, gate_c):
    grp = lambda i: _group_of_tile(i, TMF)
    jh = lambda j: jnp.minimum(j, NJ - 1)
    jn = lambda j: jnp.maximum(j - NJ, 0)
    mod_spec = lambda c: pl.BlockSpec((None, None, 1, D), lambda i, j: (grp(i), c, 0, 0))
    return pl.pallas_call(
        functools.partial(_swiglu_kernel, moe=False),
        grid=(M_TOK // TMF, NJ + NN),
        in_specs=[
            pl.BlockSpec((TMF, D), lambda i, j: (i, 0)),
            pl.BlockSpec((None, D, TH), lambda i, j: (f, 0, jh(j))),
            pl.BlockSpec((None, D, TH), lambda i, j: (f, 0, jh(j))),
            pl.BlockSpec((None, FFN_HIDDEN, TN2), lambda i, j: (f, 0, jn(j))),
            pl.BlockSpec((1, D), lambda i, j: (0, 0)),
            mod_spec(sh_c), mod_spec(sc_c),
            pl.BlockSpec((TMF, TN2), lambda i, j: (i, jn(j))),
            pl.BlockSpec((None, None, 1, TN2), lambda i, j: (grp(i), gate_c, 0, jn(j))),
        ],
        out_specs=pl.BlockSpec((TMF, TN2), lambda i, j: (i, jn(j))),
        out_shape=jax.ShapeDtypeStruct((M_TOK, D), F32),
        scratch_shapes=_SWIGLU_SCRATCH,
        compiler_params=_cparams(("parallel", "arbitrary")),
        name="ffn_dense",
    )(x, wg, wu, wd, norm_g.reshape(1, D), mods, mods, x, mods)


def _ffn_moe(xs, wg, wu, wd, f, tile_expert, tile_valid):
    jh = lambda i, j, tv: jnp.where(tv[i] > 0, jnp.minimum(j, NJ - 1), NJ - 1)
    jn = lambda i, j, tv: jnp.where(tv[i] > 0, jnp.maximum(j - NJ, 0), NN - 1)
    return pl.pallas_call(
        functools.partial(_swiglu_kernel, moe=True),
        grid_spec=pltpu.PrefetchScalarGridSpec(
            num_scalar_prefetch=2,
            grid=(MOE_TILES, NJ + NN),
            in_specs=[
                pl.BlockSpec((TMF, D), lambda i, j, te, tv: (i, 0)),
                pl.BlockSpec((None, None, D, TH), lambda i, j, te, tv: (f, te[i], 0, jh(i, j, tv))),
                pl.BlockSpec((None, None, D, TH), lambda i, j, te, tv: (f, te[i], 0, jh(i, j, tv))),
                pl.BlockSpec((None, None, FFN_HIDDEN, TN2), lambda i, j, te, tv: (f, te[i], 0, jn(i, j, tv))),
            ],
            out_specs=pl.BlockSpec((TMF, TN2), lambda i, j, te, tv: (i, jnp.maximum(j - NJ, 0))),
            scratch_shapes=_SWIGLU_SCRATCH,
        ),
        out_shape=jax.ShapeDtypeStruct((MOE_ROWS, D), F32),
        compiler_params=_cparams(("parallel", "arbitrary")),
        name="ffn_moe",
    )(tile_expert, tile_valid, xs, wg, wu, wd)


def _router_kernel(x_ref, g_ref, sh_ref, sc_ref, r_ref, h_ref, idx_ref, gates_ref):
    h = _rms(x_ref[...], g_ref[...], NORM_EPS) * (1.0 + sc_ref[...]) + sh_ref[...]
    h_ref[...] = h
    w = r_ref[...]
    h_hi = h.astype(BF16)
    h_lo = (h - h_hi.astype(F32)).astype(BF16)
    w_hi = w.astype(BF16)
    w_lo = (w - w_hi.astype(F32)).astype(BF16)
    logits = _dot(h_hi, w_hi) + (_dot(h_lo, w_hi) + _dot(h_hi, w_lo))
    col = lax.broadcasted_iota(jnp.int32, logits.shape, 1)
    colf = col.astype(F32)
    lg = jnp.where(col < N_EXPERTS, logits, -jnp.inf)
    m1 = jnp.max(lg, axis=-1, keepdims=True)
    i1 = jnp.min(jnp.where(lg == m1, colf, float(LANE)), axis=-1, keepdims=True)
    lg2 = jnp.where(colf == i1, -jnp.inf, lg)
    m2 = jnp.max(lg2, axis=-1, keepdims=True)
    i2 = jnp.min(jnp.where(lg2 == m2, colf, float(LANE)), axis=-1, keepdims=True)
    i1 = i1.astype(jnp.int32)
    i2 = i2.astype(jnp.int32)
    e = jnp.exp(m2 - m1)
    g1 = 1.0 / (1.0 + e)
    g2 = e / (1.0 + e)
    idx_ref[...] = jnp.where(col == 0, i1, jnp.where(col == 1, i2, 0))
    gates_ref[...] = jnp.where(col == 0, g1, jnp.where(col == 1, g2, 0.0))


def _router(x, norm_g, mods, sh_c, sc_c, router):
    r_pad = jnp.pad(router, ((0, 0), (0, LANE - N_EXPERTS)))
    grp = lambda i: _group_of_tile(i, TM)
    return pl.pallas_call(
        _router_kernel,
        grid=(M_TOK // TM,),
        in_specs=[
            pl.BlockSpec((TM, D), lambda i: (i, 0)),
            pl.BlockSpec((1, D), lambda i: (0, 0)),
            pl.BlockSpec((None, None, 1, D), lambda i: (grp(i), sh_c, 0, 0)),
            pl.BlockSpec((None, None, 1, D), lambda i: (grp(i), sc_c, 0, 0)),
            pl.BlockSpec((D, LANE), lambda i: (0, 0)),
        ],
        out_specs=[
            pl.BlockSpec((TM, D), lambda i: (i, 0)),
            pl.BlockSpec((TM, LANE), lambda i: (i, 0)),
            pl.BlockSpec((TM, LANE), lambda i: (i, 0)),
        ],
        out_shape=[
            jax.ShapeDtypeStruct((M_TOK, D), F32),
            jax.ShapeDtypeStruct((M_TOK, LANE), jnp.int32),
            jax.ShapeDtypeStruct((M_TOK, LANE), F32),
        ],
        compiler_params=_cparams(("parallel",)),
        name="router",
    )(x, norm_g.reshape(1, D), mods, mods, r_pad)


def _route_plan(e1, e2):
    e = jnp.concatenate([e1, e2])
    oh = (e[:, None] == jnp.arange(N_EXPERTS, dtype=jnp.int32)[None, :]).astype(jnp.int32)
    csum = jnp.cumsum(oh, axis=0)
    rank = jnp.sum((csum - oh) * oh, axis=1)
    counts = csum[-1]
    padded = ((counts + TMF - 1) // TMF) * TMF
    ends = jnp.cumsum(padded)
    starts = ends - padded
    pos = jnp.sum(oh * starts[None, :], axis=1) + rank
    tile_start = jnp.arange(MOE_TILES, dtype=jnp.int32) * TMF
    te = jnp.minimum(jnp.sum((tile_start[:, None] >= ends[None, :]).astype(jnp.int32), axis=1),
                     N_EXPERTS - 1)
    used = tile_start < ends[-1]
    valid = jnp.where(used, jnp.clip(counts[te] - (tile_start - starts[te]), 0, TMF), 0)
    last_e = te[jnp.maximum(ends[-1] // TMF - 1, 0)]
    te = jnp.where(used, te, last_e)
    sub_off = jnp.arange(TMF // SUB, dtype=jnp.int32) * SUB
    sub_valid = jnp.clip(valid[:, None] - sub_off[None, :], 0, SUB).reshape(-1)
    return pos.astype(jnp.int32), te.astype(jnp.int32), valid.astype(jnp.int32), sub_valid.astype(jnp.int32)


DMA_UNROLL = 8


def _dispatch_kernel(pos_ref, sv_ref, h_ref, xs_hbm, zero_ref, sem, zsem):
    i = pl.program_id(0)

    @pl.when(i == 0)
    def _():
        zero_ref[...] = jnp.zeros_like(zero_ref)

        def zero_copy(b):
            return pltpu.make_async_copy(zero_ref, xs_hbm.at[pl.ds(pl.multiple_of(b * SUB, SUB), SUB)], zsem)

        def start(b, c):
            pl.when(sv_ref[b] < SUB)(lambda: zero_copy(b).start())
            return c

        def wait(b, c):
            pl.when(sv_ref[b] < SUB)(lambda: zero_copy(b).wait())
            return c

        lax.fori_loop(0, MOE_ROWS // SUB, start, 0)
        lax.fori_loop(0, MOE_ROWS // SUB, wait, 0)

    def issue(r, c):
        p1 = pos_ref[i * SUB + r]
        p2 = pos_ref[M_TOK + i * SUB + r]
        pltpu.make_async_copy(h_ref.at[pl.ds(r, 1)], xs_hbm.at[pl.ds(p1, 1)], sem).start()
        pltpu.make_async_copy(h_ref.at[pl.ds(r, 1)], xs_hbm.at[pl.ds(p2, 1)], sem).start()
        return c

    lax.fori_loop(0, SUB, issue, 0, unroll=DMA_UNROLL)
    pltpu.make_async_copy(h_ref, xs_hbm.at[pl.ds(0, SUB)], sem).wait()
    pltpu.make_async_copy(h_ref, xs_hbm.at[pl.ds(0, SUB)], sem).wait()


def _moe_dispatch(h, pos, sub_valid):
    return pl.pallas_call(
        _dispatch_kernel,
        grid_spec=pltpu.PrefetchScalarGridSpec(
            num_scalar_prefetch=2,
            grid=(M_TOK // SUB,),
            in_specs=[pl.BlockSpec((SUB, D), lambda i, p, z: (i, 0))],
            out_specs=pl.BlockSpec(memory_space=pl.ANY),
            scratch_shapes=[pltpu.VMEM((SUB, D), F32), pltpu.SemaphoreType.DMA, pltpu.SemaphoreType.DMA],
        ),
        out_shape=jax.ShapeDtypeStruct((MOE_ROWS, D), F32),
        compiler_params=_cparams(("arbitrary",)),
        name="moe_dispatch",
    )(pos, sub_valid, h)


def _combine_kernel(pos_ref, y_hbm, x_ref, tg_ref, gate_ref, o_ref, y1_ref, y2_ref, sem):
    i = pl.program_id(0)

    def issue(r, c):
        p1 = pos_ref[i * SUB + r]
        p2 = pos_ref[M_TOK + i * SUB + r]
        pltpu.make_async_copy(y_hbm.at[pl.ds(p1, 1)], y1_ref.at[pl.ds(r, 1)], sem.at[0]).start()
        pltpu.make_async_copy(y_hbm.at[pl.ds(p2, 1)], y2_ref.at[pl.ds(r, 1)], sem.at[1]).start()
        return c

    lax.fori_loop(0, SUB, issue, 0, unroll=DMA_UNROLL)
    pltpu.make_async_copy(y_hbm.at[pl.ds(0, SUB)], y1_ref, sem.at[0]).wait()
    pltpu.make_async_copy(y_hbm.at[pl.ds(0, SUB)], y2_ref, sem.at[1]).wait()
    tg = tg_ref[...]
    moe_out = tg[:, 0:1] * y1_ref[...] + tg[:, 1:2] * y2_ref[...]
    o_ref[...] = x_ref[...] + gate_ref[...] * moe_out


def _moe_combine(y, pos, x, top_gates, mods, gate_c):
    grp = lambda i: _group_of_tile(i, SUB)
    return pl.pallas_call(
        _combine_kernel,
        grid_spec=pltpu.PrefetchScalarGridSpec(
            num_scalar_prefetch=1,
            grid=(M_TOK // SUB,),
            in_specs=[
                pl.BlockSpec(memory_space=pl.ANY),
                pl.BlockSpec((SUB, D), lambda i, p: (i, 0)),
                pl.BlockSpec((SUB, LANE), lambda i, p: (i, 0)),
                pl.BlockSpec((None, None, 1, D), lambda i, p: (grp(i), gate_c, 0, 0)),
            ],
            out_specs=pl.BlockSpec((SUB, D), lambda i, p: (i, 0)),
            scratch_shapes=[pltpu.VMEM((SUB, D), F32), pltpu.VMEM((SUB, D), F32),
                            pltpu.SemaphoreType.DMA((2,))],
        ),
        out_shape=jax.ShapeDtypeStruct((M_TOK, D), F32),
        compiler_params=_cparams(("arbitrary",)),
        name="moe_combine",
    )(pos, y, x, top_gates, mods)


def _moe(x, norm_g, mods, sh_c, sc_c, gate_c, router, wg, wu, wd, f):
    h, idx, gates = _router(x, norm_g, mods, sh_c, sc_c, router)
    pos, te, tv, sub_valid = _route_plan(idx[:, 0], idx[:, 1])
    xs = _moe_dispatch(h, pos, sub_valid)
    ys = _ffn_moe(xs, wg, wu, wd, f, te, tv)
    return _moe_combine(ys, pos, x, gates, mods, gate_c)


def _rope_tables(width, lo, hi):
    pos = jnp.arange(T_LAT)
    rows = (pos // GRID_W).astype(F32)
    cols = (pos % GRID_W).astype(F32)
    n_freq = 16
    inv = ROPE_BASE ** (-jnp.arange(n_freq, dtype=F32) / n_freq)
    ang = jnp.concatenate([rows[:, None] * inv[None, :], cols[:, None] * inv[None, :]], axis=-1)
    cos, sin = jnp.cos(ang), jnp.sin(ang)
    lane = np.arange(width)
    active = (lane >= lo) & (lane < hi)
    reps = width // 64
    c = jnp.tile(jnp.concatenate([cos, cos], axis=-1), (1, reps))
    s = jnp.tile(jnp.concatenate([-sin, sin], axis=-1), (1, reps))
    return jnp.where(active[None, :], c, 1.0), jnp.where(active[None, :], s, 0.0)


def _rope_kernel(x_ref, c_ref, s_ref, o_ref):
    x = x_ref[...]
    w = x.shape[-1]
    lane = lax.broadcasted_iota(jnp.int32, (1, w), 1)
    first = (lane & 63) < 32
    partner = jnp.where(first, pltpu.roll(x, w - 32, 1), pltpu.roll(x, 32, 1))
    o_ref[...] = x * c_ref[...] + partner * s_ref[...]


def _rope(x, c, s, *, planes, row0, width, col0=0, ncol=1, tr=512):
    r0 = row0 // tr
    nt = T_LAT // tr
    if x.ndim == 3:
        ncol = x.shape[2] // width
        x_spec = pl.BlockSpec((None, tr, width), lambda p, i, j: (p, i + r0, j))
    else:
        x_spec = pl.BlockSpec((tr, width), lambda p, i, j: (i + r0, col0 + j))
    return pl.pallas_call(
        _rope_kernel,
        grid=(planes, M_LAT // tr, ncol),
        in_specs=[
            x_spec,
            pl.BlockSpec((tr, width), lambda p, i, j: (i % nt, 0)),
            pl.BlockSpec((tr, width), lambda p, i, j: (i % nt, 0)),
        ],
        out_specs=pl.BlockSpec((None, tr, width), lambda p, i, j: (p, i, j)),
        out_shape=jax.ShapeDtypeStruct((planes, M_LAT, ncol * width), F32),
        compiler_params=_cparams(("parallel", "parallel", "parallel")),
        name="rope",
    )(x, c, s)


def _attn_kernel(*refs, kind, hb, nseg, scale, lam_init):
    it = iter(refs)
    q_ref = next(it)
    k_refs = [next(it) for _ in range(nseg)]
    kpe_refs = [next(it) for _ in range(nseg)] if kind == "mla" else None
    v_refs = [next(it) for _ in range(nseg)] if kind != "mla" else k_refs
    lam_ref = next(it) if kind == "diff" else None
    subln_ref = next(it) if kind == "diff" else None
    o_ref = next(it)

    lane = lax.broadcasted_iota(jnp.int32, (1, LANE), 1)
    lo_half = lane < 64
    log2_scale = scale * LOG2_E

    def softmax_parts(s_list):
        t_list = [s * log2_scale for s in s_list]
        m = functools.reduce(jnp.maximum, [jnp.max(t, axis=-1, keepdims=True) for t in t_list])
        p_list = [jnp.exp2(t - m) for t in t_list]
        l = functools.reduce(jnp.add, [jnp.sum(p, axis=-1, keepdims=True) for p in p_list])
        return p_list, 1.0 / l

    def attend(s_list, v_list):
        p_list, inv_l = softmax_parts(s_list)
        return functools.reduce(jnp.add, [_dot(p.astype(BF16), v) for p, v in zip(p_list, v_list)]) * inv_l

    for hh in range(hb):
        if kind == "mla":
            q2 = q_ref[:, hh * 256:(hh + 1) * 256].astype(BF16)
            s_list = []
            for kr, pr in zip(k_refs, kpe_refs):
                kcat = jnp.concatenate([kr[:, hh * 256:hh * 256 + 128], pr[...]], axis=1).astype(BF16)
                s_list.append(_dot_nt(q2, kcat))
            v2 = [vr[:, hh * 256 + 128:hh * 256 + 256].astype(BF16) for vr in v_refs]
            o_ref[:, hh * LANE:(hh + 1) * LANE] = attend(s_list, v2)
            continue

        sl = slice(hh * LANE, (hh + 1) * LANE)
        q2 = q_ref[:, sl]
        k2 = [kr[:, sl].astype(BF16) for kr in k_refs]
        v2 = [vr[:, sl].astype(BF16) for vr in v_refs]
        tq = q2.shape[0]
        q_heads = [jnp.where(lo_half, q2, 0.0), jnp.where(lo_half, 0.0, q2)]
        if hb > 1:
            qs = jnp.concatenate(q_heads, axis=0).astype(BF16)
            o2 = attend([_dot_nt(qs, kk) for kk in k2], v2)
            o_heads = [o2[:tq], o2[tq:]]
            if kind == "diff":
                o_diff = o_heads[0] - lam_ref[...][:, :1] * o_heads[1]
        elif kind == "na":
            o_heads = [attend([_dot_nt(qh.astype(BF16), kk) for kk in k2], v2) for qh in q_heads]
        else:
            (p0, il0), (p1, il1) = [softmax_parts([_dot_nt(qh.astype(BF16), kk) for kk in k2]) for qh in q_heads]
            lam = lam_ref[...][:, :1]
            o_diff = functools.reduce(jnp.add, [
                _dot((a * il0 - lam * (b * il1)).astype(BF16), vv) for a, b, vv in zip(p0, p1, v2)])
        if kind == "na":
            o_ref[:, sl] = jnp.where(lo_half, o_heads[0], o_heads[1])
        else:
            o_ref[:, sl] = _rms(o_diff, subln_ref[...], DIFF_SUBLN_EPS) * (1.0 - lam_init)


def _attention(kind, *, grid, q, ks, vs, kpes=None, extra=(), rows, out_spec, hb, scale, lam_init=0.0,
               name="attn"):
    ops = [q] + list(ks) + (list(kpes) if kind == "mla" else []) + (list(vs) if kind != "mla" else [])
    ops += list(extra)
    return pl.pallas_call(
        functools.partial(_attn_kernel, kind=kind, hb=hb, nseg=len(ks), scale=scale, lam_init=lam_init),
        grid=grid,
        in_specs=[s for _, s in ops],
        out_specs=out_spec,
        out_shape=jax.ShapeDtypeStruct((rows, D), F32),
        compiler_params=_cparams(("parallel",) * len(grid)),
        name=name,
    )(*[a for a, _ in ops])


NA_QROWS = 4
NA_KROWS = 12
NA_TQ = NA_QROWS * GRID_W
NA_TK = NA_KROWS * GRID_W


def _na_bias_table(rpb):
    h = rpb.shape[0]
    qc = np.arange(GRID_W)[:, None]
    kc = np.arange(GRID_W)[None, :]
    qs = np.clip(qc - NA_WIN_C // 2, 0, GRID_W - NA_WIN_C)
    col_ok = (kc >= qs) & (kc < qs + NA_WIN_C)
    ic = np.clip(kc - qc + NA_WIN_C - 1, 0, 2 * NA_WIN_C - 2)
    n_ir = 2 * NA_WIN_R - 1
    onehot = (ic.reshape(1, -1) == np.arange(2 * NA_WIN_C - 1)[:, None]).astype(np.float32)
    tab = jnp.einsum("hrm,mx->hrx", rpb.astype(F32) * LOG2_E, jnp.asarray(onehot),
                     precision=lax.Precision.HIGHEST)
    tab = jnp.where(jnp.asarray(col_ok.reshape(1, 1, -1)), tab, NEG_INF).reshape(h, n_ir, GRID_W, GRID_W)
    tab = jnp.concatenate([tab, tab], axis=-1)
    n_rows = T_LAT // GRID_W
    n_g = n_rows // NA_QROWS
    ir_idx = np.zeros((3, NA_QROWS, NA_KROWS), np.int32)
    row_ok = np.zeros((3, NA_QROWS, NA_KROWS), bool)
    for pat, g in enumerate((0, 1, n_g - 1)):
        base = int(np.clip(NA_QROWS * g - NA_QROWS, 0, n_rows - NA_KROWS))
        for qr in range(NA_QROWS):
            r = NA_QROWS * g + qr
            r0 = int(np.clip(r - NA_WIN_R // 2, 0, n_rows - NA_WIN_R))
            for kr in range(NA_KROWS):
                ok = r0 <= base + kr < r0 + NA_WIN_R
                row_ok[pat, qr, kr] = ok
                ir_idx[pat, qr, kr] = (base + kr - r + NA_WIN_R - 1) if ok else 0

    def build(tab_ref, o_ref):
        pat_id = pl.program_id(0)
        lo_half = lax.broadcasted_iota(jnp.int32, (1, LANE), 1) < GRID_W
        masked = jnp.full((GRID_W, LANE), NEG_INF, F32)
        for pat in range(3):
            @pl.when(pat_id == pat)
            def _(pat=pat):
                for qr in range(NA_QROWS):
                    for kp in range(NA_KROWS // 2):
                        halves = [tab_ref[int(ir_idx[pat, qr, kr])] if row_ok[pat, qr, kr] else masked
                                  for kr in (2 * kp, 2 * kp + 1)]
                        o_ref[qr * GRID_W:(qr + 1) * GRID_W, kp * LANE:(kp + 1) * LANE] = jnp.where(
                            lo_half, halves[0], halves[1])

    return pl.pallas_call(
        build,
        grid=(3, h),
        in_specs=[pl.BlockSpec((None, n_ir, GRID_W, LANE), lambda p, hh: (hh, 0, 0, 0))],
        out_specs=pl.BlockSpec((None, None, NA_TQ, NA_TK), lambda p, hh: (p, hh, 0, 0)),
        out_shape=jax.ShapeDtypeStruct((3, h, NA_TQ, NA_TK), F32),
        compiler_params=_cparams(("parallel", "parallel")),
        name="na_bias",
    )(tab)


def _na_lat_kernel(q_ref, kc_ref, vc_ref, k_ref, v_ref, b_ref, o_ref, *, scale):
    g = pl.program_id(2)
    n_rows = T_LAT // GRID_W
    base = jnp.clip(NA_QROWS * g - NA_QROWS, 0, n_rows - NA_KROWS) * GRID_W
    base = pl.multiple_of(base, GRID_W)
    lane = lax.broadcasted_iota(jnp.int32, (1, LANE), 1)
    lo_half = lane < 64
    q2 = q_ref[...]
    kc = kc_ref[...].astype(BF16)
    vc = vc_ref[...].astype(BF16)
    kl = k_ref[pl.ds(base, NA_TK), :].astype(BF16)
    vl = v_ref[pl.ds(base, NA_TK), :].astype(BF16)
    log2_scale = scale * LOG2_E
    outs = []
    for n in range(2):
        qm = jnp.where(lo_half if n == 0 else jnp.logical_not(lo_half), q2, 0.0).astype(BF16)
        s_c = _dot_nt(qm, kc) * log2_scale
        s_l = _dot_nt(qm, kl) * log2_scale + b_ref[n]
        m = jnp.maximum(jnp.max(s_c, axis=-1, keepdims=True), jnp.max(s_l, axis=-1, keepdims=True))
        p_c = jnp.exp2(s_c - m)
        p_l = jnp.exp2(s_l - m)
        l = jnp.sum(p_c, axis=-1, keepdims=True) + jnp.sum(p_l, axis=-1, keepdims=True)
        o = _dot(p_c.astype(BF16), vc) + _dot(p_l.astype(BF16), vl)
        outs.append(o * (1.0 / l))
    o_ref[...] = jnp.where(lo_half, outs[0], outs[1])


def _na_lat(qkv, cache_k, cache_v, bias):
    n_g = T_LAT // NA_TQ
    q_blk0 = M_CTX // NA_TQ
    kv_blk0 = M_CTX // T_LAT
    pat = lambda g: jnp.where(g == 0, 0, jnp.where(g == n_g - 1, 2, 1))
    return pl.pallas_call(
        functools.partial(_na_lat_kernel, scale=0.125),
        grid=(N_LAT_SEQ, D // LANE, n_g),
        in_specs=[
            pl.BlockSpec((None, NA_TQ, LANE), lambda b, h, g: (0, q_blk0 + b * n_g + g, h)),
            pl.BlockSpec((None, PAST, LANE), lambda b, h, g: (b, 0, h)),
            pl.BlockSpec((None, PAST, LANE), lambda b, h, g: (b, 0, h)),
            pl.BlockSpec((None, T_LAT, LANE), lambda b, h, g: (1, kv_blk0 + b, h)),
            pl.BlockSpec((None, T_LAT, LANE), lambda b, h, g: (2, kv_blk0 + b, h)),
            pl.BlockSpec((None, 2, NA_TQ, NA_TK), lambda b, h, g: (pat(g), h, 0, 0)),
        ],
        out_specs=pl.BlockSpec((NA_TQ, LANE), lambda b, h, g: (b * n_g + g, h)),
        out_shape=jax.ShapeDtypeStruct((M_LAT, D), F32),
        compiler_params=_cparams(("parallel", "parallel", "parallel")),
        name="na_latent",
    )(qkv, cache_k, cache_v, qkv, qkv, bias)


def _seq_t_kernel(x_ref, o_ref):
    o_ref[...] = x_ref[...].T


def _seq_transpose(qkv, plane):
    return pl.pallas_call(
        _seq_t_kernel,
        grid=(N_CTX_SEQ,),
        in_specs=[pl.BlockSpec((None, T_CTX, D), lambda b: (plane, b, 0))],
        out_specs=pl.BlockSpec((None, D, T_CTX), lambda b: (b, 0, 0)),
        out_shape=jax.ShapeDtypeStruct((N_CTX_SEQ, D, T_CTX), F32),
        compiler_params=_cparams(("parallel",)),
        name="seq_transpose",
    )(qkv)


def kernel(x_prompt, x_sample, c, c_ctx, cache_na_k, cache_na_v, cache_diff_k, cache_diff_v, cache_mla_ckv, cache_mla_kpe, ada_w, ada_b, norm1, norm2, pool_w, pool_scale, na_qkv, na_rpb, na_out, diff_qkv, diff_lambda, diff_subln, diff_out, mla_qa, mla_qnorm, mla_qb, mla_kva, mla_kvnorm, mla_kvb, mla_out, ffn_gate, ffn_up, ffn_down, moe_router, moe_gate, moe_up, moe_down, final_norm):
    depth = ada_w.shape[0]
    x = jnp.concatenate([x_prompt.reshape(M_CTX, D), x_sample.reshape(M_LAT, D)], axis=0)
    cond8 = jnp.concatenate([c_ctx[None, :], c, jnp.zeros((5, D), F32)], axis=0)
    mods_all = _ada(cond8, ada_w, ada_b)[:, :3].reshape(depth, 3, 6, 1, D)
    outs = {}

    for l in range(depth):
        kind, li = l % 4, l // 4
        mods = mods_all[l]
        if kind == 0:
            h = _modrows(x, norm1[l], mods, 0, 1, rows=M_TOK, row0=0, width=D)
            x = _pool(h, x, pool_w[li], pool_scale[li], mods, 2)
        elif kind == 1:
            qkv = _linear(x, na_qkv[li], tn=D, norm_g=norm1[l], mods=mods, sh_c=0, sc_c=1, planes=True,
                          name="na_qkv")
            to_cache = lambda a: jnp.transpose(a.reshape(N_CTX_SEQ, 1, 16, 64, T_CTX), (0, 1, 4, 2, 3))
            outs["na_k"] = to_cache(_seq_transpose(qkv, 1))
            outs["na_v"] = to_cache(_seq_transpose(qkv, 2))
            seq_spec = lambda p: pl.BlockSpec((None, T_CTX, D), lambda b, p=p: (p, b, 0))
            o_ctx = _attention(
                "na", grid=(N_CTX_SEQ,), q=(qkv, seq_spec(0)), ks=[(qkv, seq_spec(1))], vs=[(qkv, seq_spec(2))],
                rows=M_CTX, out_spec=pl.BlockSpec((T_CTX, D), lambda b: (b, 0)), hb=D // LANE, scale=0.125,
                name="na_ctx")
            bias = _na_bias_table(na_rpb[li])
            o_lat = _na_lat(qkv, cache_na_k[:, li].reshape(N_LAT_SEQ, PAST, D),
                            cache_na_v[:, li].reshape(N_LAT_SEQ, PAST, D), bias)
            x = _linear((o_ctx, o_lat), na_out[li], tn=D, resid=x, mods=mods, gate_c=2, name="na_out")
        elif kind == 2:
            lam_init = 0.8 - 0.6 * math.exp(-0.3 * l)
            lp = diff_lambda[li].astype(F32)
            lam = jnp.exp(jnp.sum(lp[0] * lp[1])) - jnp.exp(jnp.sum(lp[2] * lp[3])) + lam_init
            lam_row = jnp.full((1, LANE), lam, F32)
            subln = diff_subln[li].reshape(1, LANE)
            extra = [(lam_row, pl.BlockSpec((1, LANE), lambda *_: (0, 0))),
                     (subln, pl.BlockSpec((1, LANE), lambda *_: (0, 0)))]
            qkv = _linear(x, diff_qkv[li], tn=D, norm_g=norm1[l], mods=mods, sh_c=0, sc_c=1, planes=True,
                          name="diff_qkv")
            outs["diff_k"] = jnp.transpose(
                _seq_transpose(qkv, 1).reshape(N_CTX_SEQ, 1, 8, 2, 64, T_CTX), (0, 1, 5, 2, 3, 4))
            outs["diff_v"] = qkv[2, :M_CTX].reshape(N_CTX_SEQ, 1, T_CTX, 8, 128)
            seq_spec = lambda p: pl.BlockSpec((None, T_CTX, D), lambda b, p=p: (p, b, 0))
            o_ctx = _attention(
                "diff", grid=(N_CTX_SEQ,), q=(qkv, seq_spec(0)), ks=[(qkv, seq_spec(1))], vs=[(qkv, seq_spec(2))],
                extra=extra, rows=M_CTX, out_spec=pl.BlockSpec((T_CTX, D), lambda b: (b, 0)), hb=D // LANE,
                scale=0.125, lam_init=lam_init, name="diff_ctx")
            rc, rs = _rope_tables(LANE, 0, LANE)
            qk_r = _rope(qkv, rc, rs, planes=2, row0=M_CTX, width=LANE)
            tq = 256
            nq = T_LAT // tq
            ck = cache_diff_k[:, li].reshape(N_LAT_SEQ, PAST, D)
            cv = cache_diff_v[:, li].reshape(N_LAT_SEQ, PAST, D)
            o_lat = _attention(
                "diff", grid=(N_LAT_SEQ, D // LANE, nq),
                q=(qk_r, pl.BlockSpec((None, tq, LANE), lambda b, h, i: (0, b * nq + i, h))),
                ks=[(ck, pl.BlockSpec((None, PAST, LANE), lambda b, h, i: (b, 0, h))),
                    (qk_r, pl.BlockSpec((None, T_LAT, LANE), lambda b, h, i: (1, b, h)))],
                vs=[(cv, pl.BlockSpec((None, PAST, LANE), lambda b, h, i: (b, 0, h))),
                    (qkv, pl.BlockSpec((None, T_LAT, LANE), lambda b, h, i: (2, M_CTX // T_LAT + b, h)))],
                extra=extra,
                rows=M_LAT, out_spec=pl.BlockSpec((tq, LANE), lambda b, h, i: (b * nq + i, h)),
                hb=1, scale=0.125, lam_init=lam_init, name="diff_latent")
            x = _linear((o_ctx, o_lat), diff_out[li], tn=D, resid=x, mods=mods, gate_c=2, name="diff_out")
        else:
            hd = MLA_NOPE + MLA_ROPE
            wqb = mla_qb[li].reshape(MLA_Q_LORA, MLA_HEADS, hd)
            wqb = jnp.pad(wqb, ((0, 0), (0, 0), (0, 256 - hd))).reshape(MLA_Q_LORA, MLA_HEADS * 256)
            w_a = jnp.concatenate(
                [mla_qa[li], jnp.pad(mla_kva[li], ((0, 0), (0, 512 - MLA_KV_LORA - MLA_ROPE)))], axis=1)
            a = _linear(x, w_a, tn=D, norm_g=norm1[l], mods=mods, sh_c=0, sc_c=1, name="mla_a")
            q = _linear(a, wqb, tn=D, k=MLA_Q_LORA, x_col=0, norm_g=mla_qnorm[li], name="mla_q")
            zero_mod = jnp.zeros((1, 2, 1, MLA_KV_LORA), F32)
            ckv = _modrows(a, mla_kvnorm[li], zero_mod, 0, 1, rows=M_TOK, row0=0, width=MLA_KV_LORA, col_blk=2,
                           grouped=False)
            outs["mla_ckv"] = ckv[:M_CTX].reshape(N_CTX_SEQ, 1, T_CTX, MLA_KV_LORA)
            outs["mla_kpe"] = a[:M_CTX, 768:768 + MLA_ROPE].reshape(N_CTX_SEQ, 1, T_CTX, MLA_ROPE)
            ckv_all = jnp.concatenate([ckv, cache_mla_ckv[:, li].reshape(N_LAT_SEQ * PAST, MLA_KV_LORA)], axis=0)
            kvb = _linear(ckv_all, mla_kvb[li], tn=D, tm=512, name="mla_kvb")
            scale = float(hd) ** -0.5
            o_ctx = _attention(
                "mla", grid=(N_CTX_SEQ,),
                q=(q, pl.BlockSpec((T_CTX, 2048), lambda b: (b, 0))),
                ks=[(kvb, pl.BlockSpec((T_CTX, 2048), lambda b: (b, 0)))],
                kpes=[(a, pl.BlockSpec((T_CTX, LANE), lambda b: (b, 6)))],
                vs=None, rows=M_CTX, out_spec=pl.BlockSpec((T_CTX, D), lambda b: (b, 0)), hb=MLA_HEADS,
                scale=scale, name="mla_ctx")
            rc, rs = _rope_tables(256, 128, 192)
            q_r = _rope(q, rc, rs, planes=1, row0=M_CTX, width=256, ncol=MLA_HEADS)[0]
            rc1, rs1 = _rope_tables(LANE, 0, 64)
            kpe_r = _rope(a, rc1, rs1, planes=1, row0=M_CTX, width=LANE, col0=6)[0]
            kpe_c = jnp.pad(cache_mla_kpe[:, li], ((0, 0), (0, 0), (0, LANE - MLA_ROPE)))
            tq = 256
            nq = T_LAT // tq
            o_lat = _attention(
                "mla", grid=(N_LAT_SEQ, MLA_HEADS, nq),
                q=(q_r, pl.BlockSpec((tq, 256), lambda b, h, i: (b * nq + i, h))),
                ks=[(kvb, pl.BlockSpec((PAST, 256), lambda b, h, i: (M_TOK // PAST + b, h))),
                    (kvb, pl.BlockSpec((T_LAT, 256), lambda b, h, i: (M_CTX // T_LAT + b, h)))],
                kpes=[(kpe_c, pl.BlockSpec((None, PAST, LANE), lambda b, h, i: (b, 0, 0))),
                      (kpe_r, pl.BlockSpec((T_LAT, LANE), lambda b, h, i: (b, 0)))],
                vs=None, rows=M_LAT,
                out_spec=pl.BlockSpec((tq, LANE), lambda b, h, i: (b * nq + i, h)),
                hb=1, scale=scale, name="mla_latent")
            x = _linear((o_ctx, o_lat), mla_out[li], tn=D, resid=x, mods=mods, gate_c=2, name="mla_out")

        f = l // 2
        if l % 2 == 0:
            x = _ffn_dense(x, ffn_gate, ffn_up, ffn_down, f, norm2[l], mods, 3, 4, 5)
        else:
            x = _moe(x, norm2[l], mods, 3, 4, 5, moe_router[f], moe_gate, moe_up, moe_down, f)

    zero_mod = jnp.zeros((1, 2, 1, D), F32)
    y_prompt = _modrows(x, final_norm, zero_mod, 0, 1, rows=M_CTX, row0=0, width=D, grouped=False)
    y_sample = _modrows(x, final_norm, zero_mod, 0, 1, rows=M_LAT, row0=M_CTX, width=D, grouped=False)
    return (y_prompt.reshape(N_CTX_SEQ, T_CTX, D), y_sample.reshape(N_LAT_SEQ, T_LAT, D),
            outs["na_k"], outs["na_v"], outs["diff_k"], outs["diff_v"], outs["mla_ckv"], outs["mla_kpe"])
```

```python
import functools
import math

import numpy as np
import jax
import jax.numpy as jnp
from jax import lax
from jax.experimental import pallas as pl
from jax.experimental.pallas import tpu as pltpu

F32 = jnp.float32
BF16 = jnp.bfloat16

D = 1024
N_CTX_SEQ = 32
T_CTX = 256
N_LAT_SEQ = 2
T_LAT = 2048
PAST = 512
M_CTX = N_CTX_SEQ * T_CTX
M_LAT = N_LAT_SEQ * T_LAT
M_TOK = M_CTX + M_LAT
GRID_W = 64
NORM_EPS = 1e-6
NEG_INF = -1e30
ROPE_BASE = 10000.0
POOL_WINDOWS = (2, 4, 8, 16)
NA_WIN_R = 8
NA_WIN_C = 16
DIFF_SUBLN_EPS = 1e-5
MLA_HEADS = 8
MLA_NOPE = 128
MLA_ROPE = 64
MLA_KV_LORA = 256
MLA_Q_LORA = 512
FFN_HIDDEN = 2816
N_EXPERTS = 8

LOG2_E = math.log2(math.e)
LANE = 128
TM = 1024
TH = 256
SUB = 256
VMEM_LIMIT = 56 * 1024 * 1024

TMF = 2048
MOE_ROWS = 2 * M_TOK + N_EXPERTS * TMF
MOE_TILES = MOE_ROWS // TMF


def _cparams(sem):
    return pltpu.CompilerParams(dimension_semantics=sem, vmem_limit_bytes=VMEM_LIMIT)


def _group_of_tile(i, tm):
    n_ctx = M_CTX // tm
    return jnp.where(i < n_ctx, 0, 1 + (i - n_ctx) // (T_LAT // tm))


def _dot(a, b):
    return jnp.dot(a, b, preferred_element_type=F32)


def _dot_nt(a, b):
    return lax.dot_general(a, b, (((1,), (1,)), ((), ())), preferred_element_type=F32)


def _rms(x, g, eps):
    return x * lax.rsqrt(jnp.mean(x * x, axis=-1, keepdims=True) + eps) * g


def _ada_kernel(c_ref, w_ref, b_ref, o_ref):
    c = c_ref[...]
    s = (c * jax.nn.sigmoid(c)).astype(BF16)
    o_ref[...] = _dot(s, w_ref[...].astype(BF16)) + b_ref[...]


def _ada(cond8, ada_w, ada_b):
    depth = ada_w.shape[0]
    tn = 1536
    return pl.pallas_call(
        _ada_kernel,
        grid=(depth, 6 * D // tn),
        in_specs=[
            pl.BlockSpec((8, D), lambda l, j: (0, 0)),
            pl.BlockSpec((None, D, tn), lambda l, j: (l, 0, j)),
            pl.BlockSpec((None, 1, tn), lambda l, j: (l, 0, j)),
        ],
        out_specs=pl.BlockSpec((None, 8, tn), lambda l, j: (l, 0, j)),
        out_shape=jax.ShapeDtypeStruct((depth, 8, 6 * D), F32),
        compiler_params=_cparams(("parallel", "parallel")),
        name="ada",
    )(cond8, ada_w, ada_b.reshape(depth, 1, 6 * D))


def _modrows_kernel(x_ref, g_ref, sh_ref, sc_ref, o_ref, *, eps):
    x = _rms(x_ref[...], g_ref[...], eps)
    o_ref[...] = x * (1.0 + sc_ref[...]) + sh_ref[...]


def _modrows(x, g, mods, sh_c, sc_c, *, rows, row0, width, col_blk=0, tm=TM, eps=NORM_EPS,
             grouped=True):
    r0 = row0 // tm
    if grouped:
        grp = lambda i: _group_of_tile(i + r0, tm)
    else:
        grp = lambda i: 0
    return pl.pallas_call(
        functools.partial(_modrows_kernel, eps=eps),
        grid=(rows // tm,),
        in_specs=[
            pl.BlockSpec((tm, width), lambda i: (i + r0, col_blk)),
            pl.BlockSpec((1, width), lambda i: (0, 0)),
            pl.BlockSpec((None, None, 1, width), lambda i: (grp(i), sh_c, 0, 0)),
            pl.BlockSpec((None, None, 1, width), lambda i: (grp(i), sc_c, 0, 0)),
        ],
        out_specs=pl.BlockSpec((tm, width), lambda i: (i, 0)),
        out_shape=jax.ShapeDtypeStruct((rows, width), F32),
        compiler_params=_cparams(("parallel",)),
        name="modrows",
    )(x, g.reshape(1, width), mods, mods)


def _linear_kernel(*refs, prologue, eps, epilogue, split, tm, tn, planes):
    it = iter(refs)
    x_ref = next(it)
    x_lat_ref = next(it) if split else None
    w_ref = next(it)
    g_ref = next(it) if prologue != "none" else None
    sh_ref = next(it) if prologue == "mod" else None
    sc_ref = next(it) if prologue == "mod" else None
    res_ref = next(it) if epilogue == "resid" else None
    gate_ref = next(it) if epilogue == "resid" else None
    o_ref = next(it)
    wb_ref = next(it)
    n = w_ref.shape[1]

    @pl.when(pl.program_id(0) == 0)
    def _():
        wb_ref[...] = w_ref[...].astype(BF16)

    def run(src_ref):
        x = src_ref[...]
        if prologue != "none":
            x = _rms(x, g_ref[...], eps)
        if prologue == "mod":
            x = x * (1.0 + sc_ref[...]) + sh_ref[...]
        xb = x.astype(BF16)
        for p in range(n // tn):
            cols = slice(p * tn, (p + 1) * tn)
            acc = _dot(xb, wb_ref[:, cols])
            if epilogue == "resid":
                acc = res_ref[:, cols] + gate_ref[:, cols] * acc
            if planes:
                o_ref[p] = acc
            else:
                o_ref[:, cols] = acc

    if split:
        is_ctx = pl.program_id(0) < M_CTX // tm
        pl.when(is_ctx)(lambda: run(x_ref))
        pl.when(jnp.logical_not(is_ctx))(lambda: run(x_lat_ref))
    else:
        run(x_ref)


TML = 512


def _linear(x, w, *, tn, k=None, x_col=0, norm_g=None, eps=NORM_EPS, mods=None, sh_c=None, sc_c=None,
            resid=None, gate_c=None, planes=False, tm=TML, name="linear"):
    split = isinstance(x, tuple)
    n = w.shape[1]
    prologue = "none" if norm_g is None else ("mod" if sh_c is not None else "norm")
    epilogue = "none" if resid is None else "resid"
    grp = lambda i: _group_of_tile(i, tm)
    if split:
        x_ctx, x_lat = x
        m = x_ctx.shape[0] + x_lat.shape[0]
        k = x_ctx.shape[1]
        n_ctx = x_ctx.shape[0] // tm
        args = [x_ctx, x_lat]
        in_specs = [
            pl.BlockSpec((tm, k), lambda i: (jnp.minimum(i, n_ctx - 1), 0)),
            pl.BlockSpec((tm, k), lambda i: (jnp.maximum(i - n_ctx, 0), 0)),
        ]
    else:
        m = x.shape[0]
        k = x.shape[1] if k is None else k
        args = [x]
        in_specs = [pl.BlockSpec((tm, k), lambda i: (i, x_col))]
    args.append(w)
    in_specs.append(pl.BlockSpec((k, n), lambda i: (0, 0)))
    if prologue != "none":
        args.append(norm_g.reshape(1, k))
        in_specs.append(pl.BlockSpec((1, k), lambda i: (0, 0)))
    if prologue == "mod":
        args += [mods, mods]
        in_specs += [
            pl.BlockSpec((None, None, 1, k), lambda i: (grp(i), sh_c, 0, 0)),
            pl.BlockSpec((None, None, 1, k), lambda i: (grp(i), sc_c, 0, 0)),
        ]
    if epilogue == "resid":
        args += [resid, mods]
        in_specs += [
            pl.BlockSpec((tm, n), lambda i: (i, 0)),
            pl.BlockSpec((None, None, 1, n), lambda i: (grp(i), gate_c, 0, 0)),
        ]
    if planes:
        out_spec = pl.BlockSpec((n // tn, tm, tn), lambda i: (0, i, 0))
        out_shape = jax.ShapeDtypeStruct((n // tn, m, tn), F32)
    else:
        out_spec = pl.BlockSpec((tm, n), lambda i: (i, 0))
        out_shape = jax.ShapeDtypeStruct((m, n), F32)
    return pl.pallas_call(
        functools.partial(_linear_kernel, prologue=prologue, eps=eps, epilogue=epilogue, split=split, tm=tm,
                          tn=tn, planes=planes),
        grid=(m // tm,),
        in_specs=in_specs,
        out_specs=out_spec,
        out_shape=out_shape,
        scratch_shapes=[pltpu.VMEM((k, n), BF16)],
        compiler_params=_cparams(("arbitrary",)),
        name=name,
    )(*args)


POOL_ROWS = 2048


def _pool_kernel(h_ref, x_ref, w_ref, ps_ref, gate_ref, o_ref):
    i = pl.program_id(0)
    g = pl.program_id(1)
    seq = jnp.where(i < M_CTX // POOL_ROWS, T_CTX, T_LAT)
    t = lax.broadcasted_iota(jnp.int32, (POOL_ROWS, 1), 0) & (seq - 1)
    for gi, win in enumerate(POOL_WINDOWS):
        @pl.when(g == gi)
        def _(win=win):
            h = h_ref[...]
            acc = jnp.zeros_like(h)
            for d in range(-(win // 2), win - win // 2):
                sh = h if d == 0 else pltpu.roll(h, (-d) % POOL_ROWS, 0)
                ok = (t + d >= 0) & (t + d < seq)
                acc = acc + jnp.where(ok, sh, 0.0)
            cnt = jnp.minimum(t + (win - win // 2), seq) - jnp.maximum(t - win // 2, 0)
            pooled = acc / cnt.astype(F32)
            y = _dot((pooled - h).astype(BF16), w_ref[...].astype(BF16)) * ps_ref[...]
            o_ref[...] = x_ref[...] + gate_ref[...] * y


def _pool(h, x, pool_w, pool_scale, mods, gate_c):
    gw = D // len(POOL_WINDOWS)
    grp = lambda i: _group_of_tile(i, POOL_ROWS)
    return pl.pallas_call(
        _pool_kernel,
        grid=(M_TOK // POOL_ROWS, len(POOL_WINDOWS)),
        in_specs=[
            pl.BlockSpec((POOL_ROWS, gw), lambda i, g: (i, g)),
            pl.BlockSpec((POOL_ROWS, gw), lambda i, g: (i, g)),
            pl.BlockSpec((None, gw, gw), lambda i, g: (g, 0, 0)),
            pl.BlockSpec((1, gw), lambda i, g: (0, g)),
            pl.BlockSpec((None, None, 1, gw), lambda i, g: (grp(i), gate_c, 0, g)),
        ],
        out_specs=pl.BlockSpec((POOL_ROWS, gw), lambda i, g: (i, g)),
        out_shape=jax.ShapeDtypeStruct((M_TOK, D), F32),
        compiler_params=_cparams(("parallel", "parallel")),
        name="pool",
    )(h, x, pool_w, pool_scale.reshape(1, D), mods)


NJ = FFN_HIDDEN // TH
TN2 = 256
NN = D // TN2


def _swiglu_kernel(*refs, moe):
    if moe:
        te_ref, tv_ref, x_ref, wg_ref, wu_ref, wd_ref, o_ref, xb_ref, h_ref = refs
    else:
        x_ref, wg_ref, wu_ref, wd_ref, g_ref, sh_ref, sc_ref, res_ref, gate_ref, o_ref, xb_ref, h_ref = refs
    j = pl.program_id(1)
    n_valid = tv_ref[pl.program_id(0)] if moe else TMF

    def per_sub_block(fn, fn_skipped=None):
        sub_blocks = [pl.ds(s * SUB, SUB) for s in range(TMF // SUB)]

        def all_rows():
            for rows in sub_blocks:
                fn(rows)

        def valid_rows_only():
            for s, rows in enumerate(sub_blocks):
                pl.when(s * SUB < n_valid)(functools.partial(fn, rows))
                if fn_skipped is not None:
                    pl.when(s * SUB >= n_valid)(functools.partial(fn_skipped, rows))

        if moe:
            pl.when(n_valid == TMF)(all_rows)
            pl.when(n_valid < TMF)(valid_rows_only)
        else:
            all_rows()

    @pl.when(j == 0)
    def _():
        def cast(rows):
            x = x_ref[rows, :]
            if not moe:
                x = _rms(x, g_ref[...], NORM_EPS) * (1.0 + sc_ref[...]) + sh_ref[...]
            xb_ref[rows, :] = x.astype(BF16)

        per_sub_block(cast)

    @pl.when(j < NJ)
    def _():
        wg = wg_ref[...].astype(BF16)
        wu = wu_ref[...].astype(BF16)

        def up(rows):
            xb = xb_ref[rows, :]
            a = _dot(xb, wg)
            u = _dot(xb, wu)
            h_ref[j, rows, :] = (a * jax.nn.sigmoid(a) * u).astype(BF16)

        per_sub_block(up)

    @pl.when(j >= NJ)
    def _():
        wd = wd_ref[...].astype(BF16)

        def down(rows):
            hcat = jnp.concatenate([h_ref[jj, rows, :] for jj in range(NJ)], axis=1)
            y = _dot(hcat, wd)
            if moe:
                o_ref[rows, :] = y
            else:
                o_ref[rows, :] = res_ref[rows, :] + gate_ref[...] * y

        def skipped(rows):
            o_ref[rows, :] = jnp.zeros((SUB, TN2), F32)

        per_sub_block(down, skipped)


_SWIGLU_SCRATCH = [pltpu.VMEM((TMF, D), BF16), pltpu.VMEM((NJ, TMF, TH), BF16)]


def _ffn_dense(x, wg, wu, wd, f, norm_g, mods, sh_c, sc_c, gate_c):
    grp = lambda i: _group_of_tile(i, TMF)
    jh = lambda j: jnp.minimum(j, NJ - 1)
    jn = lambda j: jnp.maximum(j - NJ, 0)
    mod_spec = lambda c: pl.BlockSpec((None, None, 1, D), lambda i, j: (grp(i), c, 0, 0))
    return pl.pallas_call(
        functools.partial(_swiglu_kernel, moe=False),
        grid=(M_TOK // TMF, NJ + NN),
        in_specs=[
            pl.BlockSpec((TMF, D), lambda i, j: (i, 0)),
            pl.BlockSpec((None, D, TH), lambda i, j: (f, 0, jh(j))),
            pl.BlockSpec((None, D, TH), lambda i, j: (f, 0, jh(j))),
            pl.BlockSpec((None, FFN_HIDDEN, TN2), lambda i, j: (f, 0, jn(j))),
            pl.BlockSpec((1, D), lambda i, j: (0, 0)),
            mod_spec(sh_c), mod_spec(sc_c),
            pl.BlockSpec((TMF, TN2), lambda i, j: (i, jn(j))),
            pl.BlockSpec((None, None, 1, TN2), lambda i, j: (grp(i), gate_c, 0, jn(j))),
        ],
        out_specs=pl.BlockSpec((TMF, TN2), lambda i, j: (i, jn(j))),
        out_shape=jax.ShapeDtypeStruct((M_TOK, D), F32),
        scratch_shapes=_SWIGLU_SCRATCH,
        compiler_params=_cparams(("parallel", "arbitrary")),
        name="ffn_dense",
    )(x, wg, wu, wd, norm_g.reshape(1, D), mods, mods, x, mods)


def _ffn_moe(xs, wg, wu, wd, f, tile_expert, tile_valid):
    jh = lambda i, j, tv: jnp.where(tv[i] > 0, jnp.minimum(j, NJ - 1), NJ - 1)
    jn = lambda i, j, tv: jnp.where(tv[i] > 0, jnp.maximum(j - NJ, 0), NN - 1)
    return pl.pallas_call(
        functools.partial(_swiglu_kernel, moe=True),
        grid_spec=pltpu.PrefetchScalarGridSpec(
            num_scalar_prefetch=2,
            grid=(MOE_TILES, NJ + NN),
            in_specs=[
                pl.BlockSpec((TMF, D), lambda i, j, te, tv: (i, 0)),
                pl.BlockSpec((None, None, D, TH), lambda i, j, te, tv: (f, te[i], 0, jh(i, j, tv))),
                pl.BlockSpec((None, None, D, TH), lambda i, j, te, tv: (f, te[i], 0, jh(i, j, tv))),
                pl.BlockSpec((None, None, FFN_HIDDEN, TN2), lambda i, j, te, tv: (f, te[i], 0, jn(i, j, tv))),
            ],
            out_specs=pl.BlockSpec((TMF, TN2), lambda i, j, te, tv: (i, jnp.maximum(j - NJ, 0))),
            scratch_shapes=_SWIGLU_SCRATCH,
        ),
        out_shape=jax.ShapeDtypeStruct((MOE_ROWS, D), F32),
        compiler_params=_cparams(("parallel", "arbitrary")),
        name="ffn_moe",
    )(tile_expert, tile_valid, xs, wg, wu, wd)


def _router_kernel(x_ref, g_ref, sh_ref, sc_ref, r_ref, h_ref, idx_ref, gates_ref):
    h = _rms(x_ref[...], g_ref[...], NORM_EPS) * (1.0 + sc_ref[...]) + sh_ref[...]
    h_ref[...] = h
    w = r_ref[...]
    h_hi = h.astype(BF16)
    h_lo = (h - h_hi.astype(F32)).astype(BF16)
    w_hi = w.astype(BF16)
    w_lo = (w - w_hi.astype(F32)).astype(BF16)
    logits = _dot(h_hi, w_hi) + (_dot(h_lo, w_hi) + _dot(h_hi, w_lo))
    col = lax.broadcasted_iota(jnp.int32, logits.shape, 1)
    colf = col.astype(F32)
    lg = jnp.where(col < N_EXPERTS, logits, -jnp.inf)
    m1 = jnp.max(lg, axis=-1, keepdims=True)
    i1 = jnp.min(jnp.where(lg == m1, colf, float(LANE)), axis=-1, keepdims=True)
    lg2 = jnp.where(colf == i1, -jnp.inf, lg)
    m2 = jnp.max(lg2, axis=-1, keepdims=True)
    i2 = jnp.min(jnp.where(lg2 == m2, colf, float(LANE)), axis=-1, keepdims=True)
    i1 = i1.astype(jnp.int32)
    i2 = i2.astype(jnp.int32)
    e = jnp.exp(m2 - m1)
    g1 = 1.0 / (1.0 + e)
    g2 = e / (1.0 + e)
    idx_ref[...] = jnp.where(col == 0, i1, jnp.where(col == 1, i2, 0))
    gates_ref[...] = jnp.where(col == 0, g1, jnp.where(col == 1, g2, 0.0))


def _router(x, norm_g, mods, sh_c, sc_c, router):
    r_pad = jnp.pad(router, ((0, 0), (0, LANE - N_EXPERTS)))
    grp = lambda i: _group_of_tile(i, TM)
    return pl.pallas_call(
        _router_kernel,
        grid=(M_TOK // TM,),
        in_specs=[
            pl.BlockSpec((TM, D), lambda i: (i, 0)),
            pl.BlockSpec((1, D), lambda i: (0, 0)),
            pl.BlockSpec((None, None, 1, D), lambda i: (grp(i), sh_c, 0, 0)),
            pl.BlockSpec((None, None, 1, D), lambda i: (grp(i), sc_c, 0, 0)),
            pl.BlockSpec((D, LANE), lambda i: (0, 0)),
        ],
        out_specs=[
            pl.BlockSpec((TM, D), lambda i: (i, 0)),
            pl.BlockSpec((TM, LANE), lambda i: (i, 0)),
            pl.BlockSpec((TM, LANE), lambda i: (i, 0)),
        ],
        out_shape=[
            jax.ShapeDtypeStruct((M_TOK, D), F32),
            jax.ShapeDtypeStruct((M_TOK, LANE), jnp.int32),
            jax.ShapeDtypeStruct((M_TOK, LANE), F32),
        ],
        compiler_params=_cparams(("parallel",)),
        name="router",
    )(x, norm_g.reshape(1, D), mods, mods, r_pad)


def _route_plan(e1, e2):
    e = jnp.concatenate([e1, e2])
    oh = (e[:, None] == jnp.arange(N_EXPERTS, dtype=jnp.int32)[None, :]).astype(jnp.int32)
    csum = jnp.cumsum(oh, axis=0)
    rank = jnp.sum((csum - oh) * oh, axis=1)
    counts = csum[-1]
    padded = ((counts + TMF - 1) // TMF) * TMF
    ends = jnp.cumsum(padded)
    starts = ends - padded
    pos = jnp.sum(oh * starts[None, :], axis=1) + rank
    tile_start = jnp.arange(MOE_TILES, dtype=jnp.int32) * TMF
    te = jnp.minimum(jnp.sum((tile_start[:, None] >= ends[None, :]).astype(jnp.int32), axis=1),
                     N_EXPERTS - 1)
    used = tile_start < ends[-1]
    valid = jnp.where(used, jnp.clip(counts[te] - (tile_start - starts[te]), 0, TMF), 0)
    last_e = te[jnp.maximum(ends[-1] // TMF - 1, 0)]
    te = jnp.where(used, te, last_e)
    sub_off = jnp.arange(TMF // SUB, dtype=jnp.int32) * SUB
    sub_valid = jnp.clip(valid[:, None] - sub_off[None, :], 0, SUB).reshape(-1)
    return pos.astype(jnp.int32), te.astype(jnp.int32), valid.astype(jnp.int32), sub_valid.astype(jnp.int32)


DMA_UNROLL = 8
DISPATCH_ROWS = 1024
COMBINE_ROWS = 512


def _dispatch_kernel(pos_ref, sv_ref, h_ref, xs_hbm, zero_ref, sem, zsem):
    i = pl.program_id(0)

    @pl.when(i == 0)
    def _():
        zero_ref[...] = jnp.zeros_like(zero_ref)

        def zero_copy(b):
            return pltpu.make_async_copy(zero_ref, xs_hbm.at[pl.ds(pl.multiple_of(b * SUB, SUB), SUB)], zsem)

        def start(b, c):
            pl.when(sv_ref[b] < SUB)(lambda: zero_copy(b).start())
            return c

        def wait(b, c):
            pl.when(sv_ref[b] < SUB)(lambda: zero_copy(b).wait())
            return c

        lax.fori_loop(0, MOE_ROWS // SUB, start, 0)
        lax.fori_loop(0, MOE_ROWS // SUB, wait, 0)

    def issue(r, c):
        p1 = pos_ref[i * DISPATCH_ROWS + r]
        p2 = pos_ref[M_TOK + i * DISPATCH_ROWS + r]
        pltpu.make_async_copy(h_ref.at[pl.ds(r, 1)], xs_hbm.at[pl.ds(p1, 1)], sem).start()
        pltpu.make_async_copy(h_ref.at[pl.ds(r, 1)], xs_hbm.at[pl.ds(p2, 1)], sem).start()
        return c

    lax.fori_loop(0, DISPATCH_ROWS, issue, 0, unroll=DMA_UNROLL)
    pltpu.make_async_copy(h_ref, xs_hbm.at[pl.ds(0, DISPATCH_ROWS)], sem).wait()
    pltpu.make_async_copy(h_ref, xs_hbm.at[pl.ds(0, DISPATCH_ROWS)], sem).wait()


def _moe_dispatch(h, pos, sub_valid):
    return pl.pallas_call(
        _dispatch_kernel,
        grid_spec=pltpu.PrefetchScalarGridSpec(
            num_scalar_prefetch=2,
            grid=(M_TOK // DISPATCH_ROWS,),
            in_specs=[pl.BlockSpec((DISPATCH_ROWS, D), lambda i, p, z: (i, 0))],
            out_specs=pl.BlockSpec(memory_space=pl.ANY),
            scratch_shapes=[pltpu.VMEM((SUB, D), F32), pltpu.SemaphoreType.DMA, pltpu.SemaphoreType.DMA],
        ),
        out_shape=jax.ShapeDtypeStruct((MOE_ROWS, D), F32),
        compiler_params=_cparams(("arbitrary",)),
        name="moe_dispatch",
    )(pos, sub_valid, h)


def _combine_kernel(pos_ref, y_hbm, x_ref, tg_ref, gate_ref, o_ref, y_ref, sem):
    i = pl.program_id(0)
    slot = i % 2

    def picked(k, s):
        return pltpu.make_async_copy(y_hbm.at[pl.ds(0, COMBINE_ROWS)], y_ref.at[s, k], sem.at[s, k])

    def gather_tile(t, s):
        def issue(r, c):
            for k in range(2):
                p = pos_ref[k * M_TOK + t * COMBINE_ROWS + r]
                pltpu.make_async_copy(y_hbm.at[pl.ds(p, 1)], y_ref.at[s, k, pl.ds(r, 1)], sem.at[s, k]).start()
            return c

        lax.fori_loop(0, COMBINE_ROWS, issue, 0, unroll=DMA_UNROLL)

    pl.when(i == 0)(lambda: gather_tile(0, 0))
    pl.when(i + 1 < pl.num_programs(0))(lambda: gather_tile(i + 1, 1 - slot))
    picked(0, slot).wait()
    picked(1, slot).wait()
    tg = tg_ref[...]
    moe_out = tg[:, 0:1] * y_ref[slot, 0] + tg[:, 1:2] * y_ref[slot, 1]
    o_ref[...] = x_ref[...] + gate_ref[...] * moe_out


def _moe_combine(y, pos, x, top_gates, mods, gate_c):
    grp = lambda i: _group_of_tile(i, COMBINE_ROWS)
    return pl.pallas_call(
        _combine_kernel,
        grid_spec=pltpu.PrefetchScalarGridSpec(
            num_scalar_prefetch=1,
            grid=(M_TOK // COMBINE_ROWS,),
            in_specs=[
                pl.BlockSpec(memory_space=pl.ANY),
                pl.BlockSpec((COMBINE_ROWS, D), lambda i, p: (i, 0)),
                pl.BlockSpec((COMBINE_ROWS, LANE), lambda i, p: (i, 0)),
                pl.BlockSpec((None, None, 1, D), lambda i, p: (grp(i), gate_c, 0, 0)),
            ],
            out_specs=pl.BlockSpec((COMBINE_ROWS, D), lambda i, p: (i, 0)),
            scratch_shapes=[pltpu.VMEM((2, 2, COMBINE_ROWS, D), F32), pltpu.SemaphoreType.DMA((2, 2))],
        ),
        out_shape=jax.ShapeDtypeStruct((M_TOK, D), F32),
        compiler_params=_cparams(("arbitrary",)),
        name="moe_combine",
    )(pos, y, x, top_gates, mods)


def _moe(x, norm_g, mods, sh_c, sc_c, gate_c, router, wg, wu, wd, f):
    h, idx, gates = _router(x, norm_g, mods, sh_c, sc_c, router)
    pos, te, tv, sub_valid = _route_plan(idx[:, 0], idx[:, 1])
    xs = _moe_dispatch(h, pos, sub_valid)
    ys = _ffn_moe(xs, wg, wu, wd, f, te, tv)
    return _moe_combine(ys, pos, x, gates, mods, gate_c)


def _rope_tables(width, lo, hi):
    pos = jnp.arange(T_LAT)
    rows = (pos // GRID_W).astype(F32)
    cols = (pos % GRID_W).astype(F32)
    n_freq = 16
    inv = ROPE_BASE ** (-jnp.arange(n_freq, dtype=F32) / n_freq)
    ang = jnp.concatenate([rows[:, None] * inv[None, :], cols[:, None] * inv[None, :]], axis=-1)
    cos, sin = jnp.cos(ang), jnp.sin(ang)
    lane = np.arange(width)
    active = (lane >= lo) & (lane < hi)
    reps = width // 64
    c = jnp.tile(jnp.concatenate([cos, cos], axis=-1), (1, reps))
    s = jnp.tile(jnp.concatenate([-sin, sin], axis=-1), (1, reps))
    return jnp.where(active[None, :], c, 1.0), jnp.where(active[None, :], s, 0.0)


def _rope_kernel(x_ref, c_ref, s_ref, o_ref, *, width):
    lane = lax.broadcasted_iota(jnp.int32, (1, width), 1)
    first = (lane & 63) < 32
    c = c_ref[...]
    s = s_ref[...]
    for j in range(x_ref.shape[-1] // width):
        cols = slice(j * width, (j + 1) * width)
        x = x_ref[:, cols]
        partner = jnp.where(first, pltpu.roll(x, width - 32, 1), pltpu.roll(x, 32, 1))
        o_ref[:, cols] = x * c + partner * s


def _rope(x, c, s, *, planes, row0, width, col0=0, ncol=1, tr=512):
    r0 = row0 // tr
    nt = T_LAT // tr
    if x.ndim == 3:
        ncol = x.shape[2] // width
        x_spec = pl.BlockSpec((None, tr, ncol * width), lambda p, i: (p, i + r0, 0))
    else:
        x_spec = pl.BlockSpec((tr, ncol * width), lambda p, i: (i + r0, col0 // ncol))
    return pl.pallas_call(
        functools.partial(_rope_kernel, width=width),
        grid=(planes, M_LAT // tr),
        in_specs=[
            x_spec,
            pl.BlockSpec((tr, width), lambda p, i: (i % nt, 0)),
            pl.BlockSpec((tr, width), lambda p, i: (i % nt, 0)),
        ],
        out_specs=pl.BlockSpec((None, tr, ncol * width), lambda p, i: (p, i, 0)),
        out_shape=jax.ShapeDtypeStruct((planes, M_LAT, ncol * width), F32),
        compiler_params=_cparams(("parallel", "parallel")),
        name="rope",
    )(x, c, s)


def _attn_kernel(*refs, kind, hb, nseg, scale, lam_init):
    it = iter(refs)
    q_ref = next(it)
    k_refs = [next(it) for _ in range(nseg)]
    kpe_refs = [next(it) for _ in range(nseg)] if kind == "mla" else None
    v_refs = [next(it) for _ in range(nseg)] if kind != "mla" else k_refs
    lam_ref = next(it) if kind == "diff" else None
    subln_ref = next(it) if kind == "diff" else None
    o_ref = next(it)

    lane = lax.broadcasted_iota(jnp.int32, (1, LANE), 1)
    lo_half = lane < 64
    log2_scale = scale * LOG2_E

    def softmax_parts(s_list):
        t_list = [s * log2_scale for s in s_list]
        m = functools.reduce(jnp.maximum, [jnp.max(t, axis=-1, keepdims=True) for t in t_list])
        p_list = [jnp.exp2(t - m) for t in t_list]
        l = functools.reduce(jnp.add, [jnp.sum(p, axis=-1, keepdims=True) for p in p_list])
        return p_list, 1.0 / l

    def attend(s_list, v_list):
        p_list, inv_l = softmax_parts(s_list)
        return functools.reduce(jnp.add, [_dot(p.astype(BF16), v) for p, v in zip(p_list, v_list)]) * inv_l

    for hh in range(hb):
        if kind == "mla":
            q2 = q_ref[:, hh * 256:(hh + 1) * 256].astype(BF16)
            s_list = []
            for kr, pr in zip(k_refs, kpe_refs):
                kcat = jnp.concatenate([kr[:, hh * 256:hh * 256 + 128], pr[...]], axis=1).astype(BF16)
                s_list.append(_dot_nt(q2, kcat))
            v2 = [vr[:, hh * 256 + 128:hh * 256 + 256].astype(BF16) for vr in v_refs]
            o_ref[:, hh * LANE:(hh + 1) * LANE] = attend(s_list, v2)
            continue

        sl = slice(hh * LANE, (hh + 1) * LANE)
        q2 = q_ref[:, sl]
        k2 = [kr[:, sl].astype(BF16) for kr in k_refs]
        v2 = [vr[:, sl].astype(BF16) for vr in v_refs]
        tq = q2.shape[0]
        q_heads = [jnp.where(lo_half, q2, 0.0), jnp.where(lo_half, 0.0, q2)]
        if hb > 1:
            qs = jnp.concatenate(q_heads, axis=0).astype(BF16)
            o2 = attend([_dot_nt(qs, kk) for kk in k2], v2)
            o_heads = [o2[:tq], o2[tq:]]
            if kind == "diff":
                o_diff = o_heads[0] - lam_ref[...][:, :1] * o_heads[1]
        elif kind == "na":
            o_heads = [attend([_dot_nt(qh.astype(BF16), kk) for kk in k2], v2) for qh in q_heads]
        else:
            (p0, il0), (p1, il1) = [softmax_parts([_dot_nt(qh.astype(BF16), kk) for kk in k2]) for qh in q_heads]
            lam = lam_ref[...][:, :1]
            o_diff = functools.reduce(jnp.add, [
                _dot((a * il0 - lam * (b * il1)).astype(BF16), vv) for a, b, vv in zip(p0, p1, v2)])
        if kind == "na":
            o_ref[:, sl] = jnp.where(lo_half, o_heads[0], o_heads[1])
        else:
            o_ref[:, sl] = _rms(o_diff, subln_ref[...], DIFF_SUBLN_EPS) * (1.0 - lam_init)


def _attention(kind, *, grid, q, ks, vs, kpes=None, extra=(), rows, out_spec, hb, scale, lam_init=0.0,
               name="attn"):
    ops = [q] + list(ks) + (list(kpes) if kind == "mla" else []) + (list(vs) if kind != "mla" else [])
    ops += list(extra)
    return pl.pallas_call(
        functools.partial(_attn_kernel, kind=kind, hb=hb, nseg=len(ks), scale=scale, lam_init=lam_init),
        grid=grid,
        in_specs=[s for _, s in ops],
        out_specs=out_spec,
        out_shape=jax.ShapeDtypeStruct((rows, D), F32),
        compiler_params=_cparams(("parallel",) * len(grid)),
        name=name,
    )(*[a for a, _ in ops])


NA_QROWS = 4
NA_KROWS = 12
NA_TQ = NA_QROWS * GRID_W
NA_TK = NA_KROWS * GRID_W


def _na_bias_table(rpb):
    h = rpb.shape[0]
    qc = np.arange(GRID_W)[:, None]
    kc = np.arange(GRID_W)[None, :]
    qs = np.clip(qc - NA_WIN_C // 2, 0, GRID_W - NA_WIN_C)
    col_ok = (kc >= qs) & (kc < qs + NA_WIN_C)
    ic = np.clip(kc - qc + NA_WIN_C - 1, 0, 2 * NA_WIN_C - 2)
    n_ir = 2 * NA_WIN_R - 1
    onehot = (ic.reshape(1, -1) == np.arange(2 * NA_WIN_C - 1)[:, None]).astype(np.float32)
    tab = jnp.einsum("hrm,mx->hrx", rpb.astype(F32) * LOG2_E, jnp.asarray(onehot),
                     precision=lax.Precision.HIGHEST)
    tab = jnp.where(jnp.asarray(col_ok.reshape(1, 1, -1)), tab, NEG_INF).reshape(h, n_ir, GRID_W, GRID_W)
    tab = jnp.concatenate([tab, tab], axis=-1)
    n_rows = T_LAT // GRID_W
    n_g = n_rows // NA_QROWS
    ir_idx = np.zeros((3, NA_QROWS, NA_KROWS), np.int32)
    row_ok = np.zeros((3, NA_QROWS, NA_KROWS), bool)
    for pat, g in enumerate((0, 1, n_g - 1)):
        base = int(np.clip(NA_QROWS * g - NA_QROWS, 0, n_rows - NA_KROWS))
        for qr in range(NA_QROWS):
            r = NA_QROWS * g + qr
            r0 = int(np.clip(r - NA_WIN_R // 2, 0, n_rows - NA_WIN_R))
            for kr in range(NA_KROWS):
                ok = r0 <= base + kr < r0 + NA_WIN_R
                row_ok[pat, qr, kr] = ok
                ir_idx[pat, qr, kr] = (base + kr - r + NA_WIN_R - 1) if ok else 0

    def build(tab_ref, o_ref):
        pat_id = pl.program_id(0)
        lo_half = lax.broadcasted_iota(jnp.int32, (1, LANE), 1) < GRID_W
        masked = jnp.full((GRID_W, LANE), NEG_INF, F32)
        for pat in range(3):
            @pl.when(pat_id == pat)
            def _(pat=pat):
                for qr in range(NA_QROWS):
                    for kp in range(NA_KROWS // 2):
                        halves = [tab_ref[int(ir_idx[pat, qr, kr])] if row_ok[pat, qr, kr] else masked
                                  for kr in (2 * kp, 2 * kp + 1)]
                        o_ref[qr * GRID_W:(qr + 1) * GRID_W, kp * LANE:(kp + 1) * LANE] = jnp.where(
                            lo_half, halves[0], halves[1])

    return pl.pallas_call(
        build,
        grid=(3, h),
        in_specs=[pl.BlockSpec((None, n_ir, GRID_W, LANE), lambda p, hh: (hh, 0, 0, 0))],
        out_specs=pl.BlockSpec((None, None, NA_TQ, NA_TK), lambda p, hh: (p, hh, 0, 0)),
        out_shape=jax.ShapeDtypeStruct((3, h, NA_TQ, NA_TK), F32),
        compiler_params=_cparams(("parallel", "parallel")),
        name="na_bias",
    )(tab)


def _na_lat_kernel(q_ref, kc_ref, vc_ref, k_ref, v_ref, b_ref, o_ref, *, scale):
    g = pl.program_id(2)
    n_rows = T_LAT // GRID_W
    base = jnp.clip(NA_QROWS * g - NA_QROWS, 0, n_rows - NA_KROWS) * GRID_W
    base = pl.multiple_of(base, GRID_W)
    lane = lax.broadcasted_iota(jnp.int32, (1, LANE), 1)
    lo_half = lane < 64
    q2 = q_ref[...]
    kc = kc_ref[...].astype(BF16)
    vc = vc_ref[...].astype(BF16)
    kl = k_ref[pl.ds(base, NA_TK), :].astype(BF16)
    vl = v_ref[pl.ds(base, NA_TK), :].astype(BF16)
    log2_scale = scale * LOG2_E
    outs = []
    for n in range(2):
        qm = jnp.where(lo_half if n == 0 else jnp.logical_not(lo_half), q2, 0.0).astype(BF16)
        s_c = _dot_nt(qm, kc) * log2_scale
        s_l = _dot_nt(qm, kl) * log2_scale + b_ref[n]
        m = jnp.maximum(jnp.max(s_c, axis=-1, keepdims=True), jnp.max(s_l, axis=-1, keepdims=True))
        p_c = jnp.exp2(s_c - m)
        p_l = jnp.exp2(s_l - m)
        l = jnp.sum(p_c, axis=-1, keepdims=True) + jnp.sum(p_l, axis=-1, keepdims=True)
        o = _dot(p_c.astype(BF16), vc) + _dot(p_l.astype(BF16), vl)
        outs.append(o * (1.0 / l))
    o_ref[...] = jnp.where(lo_half, outs[0], outs[1])


def _na_lat(qkv, cache_k, cache_v, bias):
    n_g = T_LAT // NA_TQ
    q_blk0 = M_CTX // NA_TQ
    kv_blk0 = M_CTX // T_LAT
    pat = lambda g: jnp.where(g == 0, 0, jnp.where(g == n_g - 1, 2, 1))
    return pl.pallas_call(
        functools.partial(_na_lat_kernel, scale=0.125),
        grid=(N_LAT_SEQ, D // LANE, n_g),
        in_specs=[
            pl.BlockSpec((None, NA_TQ, LANE), lambda b, h, g: (0, q_blk0 + b * n_g + g, h)),
            pl.BlockSpec((None, PAST, LANE), lambda b, h, g: (b, 0, h)),
            pl.BlockSpec((None, PAST, LANE), lambda b, h, g: (b, 0, h)),
            pl.BlockSpec((None, T_LAT, LANE), lambda b, h, g: (1, kv_blk0 + b, h)),
            pl.BlockSpec((None, T_LAT, LANE), lambda b, h, g: (2, kv_blk0 + b, h)),
            pl.BlockSpec((None, 2, NA_TQ, NA_TK), lambda b, h, g: (pat(g), h, 0, 0)),
        ],
        out_specs=pl.BlockSpec((NA_TQ, LANE), lambda b, h, g: (b * n_g + g, h)),
        out_shape=jax.ShapeDtypeStruct((M_LAT, D), F32),
        compiler_params=_cparams(("parallel", "parallel", "parallel")),
        name="na_latent",
    )(qkv, cache_k, cache_v, qkv, qkv, bias)


def _seq_t_kernel(x_ref, o_ref):
    o_ref[...] = x_ref[...].T


def _seq_transpose(qkv, plane):
    return pl.pallas_call(
        _seq_t_kernel,
        grid=(N_CTX_SEQ,),
        in_specs=[pl.BlockSpec((None, T_CTX, D), lambda b: (plane, b, 0))],
        out_specs=pl.BlockSpec((None, D, T_CTX), lambda b: (b, 0, 0)),
        out_shape=jax.ShapeDtypeStruct((N_CTX_SEQ, D, T_CTX), F32),
        compiler_params=_cparams(("parallel",)),
        name="seq_transpose",
    )(qkv)


def kernel(x_prompt, x_sample, c, c_ctx, cache_na_k, cache_na_v, cache_diff_k, cache_diff_v, cache_mla_ckv, cache_mla_kpe, ada_w, ada_b, norm1, norm2, pool_w, pool_scale, na_qkv, na_rpb, na_out, diff_qkv, diff_lambda, diff_subln, diff_out, mla_qa, mla_qnorm, mla_qb, mla_kva, mla_kvnorm, mla_kvb, mla_out, ffn_gate, ffn_up, ffn_down, moe_router, moe_gate, moe_up, moe_down, final_norm):
    depth = ada_w.shape[0]
    x = jnp.concatenate([x_prompt.reshape(M_CTX, D), x_sample.reshape(M_LAT, D)], axis=0)
    cond8 = jnp.concatenate([c_ctx[None, :], c, jnp.zeros((5, D), F32)], axis=0)
    mods_all = _ada(cond8, ada_w, ada_b)[:, :3].reshape(depth, 3, 6, 1, D)
    outs = {}

    for l in range(depth):
        kind, li = l % 4, l // 4
        mods = mods_all[l]
        if kind == 0:
            h = _modrows(x, norm1[l], mods, 0, 1, rows=M_TOK, row0=0, width=D)
            x = _pool(h, x, pool_w[li], pool_scale[li], mods, 2)
        elif kind == 1:
            qkv = _linear(x, na_qkv[li], tn=D, norm_g=norm1[l], mods=mods, sh_c=0, sc_c=1, planes=True,
                          name="na_qkv")
            to_cache = lambda a: jnp.transpose(a.reshape(N_CTX_SEQ, 1, 16, 64, T_CTX), (0, 1, 4, 2, 3))
            outs["na_k"] = to_cache(_seq_transpose(qkv, 1))
            outs["na_v"] = to_cache(_seq_transpose(qkv, 2))
            seq_spec = lambda p: pl.BlockSpec((None, T_CTX, D), lambda b, p=p: (p, b, 0))
            o_ctx = _attention(
                "na", grid=(N_CTX_SEQ,), q=(qkv, seq_spec(0)), ks=[(qkv, seq_spec(1))], vs=[(qkv, seq_spec(2))],
                rows=M_CTX, out_spec=pl.BlockSpec((T_CTX, D), lambda b: (b, 0)), hb=D // LANE, scale=0.125,
                name="na_ctx")
            bias = _na_bias_table(na_rpb[li])
            o_lat = _na_lat(qkv, cache_na_k[:, li].reshape(N_LAT_SEQ, PAST, D),
                            cache_na_v[:, li].reshape(N_LAT_SEQ, PAST, D), bias)
            x = _linear((o_ctx, o_lat), na_out[li], tn=D, resid=x, mods=mods, gate_c=2, name="na_out")
        elif kind == 2:
            lam_init = 0.8 - 0.6 * math.exp(-0.3 * l)
            lp = diff_lambda[li].astype(F32)
            lam = jnp.exp(jnp.sum(lp[0] * lp[1])) - jnp.exp(jnp.sum(lp[2] * lp[3])) + lam_init
            lam_row = jnp.full((1, LANE), lam, F32)
            subln = diff_subln[li].reshape(1, LANE)
            extra = [(lam_row, pl.BlockSpec((1, LANE), lambda *_: (0, 0))),
                     (subln, pl.BlockSpec((1, LANE), lambda *_: (0, 0)))]
            qkv = _linear(x, diff_qkv[li], tn=D, norm_g=norm1[l], mods=mods, sh_c=0, sc_c=1, planes=True,
                          name="diff_qkv")
            outs["diff_k"] = jnp.transpose(
                _seq_transpose(qkv, 1).reshape(N_CTX_SEQ, 1, 8, 2, 64, T_CTX), (0, 1, 5, 2, 3, 4))
            outs["diff_v"] = qkv[2, :M_CTX].reshape(N_CTX_SEQ, 1, T_CTX, 8, 128)
            seq_spec = lambda p: pl.BlockSpec((None, T_CTX, D), lambda b, p=p: (p, b, 0))
            o_ctx = _attention(
                "diff", grid=(N_CTX_SEQ,), q=(qkv, seq_spec(0)), ks=[(qkv, seq_spec(1))], vs=[(qkv, seq_spec(2))],
                extra=extra, rows=M_CTX, out_spec=pl.BlockSpec((T_CTX, D), lambda b: (b, 0)), hb=D // LANE,
                scale=0.125, lam_init=lam_init, name="diff_ctx")
            rc, rs = _rope_tables(LANE, 0, LANE)
            qk_r = _rope(qkv, rc, rs, planes=2, row0=M_CTX, width=LANE)
            tq = 256
            nq = T_LAT // tq
            ck = cache_diff_k[:, li].reshape(N_LAT_SEQ, PAST, D)
            cv = cache_diff_v[:, li].reshape(N_LAT_SEQ, PAST, D)
            o_lat = _attention(
                "diff", grid=(N_LAT_SEQ, D // LANE, nq),
                q=(qk_r, pl.BlockSpec((None, tq, LANE), lambda b, h, i: (0, b * nq + i, h))),
                ks=[(ck, pl.BlockSpec((None, PAST, LANE), lambda b, h, i: (b, 0, h))),
                    (qk_r, pl.BlockSpec((None, T_LAT, LANE), lambda b, h, i: (1, b, h)))],
                vs=[(cv, pl.BlockSpec((None, PAST, LANE), lambda b, h, i: (b, 0, h))),
                    (qkv, pl.BlockSpec((None, T_LAT, LANE), lambda b, h, i: (2, M_CTX // T_LAT + b, h)))],
                extra=extra,
                rows=M_LAT, out_spec=pl.BlockSpec((tq, LANE), lambda b, h, i: (b * nq + i, h)),
                hb=1, scale=0.125, lam_init=lam_init, name="diff_latent")
            x = _linear((o_ctx, o_lat), diff_out[li], tn=D, resid=x, mods=mods, gate_c=2, name="diff_out")
        else:
            hd = MLA_NOPE + MLA_ROPE
            wqb = mla_qb[li].reshape(MLA_Q_LORA, MLA_HEADS, hd)
            wqb = jnp.pad(wqb, ((0, 0), (0, 0), (0, 256 - hd))).reshape(MLA_Q_LORA, MLA_HEADS * 256)
            w_a = jnp.concatenate(
                [mla_qa[li], jnp.pad(mla_kva[li], ((0, 0), (0, 512 - MLA_KV_LORA - MLA_ROPE)))], axis=1)
            a = _linear(x, w_a, tn=D, norm_g=norm1[l], mods=mods, sh_c=0, sc_c=1, name="mla_a")
            q = _linear(a, wqb, tn=D, k=MLA_Q_LORA, x_col=0, norm_g=mla_qnorm[li], name="mla_q")
            zero_mod = jnp.zeros((1, 2, 1, MLA_KV_LORA), F32)
            ckv = _modrows(a, mla_kvnorm[li], zero_mod, 0, 1, rows=M_TOK, row0=0, width=MLA_KV_LORA, col_blk=2,
                           grouped=False)
            outs["mla_ckv"] = ckv[:M_CTX].reshape(N_CTX_SEQ, 1, T_CTX, MLA_KV_LORA)
            outs["mla_kpe"] = a[:M_CTX, 768:768 + MLA_ROPE].reshape(N_CTX_SEQ, 1, T_CTX, MLA_ROPE)
            ckv_all = jnp.concatenate([ckv, cache_mla_ckv[:, li].reshape(N_LAT_SEQ * PAST, MLA_KV_LORA)], axis=0)
            kvb = _linear(ckv_all, mla_kvb[li], tn=D, tm=512, name="mla_kvb")
            scale = float(hd) ** -0.5
            o_ctx = _attention(
                "mla", grid=(N_CTX_SEQ,),
                q=(q, pl.BlockSpec((T_CTX, 2048), lambda b: (b, 0))),
                ks=[(kvb, pl.BlockSpec((T_CTX, 2048), lambda b: (b, 0)))],
                kpes=[(a, pl.BlockSpec((T_CTX, LANE), lambda b: (b, 6)))],
                vs=None, rows=M_CTX, out_spec=pl.BlockSpec((T_CTX, D), lambda b: (b, 0)), hb=MLA_HEADS,
                scale=scale, name="mla_ctx")
            rc, rs = _rope_tables(256, 128, 192)
            q_r = _rope(q, rc, rs, planes=1, row0=M_CTX, width=256, ncol=MLA_HEADS)[0]
            rc1, rs1 = _rope_tables(LANE, 0, 64)
            kpe_r = _rope(a, rc1, rs1, planes=1, row0=M_CTX, width=LANE, col0=6)[0]
            kpe_c = jnp.pad(cache_mla_kpe[:, li], ((0, 0), (0, 0), (0, LANE - MLA_ROPE)))
            tq = 256
            nq = T_LAT // tq
            o_lat = _attention(
                "mla", grid=(N_LAT_SEQ, MLA_HEADS, nq),
                q=(q_r, pl.BlockSpec((tq, 256), lambda b, h, i: (b * nq + i, h))),
                ks=[(kvb, pl.BlockSpec((PAST, 256), lambda b, h, i: (M_TOK // PAST + b, h))),
                    (kvb, pl.BlockSpec((T_LAT, 256), lambda b, h, i: (M_CTX // T_LAT + b, h)))],
                kpes=[(kpe_c, pl.BlockSpec((None, PAST, LANE), lambda b, h, i: (b, 0, 0))),
                      (kpe_r, pl.BlockSpec((T_LAT, LANE), lambda b, h, i: (b, 0)))],
                vs=None, rows=M_LAT,
                out_spec=pl.BlockSpec((tq, LANE), lambda b, h, i: (b * nq + i, h)),
                hb=1, scale=scale, name="mla_latent")
            x = _linear((o_ctx, o_lat), mla_out[li], tn=D, resid=x, mods=mods, gate_c=2, name="mla_out")

        f = l // 2
        if l % 2 == 0:
            x = _ffn_dense(x, ffn_gate, ffn_up, ffn_down, f, norm2[l], mods, 3, 4, 5)
        else:
            x = _moe(x, norm2[l], mods, 3, 4, 5, moe_router[f], moe_gate, moe_up, moe_down, f)

    zero_mod = jnp.zeros((1, 2, 1, D), F32)
    y_prompt = _modrows(x, final_norm, zero_mod, 0, 1, rows=M_CTX, row0=0, width=D, grouped=False)
    y_sample = _modrows(x, final_norm, zero_mod, 0, 1, rows=M_LAT, row0=M_CTX, width=D, grouped=False)
    return (y_prompt.reshape(N_CTX_SEQ, T_CTX, D), y_sample.reshape(N_LAT_SEQ, T_LAT, D),
            outs["na_k"], outs["na_v"], outs["diff_k"], outs["diff_v"], outs["mla_ckv"], outs["mla_kpe"])
```

```python
import functools
import math

import numpy as np
import jax
import jax.numpy as jnp
from jax import lax
from jax.experimental import pallas as pl
from jax.experimental.pallas import tpu as pltpu

F32 = jnp.float32
BF16 = jnp.bfloat16

D = 1024
N_CTX_SEQ = 32
T_CTX = 256
N_LAT_SEQ = 2
T_LAT = 2048
PAST = 512
M_CTX = N_CTX_SEQ * T_CTX
M_LAT = N_LAT_SEQ * T_LAT
M_TOK = M_CTX + M_LAT
GRID_W = 64
NORM_EPS = 1e-6
NEG_INF = -1e30
ROPE_BASE = 10000.0
POOL_WINDOWS = (2, 4, 8, 16)
NA_WIN_R = 8
NA_WIN_C = 16
DIFF_SUBLN_EPS = 1e-5
MLA_HEADS = 8
MLA_NOPE = 128
MLA_ROPE = 64
MLA_KV_LORA = 256
MLA_Q_LORA = 512
FFN_HIDDEN = 2816
N_EXPERTS = 8

LOG2_E = math.log2(math.e)
LANE = 128
TM = 1024
TH = 256
SUB = 256
VMEM_LIMIT = 56 * 1024 * 1024

TMF = 2048
MOE_ROWS = 2 * M_TOK + N_EXPERTS * TMF
MOE_TILES = MOE_ROWS // TMF


def _cparams(sem):
    return pltpu.CompilerParams(dimension_semantics=sem, vmem_limit_bytes=VMEM_LIMIT)


def _group_of_tile(i, tm):
    n_ctx = M_CTX // tm
    return jnp.where(i < n_ctx, 0, 1 + (i - n_ctx) // (T_LAT // tm))


def _dot(a, b):
    return jnp.dot(a, b, preferred_element_type=F32)


def _dot_nt(a, b):
    return lax.dot_general(a, b, (((1,), (1,)), ((), ())), preferred_element_type=F32)


def _rms(x, g, eps):
    return x * lax.rsqrt(jnp.mean(x * x, axis=-1, keepdims=True) + eps) * g


def _ada_kernel(c_ref, w_ref, b_ref, o_ref):
    c = c_ref[...]
    s = (c * jax.nn.sigmoid(c)).astype(BF16)
    o_ref[...] = _dot(s, w_ref[...].astype(BF16)) + b_ref[...]


def _ada(cond8, ada_w, ada_b):
    depth = ada_w.shape[0]
    tn = 1536
    return pl.pallas_call(
        _ada_kernel,
        grid=(depth, 6 * D // tn),
        in_specs=[
            pl.BlockSpec((8, D), lambda l, j: (0, 0)),
            pl.BlockSpec((None, D, tn), lambda l, j: (l, 0, j)),
            pl.BlockSpec((None, 1, tn), lambda l, j: (l, 0, j)),
        ],
        out_specs=pl.BlockSpec((None, 8, tn), lambda l, j: (l, 0, j)),
        out_shape=jax.ShapeDtypeStruct((depth, 8, 6 * D), F32),
        compiler_params=_cparams(("parallel", "parallel")),
        name="ada",
    )(cond8, ada_w, ada_b.reshape(depth, 1, 6 * D))


def _token_tile(xc_ref, xl_ref, tile, tm):
    return jnp.where(tile < M_CTX // tm, xc_ref[...], xl_ref[...])


def _split_specs(block, tm, col):
    n_ctx = M_CTX // tm
    return [pl.BlockSpec(block, lambda i, *_: (jnp.minimum(i, n_ctx - 1), col(*_))),
            pl.BlockSpec(block, lambda i, *_: (jnp.maximum(i - n_ctx, 0), col(*_)))]


def _modrows_kernel(*refs, eps, split, tm):
    if split:
        xc_ref, xl_ref, g_ref, sh_ref, sc_ref, o_ref = refs
        x = _token_tile(xc_ref, xl_ref, pl.program_id(0), tm)
    else:
        x_ref, g_ref, sh_ref, sc_ref, o_ref = refs
        x = x_ref[...]
    o_ref[...] = _rms(x, g_ref[...], eps) * (1.0 + sc_ref[...]) + sh_ref[...]


def _modrows(x, g, mods, sh_c, sc_c, *, rows, row0, width, col_blk=0, tm=TM, eps=NORM_EPS,
             grouped=True):
    r0 = row0 // tm
    if grouped:
        grp = lambda i: _group_of_tile(i + r0, tm)
    else:
        grp = lambda i: 0
    split = isinstance(x, tuple)
    if split:
        xs, x_specs = list(x), _split_specs((tm, width), tm, lambda: 0)
    else:
        xs, x_specs = [x], [pl.BlockSpec((tm, width), lambda i: (i + r0, col_blk))]
    return pl.pallas_call(
        functools.partial(_modrows_kernel, eps=eps, split=split, tm=tm),
        grid=(rows // tm,),
        in_specs=x_specs + [
            pl.BlockSpec((1, width), lambda i: (0, 0)),
            pl.BlockSpec((None, None, 1, width), lambda i: (grp(i), sh_c, 0, 0)),
            pl.BlockSpec((None, None, 1, width), lambda i: (grp(i), sc_c, 0, 0)),
        ],
        out_specs=pl.BlockSpec((tm, width), lambda i: (i, 0)),
        out_shape=jax.ShapeDtypeStruct((rows, width), F32),
        compiler_params=_cparams(("parallel",)),
        name="modrows",
    )(*xs, g.reshape(1, width), mods, mods)


def _linear_kernel(*refs, prologue, eps, epilogue, split, tm, tn, planes):
    it = iter(refs)
    x_ref = next(it)
    x_lat_ref = next(it) if split else None
    w_ref = next(it)
    g_ref = next(it) if prologue != "none" else None
    sh_ref = next(it) if prologue == "mod" else None
    sc_ref = next(it) if prologue == "mod" else None
    res_ref = next(it) if epilogue == "resid" else None
    gate_ref = next(it) if epilogue == "resid" else None
    o_ref = next(it)
    wb_ref = next(it)
    n = w_ref.shape[1]

    @pl.when(pl.program_id(0) == 0)
    def _():
        wb_ref[...] = w_ref[...].astype(BF16)

    def run(src_ref):
        x = src_ref[...]
        if prologue != "none":
            x = _rms(x, g_ref[...], eps)
        if prologue == "mod":
            x = x * (1.0 + sc_ref[...]) + sh_ref[...]
        xb = x.astype(BF16)
        for p in range(n // tn):
            cols = slice(p * tn, (p + 1) * tn)
            acc = _dot(xb, wb_ref[:, cols])
            if epilogue == "resid":
                acc = res_ref[:, cols] + gate_ref[:, cols] * acc
            if planes:
                o_ref[p] = acc.astype(o_ref.dtype)
            else:
                o_ref[:, cols] = acc.astype(o_ref.dtype)

    if split:
        is_ctx = pl.program_id(0) < M_CTX // tm
        pl.when(is_ctx)(lambda: run(x_ref))
        pl.when(jnp.logical_not(is_ctx))(lambda: run(x_lat_ref))
    else:
        run(x_ref)


TML = 512
LATENT_TQ = 256


def _linear(x, w, *, tn, k=None, x_col=0, norm_g=None, eps=NORM_EPS, mods=None, sh_c=None, sc_c=None,
            resid=None, gate_c=None, planes=False, tm=TML, out_dtype=F32, name="linear"):
    split = isinstance(x, tuple)
    n = w.shape[1]
    prologue = "none" if norm_g is None else ("mod" if sh_c is not None else "norm")
    epilogue = "none" if resid is None else "resid"
    grp = lambda i: _group_of_tile(i, tm)
    if split:
        x_ctx, x_lat = x
        m = x_ctx.shape[0] + x_lat.shape[0]
        k = x_ctx.shape[1]
        n_ctx = x_ctx.shape[0] // tm
        args = [x_ctx, x_lat]
        in_specs = [
            pl.BlockSpec((tm, k), lambda i: (jnp.minimum(i, n_ctx - 1), 0)),
            pl.BlockSpec((tm, k), lambda i: (jnp.maximum(i - n_ctx, 0), 0)),
        ]
    else:
        m = x.shape[0]
        k = x.shape[1] if k is None else k
        args = [x]
        in_specs = [pl.BlockSpec((tm, k), lambda i: (i, x_col))]
    args.append(w)
    in_specs.append(pl.BlockSpec((k, n), lambda i: (0, 0)))
    if prologue != "none":
        args.append(norm_g.reshape(1, k))
        in_specs.append(pl.BlockSpec((1, k), lambda i: (0, 0)))
    if prologue == "mod":
        args += [mods, mods]
        in_specs += [
            pl.BlockSpec((None, None, 1, k), lambda i: (grp(i), sh_c, 0, 0)),
            pl.BlockSpec((None, None, 1, k), lambda i: (grp(i), sc_c, 0, 0)),
        ]
    if epilogue == "resid":
        args += [resid, mods]
        in_specs += [
            pl.BlockSpec((tm, n), lambda i: (i, 0)),
            pl.BlockSpec((None, None, 1, n), lambda i: (grp(i), gate_c, 0, 0)),
        ]
    if planes:
        out_spec = pl.BlockSpec((n // tn, tm, tn), lambda i: (0, i, 0))
        out_shape = jax.ShapeDtypeStruct((n // tn, m, tn), out_dtype)
    else:
        out_spec = pl.BlockSpec((tm, n), lambda i: (i, 0))
        out_shape = jax.ShapeDtypeStruct((m, n), out_dtype)
    return pl.pallas_call(
        functools.partial(_linear_kernel, prologue=prologue, eps=eps, epilogue=epilogue, split=split, tm=tm,
                          tn=tn, planes=planes),
        grid=(m // tm,),
        in_specs=in_specs,
        out_specs=out_spec,
        out_shape=out_shape,
        scratch_shapes=[pltpu.VMEM((k, n), BF16)],
        compiler_params=_cparams(("arbitrary",)),
        name=name,
    )(*args)


POOL_ROWS = 2048


def _pool_kernel(h_ref, xc_ref, xl_ref, w_ref, ps_ref, gate_ref, o_ref):
    i = pl.program_id(0)
    g = pl.program_id(1)
    seq = jnp.where(i < M_CTX // POOL_ROWS, T_CTX, T_LAT)
    t = lax.broadcasted_iota(jnp.int32, (POOL_ROWS, 1), 0) & (seq - 1)

    def ahead(a, k):
        return jnp.where(t + k < seq, pltpu.roll(a, POOL_ROWS - k, 0), 0.0)

    def behind(a, k):
        return jnp.where(t - k >= 0, pltpu.roll(a, k, 0), 0.0)

    for gi, win in enumerate(POOL_WINDOWS):
        @pl.when(g == gi)
        def _(win=win):
            h = h_ref[...]
            fwd, bwd, k = h, h, 1
            while k < win // 2:
                fwd, bwd, k = fwd + ahead(fwd, k), bwd + behind(bwd, k), 2 * k
            acc = fwd + behind(bwd, 1)
            cnt = jnp.minimum(t + (win - win // 2), seq) - jnp.maximum(t - win // 2, 0)
            pooled = acc / cnt.astype(F32)
            y = _dot((pooled - h).astype(BF16), w_ref[...].astype(BF16)) * ps_ref[...]
            o_ref[...] = _token_tile(xc_ref, xl_ref, i, POOL_ROWS) + gate_ref[...] * y


def _pool(h, x_ctx, x_lat, pool_w, pool_scale, mods, gate_c):
    gw = D // len(POOL_WINDOWS)
    grp = lambda i: _group_of_tile(i, POOL_ROWS)
    return pl.pallas_call(
        _pool_kernel,
        grid=(M_TOK // POOL_ROWS, len(POOL_WINDOWS)),
        in_specs=[
            pl.BlockSpec((POOL_ROWS, gw), lambda i, g: (i, g)),
            *_split_specs((POOL_ROWS, gw), POOL_ROWS, lambda g: g),
            pl.BlockSpec((None, gw, gw), lambda i, g: (g, 0, 0)),
            pl.BlockSpec((1, gw), lambda i, g: (0, g)),
            pl.BlockSpec((None, None, 1, gw), lambda i, g: (grp(i), gate_c, 0, g)),
        ],
        out_specs=pl.BlockSpec((POOL_ROWS, gw), lambda i, g: (i, g)),
        out_shape=jax.ShapeDtypeStruct((M_TOK, D), F32),
        compiler_params=_cparams(("parallel", "parallel")),
        name="pool",
    )(h, x_ctx, x_lat, pool_w, pool_scale.reshape(1, D), mods)


NJ = FFN_HIDDEN // TH
TN2 = 256
NN = D // TN2


def _swiglu_kernel(*refs, moe):
    if moe:
        te_ref, tv_ref, x_ref, wg_ref, wu_ref, wd_ref, o_ref, xb_ref, h_ref = refs
    else:
        x_ref, wg_ref, wu_ref, wd_ref, g_ref, sh_ref, sc_ref, res_ref, gate_ref, o_ref, xb_ref, h_ref = refs
    j = pl.program_id(1)
    n_valid = tv_ref[pl.program_id(0)] if moe else TMF

    def per_sub_block(fn, fn_skipped=None):
        sub_blocks = [pl.ds(s * SUB, SUB) for s in range(TMF // SUB)]

        def all_rows():
            for rows in sub_blocks:
                fn(rows)

        def valid_rows_only():
            for s, rows in enumerate(sub_blocks):
                pl.when(s * SUB < n_valid)(functools.partial(fn, rows))
                if fn_skipped is not None:
                    pl.when(s * SUB >= n_valid)(functools.partial(fn_skipped, rows))

        if moe:
            pl.when(n_valid == TMF)(all_rows)
            pl.when(n_valid < TMF)(valid_rows_only)
        else:
            all_rows()

    @pl.when(j == 0)
    def _():
        def cast(rows):
            x = x_ref[rows, :]
            if not moe:
                x = _rms(x, g_ref[...], NORM_EPS) * (1.0 + sc_ref[...]) + sh_ref[...]
            xb_ref[rows, :] = x.astype(BF16)

        per_sub_block(cast)

    @pl.when(j < NJ)
    def _():
        wg = wg_ref[...].astype(BF16)
        wu = wu_ref[...].astype(BF16)

        def up(rows):
            xb = xb_ref[rows, :]
            a = _dot(xb, wg)
            u = _dot(xb, wu)
            h_ref[j, rows, :] = (a * jax.nn.sigmoid(a) * u).astype(BF16)

        per_sub_block(up)

    @pl.when(j >= NJ)
    def _():
        wd = wd_ref[...].astype(BF16)

        def down(rows):
            hcat = jnp.concatenate([h_ref[jj, rows, :] for jj in range(NJ)], axis=1)
            y = _dot(hcat, wd)
            if moe:
                o_ref[rows, :] = y
            else:
                o_ref[rows, :] = res_ref[rows, :] + gate_ref[...] * y

        def skipped(rows):
            o_ref[rows, :] = jnp.zeros((SUB, TN2), F32)

        per_sub_block(down, skipped)


_SWIGLU_SCRATCH = [pltpu.VMEM((TMF, D), BF16), pltpu.VMEM((NJ, TMF, TH), BF16)]


def _ffn_dense(x, wg, wu, wd, f, norm_g, mods, sh_c, sc_c, gate_c):
    grp = lambda i: _group_of_tile(i, TMF)
    jh = lambda j: jnp.minimum(j, NJ - 1)
    jn = lambda j: jnp.maximum(j - NJ, 0)
    mod_spec = lambda c: pl.BlockSpec((None, None, 1, D), lambda i, j: (grp(i), c, 0, 0))
    return pl.pallas_call(
        functools.partial(_swiglu_kernel, moe=False),
        grid=(M_TOK // TMF, NJ + NN),
        in_specs=[
            pl.BlockSpec((TMF, D), lambda i, j: (i, 0)),
            pl.BlockSpec((None, D, TH), lambda i, j: (f, 0, jh(j))),
            pl.BlockSpec((None, D, TH), lambda i, j: (f, 0, jh(j))),
            pl.BlockSpec((None, FFN_HIDDEN, TN2), lambda i, j: (f, 0, jn(j))),
            pl.BlockSpec((1, D), lambda i, j: (0, 0)),
            mod_spec(sh_c), mod_spec(sc_c),
            pl.BlockSpec((TMF, TN2), lambda i, j: (i, jn(j))),
            pl.BlockSpec((None, None, 1, TN2), lambda i, j: (grp(i), gate_c, 0, jn(j))),
        ],
        out_specs=pl.BlockSpec((TMF, TN2), lambda i, j: (i, jn(j))),
        out_shape=jax.ShapeDtypeStruct((M_TOK, D), F32),
        scratch_shapes=_SWIGLU_SCRATCH,
        compiler_params=_cparams(("parallel", "arbitrary")),
        name="ffn_dense",
    )(x, wg, wu, wd, norm_g.reshape(1, D), mods, mods, x, mods)


def _ffn_moe(xs, wg, wu, wd, f, tile_expert, tile_valid):
    jh = lambda i, j, tv: jnp.where(tv[i] > 0, jnp.minimum(j, NJ - 1), NJ - 1)
    jn = lambda i, j, tv: jnp.where(tv[i] > 0, jnp.maximum(j - NJ, 0), NN - 1)
    return pl.pallas_call(
        functools.partial(_swiglu_kernel, moe=True),
        grid_spec=pltpu.PrefetchScalarGridSpec(
            num_scalar_prefetch=2,
            grid=(MOE_TILES, NJ + NN),
            in_specs=[
                pl.BlockSpec((TMF, D), lambda i, j, te, tv: (i, 0)),
                pl.BlockSpec((None, None, D, TH), lambda i, j, te, tv: (f, te[i], 0, jh(i, j, tv))),
                pl.BlockSpec((None, None, D, TH), lambda i, j, te, tv: (f, te[i], 0, jh(i, j, tv))),
                pl.BlockSpec((None, None, FFN_HIDDEN, TN2), lambda i, j, te, tv: (f, te[i], 0, jn(i, j, tv))),
            ],
            out_specs=pl.BlockSpec((TMF, TN2), lambda i, j, te, tv: (i, jnp.maximum(j - NJ, 0))),
            scratch_shapes=_SWIGLU_SCRATCH,
        ),
        out_shape=jax.ShapeDtypeStruct((MOE_ROWS, D), F32),
        compiler_params=_cparams(("parallel", "arbitrary")),
        name="ffn_moe",
    )(tile_expert, tile_valid, xs, wg, wu, wd)


def _router_kernel(x_ref, g_ref, sh_ref, sc_ref, r_ref, h_ref, idx_ref, gates_ref):
    h = _rms(x_ref[...], g_ref[...], NORM_EPS) * (1.0 + sc_ref[...]) + sh_ref[...]
    h_ref[...] = h
    w = r_ref[...]
    h_hi = h.astype(BF16)
    h_lo = (h - h_hi.astype(F32)).astype(BF16)
    w_hi = w.astype(BF16)
    w_lo = (w - w_hi.astype(F32)).astype(BF16)
    logits = _dot(h_hi, w_hi) + (_dot(h_lo, w_hi) + _dot(h_hi, w_lo))
    col = lax.broadcasted_iota(jnp.int32, logits.shape, 1)
    colf = col.astype(F32)
    lg = jnp.where(col < N_EXPERTS, logits, -jnp.inf)
    m1 = jnp.max(lg, axis=-1, keepdims=True)
    i1 = jnp.min(jnp.where(lg == m1, colf, float(LANE)), axis=-1, keepdims=True)
    lg2 = jnp.where(colf == i1, -jnp.inf, lg)
    m2 = jnp.max(lg2, axis=-1, keepdims=True)
    i2 = jnp.min(jnp.where(lg2 == m2, colf, float(LANE)), axis=-1, keepdims=True)
    i1 = i1.astype(jnp.int32)
    i2 = i2.astype(jnp.int32)
    e = jnp.exp(m2 - m1)
    g1 = 1.0 / (1.0 + e)
    g2 = e / (1.0 + e)
    idx_ref[...] = jnp.where(col == 0, i1, jnp.where(col == 1, i2, 0))
    gates_ref[...] = jnp.where(col == 0, g1, jnp.where(col == 1, g2, 0.0))


def _router(x, norm_g, mods, sh_c, sc_c, router):
    r_pad = jnp.pad(router, ((0, 0), (0, LANE - N_EXPERTS)))
    grp = lambda i: _group_of_tile(i, TM)
    return pl.pallas_call(
        _router_kernel,
        grid=(M_TOK // TM,),
        in_specs=[
            pl.BlockSpec((TM, D), lambda i: (i, 0)),
            pl.BlockSpec((1, D), lambda i: (0, 0)),
            pl.BlockSpec((None, None, 1, D), lambda i: (grp(i), sh_c, 0, 0)),
            pl.BlockSpec((None, None, 1, D), lambda i: (grp(i), sc_c, 0, 0)),
            pl.BlockSpec((D, LANE), lambda i: (0, 0)),
        ],
        out_specs=[
            pl.BlockSpec((TM, D), lambda i: (i, 0)),
            pl.BlockSpec((TM, LANE), lambda i: (i, 0)),
            pl.BlockSpec((TM, LANE), lambda i: (i, 0)),
        ],
        out_shape=[
            jax.ShapeDtypeStruct((M_TOK, D), F32),
            jax.ShapeDtypeStruct((M_TOK, LANE), jnp.int32),
            jax.ShapeDtypeStruct((M_TOK, LANE), F32),
        ],
        compiler_params=_cparams(("parallel",)),
        name="router",
    )(x, norm_g.reshape(1, D), mods, mods, r_pad)


def _route_plan(e1, e2):
    e = jnp.concatenate([e1, e2])
    oh = (e[:, None] == jnp.arange(N_EXPERTS, dtype=jnp.int32)[None, :]).astype(jnp.int32)
    csum = jnp.cumsum(oh, axis=0)
    rank = jnp.sum((csum - oh) * oh, axis=1)
    counts = csum[-1]
    padded = ((counts + TMF - 1) // TMF) * TMF
    ends = jnp.cumsum(padded)
    starts = ends - padded
    pos = jnp.sum(oh * starts[None, :], axis=1) + rank
    tile_start = jnp.arange(MOE_TILES, dtype=jnp.int32) * TMF
    te = jnp.minimum(jnp.sum((tile_start[:, None] >= ends[None, :]).astype(jnp.int32), axis=1),
                     N_EXPERTS - 1)
    used = tile_start < ends[-1]
    valid = jnp.where(used, jnp.clip(counts[te] - (tile_start - starts[te]), 0, TMF), 0)
    last_e = te[jnp.maximum(ends[-1] // TMF - 1, 0)]
    te = jnp.where(used, te, last_e)
    sub_off = jnp.arange(TMF // SUB, dtype=jnp.int32) * SUB
    sub_valid = jnp.clip(valid[:, None] - sub_off[None, :], 0, SUB).reshape(-1)
    return pos.astype(jnp.int32), te.astype(jnp.int32), valid.astype(jnp.int32), sub_valid.astype(jnp.int32)


DMA_UNROLL = 8
DISPATCH_ROWS = 1024
COMBINE_ROWS = 512


def _dispatch_kernel(pos_ref, sv_ref, h_ref, xs_hbm, zero_ref, sem, zsem):
    i = pl.program_id(0)

    @pl.when(i == 0)
    def _():
        zero_ref[...] = jnp.zeros_like(zero_ref)

        def zero_copy(b):
            return pltpu.make_async_copy(zero_ref, xs_hbm.at[pl.ds(pl.multiple_of(b * SUB, SUB), SUB)], zsem)

        def start(b, c):
            pl.when(sv_ref[b] < SUB)(lambda: zero_copy(b).start())
            return c

        def wait(b, c):
            pl.when(sv_ref[b] < SUB)(lambda: zero_copy(b).wait())
            return c

        lax.fori_loop(0, MOE_ROWS // SUB, start, 0)
        lax.fori_loop(0, MOE_ROWS // SUB, wait, 0)

    def issue(r, c):
        p1 = pos_ref[i * DISPATCH_ROWS + r]
        p2 = pos_ref[M_TOK + i * DISPATCH_ROWS + r]
        pltpu.make_async_copy(h_ref.at[pl.ds(r, 1)], xs_hbm.at[pl.ds(p1, 1)], sem).start()
        pltpu.make_async_copy(h_ref.at[pl.ds(r, 1)], xs_hbm.at[pl.ds(p2, 1)], sem).start()
        return c

    lax.fori_loop(0, DISPATCH_ROWS, issue, 0, unroll=DMA_UNROLL)
    pltpu.make_async_copy(h_ref, xs_hbm.at[pl.ds(0, DISPATCH_ROWS)], sem).wait()
    pltpu.make_async_copy(h_ref, xs_hbm.at[pl.ds(0, DISPATCH_ROWS)], sem).wait()


def _moe_dispatch(h, pos, sub_valid):
    return pl.pallas_call(
        _dispatch_kernel,
        grid_spec=pltpu.PrefetchScalarGridSpec(
            num_scalar_prefetch=2,
            grid=(M_TOK // DISPATCH_ROWS,),
            in_specs=[pl.BlockSpec((DISPATCH_ROWS, D), lambda i, p, z: (i, 0))],
            out_specs=pl.BlockSpec(memory_space=pl.ANY),
            scratch_shapes=[pltpu.VMEM((SUB, D), F32), pltpu.SemaphoreType.DMA, pltpu.SemaphoreType.DMA],
        ),
        out_shape=jax.ShapeDtypeStruct((MOE_ROWS, D), F32),
        compiler_params=_cparams(("arbitrary",)),
        name="moe_dispatch",
    )(pos, sub_valid, h)


def _combine_kernel(pos_ref, y_hbm, x_ref, tg_ref, gate_ref, o_ref, y_ref, sem):
    i = pl.program_id(0)
    slot = i % 2

    def picked(k, s):
        return pltpu.make_async_copy(y_hbm.at[pl.ds(0, COMBINE_ROWS)], y_ref.at[s, k], sem.at[s, k])

    def gather_tile(t, s):
        def issue(r, c):
            for k in range(2):
                p = pos_ref[k * M_TOK + t * COMBINE_ROWS + r]
                pltpu.make_async_copy(y_hbm.at[pl.ds(p, 1)], y_ref.at[s, k, pl.ds(r, 1)], sem.at[s, k]).start()
            return c

        lax.fori_loop(0, COMBINE_ROWS, issue, 0, unroll=DMA_UNROLL)

    pl.when(i == 0)(lambda: gather_tile(0, 0))
    pl.when(i + 1 < pl.num_programs(0))(lambda: gather_tile(i + 1, 1 - slot))
    picked(0, slot).wait()
    picked(1, slot).wait()
    tg = tg_ref[...]
    moe_out = tg[:, 0:1] * y_ref[slot, 0] + tg[:, 1:2] * y_ref[slot, 1]
    o_ref[...] = x_ref[...] + gate_ref[...] * moe_out


def _moe_combine(y, pos, x, top_gates, mods, gate_c):
    grp = lambda i: _group_of_tile(i, COMBINE_ROWS)
    return pl.pallas_call(
        _combine_kernel,
        grid_spec=pltpu.PrefetchScalarGridSpec(
            num_scalar_prefetch=1,
            grid=(M_TOK // COMBINE_ROWS,),
            in_specs=[
                pl.BlockSpec(memory_space=pl.ANY),
                pl.BlockSpec((COMBINE_ROWS, D), lambda i, p: (i, 0)),
                pl.BlockSpec((COMBINE_ROWS, LANE), lambda i, p: (i, 0)),
                pl.BlockSpec((None, None, 1, D), lambda i, p: (grp(i), gate_c, 0, 0)),
            ],
            out_specs=pl.BlockSpec((COMBINE_ROWS, D), lambda i, p: (i, 0)),
            scratch_shapes=[pltpu.VMEM((2, 2, COMBINE_ROWS, D), F32), pltpu.SemaphoreType.DMA((2, 2))],
        ),
        out_shape=jax.ShapeDtypeStruct((M_TOK, D), F32),
        compiler_params=_cparams(("arbitrary",)),
        name="moe_combine",
    )(pos, y, x, top_gates, mods)


def _moe(x, norm_g, mods, sh_c, sc_c, gate_c, router, wg, wu, wd, f):
    h, idx, gates = _router(x, norm_g, mods, sh_c, sc_c, router)
    pos, te, tv, sub_valid = _route_plan(idx[:, 0], idx[:, 1])
    xs = _moe_dispatch(h, pos, sub_valid)
    ys = _ffn_moe(xs, wg, wu, wd, f, te, tv)
    return _moe_combine(ys, pos, x, gates, mods, gate_c)


def _rope_tables(width, lo, hi):
    pos = jnp.arange(T_LAT)
    rows = (pos // GRID_W).astype(F32)
    cols = (pos % GRID_W).astype(F32)
    n_freq = 16
    inv = ROPE_BASE ** (-jnp.arange(n_freq, dtype=F32) / n_freq)
    ang = jnp.concatenate([rows[:, None] * inv[None, :], cols[:, None] * inv[None, :]], axis=-1)
    cos, sin = jnp.cos(ang), jnp.sin(ang)
    lane = np.arange(width)
    active = (lane >= lo) & (lane < hi)
    reps = width // 64
    c = jnp.tile(jnp.concatenate([cos, cos], axis=-1), (1, reps))
    s = jnp.tile(jnp.concatenate([-sin, sin], axis=-1), (1, reps))
    return jnp.where(active[None, :], c, 1.0), jnp.where(active[None, :], s, 0.0)


def _rope_kernel(x_ref, c_ref, s_ref, o_ref, *, width):
    lane = lax.broadcasted_iota(jnp.int32, (1, width), 1)
    first = (lane & 63) < 32
    c = c_ref[...]
    s = s_ref[...]
    for j in range(x_ref.shape[-1] // width):
        cols = slice(j * width, (j + 1) * width)
        x = x_ref[:, cols]
        partner = jnp.where(first, pltpu.roll(x, width - 32, 1), pltpu.roll(x, 32, 1))
        o_ref[:, cols] = (x * c + partner * s).astype(o_ref.dtype)


def _rope(x, c, s, *, planes, row0, width, col0=0, ncol=1, tr=512):
    r0 = row0 // tr
    nt = T_LAT // tr
    if x.ndim == 3:
        ncol = x.shape[2] // width
        x_spec = pl.BlockSpec((None, tr, ncol * width), lambda p, i: (p, i + r0, 0))
    else:
        x_spec = pl.BlockSpec((tr, ncol * width), lambda p, i: (i + r0, col0 // ncol))
    return pl.pallas_call(
        functools.partial(_rope_kernel, width=width),
        grid=(planes, M_LAT // tr),
        in_specs=[
            x_spec,
            pl.BlockSpec((tr, width), lambda p, i: (i % nt, 0)),
            pl.BlockSpec((tr, width), lambda p, i: (i % nt, 0)),
        ],
        out_specs=pl.BlockSpec((None, tr, ncol * width), lambda p, i: (p, i, 0)),
        out_shape=jax.ShapeDtypeStruct((planes, M_LAT, ncol * width), BF16),
        compiler_params=_cparams(("parallel", "parallel")),
        name="rope",
    )(x, c, s)


def _attn_kernel(*refs, kind, hb, nseg, scale, lam_init):
    it = iter(refs)
    q_ref = next(it)
    k_refs = [next(it) for _ in range(nseg)]
    kpe_refs = [next(it) for _ in range(nseg)] if kind == "mla" else None
    v_refs = [next(it) for _ in range(nseg)] if kind != "mla" else k_refs
    lam_ref = next(it) if kind == "diff" else None
    subln_ref = next(it) if kind == "diff" else None
    o_ref = next(it)

    lane = lax.broadcasted_iota(jnp.int32, (1, LANE), 1)
    lo_half = lane < 64
    log2_scale = scale * LOG2_E

    def softmax_parts(s_list):
        t_list = [s * log2_scale for s in s_list]
        m = functools.reduce(jnp.maximum, [jnp.max(t, axis=-1, keepdims=True) for t in t_list])
        p_list = [jnp.exp2(t - m) for t in t_list]
        l = functools.reduce(jnp.add, [jnp.sum(p, axis=-1, keepdims=True) for p in p_list])
        return p_list, 1.0 / l

    def attend(s_list, v_list):
        p_list, inv_l = softmax_parts(s_list)
        return functools.reduce(jnp.add, [_dot(p.astype(BF16), v) for p, v in zip(p_list, v_list)]) * inv_l

    for hh in range(hb):
        if kind == "mla":
            q2 = q_ref[:, hh * 256:(hh + 1) * 256].astype(BF16)
            s_list = []
            for kr, pr in zip(k_refs, kpe_refs):
                kcat = jnp.concatenate([kr[:, hh * 256:hh * 256 + 128].astype(BF16), pr[...].astype(BF16)], axis=1)
                s_list.append(_dot_nt(q2, kcat))
            v2 = [vr[:, hh * 256 + 128:hh * 256 + 256].astype(BF16) for vr in v_refs]
            o_ref[:, hh * LANE:(hh + 1) * LANE] = attend(s_list, v2)
            continue

        sl = slice(hh * LANE, (hh + 1) * LANE)
        q2 = q_ref[:, sl]
        k2 = [kr[:, sl].astype(BF16) for kr in k_refs]
        v2 = [vr[:, sl].astype(BF16) for vr in v_refs]
        tq = q2.shape[0]
        q_heads = [jnp.where(lo_half, q2, 0.0), jnp.where(lo_half, 0.0, q2)]
        if hb > 1:
            qs = jnp.concatenate(q_heads, axis=0).astype(BF16)
            o2 = attend([_dot_nt(qs, kk) for kk in k2], v2)
            o_heads = [o2[:tq], o2[tq:]]
            if kind == "diff":
                o_diff = o_heads[0] - lam_ref[...][:, :1] * o_heads[1]
        elif kind == "na":
            o_heads = [attend([_dot_nt(qh.astype(BF16), kk) for kk in k2], v2) for qh in q_heads]
        else:
            (p0, il0), (p1, il1) = [softmax_parts([_dot_nt(qh.astype(BF16), kk) for kk in k2]) for qh in q_heads]
            lam = lam_ref[...][:, :1]
            o_diff = functools.reduce(jnp.add, [
                _dot((a * il0 - lam * (b * il1)).astype(BF16), vv) for a, b, vv in zip(p0, p1, v2)])
        if kind == "na":
            o_ref[:, sl] = jnp.where(lo_half, o_heads[0], o_heads[1])
        else:
            o_ref[:, sl] = _rms(o_diff, subln_ref[...], DIFF_SUBLN_EPS) * (1.0 - lam_init)


def _attention(kind, *, grid, q, ks, vs, kpes=None, extra=(), rows, out_spec, hb, scale, lam_init=0.0,
               name="attn"):
    ops = [q] + list(ks) + (list(kpes) if kind == "mla" else []) + (list(vs) if kind != "mla" else [])
    ops += list(extra)
    return pl.pallas_call(
        functools.partial(_attn_kernel, kind=kind, hb=hb, nseg=len(ks), scale=scale, lam_init=lam_init),
        grid=grid,
        in_specs=[s for _, s in ops],
        out_specs=out_spec,
        out_shape=jax.ShapeDtypeStruct((rows, D), F32),
        compiler_params=_cparams(("parallel",) * len(grid)),
        name=name,
    )(*[a for a, _ in ops])


NA_QROWS = 4
NA_KROWS = 12
NA_TQ = NA_QROWS * GRID_W
NA_TK = NA_KROWS * GRID_W


def _na_bias_table(rpb):
    h = rpb.shape[0]
    qc = np.arange(GRID_W)[:, None]
    kc = np.arange(GRID_W)[None, :]
    qs = np.clip(qc - NA_WIN_C // 2, 0, GRID_W - NA_WIN_C)
    col_ok = (kc >= qs) & (kc < qs + NA_WIN_C)
    ic = np.clip(kc - qc + NA_WIN_C - 1, 0, 2 * NA_WIN_C - 2)
    n_ir = 2 * NA_WIN_R - 1
    onehot = (ic.reshape(1, -1) == np.arange(2 * NA_WIN_C - 1)[:, None]).astype(np.float32)
    tab = jnp.einsum("hrm,mx->hrx", rpb.astype(F32) * LOG2_E, jnp.asarray(onehot),
                     precision=lax.Precision.HIGHEST)
    tab = jnp.where(jnp.asarray(col_ok.reshape(1, 1, -1)), tab, NEG_INF).reshape(h, n_ir, GRID_W, GRID_W)
    tab = jnp.concatenate([tab, tab], axis=-1)
    n_rows = T_LAT // GRID_W
    n_g = n_rows // NA_QROWS
    ir_idx = np.zeros((3, NA_QROWS, NA_KROWS), np.int32)
    row_ok = np.zeros((3, NA_QROWS, NA_KROWS), bool)
    for pat, g in enumerate((0, 1, n_g - 1)):
        base = int(np.clip(NA_QROWS * g - NA_QROWS, 0, n_rows - NA_KROWS))
        for qr in range(NA_QROWS):
            r = NA_QROWS * g + qr
            r0 = int(np.clip(r - NA_WIN_R // 2, 0, n_rows - NA_WIN_R))
            for kr in range(NA_KROWS):
                ok = r0 <= base + kr < r0 + NA_WIN_R
                row_ok[pat, qr, kr] = ok
                ir_idx[pat, qr, kr] = (base + kr - r + NA_WIN_R - 1) if ok else 0

    def build(tab_ref, o_ref):
        pat_id = pl.program_id(0)
        lo_half = lax.broadcasted_iota(jnp.int32, (1, LANE), 1) < GRID_W
        masked = jnp.full((GRID_W, LANE), NEG_INF, F32)
        for pat in range(3):
            @pl.when(pat_id == pat)
            def _(pat=pat):
                for hd in range(heads_per_step):
                    for qr in range(NA_QROWS):
                        for kp in range(NA_KROWS // 2):
                            halves = [tab_ref[hd, int(ir_idx[pat, qr, kr])] if row_ok[pat, qr, kr] else masked
                                      for kr in (2 * kp, 2 * kp + 1)]
                            o_ref[hd, qr * GRID_W:(qr + 1) * GRID_W, kp * LANE:(kp + 1) * LANE] = jnp.where(
                                lo_half, halves[0], halves[1])

    heads_per_step = 4
    return pl.pallas_call(
        build,
        grid=(3, h // heads_per_step),
        in_specs=[pl.BlockSpec((heads_per_step, n_ir, GRID_W, LANE), lambda p, hh: (hh, 0, 0, 0))],
        out_specs=pl.BlockSpec((None, heads_per_step, NA_TQ, NA_TK), lambda p, hh: (p, hh, 0, 0)),
        out_shape=jax.ShapeDtypeStruct((3, h, NA_TQ, NA_TK), F32),
        compiler_params=_cparams(("parallel", "parallel")),
        name="na_bias",
    )(tab)


def _na_lat_kernel(q_ref, kc_ref, vc_ref, k_ref, v_ref, b_ref, o_ref, *, scale):
    g = pl.program_id(2)
    n_rows = T_LAT // GRID_W
    base = jnp.clip(NA_QROWS * g - NA_QROWS, 0, n_rows - NA_KROWS) * GRID_W
    base = pl.multiple_of(base, GRID_W)
    lane = lax.broadcasted_iota(jnp.int32, (1, LANE), 1)
    lo_half = lane < 64
    q2 = q_ref[...]
    kc = kc_ref[...].astype(BF16)
    vc = vc_ref[...].astype(BF16)
    kl = k_ref[pl.ds(base, NA_TK), :].astype(BF16)
    vl = v_ref[pl.ds(base, NA_TK), :].astype(BF16)
    log2_scale = scale * LOG2_E
    outs = []
    for n in range(2):
        qm = jnp.where(lo_half if n == 0 else jnp.logical_not(lo_half), q2, 0.0).astype(BF16)
        s_c = _dot_nt(qm, kc) * log2_scale
        s_l = _dot_nt(qm, kl) * log2_scale + b_ref[n]
        m = jnp.maximum(jnp.max(s_c, axis=-1, keepdims=True), jnp.max(s_l, axis=-1, keepdims=True))
        p_c = jnp.exp2(s_c - m)
        p_l = jnp.exp2(s_l - m)
        l = jnp.sum(p_c, axis=-1, keepdims=True) + jnp.sum(p_l, axis=-1, keepdims=True)
        o = _dot(p_c.astype(BF16), vc) + _dot(p_l.astype(BF16), vl)
        outs.append(o * (1.0 / l))
    o_ref[...] = jnp.where(lo_half, outs[0], outs[1])


def _na_lat(qkv, cache_k, cache_v, bias):
    n_g = T_LAT // NA_TQ
    q_blk0 = M_CTX // NA_TQ
    kv_blk0 = M_CTX // T_LAT
    pat = lambda g: jnp.where(g == 0, 0, jnp.where(g == n_g - 1, 2, 1))
    return pl.pallas_call(
        functools.partial(_na_lat_kernel, scale=0.125),
        grid=(N_LAT_SEQ, D // LANE, n_g),
        in_specs=[
            pl.BlockSpec((None, NA_TQ, LANE), lambda b, h, g: (0, q_blk0 + b * n_g + g, h)),
            pl.BlockSpec((None, PAST, LANE), lambda b, h, g: (b, 0, h)),
            pl.BlockSpec((None, PAST, LANE), lambda b, h, g: (b, 0, h)),
            pl.BlockSpec((None, T_LAT, LANE), lambda b, h, g: (1, kv_blk0 + b, h)),
            pl.BlockSpec((None, T_LAT, LANE), lambda b, h, g: (2, kv_blk0 + b, h)),
            pl.BlockSpec((None, 2, NA_TQ, NA_TK), lambda b, h, g: (pat(g), h, 0, 0)),
        ],
        out_specs=pl.BlockSpec((NA_TQ, LANE), lambda b, h, g: (b * n_g + g, h)),
        out_shape=jax.ShapeDtypeStruct((M_LAT, D), F32),
        compiler_params=_cparams(("parallel", "parallel", "parallel")),
        name="na_latent",
    )(qkv, cache_k, cache_v, qkv, qkv, bias)


SEQ_T_PER_STEP = 4


def _seq_t_kernel(x_ref, o_ref):
    for s in range(SEQ_T_PER_STEP):
        o_ref[s] = x_ref[s * T_CTX:(s + 1) * T_CTX, :].T


def _seq_transpose(qkv, plane):
    return pl.pallas_call(
        _seq_t_kernel,
        grid=(N_CTX_SEQ // SEQ_T_PER_STEP,),
        in_specs=[pl.BlockSpec((None, SEQ_T_PER_STEP * T_CTX, D), lambda b: (plane, b, 0))],
        out_specs=pl.BlockSpec((SEQ_T_PER_STEP, D, T_CTX), lambda b: (b, 0, 0)),
        out_shape=jax.ShapeDtypeStruct((N_CTX_SEQ, D, T_CTX), F32),
        compiler_params=_cparams(("parallel",)),
        name="seq_transpose",
    )(qkv)


def kernel(x_prompt, x_sample, c, c_ctx, cache_na_k, cache_na_v, cache_diff_k, cache_diff_v, cache_mla_ckv, cache_mla_kpe, ada_w, ada_b, norm1, norm2, pool_w, pool_scale, na_qkv, na_rpb, na_out, diff_qkv, diff_lambda, diff_subln, diff_out, mla_qa, mla_qnorm, mla_qb, mla_kva, mla_kvnorm, mla_kvb, mla_out, ffn_gate, ffn_up, ffn_down, moe_router, moe_gate, moe_up, moe_down, final_norm):
    depth = ada_w.shape[0]
    x = None
    cond8 = jnp.concatenate([c_ctx[None, :], c, jnp.zeros((5, D), F32)], axis=0)
    mods_all = _ada(cond8, ada_w, ada_b)[:, :3].reshape(depth, 3, 6, 1, D)
    outs = {}

    for l in range(depth):
        kind, li = l % 4, l // 4
        mods = mods_all[l]
        if kind == 0:
            x_pair = (x_prompt.reshape(M_CTX, D), x_sample.reshape(M_LAT, D)) if x is None else (x[:M_CTX], x[M_CTX:])
            h = _modrows(x_pair, norm1[l], mods, 0, 1, rows=M_TOK, row0=0, width=D)
            x = _pool(h, *x_pair, pool_w[li], pool_scale[li], mods, 2)
        elif kind == 1:
            qkv = _linear(x, na_qkv[li], tn=D, norm_g=norm1[l], mods=mods, sh_c=0, sc_c=1, planes=True,
                          name="na_qkv")
            to_cache = lambda a: jnp.transpose(a.reshape(N_CTX_SEQ, 1, 16, 64, T_CTX), (0, 1, 4, 2, 3))
            outs["na_k"] = to_cache(_seq_transpose(qkv, 1))
            outs["na_v"] = to_cache(_seq_transpose(qkv, 2))
            seq_spec = lambda p: pl.BlockSpec((None, T_CTX, D), lambda b, p=p: (p, b, 0))
            o_ctx = _attention(
                "na", grid=(N_CTX_SEQ,), q=(qkv, seq_spec(0)), ks=[(qkv, seq_spec(1))], vs=[(qkv, seq_spec(2))],
                rows=M_CTX, out_spec=pl.BlockSpec((T_CTX, D), lambda b: (b, 0)), hb=D // LANE, scale=0.125,
                name="na_ctx")
            bias = _na_bias_table(na_rpb[li])
            o_lat = _na_lat(qkv, cache_na_k[:, li].reshape(N_LAT_SEQ, PAST, D),
                            cache_na_v[:, li].reshape(N_LAT_SEQ, PAST, D), bias)
            x = _linear((o_ctx, o_lat), na_out[li], tn=D, tm=TM, resid=x, mods=mods, gate_c=2, name="na_out")
        elif kind == 2:
            lam_init = 0.8 - 0.6 * math.exp(-0.3 * l)
            lp = diff_lambda[li].astype(F32)
            lam = jnp.exp(jnp.sum(lp[0] * lp[1])) - jnp.exp(jnp.sum(lp[2] * lp[3])) + lam_init
            lam_row = jnp.full((1, LANE), lam, F32)
            subln = diff_subln[li].reshape(1, LANE)
            extra = [(lam_row, pl.BlockSpec((1, LANE), lambda *_: (0, 0))),
                     (subln, pl.BlockSpec((1, LANE), lambda *_: (0, 0)))]
            qkv = _linear(x, diff_qkv[li], tn=D, norm_g=norm1[l], mods=mods, sh_c=0, sc_c=1, planes=True,
                          name="diff_qkv")
            outs["diff_k"] = jnp.transpose(
                _seq_transpose(qkv, 1).reshape(N_CTX_SEQ, 1, 8, 2, 64, T_CTX), (0, 1, 5, 2, 3, 4))
            outs["diff_v"] = qkv[2, :M_CTX].reshape(N_CTX_SEQ, 1, T_CTX, 8, 128)
            seq_spec = lambda p: pl.BlockSpec((None, T_CTX, D), lambda b, p=p: (p, b, 0))
            o_ctx = _attention(
                "diff", grid=(N_CTX_SEQ,), q=(qkv, seq_spec(0)), ks=[(qkv, seq_spec(1))], vs=[(qkv, seq_spec(2))],
                extra=extra, rows=M_CTX, out_spec=pl.BlockSpec((T_CTX, D), lambda b: (b, 0)), hb=D // LANE,
                scale=0.125, lam_init=lam_init, name="diff_ctx")
            rc, rs = _rope_tables(LANE, 0, LANE)
            qk_r = _rope(qkv, rc, rs, planes=2, row0=M_CTX, width=LANE)
            tq = LATENT_TQ
            nq = T_LAT // tq
            ck = cache_diff_k[:, li].reshape(N_LAT_SEQ, PAST, D)
            cv = cache_diff_v[:, li].reshape(N_LAT_SEQ, PAST, D)
            o_lat = _attention(
                "diff", grid=(N_LAT_SEQ, D // LANE, nq),
                q=(qk_r, pl.BlockSpec((None, tq, LANE), lambda b, h, i: (0, b * nq + i, h))),
                ks=[(ck, pl.BlockSpec((None, PAST, LANE), lambda b, h, i: (b, 0, h))),
                    (qk_r, pl.BlockSpec((None, T_LAT, LANE), lambda b, h, i: (1, b, h)))],
                vs=[(cv, pl.BlockSpec((None, PAST, LANE), lambda b, h, i: (b, 0, h))),
                    (qkv, pl.BlockSpec((None, T_LAT, LANE), lambda b, h, i: (2, M_CTX // T_LAT + b, h)))],
                extra=extra,
                rows=M_LAT, out_spec=pl.BlockSpec((tq, LANE), lambda b, h, i: (b * nq + i, h)),
                hb=1, scale=0.125, lam_init=lam_init, name="diff_latent")
            x = _linear((o_ctx, o_lat), diff_out[li], tn=D, tm=TM, resid=x, mods=mods, gate_c=2, name="diff_out")
        else:
            hd = MLA_NOPE + MLA_ROPE
            wqb = mla_qb[li].reshape(MLA_Q_LORA, MLA_HEADS, hd)
            wqb = jnp.pad(wqb, ((0, 0), (0, 0), (0, 256 - hd))).reshape(MLA_Q_LORA, MLA_HEADS * 256)
            w_a = jnp.concatenate(
                [mla_qa[li], jnp.pad(mla_kva[li], ((0, 0), (0, 512 - MLA_KV_LORA - MLA_ROPE)))], axis=1)
            a = _linear(x, w_a, tn=D, tm=TM, norm_g=norm1[l], mods=mods, sh_c=0, sc_c=1, name="mla_a")
            q = _linear(a, wqb, tn=D, tm=TM, k=MLA_Q_LORA, x_col=0, norm_g=mla_qnorm[li], name="mla_q")
            zero_mod = jnp.zeros((1, 2, 1, MLA_KV_LORA), F32)
            ckv = _modrows(a, mla_kvnorm[li], zero_mod, 0, 1, rows=M_TOK, row0=0, width=MLA_KV_LORA, col_blk=2,
                           grouped=False)
            outs["mla_ckv"] = ckv[:M_CTX].reshape(N_CTX_SEQ, 1, T_CTX, MLA_KV_LORA)
            outs["mla_kpe"] = a[:M_CTX, 768:768 + MLA_ROPE].reshape(N_CTX_SEQ, 1, T_CTX, MLA_ROPE)
            ckv_all = jnp.concatenate([ckv, cache_mla_ckv[:, li].reshape(N_LAT_SEQ * PAST, MLA_KV_LORA)], axis=0)
            kvb = _linear(ckv_all, mla_kvb[li], tn=D, tm=TM, out_dtype=BF16, name="mla_kvb")
            scale = float(hd) ** -0.5
            o_ctx = _attention(
                "mla", grid=(N_CTX_SEQ,),
                q=(q, pl.BlockSpec((T_CTX, 2048), lambda b: (b, 0))),
                ks=[(kvb, pl.BlockSpec((T_CTX, 2048), lambda b: (b, 0)))],
                kpes=[(a, pl.BlockSpec((T_CTX, LANE), lambda b: (b, 6)))],
                vs=None, rows=M_CTX, out_spec=pl.BlockSpec((T_CTX, D), lambda b: (b, 0)), hb=MLA_HEADS,
                scale=scale, name="mla_ctx")
            rc, rs = _rope_tables(256, 128, 192)
            q_r = _rope(q, rc, rs, planes=1, row0=M_CTX, width=256, ncol=MLA_HEADS)[0]
            rc1, rs1 = _rope_tables(LANE, 0, 64)
            kpe_r = _rope(a, rc1, rs1, planes=1, row0=M_CTX, width=LANE, col0=6)[0]
            kpe_c = jnp.pad(cache_mla_kpe[:, li], ((0, 0), (0, 0), (0, LANE - MLA_ROPE)))
            tq = LATENT_TQ
            nq = T_LAT // tq
            o_lat = _attention(
                "mla", grid=(N_LAT_SEQ, MLA_HEADS, nq),
                q=(q_r, pl.BlockSpec((tq, 256), lambda b, h, i: (b * nq + i, h))),
                ks=[(kvb, pl.BlockSpec((PAST, 256), lambda b, h, i: (M_TOK // PAST + b, h))),
                    (kvb, pl.BlockSpec((T_LAT, 256), lambda b, h, i: (M_CTX // T_LAT + b, h)))],
                kpes=[(kpe_c, pl.BlockSpec((None, PAST, LANE), lambda b, h, i: (b, 0, 0))),
                      (kpe_r, pl.BlockSpec((T_LAT, LANE), lambda b, h, i: (b, 0)))],
                vs=None, rows=M_LAT,
                out_spec=pl.BlockSpec((tq, LANE), lambda b, h, i: (b * nq + i, h)),
                hb=1, scale=scale, name="mla_latent")
            x = _linear((o_ctx, o_lat), mla_out[li], tn=D, tm=TM, resid=x, mods=mods, gate_c=2, name="mla_out")

        f = l // 2
        if l % 2 == 0:
            x = _ffn_dense(x, ffn_gate, ffn_up, ffn_down, f, norm2[l], mods, 3, 4, 5)
        else:
            x = _moe(x, norm2[l], mods, 3, 4, 5, moe_router[f], moe_gate, moe_up, moe_down, f)

    zero_mod = jnp.zeros((1, 2, 1, D), F32)
    y_prompt = _modrows(x, final_norm, zero_mod, 0, 1, rows=M_CTX, row0=0, width=D, grouped=False)
    y_sample = _modrows(x, final_norm, zero_mod, 0, 1, rows=M_LAT, row0=M_CTX, width=D, grouped=False)
    return (y_prompt.reshape(N_CTX_SEQ, T_CTX, D), y_sample.reshape(N_LAT_SEQ, T_LAT, D),
            outs["na_k"], outs["na_v"], outs["diff_k"], outs["diff_v"], outs["mla_ckv"], outs["mla_kpe"])
```

```python
import functools
import math

import numpy as np
import jax
import jax.numpy as jnp
from jax import lax
from jax.experimental import pallas as pl
from jax.experimental.pallas import tpu as pltpu

F32 = jnp.float32
BF16 = jnp.bfloat16

D = 1024
N_CTX_SEQ = 32
T_CTX = 256
N_LAT_SEQ = 2
T_LAT = 2048
PAST = 512
M_CTX = N_CTX_SEQ * T_CTX
M_LAT = N_LAT_SEQ * T_LAT
M_TOK = M_CTX + M_LAT
GRID_W = 64
NORM_EPS = 1e-6
NEG_INF = -1e30
ROPE_BASE = 10000.0
POOL_WINDOWS = (2, 4, 8, 16)
NA_WIN_R = 8
NA_WIN_C = 16
DIFF_SUBLN_EPS = 1e-5
MLA_HEADS = 8
MLA_NOPE = 128
MLA_ROPE = 64
MLA_KV_LORA = 256
MLA_Q_LORA = 512
FFN_HIDDEN = 2816
N_EXPERTS = 8

LOG2_E = math.log2(math.e)
LANE = 128
TM = 1024
TH = 256
SUB = 256
VMEM_LIMIT = 56 * 1024 * 1024

TMF = 2048
DMA_UNROLL = 8
DISPATCH_ROWS = 1024
N_SRC = M_TOK // DISPATCH_ROWS
SEG_ALIGN = 8
LOCAL_ROWS = 2 * DISPATCH_ROWS + N_EXPERTS * SEG_ALIGN
SEG_CHUNKS = tuple(2 ** b for b in range(11, 2, -1))
COMBINE_ROWS = 512
MOE_TILES = -(-(2 * M_TOK + N_SRC * N_EXPERTS * (SEG_ALIGN - 1) + N_EXPERTS * (TMF - 1)) // TMF)
MOE_ROWS = MOE_TILES * TMF


def _cparams(sem):
    return pltpu.CompilerParams(dimension_semantics=sem, vmem_limit_bytes=VMEM_LIMIT)


def _group_of_tile(i, tm):
    n_ctx = M_CTX // tm
    return jnp.where(i < n_ctx, 0, 1 + (i - n_ctx) // (T_LAT // tm))


def _dot(a, b):
    return jnp.dot(a, b, preferred_element_type=F32)


def _dot_nt(a, b):
    return lax.dot_general(a, b, (((1,), (1,)), ((), ())), preferred_element_type=F32)


def _rms(x, g, eps):
    return x * lax.rsqrt(jnp.mean(x * x, axis=-1, keepdims=True) + eps) * g


def _ada_kernel(c_ref, w_ref, b_ref, o_ref):
    c = c_ref[...]
    s = (c * jax.nn.sigmoid(c)).astype(BF16)
    o_ref[...] = _dot(s, w_ref[...].astype(BF16)) + b_ref[...]


def _ada(cond8, ada_w, ada_b):
    depth = ada_w.shape[0]
    tn = 1536
    return pl.pallas_call(
        _ada_kernel,
        grid=(depth, 6 * D // tn),
        in_specs=[
            pl.BlockSpec((8, D), lambda l, j: (0, 0)),
            pl.BlockSpec((None, D, tn), lambda l, j: (l, 0, j)),
            pl.BlockSpec((None, 1, tn), lambda l, j: (l, 0, j)),
        ],
        out_specs=pl.BlockSpec((None, 8, tn), lambda l, j: (l, 0, j)),
        out_shape=jax.ShapeDtypeStruct((depth, 8, 6 * D), F32),
        compiler_params=_cparams(("parallel", "parallel")),
        name="ada",
    )(cond8, ada_w, ada_b.reshape(depth, 1, 6 * D))


def _token_tile(xc_ref, xl_ref, tile, tm):
    return jnp.where(tile < M_CTX // tm, xc_ref[...], xl_ref[...])


def _split_specs(block, tm, col):
    n_ctx = M_CTX // tm
    return [pl.BlockSpec(block, lambda i, *_: (jnp.minimum(i, n_ctx - 1), col(*_))),
            pl.BlockSpec(block, lambda i, *_: (jnp.maximum(i - n_ctx, 0), col(*_)))]


def _modrows_kernel(*refs, eps, split, tm):
    if split:
        xc_ref, xl_ref, g_ref, sh_ref, sc_ref, o_ref = refs
        x = _token_tile(xc_ref, xl_ref, pl.program_id(0), tm)
    else:
        x_ref, g_ref, sh_ref, sc_ref, o_ref = refs
        x = x_ref[...]
    o_ref[...] = _rms(x, g_ref[...], eps) * (1.0 + sc_ref[...]) + sh_ref[...]


def _modrows(x, g, mods, sh_c, sc_c, *, rows, row0, width, col_blk=0, tm=TM, eps=NORM_EPS,
             grouped=True):
    r0 = row0 // tm
    if grouped:
        grp = lambda i: _group_of_tile(i + r0, tm)
    else:
        grp = lambda i: 0
    split = isinstance(x, tuple)
    if split:
        xs, x_specs = list(x), _split_specs((tm, width), tm, lambda: 0)
    else:
        xs, x_specs = [x], [pl.BlockSpec((tm, width), lambda i: (i + r0, col_blk))]
    return pl.pallas_call(
        functools.partial(_modrows_kernel, eps=eps, split=split, tm=tm),
        grid=(rows // tm,),
        in_specs=x_specs + [
            pl.BlockSpec((1, width), lambda i: (0, 0)),
            pl.BlockSpec((None, None, 1, width), lambda i: (grp(i), sh_c, 0, 0)),
            pl.BlockSpec((None, None, 1, width), lambda i: (grp(i), sc_c, 0, 0)),
        ],
        out_specs=pl.BlockSpec((tm, width), lambda i: (i, 0)),
        out_shape=jax.ShapeDtypeStruct((rows, width), F32),
        compiler_params=_cparams(("parallel",)),
        name="modrows",
    )(*xs, g.reshape(1, width), mods, mods)


def _linear_kernel(*refs, prologue, eps, epilogue, split, tm, tn, planes):
    it = iter(refs)
    x_ref = next(it)
    x_lat_ref = next(it) if split else None
    w_ref = next(it)
    g_ref = next(it) if prologue != "none" else None
    sh_ref = next(it) if prologue == "mod" else None
    sc_ref = next(it) if prologue == "mod" else None
    res_ref = next(it) if epilogue == "resid" else None
    gate_ref = next(it) if epilogue == "resid" else None
    o_ref = next(it)
    wb_ref = next(it)
    n = w_ref.shape[1]

    @pl.when(pl.program_id(0) == 0)
    def _():
        wb_ref[...] = w_ref[...].astype(BF16)

    def run(src_ref):
        x = src_ref[...]
        if prologue != "none":
            x = _rms(x, g_ref[...], eps)
        if prologue == "mod":
            x = x * (1.0 + sc_ref[...]) + sh_ref[...]
        xb = x.astype(BF16)
        for p in range(n // tn):
            cols = slice(p * tn, (p + 1) * tn)
            acc = _dot(xb, wb_ref[:, cols])
            if epilogue == "resid":
                acc = res_ref[:, cols] + gate_ref[:, cols] * acc
            if planes:
                o_ref[p] = acc.astype(o_ref.dtype)
            else:
                o_ref[:, cols] = acc.astype(o_ref.dtype)

    if split:
        is_ctx = pl.program_id(0) < M_CTX // tm
        pl.when(is_ctx)(lambda: run(x_ref))
        pl.when(jnp.logical_not(is_ctx))(lambda: run(x_lat_ref))
    else:
        run(x_ref)


TML = 512
LATENT_TQ = 256


def _linear(x, w, *, tn, k=None, x_col=0, norm_g=None, eps=NORM_EPS, mods=None, sh_c=None, sc_c=None,
            resid=None, gate_c=None, planes=False, tm=TML, out_dtype=F32, name="linear"):
    split = isinstance(x, tuple)
    n = w.shape[1]
    prologue = "none" if norm_g is None else ("mod" if sh_c is not None else "norm")
    epilogue = "none" if resid is None else "resid"
    grp = lambda i: _group_of_tile(i, tm)
    if split:
        x_ctx, x_lat = x
        m = x_ctx.shape[0] + x_lat.shape[0]
        k = x_ctx.shape[1]
        n_ctx = x_ctx.shape[0] // tm
        args = [x_ctx, x_lat]
        in_specs = [
            pl.BlockSpec((tm, k), lambda i: (jnp.minimum(i, n_ctx - 1), 0)),
            pl.BlockSpec((tm, k), lambda i: (jnp.maximum(i - n_ctx, 0), 0)),
        ]
    else:
        m = x.shape[0]
        k = x.shape[1] if k is None else k
        args = [x]
        in_specs = [pl.BlockSpec((tm, k), lambda i: (i, x_col))]
    args.append(w)
    in_specs.append(pl.BlockSpec((k, n), lambda i: (0, 0)))
    if prologue != "none":
        args.append(norm_g.reshape(1, k))
        in_specs.append(pl.BlockSpec((1, k), lambda i: (0, 0)))
    if prologue == "mod":
        args += [mods, mods]
        in_specs += [
            pl.BlockSpec((None, None, 1, k), lambda i: (grp(i), sh_c, 0, 0)),
            pl.BlockSpec((None, None, 1, k), lambda i: (grp(i), sc_c, 0, 0)),
        ]
    if epilogue == "resid":
        args += [resid, mods]
        in_specs += [
            pl.BlockSpec((tm, n), lambda i: (i, 0)),
            pl.BlockSpec((None, None, 1, n), lambda i: (grp(i), gate_c, 0, 0)),
        ]
    if planes:
        out_spec = pl.BlockSpec((n // tn, tm, tn), lambda i: (0, i, 0))
        out_shape = jax.ShapeDtypeStruct((n // tn, m, tn), out_dtype)
    else:
        out_spec = pl.BlockSpec((tm, n), lambda i: (i, 0))
        out_shape = jax.ShapeDtypeStruct((m, n), out_dtype)
    return pl.pallas_call(
        functools.partial(_linear_kernel, prologue=prologue, eps=eps, epilogue=epilogue, split=split, tm=tm,
                          tn=tn, planes=planes),
        grid=(m // tm,),
        in_specs=in_specs,
        out_specs=out_spec,
        out_shape=out_shape,
        scratch_shapes=[pltpu.VMEM((k, n), BF16)],
        compiler_params=_cparams(("arbitrary",)),
        name=name,
    )(*args)


POOL_ROWS = 2048


def _pool_kernel(h_ref, xc_ref, xl_ref, w_ref, ps_ref, gate_ref, o_ref):
    i = pl.program_id(0)
    g = pl.program_id(1)
    seq = jnp.where(i < M_CTX // POOL_ROWS, T_CTX, T_LAT)
    t = lax.broadcasted_iota(jnp.int32, (POOL_ROWS, 1), 0) & (seq - 1)

    def ahead(a, k):
        return jnp.where(t + k < seq, pltpu.roll(a, POOL_ROWS - k, 0), 0.0)

    def behind(a, k):
        return jnp.where(t - k >= 0, pltpu.roll(a, k, 0), 0.0)

    for gi, win in enumerate(POOL_WINDOWS):
        @pl.when(g == gi)
        def _(win=win):
            h = h_ref[...]
            fwd, bwd, k = h, h, 1
            while k < win // 2:
                fwd, bwd, k = fwd + ahead(fwd, k), bwd + behind(bwd, k), 2 * k
            acc = fwd + behind(bwd, 1)
            cnt = jnp.minimum(t + (win - win // 2), seq) - jnp.maximum(t - win // 2, 0)
            pooled = acc / cnt.astype(F32)
            y = _dot((pooled - h).astype(BF16), w_ref[...].astype(BF16)) * ps_ref[...]
            o_ref[...] = _token_tile(xc_ref, xl_ref, i, POOL_ROWS) + gate_ref[...] * y


def _pool(h, x_ctx, x_lat, pool_w, pool_scale, mods, gate_c):
    gw = D // len(POOL_WINDOWS)
    grp = lambda i: _group_of_tile(i, POOL_ROWS)
    return pl.pallas_call(
        _pool_kernel,
        grid=(M_TOK // POOL_ROWS, len(POOL_WINDOWS)),
        in_specs=[
            pl.BlockSpec((POOL_ROWS, gw), lambda i, g: (i, g)),
            *_split_specs((POOL_ROWS, gw), POOL_ROWS, lambda g: g),
            pl.BlockSpec((None, gw, gw), lambda i, g: (g, 0, 0)),
            pl.BlockSpec((1, gw), lambda i, g: (0, g)),
            pl.BlockSpec((None, None, 1, gw), lambda i, g: (grp(i), gate_c, 0, g)),
        ],
        out_specs=pl.BlockSpec((POOL_ROWS, gw), lambda i, g: (i, g)),
        out_shape=jax.ShapeDtypeStruct((M_TOK, D), F32),
        compiler_params=_cparams(("parallel", "parallel")),
        name="pool",
    )(h, x_ctx, x_lat, pool_w, pool_scale.reshape(1, D), mods)


NJ = FFN_HIDDEN // TH
TN2 = 256
NN = D // TN2


def _swiglu_kernel(*refs, moe):
    if moe:
        te_ref, tv_ref, x_ref, wg_ref, wu_ref, wd_ref, o_ref, xb_ref, h_ref = refs
    else:
        x_ref, wg_ref, wu_ref, wd_ref, g_ref, sh_ref, sc_ref, res_ref, gate_ref, o_ref, xb_ref, h_ref = refs
    j = pl.program_id(1)
    n_valid = tv_ref[pl.program_id(0)] if moe else TMF

    def per_sub_block(fn, fn_skipped=None):
        sub_blocks = [pl.ds(s * SUB, SUB) for s in range(TMF // SUB)]

        def all_rows():
            for rows in sub_blocks:
                fn(rows)

        def valid_rows_only():
            for s, rows in enumerate(sub_blocks):
                pl.when(s * SUB < n_valid)(functools.partial(fn, rows))
                if fn_skipped is not None:
                    pl.when(s * SUB >= n_valid)(functools.partial(fn_skipped, rows))

        if moe:
            pl.when(n_valid == TMF)(all_rows)
            pl.when(n_valid < TMF)(valid_rows_only)
        else:
            all_rows()

    @pl.when(j == 0)
    def _():
        def cast(rows):
            x = x_ref[rows, :]
            if not moe:
                x = _rms(x, g_ref[...], NORM_EPS) * (1.0 + sc_ref[...]) + sh_ref[...]
            xb_ref[rows, :] = x.astype(BF16)

        per_sub_block(cast)

    @pl.when(j < NJ)
    def _():
        wg = wg_ref[...].astype(BF16)
        wu = wu_ref[...].astype(BF16)

        def up(rows):
            xb = xb_ref[rows, :]
            a = _dot(xb, wg)
            u = _dot(xb, wu)
            h_ref[j, rows, :] = (a * jax.nn.sigmoid(a) * u).astype(BF16)

        per_sub_block(up)

    @pl.when(j >= NJ)
    def _():
        wd = wd_ref[...].astype(BF16)

        def down(rows):
            hcat = jnp.concatenate([h_ref[jj, rows, :] for jj in range(NJ)], axis=1)
            y = _dot(hcat, wd)
            if moe:
                o_ref[rows, :] = y
            else:
                o_ref[rows, :] = res_ref[rows, :] + gate_ref[...] * y

        def skipped(rows):
            o_ref[rows, :] = jnp.zeros((SUB, TN2), F32)

        per_sub_block(down, skipped)


_SWIGLU_SCRATCH = [pltpu.VMEM((TMF, D), BF16), pltpu.VMEM((NJ, TMF, TH), BF16)]


def _ffn_dense(x, wg, wu, wd, f, norm_g, mods, sh_c, sc_c, gate_c):
    grp = lambda i: _group_of_tile(i, TMF)
    jh = lambda j: jnp.minimum(j, NJ - 1)
    jn = lambda j: jnp.maximum(j - NJ, 0)
    mod_spec = lambda c: pl.BlockSpec((None, None, 1, D), lambda i, j: (grp(i), c, 0, 0))
    return pl.pallas_call(
        functools.partial(_swiglu_kernel, moe=False),
        grid=(M_TOK // TMF, NJ + NN),
        in_specs=[
            pl.BlockSpec((TMF, D), lambda i, j: (i, 0)),
            pl.BlockSpec((None, D, TH), lambda i, j: (f, 0, jh(j))),
            pl.BlockSpec((None, D, TH), lambda i, j: (f, 0, jh(j))),
            pl.BlockSpec((None, FFN_HIDDEN, TN2), lambda i, j: (f, 0, jn(j))),
            pl.BlockSpec((1, D), lambda i, j: (0, 0)),
            mod_spec(sh_c), mod_spec(sc_c),
            pl.BlockSpec((TMF, TN2), lambda i, j: (i, jn(j))),
            pl.BlockSpec((None, None, 1, TN2), lambda i, j: (grp(i), gate_c, 0, jn(j))),
        ],
        out_specs=pl.BlockSpec((TMF, TN2), lambda i, j: (i, jn(j))),
        out_shape=jax.ShapeDtypeStruct((M_TOK, D), F32),
        scratch_shapes=_SWIGLU_SCRATCH,
        compiler_params=_cparams(("parallel", "arbitrary")),
        name="ffn_dense",
    )(x, wg, wu, wd, norm_g.reshape(1, D), mods, mods, x, mods)


def _ffn_moe(xs, wg, wu, wd, f, tile_expert, tile_valid):
    jh = lambda i, j, tv: jnp.where(tv[i] > 0, jnp.minimum(j, NJ - 1), NJ - 1)
    jn = lambda i, j, tv: jnp.where(tv[i] > 0, jnp.maximum(j - NJ, 0), NN - 1)
    return pl.pallas_call(
        functools.partial(_swiglu_kernel, moe=True),
        grid_spec=pltpu.PrefetchScalarGridSpec(
            num_scalar_prefetch=2,
            grid=(MOE_TILES, NJ + NN),
            in_specs=[
                pl.BlockSpec((TMF, D), lambda i, j, te, tv: (i, 0)),
                pl.BlockSpec((None, None, D, TH), lambda i, j, te, tv: (f, te[i], 0, jh(i, j, tv))),
                pl.BlockSpec((None, None, D, TH), lambda i, j, te, tv: (f, te[i], 0, jh(i, j, tv))),
                pl.BlockSpec((None, None, FFN_HIDDEN, TN2), lambda i, j, te, tv: (f, te[i], 0, jn(i, j, tv))),
            ],
            out_specs=pl.BlockSpec((TMF, TN2), lambda i, j, te, tv: (i, jnp.maximum(j - NJ, 0))),
            scratch_shapes=_SWIGLU_SCRATCH,
        ),
        out_shape=jax.ShapeDtypeStruct((MOE_ROWS, D), F32),
        compiler_params=_cparams(("parallel", "arbitrary")),
        name="ffn_moe",
    )(tile_expert, tile_valid, xs, wg, wu, wd)


def _router_kernel(x_ref, g_ref, sh_ref, sc_ref, r_ref, h_ref, idx_ref, gates_ref):
    h = _rms(x_ref[...], g_ref[...], NORM_EPS) * (1.0 + sc_ref[...]) + sh_ref[...]
    h_ref[...] = h
    w = r_ref[...]
    h_hi = h.astype(BF16)
    h_lo = (h - h_hi.astype(F32)).astype(BF16)
    w_hi = w.astype(BF16)
    w_lo = (w - w_hi.astype(F32)).astype(BF16)
    logits = _dot(h_hi, w_hi) + (_dot(h_lo, w_hi) + _dot(h_hi, w_lo))
    col = lax.broadcasted_iota(jnp.int32, logits.shape, 1)
    colf = col.astype(F32)
    lg = jnp.where(col < N_EXPERTS, logits, -jnp.inf)
    m1 = jnp.max(lg, axis=-1, keepdims=True)
    i1 = jnp.min(jnp.where(lg == m1, colf, float(LANE)), axis=-1, keepdims=True)
    lg2 = jnp.where(colf == i1, -jnp.inf, lg)
    m2 = jnp.max(lg2, axis=-1, keepdims=True)
    i2 = jnp.min(jnp.where(lg2 == m2, colf, float(LANE)), axis=-1, keepdims=True)
    i1 = i1.astype(jnp.int32)
    i2 = i2.astype(jnp.int32)
    e = jnp.exp(m2 - m1)
    g1 = 1.0 / (1.0 + e)
    g2 = e / (1.0 + e)
    idx_ref[...] = jnp.where(col == 0, i1, jnp.where(col == 1, i2, 0))
    gates_ref[...] = jnp.where(col == 0, g1, jnp.where(col == 1, g2, 0.0))


def _router(x, norm_g, mods, sh_c, sc_c, router):
    r_pad = jnp.pad(router, ((0, 0), (0, LANE - N_EXPERTS)))
    grp = lambda i: _group_of_tile(i, TM)
    return pl.pallas_call(
        _router_kernel,
        grid=(M_TOK // TM,),
        in_specs=[
            pl.BlockSpec((TM, D), lambda i: (i, 0)),
            pl.BlockSpec((1, D), lambda i: (0, 0)),
            pl.BlockSpec((None, None, 1, D), lambda i: (grp(i), sh_c, 0, 0)),
            pl.BlockSpec((None, None, 1, D), lambda i: (grp(i), sc_c, 0, 0)),
            pl.BlockSpec((D, LANE), lambda i: (0, 0)),
        ],
        out_specs=[
            pl.BlockSpec((TM, D), lambda i: (i, 0)),
            pl.BlockSpec((TM, LANE), lambda i: (i, 0)),
            pl.BlockSpec((TM, LANE), lambda i: (i, 0)),
        ],
        out_shape=[
            jax.ShapeDtypeStruct((M_TOK, D), F32),
            jax.ShapeDtypeStruct((M_TOK, LANE), jnp.int32),
            jax.ShapeDtypeStruct((M_TOK, LANE), F32),
        ],
        compiler_params=_cparams(("parallel",)),
        name="router",
    )(x, norm_g.reshape(1, D), mods, mods, r_pad)


def _route_plan(e1, e2):
    experts = jnp.arange(N_EXPERTS, dtype=jnp.int32)
    e = jnp.stack([e1, e2]).reshape(2, N_SRC, DISPATCH_ROWS)
    oh = (e[..., None] == experts).astype(jnp.int32)
    csum = jnp.cumsum(oh, axis=2)
    cnt_k = csum[:, :, -1, :]
    rank = jnp.sum((csum - oh) * oh, axis=-1)
    rank = rank + jnp.stack([jnp.zeros_like(rank[0]), jnp.sum(oh[1] * cnt_k[0][:, None, :], axis=-1)])
    seg = ((cnt_k[0] + cnt_k[1] + SEG_ALIGN - 1) // SEG_ALIGN) * SEG_ALIGN
    loc_off = jnp.cumsum(seg, axis=1) - seg
    rows_e = jnp.sum(seg, axis=0)
    padded = ((rows_e + TMF - 1) // TMF) * TMF
    ends = jnp.cumsum(padded)
    starts = ends - padded
    gpos = starts[None, :] + jnp.cumsum(seg, axis=0) - seg
    local_pos = jnp.sum(oh * loc_off[None, :, None, :], axis=-1) + rank
    pos = (jnp.sum(oh * gpos[None, :, None, :], axis=-1) + rank).reshape(-1)
    tile_start = jnp.arange(MOE_TILES, dtype=jnp.int32) * TMF
    te = jnp.minimum(jnp.sum((tile_start[:, None] >= ends[None, :]).astype(jnp.int32), axis=1),
                     N_EXPERTS - 1)
    used = tile_start < ends[-1]
    valid = jnp.where(used, jnp.clip(rows_e[te] - (tile_start - starts[te]), 0, TMF), 0)
    last_e = te[jnp.maximum(ends[-1] // TMF - 1, 0)]
    te = jnp.where(used, te, last_e)
    sub_off = jnp.arange(TMF // SUB, dtype=jnp.int32) * SUB
    sub_valid = jnp.clip(valid[:, None] - sub_off[None, :], 0, SUB).reshape(-1)
    i32 = lambda a: a.astype(jnp.int32)
    segs = (i32(seg.reshape(-1)), i32(loc_off.reshape(-1)), i32(gpos.reshape(-1)))
    return i32(pos), i32(jnp.transpose(local_pos, (1, 0, 2))), segs, i32(te), i32(valid), i32(sub_valid)


def _dispatch_kernel(seg_ref, loc_ref, gpos_ref, sv_ref, h_ref, lp_ref, xs_hbm, sorted_ref, zero_ref, sem, zsem):
    i = pl.program_id(0)

    @pl.when(i == 0)
    def _():
        zero_ref[...] = jnp.zeros_like(zero_ref)

        def zero_copy(b):
            return pltpu.make_async_copy(zero_ref, xs_hbm.at[pl.ds(pl.multiple_of(b * SUB, SUB), SUB)], zsem)

        def start(b, c):
            pl.when(sv_ref[b] < SUB)(lambda: zero_copy(b).start())
            return c

        def wait(b, c):
            pl.when(sv_ref[b] < SUB)(lambda: zero_copy(b).wait())
            return c

        lax.fori_loop(0, MOE_ROWS // SUB, start, 0)
        lax.fori_loop(0, MOE_ROWS // SUB, wait, 0)

    hb = h_ref[...].astype(BF16)
    lp = lp_ref[...]
    row = lax.broadcasted_iota(jnp.int32, (LOCAL_ROWS, DISPATCH_ROWS), 0)
    pick = ((row == lp[0:1, :]) | (row == lp[1:2, :])).astype(BF16)
    sorted_ref[...] = _dot(pick, hb)

    def segment_copies(fn):
        for ex in range(N_EXPERTS):
            n = seg_ref[i * N_EXPERTS + ex]
            src0 = loc_ref[i * N_EXPERTS + ex]
            dst0 = gpos_ref[i * N_EXPERTS + ex]
            for size in SEG_CHUNKS:
                off = (n // (2 * size)) * (2 * size)
                src = pl.multiple_of(src0 + off, SEG_ALIGN)
                dst = pl.multiple_of(dst0 + off, SEG_ALIGN)
                copy = pltpu.make_async_copy(sorted_ref.at[pl.ds(src, size)], xs_hbm.at[pl.ds(dst, size)], sem)
                pl.when((n & size) != 0)(functools.partial(fn, copy))

    segment_copies(lambda copy: copy.start())
    segment_copies(lambda copy: copy.wait())


def _moe_dispatch(h, local_pos, segs, sub_valid):
    return pl.pallas_call(
        _dispatch_kernel,
        grid_spec=pltpu.PrefetchScalarGridSpec(
            num_scalar_prefetch=4,
            grid=(N_SRC,),
            in_specs=[pl.BlockSpec((DISPATCH_ROWS, D), lambda i, *_: (i, 0)),
                      pl.BlockSpec((None, 2, DISPATCH_ROWS), lambda i, *_: (i, 0, 0))],
            out_specs=pl.BlockSpec(memory_space=pl.ANY),
            scratch_shapes=[pltpu.VMEM((LOCAL_ROWS, D), F32), pltpu.VMEM((SUB, D), F32),
                            pltpu.SemaphoreType.DMA, pltpu.SemaphoreType.DMA],
        ),
        out_shape=jax.ShapeDtypeStruct((MOE_ROWS, D), F32),
        compiler_params=_cparams(("arbitrary",)),
        name="moe_dispatch",
    )(*segs, sub_valid, h, local_pos)


def _combine_kernel(pos_ref, y_hbm, x_ref, tg_ref, gate_ref, o_ref, y_ref, sem):
    i = pl.program_id(0)
    slot = i % 2

    def picked(k, s):
        return pltpu.make_async_copy(y_hbm.at[pl.ds(0, COMBINE_ROWS)], y_ref.at[s, k], sem.at[s, k])

    def gather_tile(t, s):
        def issue(r, c):
            for k in range(2):
                p = pos_ref[k * M_TOK + t * COMBINE_ROWS + r]
                pltpu.make_async_copy(y_hbm.at[pl.ds(p, 1)], y_ref.at[s, k, pl.ds(r, 1)], sem.at[s, k]).start()
            return c

        lax.fori_loop(0, COMBINE_ROWS, issue, 0, unroll=DMA_UNROLL)

    pl.when(i == 0)(lambda: gather_tile(0, 0))
    pl.when(i + 1 < pl.num_programs(0))(lambda: gather_tile(i + 1, 1 - slot))
    picked(0, slot).wait()
    picked(1, slot).wait()
    tg = tg_ref[...]
    moe_out = tg[:, 0:1] * y_ref[slot, 0] + tg[:, 1:2] * y_ref[slot, 1]
    o_ref[...] = x_ref[...] + gate_ref[...] * moe_out


def _moe_combine(y, pos, x, top_gates, mods, gate_c):
    grp = lambda i: _group_of_tile(i, COMBINE_ROWS)
    return pl.pallas_call(
        _combine_kernel,
        grid_spec=pltpu.PrefetchScalarGridSpec(
            num_scalar_prefetch=1,
            grid=(M_TOK // COMBINE_ROWS,),
            in_specs=[
                pl.BlockSpec(memory_space=pl.ANY),
                pl.BlockSpec((COMBINE_ROWS, D), lambda i, p: (i, 0)),
                pl.BlockSpec((COMBINE_ROWS, LANE), lambda i, p: (i, 0)),
                pl.BlockSpec((None, None, 1, D), lambda i, p: (grp(i), gate_c, 0, 0)),
            ],
            out_specs=pl.BlockSpec((COMBINE_ROWS, D), lambda i, p: (i, 0)),
            scratch_shapes=[pltpu.VMEM((2, 2, COMBINE_ROWS, D), F32), pltpu.SemaphoreType.DMA((2, 2))],
        ),
        out_shape=jax.ShapeDtypeStruct((M_TOK, D), F32),
        compiler_params=_cparams(("arbitrary",)),
        name="moe_combine",
    )(pos, y, x, top_gates, mods)


def _moe(x, norm_g, mods, sh_c, sc_c, gate_c, router, wg, wu, wd, f):
    h, idx, gates = _router(x, norm_g, mods, sh_c, sc_c, router)
    pos, local_pos, segs, te, tv, sub_valid = _route_plan(idx[:, 0], idx[:, 1])
    xs = _moe_dispatch(h, local_pos, segs, sub_valid)
    ys = _ffn_moe(xs, wg, wu, wd, f, te, tv)
    return _moe_combine(ys, pos, x, gates, mods, gate_c)


def _rope_tables(width, lo, hi):
    pos = jnp.arange(T_LAT)
    rows = (pos // GRID_W).astype(F32)
    cols = (pos % GRID_W).astype(F32)
    n_freq = 16
    inv = ROPE_BASE ** (-jnp.arange(n_freq, dtype=F32) / n_freq)
    ang = jnp.concatenate([rows[:, None] * inv[None, :], cols[:, None] * inv[None, :]], axis=-1)
    cos, sin = jnp.cos(ang), jnp.sin(ang)
    lane = np.arange(width)
    active = (lane >= lo) & (lane < hi)
    reps = width // 64
    c = jnp.tile(jnp.concatenate([cos, cos], axis=-1), (1, reps))
    s = jnp.tile(jnp.concatenate([-sin, sin], axis=-1), (1, reps))
    return jnp.where(active[None, :], c, 1.0), jnp.where(active[None, :], s, 0.0)


def _rope_kernel(x_ref, c_ref, s_ref, o_ref, *, width):
    lane = lax.broadcasted_iota(jnp.int32, (1, width), 1)
    first = (lane & 63) < 32
    c = c_ref[...]
    s = s_ref[...]
    for j in range(x_ref.shape[-1] // width):
        cols = slice(j * width, (j + 1) * width)
        x = x_ref[:, cols]
        partner = jnp.where(first, pltpu.roll(x, width - 32, 1), pltpu.roll(x, 32, 1))
        o_ref[:, cols] = (x * c + partner * s).astype(o_ref.dtype)


def _rope(x, c, s, *, planes, row0, width, col0=0, ncol=1, tr=512):
    r0 = row0 // tr
    nt = T_LAT // tr
    if x.ndim == 3:
        ncol = x.shape[2] // width
        x_spec = pl.BlockSpec((None, tr, ncol * width), lambda p, i: (p, i + r0, 0))
    else:
        x_spec = pl.BlockSpec((tr, ncol * width), lambda p, i: (i + r0, col0 // ncol))
    return pl.pallas_call(
        functools.partial(_rope_kernel, width=width),
        grid=(planes, M_LAT // tr),
        in_specs=[
            x_spec,
            pl.BlockSpec((tr, width), lambda p, i: (i % nt, 0)),
            pl.BlockSpec((tr, width), lambda p, i: (i % nt, 0)),
        ],
        out_specs=pl.BlockSpec((None, tr, ncol * width), lambda p, i: (p, i, 0)),
        out_shape=jax.ShapeDtypeStruct((planes, M_LAT, ncol * width), BF16),
        compiler_params=_cparams(("parallel", "parallel")),
        name="rope",
    )(x, c, s)


def _attn_kernel(*refs, kind, hb, nseg, scale, lam_init):
    it = iter(refs)
    q_ref = next(it)
    k_refs = [next(it) for _ in range(nseg)]
    kpe_refs = [next(it) for _ in range(nseg)] if kind == "mla" else None
    v_refs = [next(it) for _ in range(nseg)] if kind != "mla" else k_refs
    lam_ref = next(it) if kind == "diff" else None
    subln_ref = next(it) if kind == "diff" else None
    o_ref = next(it)

    lane = lax.broadcasted_iota(jnp.int32, (1, LANE), 1)
    lo_half = lane < 64
    log2_scale = scale * LOG2_E

    def softmax_parts(s_list):
        t_list = [s * log2_scale for s in s_list]
        m = functools.reduce(jnp.maximum, [jnp.max(t, axis=-1, keepdims=True) for t in t_list])
        p_list = [jnp.exp2(t - m) for t in t_list]
        l = functools.reduce(jnp.add, [jnp.sum(p, axis=-1, keepdims=True) for p in p_list])
        return p_list, 1.0 / l

    def attend(s_list, v_list):
        p_list, inv_l = softmax_parts(s_list)
        return functools.reduce(jnp.add, [_dot(p.astype(BF16), v) for p, v in zip(p_list, v_list)]) * inv_l

    for hh in range(hb):
        if kind == "mla":
            q2 = q_ref[:, hh * 256:(hh + 1) * 256].astype(BF16)
            s_list = []
            for kr, pr in zip(k_refs, kpe_refs):
                kcat = jnp.concatenate([kr[:, hh * 256:hh * 256 + 128].astype(BF16), pr[...].astype(BF16)], axis=1)
                s_list.append(_dot_nt(q2, kcat))
            v2 = [vr[:, hh * 256 + 128:hh * 256 + 256].astype(BF16) for vr in v_refs]
            o_ref[:, hh * LANE:(hh + 1) * LANE] = attend(s_list, v2)
            continue

        sl = slice(hh * LANE, (hh + 1) * LANE)
        q2 = q_ref[:, sl]
        k2 = [kr[:, sl].astype(BF16) for kr in k_refs]
        v2 = [vr[:, sl].astype(BF16) for vr in v_refs]
        tq = q2.shape[0]
        q_heads = [jnp.where(lo_half, q2, 0.0), jnp.where(lo_half, 0.0, q2)]
        if hb > 1:
            qs = jnp.concatenate(q_heads, axis=0).astype(BF16)
            o2 = attend([_dot_nt(qs, kk) for kk in k2], v2)
            o_heads = [o2[:tq], o2[tq:]]
            if kind == "diff":
                o_diff = o_heads[0] - lam_ref[...][:, :1] * o_heads[1]
        elif kind == "na":
            o_heads = [attend([_dot_nt(qh.astype(BF16), kk) for kk in k2], v2) for qh in q_heads]
        else:
            (p0, il0), (p1, il1) = [softmax_parts([_dot_nt(qh.astype(BF16), kk) for kk in k2]) for qh in q_heads]
            lam = lam_ref[...][:, :1]
            o_diff = functools.reduce(jnp.add, [
                _dot((a * il0 - lam * (b * il1)).astype(BF16), vv) for a, b, vv in zip(p0, p1, v2)])
        if kind == "na":
            o_ref[:, sl] = jnp.where(lo_half, o_heads[0], o_heads[1])
        else:
            o_ref[:, sl] = _rms(o_diff, subln_ref[...], DIFF_SUBLN_EPS) * (1.0 - lam_init)


def _attention(kind, *, grid, q, ks, vs, kpes=None, extra=(), rows, out_spec, hb, scale, lam_init=0.0,
               name="attn"):
    ops = [q] + list(ks) + (list(kpes) if kind == "mla" else []) + (list(vs) if kind != "mla" else [])
    ops += list(extra)
    return pl.pallas_call(
        functools.partial(_attn_kernel, kind=kind, hb=hb, nseg=len(ks), scale=scale, lam_init=lam_init),
        grid=grid,
        in_specs=[s for _, s in ops],
        out_specs=out_spec,
        out_shape=jax.ShapeDtypeStruct((rows, D), F32),
        compiler_params=_cparams(("parallel",) * len(grid)),
        name=name,
    )(*[a for a, _ in ops])


NA_QROWS = 4
NA_KROWS = 12
NA_TQ = NA_QROWS * GRID_W
NA_TK = NA_KROWS * GRID_W


def _na_bias_table(rpb):
    h = rpb.shape[0]
    qc = np.arange(GRID_W)[:, None]
    kc = np.arange(GRID_W)[None, :]
    qs = np.clip(qc - NA_WIN_C // 2, 0, GRID_W - NA_WIN_C)
    col_ok = (kc >= qs) & (kc < qs + NA_WIN_C)
    ic = np.clip(kc - qc + NA_WIN_C - 1, 0, 2 * NA_WIN_C - 2)
    n_ir = 2 * NA_WIN_R - 1
    onehot = (ic.reshape(1, -1) == np.arange(2 * NA_WIN_C - 1)[:, None]).astype(np.float32)
    tab = jnp.einsum("hrm,mx->hrx", rpb.astype(F32) * LOG2_E, jnp.asarray(onehot),
                     precision=lax.Precision.HIGHEST)
    tab = jnp.where(jnp.asarray(col_ok.reshape(1, 1, -1)), tab, NEG_INF).reshape(h, n_ir, GRID_W, GRID_W)
    tab = jnp.concatenate([tab, tab], axis=-1)
    n_rows = T_LAT // GRID_W
    n_g = n_rows // NA_QROWS
    ir_idx = np.zeros((3, NA_QROWS, NA_KROWS), np.int32)
    row_ok = np.zeros((3, NA_QROWS, NA_KROWS), bool)
    for pat, g in enumerate((0, 1, n_g - 1)):
        base = int(np.clip(NA_QROWS * g - NA_QROWS, 0, n_rows - NA_KROWS))
        for qr in range(NA_QROWS):
            r = NA_QROWS * g + qr
            r0 = int(np.clip(r - NA_WIN_R // 2, 0, n_rows - NA_WIN_R))
            for kr in range(NA_KROWS):
                ok = r0 <= base + kr < r0 + NA_WIN_R
                row_ok[pat, qr, kr] = ok
                ir_idx[pat, qr, kr] = (base + kr - r + NA_WIN_R - 1) if ok else 0

    def build(tab_ref, o_ref):
        pat_id = pl.program_id(0)
        lo_half = lax.broadcasted_iota(jnp.int32, (1, LANE), 1) < GRID_W
        masked = jnp.full((GRID_W, LANE), NEG_INF, F32)
        for pat in range(3):
            @pl.when(pat_id == pat)
            def _(pat=pat):
                for hd in range(heads_per_step):
                    for qr in range(NA_QROWS):
                        for kp in range(NA_KROWS // 2):
                            halves = [tab_ref[hd, int(ir_idx[pat, qr, kr])] if row_ok[pat, qr, kr] else masked
                                      for kr in (2 * kp, 2 * kp + 1)]
                            o_ref[hd, qr * GRID_W:(qr + 1) * GRID_W, kp * LANE:(kp + 1) * LANE] = jnp.where(
                                lo_half, halves[0], halves[1])

    heads_per_step = 4
    return pl.pallas_call(
        build,
        grid=(3, h // heads_per_step),
        in_specs=[pl.BlockSpec((heads_per_step, n_ir, GRID_W, LANE), lambda p, hh: (hh, 0, 0, 0))],
        out_specs=pl.BlockSpec((None, heads_per_step, NA_TQ, NA_TK), lambda p, hh: (p, hh, 0, 0)),
        out_shape=jax.ShapeDtypeStruct((3, h, NA_TQ, NA_TK), F32),
        compiler_params=_cparams(("parallel", "parallel")),
        name="na_bias",
    )(tab)


def _na_lat_kernel(q_ref, kc_ref, vc_ref, k_ref, v_ref, b_ref, o_ref, *, scale):
    g = pl.program_id(2)
    n_rows = T_LAT // GRID_W
    base = jnp.clip(NA_QROWS * g - NA_QROWS, 0, n_rows - NA_KROWS) * GRID_W
    base = pl.multiple_of(base, GRID_W)
    lane = lax.broadcasted_iota(jnp.int32, (1, LANE), 1)
    lo_half = lane < 64
    q2 = q_ref[...]
    kc = kc_ref[...].astype(BF16)
    vc = vc_ref[...].astype(BF16)
    kl = k_ref[pl.ds(base, NA_TK), :].astype(BF16)
    vl = v_ref[pl.ds(base, NA_TK), :].astype(BF16)
    log2_scale = scale * LOG2_E
    outs = []
    for n in range(2):
        qm = jnp.where(lo_half if n == 0 else jnp.logical_not(lo_half), q2, 0.0).astype(BF16)
        s_c = _dot_nt(qm, kc) * log2_scale
        s_l = _dot_nt(qm, kl) * log2_scale + b_ref[n]
        m = jnp.maximum(jnp.max(s_c, axis=-1, keepdims=True), jnp.max(s_l, axis=-1, keepdims=True))
        p_c = jnp.exp2(s_c - m)
        p_l = jnp.exp2(s_l - m)
        l = jnp.sum(p_c, axis=-1, keepdims=True) + jnp.sum(p_l, axis=-1, keepdims=True)
        o = _dot(p_c.astype(BF16), vc) + _dot(p_l.astype(BF16), vl)
        outs.append(o * (1.0 / l))
    o_ref[...] = jnp.where(lo_half, outs[0], outs[1])


def _na_lat(qkv, cache_k, cache_v, bias):
    n_g = T_LAT // NA_TQ
    q_blk0 = M_CTX // NA_TQ
    kv_blk0 = M_CTX // T_LAT
    pat = lambda g: jnp.where(g == 0, 0, jnp.where(g == n_g - 1, 2, 1))
    return pl.pallas_call(
        functools.partial(_na_lat_kernel, scale=0.125),
        grid=(N_LAT_SEQ, D // LANE, n_g),
        in_specs=[
            pl.BlockSpec((None, NA_TQ, LANE), lambda b, h, g: (0, q_blk0 + b * n_g + g, h)),
            pl.BlockSpec((None, PAST, LANE), lambda b, h, g: (b, 0, h)),
            pl.BlockSpec((None, PAST, LANE), lambda b, h, g: (b, 0, h)),
            pl.BlockSpec((None, T_LAT, LANE), lambda b, h, g: (1, kv_blk0 + b, h)),
            pl.BlockSpec((None, T_LAT, LANE), lambda b, h, g: (2, kv_blk0 + b, h)),
            pl.BlockSpec((None, 2, NA_TQ, NA_TK), lambda b, h, g: (pat(g), h, 0, 0)),
        ],
        out_specs=pl.BlockSpec((NA_TQ, LANE), lambda b, h, g: (b * n_g + g, h)),
        out_shape=jax.ShapeDtypeStruct((M_LAT, D), F32),
        compiler_params=_cparams(("parallel", "parallel", "parallel")),
        name="na_latent",
    )(qkv, cache_k, cache_v, qkv, qkv, bias)


SEQ_T_PER_STEP = 4


def _seq_t_kernel(x_ref, o_ref):
    for s in range(SEQ_T_PER_STEP):
        o_ref[s] = x_ref[s * T_CTX:(s + 1) * T_CTX, :].T


def _seq_transpose(qkv, plane):
    return pl.pallas_call(
        _seq_t_kernel,
        grid=(N_CTX_SEQ // SEQ_T_PER_STEP,),
        in_specs=[pl.BlockSpec((None, SEQ_T_PER_STEP * T_CTX, D), lambda b: (plane, b, 0))],
        out_specs=pl.BlockSpec((SEQ_T_PER_STEP, D, T_CTX), lambda b: (b, 0, 0)),
        out_shape=jax.ShapeDtypeStruct((N_CTX_SEQ, D, T_CTX), F32),
        compiler_params=_cparams(("parallel",)),
        name="seq_transpose",
    )(qkv)


def kernel(x_prompt, x_sample, c, c_ctx, cache_na_k, cache_na_v, cache_diff_k, cache_diff_v, cache_mla_ckv, cache_mla_kpe, ada_w, ada_b, norm1, norm2, pool_w, pool_scale, na_qkv, na_rpb, na_out, diff_qkv, diff_lambda, diff_subln, diff_out, mla_qa, mla_qnorm, mla_qb, mla_kva, mla_kvnorm, mla_kvb, mla_out, ffn_gate, ffn_up, ffn_down, moe_router, moe_gate, moe_up, moe_down, final_norm):
    depth = ada_w.shape[0]
    x = None
    cond8 = jnp.concatenate([c_ctx[None, :], c, jnp.zeros((5, D), F32)], axis=0)
    mods_all = _ada(cond8, ada_w, ada_b)[:, :3].reshape(depth, 3, 6, 1, D)
    outs = {}

    for l in range(depth):
        kind, li = l % 4, l // 4
        mods = mods_all[l]
        if kind == 0:
            x_pair = (x_prompt.reshape(M_CTX, D), x_sample.reshape(M_LAT, D)) if x is None else (x[:M_CTX], x[M_CTX:])
            h = _modrows(x_pair, norm1[l], mods, 0, 1, rows=M_TOK, row0=0, width=D)
            x = _pool(h, *x_pair, pool_w[li], pool_scale[li], mods, 2)
        elif kind == 1:
            qkv = _linear(x, na_qkv[li], tn=D, norm_g=norm1[l], mods=mods, sh_c=0, sc_c=1, planes=True,
                          name="na_qkv")
            to_cache = lambda a: jnp.transpose(a.reshape(N_CTX_SEQ, 1, 16, 64, T_CTX), (0, 1, 4, 2, 3))
            outs["na_k"] = to_cache(_seq_transpose(qkv, 1))
            outs["na_v"] = to_cache(_seq_transpose(qkv, 2))
            seq_spec = lambda p: pl.BlockSpec((None, T_CTX, D), lambda b, p=p: (p, b, 0))
            o_ctx = _attention(
                "na", grid=(N_CTX_SEQ,), q=(qkv, seq_spec(0)), ks=[(qkv, seq_spec(1))], vs=[(qkv, seq_spec(2))],
                rows=M_CTX, out_spec=pl.BlockSpec((T_CTX, D), lambda b: (b, 0)), hb=D // LANE, scale=0.125,
                name="na_ctx")
            bias = _na_bias_table(na_rpb[li])
            o_lat = _na_lat(qkv, cache_na_k[:, li].reshape(N_LAT_SEQ, PAST, D),
                            cache_na_v[:, li].reshape(N_LAT_SEQ, PAST, D), bias)
            x = _linear((o_ctx, o_lat), na_out[li], tn=D, tm=TM, resid=x, mods=mods, gate_c=2, name="na_out")
        elif kind == 2:
            lam_init = 0.8 - 0.6 * math.exp(-0.3 * l)
            lp = diff_lambda[li].astype(F32)
            lam = jnp.exp(jnp.sum(lp[0] * lp[1])) - jnp.exp(jnp.sum(lp[2] * lp[3])) + lam_init
            lam_row = jnp.full((1, LANE), lam, F32)
            subln = diff_subln[li].reshape(1, LANE)
            extra = [(lam_row, pl.BlockSpec((1, LANE), lambda *_: (0, 0))),
                     (subln, pl.BlockSpec((1, LANE), lambda *_: (0, 0)))]
            qkv = _linear(x, diff_qkv[li], tn=D, norm_g=norm1[l], mods=mods, sh_c=0, sc_c=1, planes=True,
                          name="diff_qkv")
            outs["diff_k"] = jnp.transpose(
                _seq_transpose(qkv, 1).reshape(N_CTX_SEQ, 1, 8, 2, 64, T_CTX), (0, 1, 5, 2, 3, 4))
            outs["diff_v"] = qkv[2, :M_CTX].reshape(N_CTX_SEQ, 1, T_CTX, 8, 128)
            seq_spec = lambda p: pl.BlockSpec((None, T_CTX, D), lambda b, p=p: (p, b, 0))
            o_ctx = _attention(
                "diff", grid=(N_CTX_SEQ,), q=(qkv, seq_spec(0)), ks=[(qkv, seq_spec(1))], vs=[(qkv, seq_spec(2))],
                extra=extra, rows=M_CTX, out_spec=pl.BlockSpec((T_CTX, D), lambda b: (b, 0)), hb=D // LANE,
                scale=0.125, lam_init=lam_init, name="diff_ctx")
            rc, rs = _rope_tables(LANE, 0, LANE)
            qk_r = _rope(qkv, rc, rs, planes=2, row0=M_CTX, width=LANE)
            tq = LATENT_TQ
            nq = T_LAT // tq
            ck = cache_diff_k[:, li].reshape(N_LAT_SEQ, PAST, D)
            cv = cache_diff_v[:, li].reshape(N_LAT_SEQ, PAST, D)
            o_lat = _attention(
                "diff", grid=(N_LAT_SEQ, D // LANE, nq),
                q=(qk_r, pl.BlockSpec((None, tq, LANE), lambda b, h, i: (0, b * nq + i, h))),
                ks=[(ck, pl.BlockSpec((None, PAST, LANE), lambda b, h, i: (b, 0, h))),
                    (qk_r, pl.BlockSpec((None, T_LAT, LANE), lambda b, h, i: (1, b, h)))],
                vs=[(cv, pl.BlockSpec((None, PAST, LANE), lambda b, h, i: (b, 0, h))),
                    (qkv, pl.BlockSpec((None, T_LAT, LANE), lambda b, h, i: (2, M_CTX // T_LAT + b, h)))],
                extra=extra,
                rows=M_LAT, out_spec=pl.BlockSpec((tq, LANE), lambda b, h, i: (b * nq + i, h)),
                hb=1, scale=0.125, lam_init=lam_init, name="diff_latent")
            x = _linear((o_ctx, o_lat), diff_out[li], tn=D, tm=TM, resid=x, mods=mods, gate_c=2, name="diff_out")
        else:
            hd = MLA_NOPE + MLA_ROPE
            wqb = mla_qb[li].reshape(MLA_Q_LORA, MLA_HEADS, hd)
            wqb = jnp.pad(wqb, ((0, 0), (0, 0), (0, 256 - hd))).reshape(MLA_Q_LORA, MLA_HEADS * 256)
            w_a = jnp.concatenate(
                [mla_qa[li], jnp.pad(mla_kva[li], ((0, 0), (0, 512 - MLA_KV_LORA - MLA_ROPE)))], axis=1)
            a = _linear(x, w_a, tn=D, tm=TM, norm_g=norm1[l], mods=mods, sh_c=0, sc_c=1, name="mla_a")
            q = _linear(a, wqb, tn=D, tm=TM, k=MLA_Q_LORA, x_col=0, norm_g=mla_qnorm[li], name="mla_q")
            zero_mod = jnp.zeros((1, 2, 1, MLA_KV_LORA), F32)
            ckv = _modrows(a, mla_kvnorm[li], zero_mod, 0, 1, rows=M_TOK, row0=0, width=MLA_KV_LORA, col_blk=2,
                           grouped=False)
            outs["mla_ckv"] = ckv[:M_CTX].reshape(N_CTX_SEQ, 1, T_CTX, MLA_KV_LORA)
            outs["mla_kpe"] = a[:M_CTX, 768:768 + MLA_ROPE].reshape(N_CTX_SEQ, 1, T_CTX, MLA_ROPE)
            ckv_all = jnp.concatenate([ckv, cache_mla_ckv[:, li].reshape(N_LAT_SEQ * PAST, MLA_KV_LORA)], axis=0)
            kvb = _linear(ckv_all, mla_kvb[li], tn=D, tm=TM, out_dtype=BF16, name="mla_kvb")
            scale = float(hd) ** -0.5
            o_ctx = _attention(
                "mla", grid=(N_CTX_SEQ,),
                q=(q, pl.BlockSpec((T_CTX, 2048), lambda b: (b, 0))),
                ks=[(kvb, pl.BlockSpec((T_CTX, 2048), lambda b: (b, 0)))],
                kpes=[(a, pl.BlockSpec((T_CTX, LANE), lambda b: (b, 6)))],
                vs=None, rows=M_CTX, out_spec=pl.BlockSpec((T_CTX, D), lambda b: (b, 0)), hb=MLA_HEADS,
                scale=scale, name="mla_ctx")
            rc, rs = _rope_tables(256, 128, 192)
            q_r = _rope(q, rc, rs, planes=1, row0=M_CTX, width=256, ncol=MLA_HEADS)[0]
            rc1, rs1 = _rope_tables(LANE, 0, 64)
            kpe_r = _rope(a, rc1, rs1, planes=1, row0=M_CTX, width=LANE, col0=6)[0]
            kpe_c = jnp.pad(cache_mla_kpe[:, li], ((0, 0), (0, 0), (0, LANE - MLA_ROPE)))
            tq = LATENT_TQ
            nq = T_LAT // tq
            o_lat = _attention(
                "mla", grid=(N_LAT_SEQ, MLA_HEADS, nq),
                q=(q_r, pl.BlockSpec((tq, 256), lambda b, h, i: (b * nq + i, h))),
                ks=[(kvb, pl.BlockSpec((PAST, 256), lambda b, h, i: (M_TOK // PAST + b, h))),
                    (kvb, pl.BlockSpec((T_LAT, 256), lambda b, h, i: (M_CTX // T_LAT + b, h)))],
                kpes=[(kpe_c, pl.BlockSpec((None, PAST, LANE), lambda b, h, i: (b, 0, 0))),
                      (kpe_r, pl.BlockSpec((T_LAT, LANE), lambda b, h, i: (b, 0)))],
                vs=None, rows=M_LAT,
                out_spec=pl.BlockSpec((tq, LANE), lambda b, h, i: (b * nq + i, h)),
                hb=1, scale=scale, name="mla_latent")
            x = _linear((o_ctx, o_lat), mla_out[li], tn=D, tm=TM, resid=x, mods=mods, gate_c=2, name="mla_out")

        f = l // 2
        if l % 2 == 0:
            x = _ffn_dense(x, ffn_gate, ffn_up, ffn_down, f, norm2[l], mods, 3, 4, 5)
        else:
            x = _moe(x, norm2[l], mods, 3, 4, 5, moe_router[f], moe_gate, moe_up, moe_down, f)

    zero_mod = jnp.zeros((1, 2, 1, D), F32)
    y_prompt = _modrows(x, final_norm, zero_mod, 0, 1, rows=M_CTX, row0=0, width=D, grouped=False)
    y_sample = _modrows(x, final_norm, zero_mod, 0, 1, rows=M_LAT, row0=M_CTX, width=D, grouped=False)
    return (y_prompt.reshape(N_CTX_SEQ, T_CTX, D), y_sample.reshape(N_LAT_SEQ, T_LAT, D),
            outs["na_k"], outs["na_v"], outs["diff_k"], outs["diff_v"], outs["mla_ckv"], outs["mla_kpe"])
```

```python
import functools
import math

import numpy as np
import jax
import jax.numpy as jnp
from jax import lax
from jax.experimental import pallas as pl
from jax.experimental.pallas import tpu as pltpu

F32 = jnp.float32
BF16 = jnp.bfloat16

D = 1024
N_CTX_SEQ = 32
T_CTX = 256
N_LAT_SEQ = 2
T_LAT = 2048
PAST = 512
M_CTX = N_CTX_SEQ * T_CTX
M_LAT = N_LAT_SEQ * T_LAT
M_TOK = M_CTX + M_LAT
GRID_W = 64
NORM_EPS = 1e-6
NEG_INF = -1e30
ROPE_BASE = 10000.0
POOL_WINDOWS = (2, 4, 8, 16)
NA_WIN_R = 8
NA_WIN_C = 16
DIFF_SUBLN_EPS = 1e-5
MLA_HEADS = 8
MLA_NOPE = 128
MLA_ROPE = 64
MLA_KV_LORA = 256
MLA_Q_LORA = 512
FFN_HIDDEN = 2816
N_EXPERTS = 8

LOG2_E = math.log2(math.e)
LANE = 128
TM = 1024
TH = 256
SUB = 256
VMEM_LIMIT = 56 * 1024 * 1024

TMF = 2048
MOE_ROWS = 2 * M_TOK + N_EXPERTS * TMF
MOE_TILES = MOE_ROWS // TMF


def _cparams(sem):
    return pltpu.CompilerParams(dimension_semantics=sem, vmem_limit_bytes=VMEM_LIMIT)


def _group_of_tile(i, tm):
    n_ctx = M_CTX // tm
    return jnp.where(i < n_ctx, 0, 1 + (i - n_ctx) // (T_LAT // tm))


def _dot(a, b):
    return jnp.dot(a, b, preferred_element_type=F32)


def _dot_nt(a, b):
    return lax.dot_general(a, b, (((1,), (1,)), ((), ())), preferred_element_type=F32)


def _rms(x, g, eps):
    return x * lax.rsqrt(jnp.mean(x * x, axis=-1, keepdims=True) + eps) * g


def _ada_kernel(c_ref, w_ref, b_ref, o_ref):
    c = c_ref[...]
    s = (c * jax.nn.sigmoid(c)).astype(BF16)
    o_ref[...] = _dot(s, w_ref[...].astype(BF16)) + b_ref[...]


def _ada(cond8, ada_w, ada_b):
    depth = ada_w.shape[0]
    tn = 1536
    return pl.pallas_call(
        _ada_kernel,
        grid=(depth, 6 * D // tn),
        in_specs=[
            pl.BlockSpec((8, D), lambda l, j: (0, 0)),
            pl.BlockSpec((None, D, tn), lambda l, j: (l, 0, j)),
            pl.BlockSpec((None, 1, tn), lambda l, j: (l, 0, j)),
        ],
        out_specs=pl.BlockSpec((None, 8, tn), lambda l, j: (l, 0, j)),
        out_shape=jax.ShapeDtypeStruct((depth, 8, 6 * D), F32),
        compiler_params=_cparams(("parallel", "parallel")),
        name="ada",
    )(cond8, ada_w, ada_b.reshape(depth, 1, 6 * D))


def _token_tile(xc_ref, xl_ref, tile, tm):
    return jnp.where(tile < M_CTX // tm, xc_ref[...], xl_ref[...])


def _split_specs(block, tm, col):
    n_ctx = M_CTX // tm
    return [pl.BlockSpec(block, lambda i, *_: (jnp.minimum(i, n_ctx - 1), col(*_))),
            pl.BlockSpec(block, lambda i, *_: (jnp.maximum(i - n_ctx, 0), col(*_)))]


def _modrows_kernel(*refs, eps, split, tm):
    if split:
        xc_ref, xl_ref, g_ref, sh_ref, sc_ref, o_ref = refs
        x = _token_tile(xc_ref, xl_ref, pl.program_id(0), tm)
    else:
        x_ref, g_ref, sh_ref, sc_ref, o_ref = refs
        x = x_ref[...]
    o_ref[...] = _rms(x, g_ref[...], eps) * (1.0 + sc_ref[...]) + sh_ref[...]


def _modrows(x, g, mods, sh_c, sc_c, *, rows, row0, width, col_blk=0, tm=TM, eps=NORM_EPS,
             grouped=True):
    r0 = row0 // tm
    if grouped:
        grp = lambda i: _group_of_tile(i + r0, tm)
    else:
        grp = lambda i: 0
    split = isinstance(x, tuple)
    if split:
        xs, x_specs = list(x), _split_specs((tm, width), tm, lambda: 0)
    else:
        xs, x_specs = [x], [pl.BlockSpec((tm, width), lambda i: (i + r0, col_blk))]
    return pl.pallas_call(
        functools.partial(_modrows_kernel, eps=eps, split=split, tm=tm),
        grid=(rows // tm,),
        in_specs=x_specs + [
            pl.BlockSpec((1, width), lambda i: (0, 0)),
            pl.BlockSpec((None, None, 1, width), lambda i: (grp(i), sh_c, 0, 0)),
            pl.BlockSpec((None, None, 1, width), lambda i: (grp(i), sc_c, 0, 0)),
        ],
        out_specs=pl.BlockSpec((tm, width), lambda i: (i, 0)),
        out_shape=jax.ShapeDtypeStruct((rows, width), F32),
        compiler_params=_cparams(("parallel",)),
        name="modrows",
    )(*xs, g.reshape(1, width), mods, mods)


def _linear_kernel(*refs, prologue, eps, epilogue, split, tm, tn, planes):
    it = iter(refs)
    x_ref = next(it)
    x_lat_ref = next(it) if split else None
    w_ref = next(it)
    g_ref = next(it) if prologue != "none" else None
    sh_ref = next(it) if prologue == "mod" else None
    sc_ref = next(it) if prologue == "mod" else None
    res_ref = next(it) if epilogue == "resid" else None
    gate_ref = next(it) if epilogue == "resid" else None
    o_ref = next(it)
    wb_ref = next(it)
    n = w_ref.shape[1]

    @pl.when(pl.program_id(0) == 0)
    def _():
        wb_ref[...] = w_ref[...].astype(BF16)

    def run(src_ref):
        x = src_ref[...]
        if prologue != "none":
            x = _rms(x, g_ref[...], eps)
        if prologue == "mod":
            x = x * (1.0 + sc_ref[...]) + sh_ref[...]
        xb = x.astype(BF16)
        for p in range(n // tn):
            cols = slice(p * tn, (p + 1) * tn)
            acc = _dot(xb, wb_ref[:, cols])
            if epilogue == "resid":
                acc = res_ref[:, cols] + gate_ref[:, cols] * acc
            if planes:
                o_ref[p] = acc.astype(o_ref.dtype)
            else:
                o_ref[:, cols] = acc.astype(o_ref.dtype)

    if split:
        is_ctx = pl.program_id(0) < M_CTX // tm
        pl.when(is_ctx)(lambda: run(x_ref))
        pl.when(jnp.logical_not(is_ctx))(lambda: run(x_lat_ref))
    else:
        run(x_ref)


TML = 512
LATENT_TQ = 256


def _linear(x, w, *, tn, k=None, x_col=0, norm_g=None, eps=NORM_EPS, mods=None, sh_c=None, sc_c=None,
            resid=None, gate_c=None, planes=False, tm=TML, out_dtype=F32, name="linear"):
    split = isinstance(x, tuple)
    n = w.shape[1]
    prologue = "none" if norm_g is None else ("mod" if sh_c is not None else "norm")
    epilogue = "none" if resid is None else "resid"
    grp = lambda i: _group_of_tile(i, tm)
    if split:
        x_ctx, x_lat = x
        m = x_ctx.shape[0] + x_lat.shape[0]
        k = x_ctx.shape[1]
        n_ctx = x_ctx.shape[0] // tm
        args = [x_ctx, x_lat]
        in_specs = [
            pl.BlockSpec((tm, k), lambda i: (jnp.minimum(i, n_ctx - 1), 0)),
            pl.BlockSpec((tm, k), lambda i: (jnp.maximum(i - n_ctx, 0), 0)),
        ]
    else:
        m = x.shape[0]
        k = x.shape[1] if k is None else k
        args = [x]
        in_specs = [pl.BlockSpec((tm, k), lambda i: (i, x_col))]
    args.append(w)
    in_specs.append(pl.BlockSpec((k, n), lambda i: (0, 0)))
    if prologue != "none":
        args.append(norm_g.reshape(1, k))
        in_specs.append(pl.BlockSpec((1, k), lambda i: (0, 0)))
    if prologue == "mod":
        args += [mods, mods]
        in_specs += [
            pl.BlockSpec((None, None, 1, k), lambda i: (grp(i), sh_c, 0, 0)),
            pl.BlockSpec((None, None, 1, k), lambda i: (grp(i), sc_c, 0, 0)),
        ]
    if epilogue == "resid":
        args += [resid, mods]
        in_specs += [
            pl.BlockSpec((tm, n), lambda i: (i, 0)),
            pl.BlockSpec((None, None, 1, n), lambda i: (grp(i), gate_c, 0, 0)),
        ]
    if planes:
        out_spec = pl.BlockSpec((n // tn, tm, tn), lambda i: (0, i, 0))
        out_shape = jax.ShapeDtypeStruct((n // tn, m, tn), out_dtype)
    else:
        out_spec = pl.BlockSpec((tm, n), lambda i: (i, 0))
        out_shape = jax.ShapeDtypeStruct((m, n), out_dtype)
    return pl.pallas_call(
        functools.partial(_linear_kernel, prologue=prologue, eps=eps, epilogue=epilogue, split=split, tm=tm,
                          tn=tn, planes=planes),
        grid=(m // tm,),
        in_specs=in_specs,
        out_specs=out_spec,
        out_shape=out_shape,
        scratch_shapes=[pltpu.VMEM((k, n), BF16)],
        compiler_params=_cparams(("arbitrary",)),
        name=name,
    )(*args)


POOL_ROWS = 2048


def _pool_kernel(h_ref, xc_ref, xl_ref, w_ref, ps_ref, gate_ref, o_ref):
    i = pl.program_id(0)
    g = pl.program_id(1)
    seq = jnp.where(i < M_CTX // POOL_ROWS, T_CTX, T_LAT)
    t = lax.broadcasted_iota(jnp.int32, (POOL_ROWS, 1), 0) & (seq - 1)

    def ahead(a, k):
        return jnp.where(t + k < seq, pltpu.roll(a, POOL_ROWS - k, 0), 0.0)

    def behind(a, k):
        return jnp.where(t - k >= 0, pltpu.roll(a, k, 0), 0.0)

    for gi, win in enumerate(POOL_WINDOWS):
        @pl.when(g == gi)
        def _(win=win):
            h = h_ref[...]
            fwd, bwd, k = h, h, 1
            while k < win // 2:
                fwd, bwd, k = fwd + ahead(fwd, k), bwd + behind(bwd, k), 2 * k
            acc = fwd + behind(bwd, 1)
            cnt = jnp.minimum(t + (win - win // 2), seq) - jnp.maximum(t - win // 2, 0)
            pooled = acc / cnt.astype(F32)
            y = _dot((pooled - h).astype(BF16), w_ref[...].astype(BF16)) * ps_ref[...]
            o_ref[...] = _token_tile(xc_ref, xl_ref, i, POOL_ROWS) + gate_ref[...] * y


def _pool(h, x_ctx, x_lat, pool_w, pool_scale, mods, gate_c):
    gw = D // len(POOL_WINDOWS)
    grp = lambda i: _group_of_tile(i, POOL_ROWS)
    return pl.pallas_call(
        _pool_kernel,
        grid=(M_TOK // POOL_ROWS, len(POOL_WINDOWS)),
        in_specs=[
            pl.BlockSpec((POOL_ROWS, gw), lambda i, g: (i, g)),
            *_split_specs((POOL_ROWS, gw), POOL_ROWS, lambda g: g),
            pl.BlockSpec((None, gw, gw), lambda i, g: (g, 0, 0)),
            pl.BlockSpec((1, gw), lambda i, g: (0, g)),
            pl.BlockSpec((None, None, 1, gw), lambda i, g: (grp(i), gate_c, 0, g)),
        ],
        out_specs=pl.BlockSpec((POOL_ROWS, gw), lambda i, g: (i, g)),
        out_shape=jax.ShapeDtypeStruct((M_TOK, D), F32),
        compiler_params=_cparams(("parallel", "parallel")),
        name="pool",
    )(h, x_ctx, x_lat, pool_w, pool_scale.reshape(1, D), mods)


NJ = FFN_HIDDEN // TH
TN2 = 256
NN = D // TN2
STRAIGHT_COUNTS = (TMF // SUB, 4, 5)


def _swiglu_kernel(*refs, moe):
    if moe:
        te_ref, tv_ref, x_ref, wg_ref, wu_ref, wd_ref, o_ref, xb_ref, h_ref = refs
    else:
        x_ref, wg_ref, wu_ref, wd_ref, g_ref, sh_ref, sc_ref, res_ref, gate_ref, o_ref, xb_ref, h_ref = refs
    j = pl.program_id(1)
    n_valid = tv_ref[pl.program_id(0)] if moe else TMF

    def per_sub_block(fn, fn_skipped=None):
        sub_blocks = [pl.ds(s * SUB, SUB) for s in range(TMF // SUB)]

        def leading(count):
            def run():
                for rows in sub_blocks[:count]:
                    fn(rows)
                if fn_skipped is not None:
                    for rows in sub_blocks[count:]:
                        fn_skipped(rows)
            return run

        def valid_rows_only():
            for s, rows in enumerate(sub_blocks):
                pl.when(s * SUB < n_valid)(functools.partial(fn, rows))
                if fn_skipped is not None:
                    pl.when(s * SUB >= n_valid)(functools.partial(fn_skipped, rows))

        if moe:
            n_blocks = (n_valid + SUB - 1) // SUB
            for count in STRAIGHT_COUNTS:
                pl.when(n_blocks == count)(leading(count))
            pl.when(functools.reduce(jnp.logical_and, [n_blocks != c for c in STRAIGHT_COUNTS]))(valid_rows_only)
        else:
            leading(len(sub_blocks))()

    @pl.when(j == 0)
    def _():
        def cast(rows):
            x = x_ref[rows, :]
            if not moe:
                x = _rms(x, g_ref[...], NORM_EPS) * (1.0 + sc_ref[...]) + sh_ref[...]
            xb_ref[rows, :] = x.astype(BF16)

        per_sub_block(cast)

    @pl.when(j < NJ)
    def _():
        wg = wg_ref[...].astype(BF16)
        wu = wu_ref[...].astype(BF16)

        def up(rows):
            xb = xb_ref[rows, :]
            a = _dot(xb, wg)
            u = _dot(xb, wu)
            h_ref[j, rows, :] = (a * jax.nn.sigmoid(a) * u).astype(BF16)

        per_sub_block(up)

    @pl.when(j >= NJ)
    def _():
        wd = wd_ref[...].astype(BF16)

        def down(rows):
            hcat = jnp.concatenate([h_ref[jj, rows, :] for jj in range(NJ)], axis=1)
            y = _dot(hcat, wd)
            if moe:
                o_ref[rows, :] = y
            else:
                o_ref[rows, :] = res_ref[rows, :] + gate_ref[...] * y

        def skipped(rows):
            o_ref[rows, :] = jnp.zeros((SUB, TN2), F32)

        per_sub_block(down, skipped)


_SWIGLU_SCRATCH = [pltpu.VMEM((TMF, D), BF16), pltpu.VMEM((NJ, TMF, TH), BF16)]


def _ffn_dense(x, wg, wu, wd, f, norm_g, mods, sh_c, sc_c, gate_c):
    grp = lambda i: _group_of_tile(i, TMF)
    jh = lambda j: jnp.minimum(j, NJ - 1)
    jn = lambda j: jnp.maximum(j - NJ, 0)
    mod_spec = lambda c: pl.BlockSpec((None, None, 1, D), lambda i, j: (grp(i), c, 0, 0))
    return pl.pallas_call(
        functools.partial(_swiglu_kernel, moe=False),
        grid=(M_TOK // TMF, NJ + NN),
        in_specs=[
            pl.BlockSpec((TMF, D), lambda i, j: (i, 0)),
            pl.BlockSpec((None, D, TH), lambda i, j: (f, 0, jh(j))),
            pl.BlockSpec((None, D, TH), lambda i, j: (f, 0, jh(j))),
            pl.BlockSpec((None, FFN_HIDDEN, TN2), lambda i, j: (f, 0, jn(j))),
            pl.BlockSpec((1, D), lambda i, j: (0, 0)),
            mod_spec(sh_c), mod_spec(sc_c),
            pl.BlockSpec((TMF, TN2), lambda i, j: (i, jn(j))),
            pl.BlockSpec((None, None, 1, TN2), lambda i, j: (grp(i), gate_c, 0, jn(j))),
        ],
        out_specs=pl.BlockSpec((TMF, TN2), lambda i, j: (i, jn(j))),
        out_shape=jax.ShapeDtypeStruct((M_TOK, D), F32),
        scratch_shapes=_SWIGLU_SCRATCH,
        compiler_params=_cparams(("parallel", "arbitrary")),
        name="ffn_dense",
    )(x, wg, wu, wd, norm_g.reshape(1, D), mods, mods, x, mods)


def _ffn_moe(xs, wg, wu, wd, f, tile_expert, tile_valid):
    jh = lambda i, j, tv: jnp.where(tv[i] > 0, jnp.minimum(j, NJ - 1), NJ - 1)
    jn = lambda i, j, tv: jnp.where(tv[i] > 0, jnp.maximum(j - NJ, 0), NN - 1)
    return pl.pallas_call(
        functools.partial(_swiglu_kernel, moe=True),
        grid_spec=pltpu.PrefetchScalarGridSpec(
            num_scalar_prefetch=2,
            grid=(MOE_TILES, NJ + NN),
            in_specs=[
                pl.BlockSpec((TMF, D), lambda i, j, te, tv: (i, 0)),
                pl.BlockSpec((None, None, D, TH), lambda i, j, te, tv: (f, te[i], 0, jh(i, j, tv))),
                pl.BlockSpec((None, None, D, TH), lambda i, j, te, tv: (f, te[i], 0, jh(i, j, tv))),
                pl.BlockSpec((None, None, FFN_HIDDEN, TN2), lambda i, j, te, tv: (f, te[i], 0, jn(i, j, tv))),
            ],
            out_specs=pl.BlockSpec((TMF, TN2), lambda i, j, te, tv: (i, jnp.maximum(j - NJ, 0))),
            scratch_shapes=_SWIGLU_SCRATCH,
        ),
        out_shape=jax.ShapeDtypeStruct((MOE_ROWS, D), F32),
        compiler_params=_cparams(("parallel", "arbitrary")),
        name="ffn_moe",
    )(tile_expert, tile_valid, xs, wg, wu, wd)


def _router_kernel(x_ref, g_ref, sh_ref, sc_ref, r_ref, h_ref, idx_ref, gates_ref):
    h = _rms(x_ref[...], g_ref[...], NORM_EPS) * (1.0 + sc_ref[...]) + sh_ref[...]
    h_ref[...] = h
    w = r_ref[...]
    h_hi = h.astype(BF16)
    h_lo = (h - h_hi.astype(F32)).astype(BF16)
    w_hi = w.astype(BF16)
    w_lo = (w - w_hi.astype(F32)).astype(BF16)
    logits = _dot(h_hi, w_hi) + (_dot(h_lo, w_hi) + _dot(h_hi, w_lo))
    col = lax.broadcasted_iota(jnp.int32, logits.shape, 1)
    colf = col.astype(F32)
    lg = jnp.where(col < N_EXPERTS, logits, -jnp.inf)
    m1 = jnp.max(lg, axis=-1, keepdims=True)
    i1 = jnp.min(jnp.where(lg == m1, colf, float(LANE)), axis=-1, keepdims=True)
    lg2 = jnp.where(colf == i1, -jnp.inf, lg)
    m2 = jnp.max(lg2, axis=-1, keepdims=True)
    i2 = jnp.min(jnp.where(lg2 == m2, colf, float(LANE)), axis=-1, keepdims=True)
    i1 = i1.astype(jnp.int32)
    i2 = i2.astype(jnp.int32)
    e = jnp.exp(m2 - m1)
    g1 = 1.0 / (1.0 + e)
    g2 = e / (1.0 + e)
    idx_ref[...] = jnp.where(col == 0, i1, jnp.where(col == 1, i2, 0))
    gates_ref[...] = jnp.where(col == 0, g1, jnp.where(col == 1, g2, 0.0))


def _router(x, norm_g, mods, sh_c, sc_c, router):
    r_pad = jnp.pad(router, ((0, 0), (0, LANE - N_EXPERTS)))
    grp = lambda i: _group_of_tile(i, TM)
    return pl.pallas_call(
        _router_kernel,
        grid=(M_TOK // TM,),
        in_specs=[
            pl.BlockSpec((TM, D), lambda i: (i, 0)),
            pl.BlockSpec((1, D), lambda i: (0, 0)),
            pl.BlockSpec((None, None, 1, D), lambda i: (grp(i), sh_c, 0, 0)),
            pl.BlockSpec((None, None, 1, D), lambda i: (grp(i), sc_c, 0, 0)),
            pl.BlockSpec((D, LANE), lambda i: (0, 0)),
        ],
        out_specs=[
            pl.BlockSpec((TM, D), lambda i: (i, 0)),
            pl.BlockSpec((TM, LANE), lambda i: (i, 0)),
            pl.BlockSpec((TM, LANE), lambda i: (i, 0)),
        ],
        out_shape=[
            jax.ShapeDtypeStruct((M_TOK, D), F32),
            jax.ShapeDtypeStruct((M_TOK, LANE), jnp.int32),
            jax.ShapeDtypeStruct((M_TOK, LANE), F32),
        ],
        compiler_params=_cparams(("parallel",)),
        name="router",
    )(x, norm_g.reshape(1, D), mods, mods, r_pad)


def _route_plan(e1, e2):
    e = jnp.concatenate([e1, e2])
    oh = (e[:, None] == jnp.arange(N_EXPERTS, dtype=jnp.int32)[None, :]).astype(jnp.int32)
    csum = jnp.cumsum(oh, axis=0)
    rank = jnp.sum((csum - oh) * oh, axis=1)
    counts = csum[-1]
    padded = ((counts + TMF - 1) // TMF) * TMF
    ends = jnp.cumsum(padded)
    starts = ends - padded
    pos = jnp.sum(oh * starts[None, :], axis=1) + rank
    tile_start = jnp.arange(MOE_TILES, dtype=jnp.int32) * TMF
    te = jnp.minimum(jnp.sum((tile_start[:, None] >= ends[None, :]).astype(jnp.int32), axis=1),
                     N_EXPERTS - 1)
    used = tile_start < ends[-1]
    valid = jnp.where(used, jnp.clip(counts[te] - (tile_start - starts[te]), 0, TMF), 0)
    last_e = te[jnp.maximum(ends[-1] // TMF - 1, 0)]
    te = jnp.where(used, te, last_e)
    sub_off = jnp.arange(TMF // SUB, dtype=jnp.int32) * SUB
    sub_valid = jnp.clip(valid[:, None] - sub_off[None, :], 0, SUB).reshape(-1)
    return pos.astype(jnp.int32), te.astype(jnp.int32), valid.astype(jnp.int32), sub_valid.astype(jnp.int32)


DMA_UNROLL = 8
DISPATCH_ROWS = 1024
COMBINE_ROWS = 512


def _dispatch_kernel(pos_ref, sv_ref, h_ref, xs_hbm, zero_ref, sem, zsem):
    i = pl.program_id(0)

    @pl.when(i == 0)
    def _():
        zero_ref[...] = jnp.zeros_like(zero_ref)

        def zero_copy(b):
            return pltpu.make_async_copy(zero_ref, xs_hbm.at[pl.ds(pl.multiple_of(b * SUB, SUB), SUB)], zsem)

        def start(b, c):
            pl.when(sv_ref[b] < SUB)(lambda: zero_copy(b).start())
            return c

        def wait(b, c):
            pl.when(sv_ref[b] < SUB)(lambda: zero_copy(b).wait())
            return c

        lax.fori_loop(0, MOE_ROWS // SUB, start, 0)
        lax.fori_loop(0, MOE_ROWS // SUB, wait, 0)

    def issue(r, c):
        p1 = pos_ref[i * DISPATCH_ROWS + r]
        p2 = pos_ref[M_TOK + i * DISPATCH_ROWS + r]
        pltpu.make_async_copy(h_ref.at[pl.ds(r, 1)], xs_hbm.at[pl.ds(p1, 1)], sem).start()
        pltpu.make_async_copy(h_ref.at[pl.ds(r, 1)], xs_hbm.at[pl.ds(p2, 1)], sem).start()
        return c

    lax.fori_loop(0, DISPATCH_ROWS, issue, 0, unroll=DMA_UNROLL)
    pltpu.make_async_copy(h_ref, xs_hbm.at[pl.ds(0, DISPATCH_ROWS)], sem).wait()
    pltpu.make_async_copy(h_ref, xs_hbm.at[pl.ds(0, DISPATCH_ROWS)], sem).wait()


def _moe_dispatch(h, pos, sub_valid):
    return pl.pallas_call(
        _dispatch_kernel,
        grid_spec=pltpu.PrefetchScalarGridSpec(
            num_scalar_prefetch=2,
            grid=(M_TOK // DISPATCH_ROWS,),
            in_specs=[pl.BlockSpec((DISPATCH_ROWS, D), lambda i, p, z: (i, 0))],
            out_specs=pl.BlockSpec(memory_space=pl.ANY),
            scratch_shapes=[pltpu.VMEM((SUB, D), F32), pltpu.SemaphoreType.DMA, pltpu.SemaphoreType.DMA],
        ),
        out_shape=jax.ShapeDtypeStruct((MOE_ROWS, D), F32),
        compiler_params=_cparams(("arbitrary",)),
        name="moe_dispatch",
    )(pos, sub_valid, h)


def _combine_kernel(pos_ref, y_hbm, x_ref, tg_ref, gate_ref, o_ref, y_ref, sem):
    i = pl.program_id(0)
    slot = i % 2

    def picked(k, s):
        return pltpu.make_async_copy(y_hbm.at[pl.ds(0, COMBINE_ROWS)], y_ref.at[s, k], sem.at[s, k])

    def gather_tile(t, s):
        def issue(r, c):
            for k in range(2):
                p = pos_ref[k * M_TOK + t * COMBINE_ROWS + r]
                pltpu.make_async_copy(y_hbm.at[pl.ds(p, 1)], y_ref.at[s, k, pl.ds(r, 1)], sem.at[s, k]).start()
            return c

        lax.fori_loop(0, COMBINE_ROWS, issue, 0, unroll=DMA_UNROLL)

    pl.when(i == 0)(lambda: gather_tile(0, 0))
    pl.when(i + 1 < pl.num_programs(0))(lambda: gather_tile(i + 1, 1 - slot))
    picked(0, slot).wait()
    picked(1, slot).wait()
    tg = tg_ref[...]
    moe_out = tg[:, 0:1] * y_ref[slot, 0] + tg[:, 1:2] * y_ref[slot, 1]
    o_ref[...] = x_ref[...] + gate_ref[...] * moe_out


def _moe_combine(y, pos, x, top_gates, mods, gate_c):
    grp = lambda i: _group_of_tile(i, COMBINE_ROWS)
    return pl.pallas_call(
        _combine_kernel,
        grid_spec=pltpu.PrefetchScalarGridSpec(
            num_scalar_prefetch=1,
            grid=(M_TOK // COMBINE_ROWS,),
            in_specs=[
                pl.BlockSpec(memory_space=pl.ANY),
                pl.BlockSpec((COMBINE_ROWS, D), lambda i, p: (i, 0)),
                pl.BlockSpec((COMBINE_ROWS, LANE), lambda i, p: (i, 0)),
                pl.BlockSpec((None, None, 1, D), lambda i, p: (grp(i), gate_c, 0, 0)),
            ],
            out_specs=pl.BlockSpec((COMBINE_ROWS, D), lambda i, p: (i, 0)),
            scratch_shapes=[pltpu.VMEM((2, 2, COMBINE_ROWS, D), F32), pltpu.SemaphoreType.DMA((2, 2))],
        ),
        out_shape=jax.ShapeDtypeStruct((M_TOK, D), F32),
        compiler_params=_cparams(("arbitrary",)),
        name="moe_combine",
    )(pos, y, x, top_gates, mods)


def _moe(x, norm_g, mods, sh_c, sc_c, gate_c, router, wg, wu, wd, f):
    h, idx, gates = _router(x, norm_g, mods, sh_c, sc_c, router)
    pos, te, tv, sub_valid = _route_plan(idx[:, 0], idx[:, 1])
    xs = _moe_dispatch(h, pos, sub_valid)
    ys = _ffn_moe(xs, wg, wu, wd, f, te, tv)
    return _moe_combine(ys, pos, x, gates, mods, gate_c)


def _rope_tables(width, lo, hi):
    pos = jnp.arange(T_LAT)
    rows = (pos // GRID_W).astype(F32)
    cols = (pos % GRID_W).astype(F32)
    n_freq = 16
    inv = ROPE_BASE ** (-jnp.arange(n_freq, dtype=F32) / n_freq)
    ang = jnp.concatenate([rows[:, None] * inv[None, :], cols[:, None] * inv[None, :]], axis=-1)
    cos, sin = jnp.cos(ang), jnp.sin(ang)
    lane = np.arange(width)
    active = (lane >= lo) & (lane < hi)
    reps = width // 64
    c = jnp.tile(jnp.concatenate([cos, cos], axis=-1), (1, reps))
    s = jnp.tile(jnp.concatenate([-sin, sin], axis=-1), (1, reps))
    return jnp.where(active[None, :], c, 1.0), jnp.where(active[None, :], s, 0.0)


def _rope_kernel(x_ref, c_ref, s_ref, o_ref, *, width):
    lane = lax.broadcasted_iota(jnp.int32, (1, width), 1)
    first = (lane & 63) < 32
    c = c_ref[...]
    s = s_ref[...]
    for j in range(x_ref.shape[-1] // width):
        cols = slice(j * width, (j + 1) * width)
        x = x_ref[:, cols]
        partner = jnp.where(first, pltpu.roll(x, width - 32, 1), pltpu.roll(x, 32, 1))
        o_ref[:, cols] = (x * c + partner * s).astype(o_ref.dtype)


def _rope(x, c, s, *, planes, row0, width, col0=0, ncol=1, tr=512):
    r0 = row0 // tr
    nt = T_LAT // tr
    if x.ndim == 3:
        ncol = x.shape[2] // width
        x_spec = pl.BlockSpec((None, tr, ncol * width), lambda p, i: (p, i + r0, 0))
    else:
        x_spec = pl.BlockSpec((tr, ncol * width), lambda p, i: (i + r0, col0 // ncol))
    return pl.pallas_call(
        functools.partial(_rope_kernel, width=width),
        grid=(planes, M_LAT // tr),
        in_specs=[
            x_spec,
            pl.BlockSpec((tr, width), lambda p, i: (i % nt, 0)),
            pl.BlockSpec((tr, width), lambda p, i: (i % nt, 0)),
        ],
        out_specs=pl.BlockSpec((None, tr, ncol * width), lambda p, i: (p, i, 0)),
        out_shape=jax.ShapeDtypeStruct((planes, M_LAT, ncol * width), BF16),
        compiler_params=_cparams(("parallel", "parallel")),
        name="rope",
    )(x, c, s)


def _attn_kernel(*refs, kind, hb, nseg, scale, lam_init):
    it = iter(refs)
    q_ref = next(it)
    k_refs = [next(it) for _ in range(nseg)]
    kpe_refs = [next(it) for _ in range(nseg)] if kind == "mla" else None
    v_refs = [next(it) for _ in range(nseg)] if kind != "mla" else k_refs
    lam_ref = next(it) if kind == "diff" else None
    subln_ref = next(it) if kind == "diff" else None
    o_ref = next(it)

    lane = lax.broadcasted_iota(jnp.int32, (1, LANE), 1)
    lo_half = lane < 64
    log2_scale = scale * LOG2_E

    def softmax_parts(s_list):
        t_list = [s * log2_scale for s in s_list]
        m = functools.reduce(jnp.maximum, [jnp.max(t, axis=-1, keepdims=True) for t in t_list])
        p_list = [jnp.exp2(t - m) for t in t_list]
        l = functools.reduce(jnp.add, [jnp.sum(p, axis=-1, keepdims=True) for p in p_list])
        return p_list, 1.0 / l

    def attend(s_list, v_list):
        p_list, inv_l = softmax_parts(s_list)
        return functools.reduce(jnp.add, [_dot(p.astype(BF16), v) for p, v in zip(p_list, v_list)]) * inv_l

    for hh in range(hb):
        if kind == "mla":
            q2 = q_ref[:, hh * 256:(hh + 1) * 256].astype(BF16)
            s_list = []
            for kr, pr in zip(k_refs, kpe_refs):
                kcat = jnp.concatenate([kr[:, hh * 256:hh * 256 + 128].astype(BF16), pr[...].astype(BF16)], axis=1)
                s_list.append(_dot_nt(q2, kcat))
            v2 = [vr[:, hh * 256 + 128:hh * 256 + 256].astype(BF16) for vr in v_refs]
            o_ref[:, hh * LANE:(hh + 1) * LANE] = attend(s_list, v2)
            continue

        sl = slice(hh * LANE, (hh + 1) * LANE)
        q2 = q_ref[:, sl]
        k2 = [kr[:, sl].astype(BF16) for kr in k_refs]
        v2 = [vr[:, sl].astype(BF16) for vr in v_refs]
        tq = q2.shape[0]
        q_heads = [jnp.where(lo_half, q2, 0.0), jnp.where(lo_half, 0.0, q2)]
        if hb > 1:
            qs = jnp.concatenate(q_heads, axis=0).astype(BF16)
            o2 = attend([_dot_nt(qs, kk) for kk in k2], v2)
            o_heads = [o2[:tq], o2[tq:]]
            if kind == "diff":
                o_diff = o_heads[0] - lam_ref[...][:, :1] * o_heads[1]
        elif kind == "na":
            o_heads = [attend([_dot_nt(qh.astype(BF16), kk) for kk in k2], v2) for qh in q_heads]
        else:
            (p0, il0), (p1, il1) = [softmax_parts([_dot_nt(qh.astype(BF16), kk) for kk in k2]) for qh in q_heads]
            lam = lam_ref[...][:, :1]
            o_diff = functools.reduce(jnp.add, [
                _dot((a * il0 - lam * (b * il1)).astype(BF16), vv) for a, b, vv in zip(p0, p1, v2)])
        if kind == "na":
            o_ref[:, sl] = jnp.where(lo_half, o_heads[0], o_heads[1])
        else:
            o_ref[:, sl] = _rms(o_diff, subln_ref[...], DIFF_SUBLN_EPS) * (1.0 - lam_init)


def _attention(kind, *, grid, q, ks, vs, kpes=None, extra=(), rows, out_spec, hb, scale, lam_init=0.0,
               name="attn"):
    ops = [q] + list(ks) + (list(kpes) if kind == "mla" else []) + (list(vs) if kind != "mla" else [])
    ops += list(extra)
    return pl.pallas_call(
        functools.partial(_attn_kernel, kind=kind, hb=hb, nseg=len(ks), scale=scale, lam_init=lam_init),
        grid=grid,
        in_specs=[s for _, s in ops],
        out_specs=out_spec,
        out_shape=jax.ShapeDtypeStruct((rows, D), F32),
        compiler_params=_cparams(("parallel",) * len(grid)),
        name=name,
    )(*[a for a, _ in ops])


NA_QROWS = 4
NA_KROWS = 12
NA_TQ = NA_QROWS * GRID_W
NA_TK = NA_KROWS * GRID_W


def _na_bias_table(rpb):
    h = rpb.shape[0]
    qc = np.arange(GRID_W)[:, None]
    kc = np.arange(GRID_W)[None, :]
    qs = np.clip(qc - NA_WIN_C // 2, 0, GRID_W - NA_WIN_C)
    col_ok = (kc >= qs) & (kc < qs + NA_WIN_C)
    ic = np.clip(kc - qc + NA_WIN_C - 1, 0, 2 * NA_WIN_C - 2)
    n_ir = 2 * NA_WIN_R - 1
    onehot = (ic.reshape(1, -1) == np.arange(2 * NA_WIN_C - 1)[:, None]).astype(np.float32)
    tab = jnp.einsum("hrm,mx->hrx", rpb.astype(F32) * LOG2_E, jnp.asarray(onehot),
                     precision=lax.Precision.HIGHEST)
    tab = jnp.where(jnp.asarray(col_ok.reshape(1, 1, -1)), tab, NEG_INF).reshape(h, n_ir, GRID_W, GRID_W)
    tab = jnp.concatenate([tab, tab], axis=-1)
    n_rows = T_LAT // GRID_W
    n_g = n_rows // NA_QROWS
    ir_idx = np.zeros((3, NA_QROWS, NA_KROWS), np.int32)
    row_ok = np.zeros((3, NA_QROWS, NA_KROWS), bool)
    for pat, g in enumerate((0, 1, n_g - 1)):
        base = int(np.clip(NA_QROWS * g - NA_QROWS, 0, n_rows - NA_KROWS))
        for qr in range(NA_QROWS):
            r = NA_QROWS * g + qr
            r0 = int(np.clip(r - NA_WIN_R // 2, 0, n_rows - NA_WIN_R))
            for kr in range(NA_KROWS):
                ok = r0 <= base + kr < r0 + NA_WIN_R
                row_ok[pat, qr, kr] = ok
                ir_idx[pat, qr, kr] = (base + kr - r + NA_WIN_R - 1) if ok else 0

    def build(tab_ref, o_ref):
        pat_id = pl.program_id(0)
        lo_half = lax.broadcasted_iota(jnp.int32, (1, LANE), 1) < GRID_W
        masked = jnp.full((GRID_W, LANE), NEG_INF, F32)
        for pat in range(3):
            @pl.when(pat_id == pat)
            def _(pat=pat):
                for hd in range(heads_per_step):
                    for qr in range(NA_QROWS):
                        for kp in range(NA_KROWS // 2):
                            halves = [tab_ref[hd, int(ir_idx[pat, qr, kr])] if row_ok[pat, qr, kr] else masked
                                      for kr in (2 * kp, 2 * kp + 1)]
                            o_ref[hd, qr * GRID_W:(qr + 1) * GRID_W, kp * LANE:(kp + 1) * LANE] = jnp.where(
                                lo_half, halves[0], halves[1])

    heads_per_step = 4
    return pl.pallas_call(
        build,
        grid=(3, h // heads_per_step),
        in_specs=[pl.BlockSpec((heads_per_step, n_ir, GRID_W, LANE), lambda p, hh: (hh, 0, 0, 0))],
        out_specs=pl.BlockSpec((None, heads_per_step, NA_TQ, NA_TK), lambda p, hh: (p, hh, 0, 0)),
        out_shape=jax.ShapeDtypeStruct((3, h, NA_TQ, NA_TK), F32),
        compiler_params=_cparams(("parallel", "parallel")),
        name="na_bias",
    )(tab)


def _na_lat_kernel(q_ref, kc_ref, vc_ref, k_ref, v_ref, b_ref, o_ref, *, scale):
    g = pl.program_id(2)
    n_rows = T_LAT // GRID_W
    base = jnp.clip(NA_QROWS * g - NA_QROWS, 0, n_rows - NA_KROWS) * GRID_W
    base = pl.multiple_of(base, GRID_W)
    lane = lax.broadcasted_iota(jnp.int32, (1, LANE), 1)
    lo_half = lane < 64
    q2 = q_ref[...]
    kc = kc_ref[...].astype(BF16)
    vc = vc_ref[...].astype(BF16)
    kl = k_ref[pl.ds(base, NA_TK), :].astype(BF16)
    vl = v_ref[pl.ds(base, NA_TK), :].astype(BF16)
    log2_scale = scale * LOG2_E
    outs = []
    for n in range(2):
        qm = jnp.where(lo_half if n == 0 else jnp.logical_not(lo_half), q2, 0.0).astype(BF16)
        s_c = _dot_nt(qm, kc) * log2_scale
        s_l = _dot_nt(qm, kl) * log2_scale + b_ref[n]
        m = jnp.maximum(jnp.max(s_c, axis=-1, keepdims=True), jnp.max(s_l, axis=-1, keepdims=True))
        p_c = jnp.exp2(s_c - m)
        p_l = jnp.exp2(s_l - m)
        l = jnp.sum(p_c, axis=-1, keepdims=True) + jnp.sum(p_l, axis=-1, keepdims=True)
        o = _dot(p_c.astype(BF16), vc) + _dot(p_l.astype(BF16), vl)
        outs.append(o * (1.0 / l))
    o_ref[...] = jnp.where(lo_half, outs[0], outs[1])


def _na_lat(qkv, cache_k, cache_v, bias):
    n_g = T_LAT // NA_TQ
    q_blk0 = M_CTX // NA_TQ
    kv_blk0 = M_CTX // T_LAT
    pat = lambda g: jnp.where(g == 0, 0, jnp.where(g == n_g - 1, 2, 1))
    return pl.pallas_call(
        functools.partial(_na_lat_kernel, scale=0.125),
        grid=(N_LAT_SEQ, D // LANE, n_g),
        in_specs=[
            pl.BlockSpec((None, NA_TQ, LANE), lambda b, h, g: (0, q_blk0 + b * n_g + g, h)),
            pl.BlockSpec((None, PAST, LANE), lambda b, h, g: (b, 0, h)),
            pl.BlockSpec((None, PAST, LANE), lambda b, h, g: (b, 0, h)),
            pl.BlockSpec((None, T_LAT, LANE), lambda b, h, g: (1, kv_blk0 + b, h)),
            pl.BlockSpec((None, T_LAT, LANE), lambda b, h, g: (2, kv_blk0 + b, h)),
            pl.BlockSpec((None, 2, NA_TQ, NA_TK), lambda b, h, g: (pat(g), h, 0, 0)),
        ],
        out_specs=pl.BlockSpec((NA_TQ, LANE), lambda b, h, g: (b * n_g + g, h)),
        out_shape=jax.ShapeDtypeStruct((M_LAT, D), F32),
        compiler_params=_cparams(("parallel", "parallel", "parallel")),
        name="na_latent",
    )(qkv, cache_k, cache_v, qkv, qkv, bias)


SEQ_T_PER_STEP = 4


def _seq_t_kernel(x_ref, o_ref):
    for s in range(SEQ_T_PER_STEP):
        o_ref[s] = x_ref[s * T_CTX:(s + 1) * T_CTX, :].T


def _seq_transpose(qkv, plane):
    return pl.pallas_call(
        _seq_t_kernel,
        grid=(N_CTX_SEQ // SEQ_T_PER_STEP,),
        in_specs=[pl.BlockSpec((None, SEQ_T_PER_STEP * T_CTX, D), lambda b: (plane, b, 0))],
        out_specs=pl.BlockSpec((SEQ_T_PER_STEP, D, T_CTX), lambda b: (b, 0, 0)),
        out_shape=jax.ShapeDtypeStruct((N_CTX_SEQ, D, T_CTX), F32),
        compiler_params=_cparams(("parallel",)),
        name="seq_transpose",
    )(qkv)


def kernel(x_prompt, x_sample, c, c_ctx, cache_na_k, cache_na_v, cache_diff_k, cache_diff_v, cache_mla_ckv, cache_mla_kpe, ada_w, ada_b, norm1, norm2, pool_w, pool_scale, na_qkv, na_rpb, na_out, diff_qkv, diff_lambda, diff_subln, diff_out, mla_qa, mla_qnorm, mla_qb, mla_kva, mla_kvnorm, mla_kvb, mla_out, ffn_gate, ffn_up, ffn_down, moe_router, moe_gate, moe_up, moe_down, final_norm):
    depth = ada_w.shape[0]
    x = None
    cond8 = jnp.concatenate([c_ctx[None, :], c, jnp.zeros((5, D), F32)], axis=0)
    mods_all = _ada(cond8, ada_w, ada_b)[:, :3].reshape(depth, 3, 6, 1, D)
    outs = {}

    for l in range(depth):
        kind, li = l % 4, l // 4
        mods = mods_all[l]
        if kind == 0:
            x_pair = (x_prompt.reshape(M_CTX, D), x_sample.reshape(M_LAT, D)) if x is None else (x[:M_CTX], x[M_CTX:])
            h = _modrows(x_pair, norm1[l], mods, 0, 1, rows=M_TOK, row0=0, width=D)
            x = _pool(h, *x_pair, pool_w[li], pool_scale[li], mods, 2)
        elif kind == 1:
            qkv = _linear(x, na_qkv[li], tn=D, norm_g=norm1[l], mods=mods, sh_c=0, sc_c=1, planes=True,
                          name="na_qkv")
            to_cache = lambda a: jnp.transpose(a.reshape(N_CTX_SEQ, 1, 16, 64, T_CTX), (0, 1, 4, 2, 3))
            outs["na_k"] = to_cache(_seq_transpose(qkv, 1))
            outs["na_v"] = to_cache(_seq_transpose(qkv, 2))
            seq_spec = lambda p: pl.BlockSpec((None, T_CTX, D), lambda b, p=p: (p, b, 0))
            o_ctx = _attention(
                "na", grid=(N_CTX_SEQ,), q=(qkv, seq_spec(0)), ks=[(qkv, seq_spec(1))], vs=[(qkv, seq_spec(2))],
                rows=M_CTX, out_spec=pl.BlockSpec((T_CTX, D), lambda b: (b, 0)), hb=D // LANE, scale=0.125,
                name="na_ctx")
            bias = _na_bias_table(na_rpb[li])
            o_lat = _na_lat(qkv, cache_na_k[:, li].reshape(N_LAT_SEQ, PAST, D),
                            cache_na_v[:, li].reshape(N_LAT_SEQ, PAST, D), bias)
            x = _linear((o_ctx, o_lat), na_out[li], tn=D, tm=TM, resid=x, mods=mods, gate_c=2, name="na_out")
        elif kind == 2:
            lam_init = 0.8 - 0.6 * math.exp(-0.3 * l)
            lp = diff_lambda[li].astype(F32)
            lam = jnp.exp(jnp.sum(lp[0] * lp[1])) - jnp.exp(jnp.sum(lp[2] * lp[3])) + lam_init
            lam_row = jnp.full((1, LANE), lam, F32)
            subln = diff_subln[li].reshape(1, LANE)
            extra = [(lam_row, pl.BlockSpec((1, LANE), lambda *_: (0, 0))),
                     (subln, pl.BlockSpec((1, LANE), lambda *_: (0, 0)))]
            qkv = _linear(x, diff_qkv[li], tn=D, norm_g=norm1[l], mods=mods, sh_c=0, sc_c=1, planes=True,
                          name="diff_qkv")
            outs["diff_k"] = jnp.transpose(
                _seq_transpose(qkv, 1).reshape(N_CTX_SEQ, 1, 8, 2, 64, T_CTX), (0, 1, 5, 2, 3, 4))
            outs["diff_v"] = qkv[2, :M_CTX].reshape(N_CTX_SEQ, 1, T_CTX, 8, 128)
            seq_spec = lambda p: pl.BlockSpec((None, T_CTX, D), lambda b, p=p: (p, b, 0))
            o_ctx = _attention(
                "diff", grid=(N_CTX_SEQ,), q=(qkv, seq_spec(0)), ks=[(qkv, seq_spec(1))], vs=[(qkv, seq_spec(2))],
                extra=extra, rows=M_CTX, out_spec=pl.BlockSpec((T_CTX, D), lambda b: (b, 0)), hb=D // LANE,
                scale=0.125, lam_init=lam_init, name="diff_ctx")
            rc, rs = _rope_tables(LANE, 0, LANE)
            qk_r = _rope(qkv, rc, rs, planes=2, row0=M_CTX, width=LANE)
            tq = LATENT_TQ
            nq = T_LAT // tq
            ck = cache_diff_k[:, li].reshape(N_LAT_SEQ, PAST, D)
            cv = cache_diff_v[:, li].reshape(N_LAT_SEQ, PAST, D)
            o_lat = _attention(
                "diff", grid=(N_LAT_SEQ, D // LANE, nq),
                q=(qk_r, pl.BlockSpec((None, tq, LANE), lambda b, h, i: (0, b * nq + i, h))),
                ks=[(ck, pl.BlockSpec((None, PAST, LANE), lambda b, h, i: (b, 0, h))),
                    (qk_r, pl.BlockSpec((None, T_LAT, LANE), lambda b, h, i: (1, b, h)))],
                vs=[(cv, pl.BlockSpec((None, PAST, LANE), lambda b, h, i: (b, 0, h))),
                    (qkv, pl.BlockSpec((None, T_LAT, LANE), lambda b, h, i: (2, M_CTX // T_LAT + b, h)))],
                extra=extra,
                rows=M_LAT, out_spec=pl.BlockSpec((tq, LANE), lambda b, h, i: (b * nq + i, h)),
                hb=1, scale=0.125, lam_init=lam_init, name="diff_latent")
            x = _linear((o_ctx, o_lat), diff_out[li], tn=D, tm=TM, resid=x, mods=mods, gate_c=2, name="diff_out")
        else:
            hd = MLA_NOPE + MLA_ROPE
            wqb = mla_qb[li].reshape(MLA_Q_LORA, MLA_HEADS, hd)
            wqb = jnp.pad(wqb, ((0, 0), (0, 0), (0, 256 - hd))).reshape(MLA_Q_LORA, MLA_HEADS * 256)
            w_a = jnp.concatenate(
                [mla_qa[li], jnp.pad(mla_kva[li], ((0, 0), (0, 512 - MLA_KV_LORA - MLA_ROPE)))], axis=1)
            a = _linear(x, w_a, tn=D, tm=TM, norm_g=norm1[l], mods=mods, sh_c=0, sc_c=1, name="mla_a")
            q = _linear(a, wqb, tn=D, tm=TM, k=MLA_Q_LORA, x_col=0, norm_g=mla_qnorm[li], name="mla_q")
            zero_mod = jnp.zeros((1, 2, 1, MLA_KV_LORA), F32)
            ckv = _modrows(a, mla_kvnorm[li], zero_mod, 0, 1, rows=M_TOK, row0=0, width=MLA_KV_LORA, col_blk=2,
                           grouped=False)
            outs["mla_ckv"] = ckv[:M_CTX].reshape(N_CTX_SEQ, 1, T_CTX, MLA_KV_LORA)
            outs["mla_kpe"] = a[:M_CTX, 768:768 + MLA_ROPE].reshape(N_CTX_SEQ, 1, T_CTX, MLA_ROPE)
            ckv_all = jnp.concatenate([ckv, cache_mla_ckv[:, li].reshape(N_LAT_SEQ * PAST, MLA_KV_LORA)], axis=0)
            kvb = _linear(ckv_all, mla_kvb[li], tn=D, tm=TM, out_dtype=BF16, name="mla_kvb")
            scale = float(hd) ** -0.5
            o_ctx = _attention(
                "mla", grid=(N_CTX_SEQ,),
                q=(q, pl.BlockSpec((T_CTX, 2048), lambda b: (b, 0))),
                ks=[(kvb, pl.BlockSpec((T_CTX, 2048), lambda b: (b, 0)))],
                kpes=[(a, pl.BlockSpec((T_CTX, LANE), lambda b: (b, 6)))],
                vs=None, rows=M_CTX, out_spec=pl.BlockSpec((T_CTX, D), lambda b: (b, 0)), hb=MLA_HEADS,
                scale=scale, name="mla_ctx")
            rc, rs = _rope_tables(256, 128, 192)
            q_r = _rope(q, rc, rs, planes=1, row0=M_CTX, width=256, ncol=MLA_HEADS)[0]
            rc1, rs1 = _rope_tables(LANE, 0, 64)
            kpe_r = _rope(a, rc1, rs1, planes=1, row0=M_CTX, width=LANE, col0=6)[0]
            kpe_c = jnp.pad(cache_mla_kpe[:, li], ((0, 0), (0, 0), (0, LANE - MLA_ROPE)))
            tq = LATENT_TQ
            nq = T_LAT // tq
            o_lat = _attention(
                "mla", grid=(N_LAT_SEQ, MLA_HEADS, nq),
                q=(q_r, pl.BlockSpec((tq, 256), lambda b, h, i: (b * nq + i, h))),
                ks=[(kvb, pl.BlockSpec((PAST, 256), lambda b, h, i: (M_TOK // PAST + b, h))),
                    (kvb, pl.BlockSpec((T_LAT, 256), lambda b, h, i: (M_CTX // T_LAT + b, h)))],
                kpes=[(kpe_c, pl.BlockSpec((None, PAST, LANE), lambda b, h, i: (b, 0, 0))),
                      (kpe_r, pl.BlockSpec((T_LAT, LANE), lambda b, h, i: (b, 0)))],
                vs=None, rows=M_LAT,
                out_spec=pl.BlockSpec((tq, LANE), lambda b, h, i: (b * nq + i, h)),
                hb=1, scale=scale, name="mla_latent")
            x = _linear((o_ctx, o_lat), mla_out[li], tn=D, tm=TM, resid=x, mods=mods, gate_c=2, name="mla_out")

        f = l // 2
        if l % 2 == 0:
            x = _ffn_dense(x, ffn_gate, ffn_up, ffn_down, f, norm2[l], mods, 3, 4, 5)
        else:
            x = _moe(x, norm2[l], mods, 3, 4, 5, moe_router[f], moe_gate, moe_up, moe_down, f)

    zero_mod = jnp.zeros((1, 2, 1, D), F32)
    y_prompt = _modrows(x, final_norm, zero_mod, 0, 1, rows=M_CTX, row0=0, width=D, grouped=False)
    y_sample = _modrows(x, final_norm, zero_mod, 0, 1, rows=M_LAT, row0=M_CTX, width=D, grouped=False)
    return (y_prompt.reshape(N_CTX_SEQ, T_CTX, D), y_sample.reshape(N_LAT_SEQ, T_LAT, D),
            outs["na_k"], outs["na_v"], outs["diff_k"], outs["diff_v"], outs["mla_ckv"], outs["mla_kpe"])
```

```python
import functools
import math

import numpy as np
import jax
import jax.numpy as jnp
from jax import lax
from jax.experimental import pallas as pl
from jax.experimental.pallas import tpu as pltpu

F32 = jnp.float32
BF16 = jnp.bfloat16

D = 1024
N_CTX_SEQ = 32
T_CTX = 256
N_LAT_SEQ = 2
T_LAT = 2048
PAST = 512
M_CTX = N_CTX_SEQ * T_CTX
M_LAT = N_LAT_SEQ * T_LAT
M_TOK = M_CTX + M_LAT
GRID_W = 64
NORM_EPS = 1e-6
NEG_INF = -1e30
ROPE_BASE = 10000.0
POOL_WINDOWS = (2, 4, 8, 16)
NA_WIN_R = 8
NA_WIN_C = 16
DIFF_SUBLN_EPS = 1e-5
MLA_HEADS = 8
MLA_NOPE = 128
MLA_ROPE = 64
MLA_KV_LORA = 256
MLA_Q_LORA = 512
FFN_HIDDEN = 2816
N_EXPERTS = 8

LOG2_E = math.log2(math.e)
LANE = 128
TM = 1024
TH = 256
SUB = 256
VMEM_LIMIT = 56 * 1024 * 1024

TMF = 2048
MOE_ROWS = 2 * M_TOK + N_EXPERTS * TMF
MOE_TILES = MOE_ROWS // TMF


def _cparams(sem):
    return pltpu.CompilerParams(dimension_semantics=sem, vmem_limit_bytes=VMEM_LIMIT)


def _group_of_tile(i, tm):
    n_ctx = M_CTX // tm
    return jnp.where(i < n_ctx, 0, 1 + (i - n_ctx) // (T_LAT // tm))


def _dot(a, b):
    return jnp.dot(a, b, preferred_element_type=F32)


def _dot_nt(a, b):
    return lax.dot_general(a, b, (((1,), (1,)), ((), ())), preferred_element_type=F32)


def _rms(x, g, eps):
    return x * lax.rsqrt(jnp.mean(x * x, axis=-1, keepdims=True) + eps) * g


def _ada_kernel(c_ref, w_ref, b_ref, o_ref):
    c = c_ref[...]
    s = (c * jax.nn.sigmoid(c)).astype(BF16)
    o_ref[...] = _dot(s, w_ref[...].astype(BF16)) + b_ref[...]


def _ada(cond8, ada_w, ada_b):
    depth = ada_w.shape[0]
    tn = 1536
    return pl.pallas_call(
        _ada_kernel,
        grid=(depth, 6 * D // tn),
        in_specs=[
            pl.BlockSpec((8, D), lambda l, j: (0, 0)),
            pl.BlockSpec((None, D, tn), lambda l, j: (l, 0, j)),
            pl.BlockSpec((None, 1, tn), lambda l, j: (l, 0, j)),
        ],
        out_specs=pl.BlockSpec((None, 8, tn), lambda l, j: (l, 0, j)),
        out_shape=jax.ShapeDtypeStruct((depth, 8, 6 * D), F32),
        compiler_params=_cparams(("parallel", "parallel")),
        name="ada",
    )(cond8, ada_w, ada_b.reshape(depth, 1, 6 * D))


def _token_tile(xc_ref, xl_ref, tile, tm):
    return jnp.where(tile < M_CTX // tm, xc_ref[...], xl_ref[...])


def _split_specs(block, tm, col):
    n_ctx = M_CTX // tm
    return [pl.BlockSpec(block, lambda i, *_: (jnp.minimum(i, n_ctx - 1), col(*_))),
            pl.BlockSpec(block, lambda i, *_: (jnp.maximum(i - n_ctx, 0), col(*_)))]


def _modrows_kernel(*refs, eps, split, tm):
    if split:
        xc_ref, xl_ref, g_ref, sh_ref, sc_ref, o_ref = refs
        x = _token_tile(xc_ref, xl_ref, pl.program_id(0), tm)
    else:
        x_ref, g_ref, sh_ref, sc_ref, o_ref = refs
        x = x_ref[...]
    o_ref[...] = _rms(x, g_ref[...], eps) * (1.0 + sc_ref[...]) + sh_ref[...]


def _modrows(x, g, mods, sh_c, sc_c, *, rows, row0, width, col_blk=0, tm=TM, eps=NORM_EPS,
             grouped=True):
    r0 = row0 // tm
    if grouped:
        grp = lambda i: _group_of_tile(i + r0, tm)
    else:
        grp = lambda i: 0
    split = isinstance(x, tuple)
    if split:
        xs, x_specs = list(x), _split_specs((tm, width), tm, lambda: 0)
    else:
        xs, x_specs = [x], [pl.BlockSpec((tm, width), lambda i: (i + r0, col_blk))]
    return pl.pallas_call(
        functools.partial(_modrows_kernel, eps=eps, split=split, tm=tm),
        grid=(rows // tm,),
        in_specs=x_specs + [
            pl.BlockSpec((1, width), lambda i: (0, 0)),
            pl.BlockSpec((None, None, 1, width), lambda i: (grp(i), sh_c, 0, 0)),
            pl.BlockSpec((None, None, 1, width), lambda i: (grp(i), sc_c, 0, 0)),
        ],
        out_specs=pl.BlockSpec((tm, width), lambda i: (i, 0)),
        out_shape=jax.ShapeDtypeStruct((rows, width), F32),
        compiler_params=_cparams(("parallel",)),
        name="modrows",
    )(*xs, g.reshape(1, width), mods, mods)


def _linear_kernel(*refs, prologue, eps, epilogue, split, tm, tn, planes):
    it = iter(refs)
    x_ref = next(it)
    x_lat_ref = next(it) if split else None
    w_ref = next(it)
    g_ref = next(it) if prologue != "none" else None
    sh_ref = next(it) if prologue == "mod" else None
    sc_ref = next(it) if prologue == "mod" else None
    res_ref = next(it) if epilogue == "resid" else None
    gate_ref = next(it) if epilogue == "resid" else None
    o_ref = next(it)
    wb_ref = next(it)
    n = w_ref.shape[1]

    @pl.when(pl.program_id(0) == 0)
    def _():
        wb_ref[...] = w_ref[...].astype(BF16)

    def run(src_ref):
        x = src_ref[...]
        if prologue != "none":
            x = _rms(x, g_ref[...], eps)
        if prologue == "mod":
            x = x * (1.0 + sc_ref[...]) + sh_ref[...]
        xb = x.astype(BF16)
        for p in range(n // tn):
            cols = slice(p * tn, (p + 1) * tn)
            acc = _dot(xb, wb_ref[:, cols])
            if epilogue == "resid":
                acc = res_ref[:, cols] + gate_ref[:, cols] * acc
            if planes:
                o_ref[p] = acc.astype(o_ref.dtype)
            else:
                o_ref[:, cols] = acc.astype(o_ref.dtype)

    if split:
        is_ctx = pl.program_id(0) < M_CTX // tm
        pl.when(is_ctx)(lambda: run(x_ref))
        pl.when(jnp.logical_not(is_ctx))(lambda: run(x_lat_ref))
    else:
        run(x_ref)


TML = 512
LATENT_TQ = 256
LATENT_HB = 4


def _linear(x, w, *, tn, k=None, x_col=0, norm_g=None, eps=NORM_EPS, mods=None, sh_c=None, sc_c=None,
            resid=None, gate_c=None, planes=False, tm=TML, out_dtype=F32, name="linear"):
    split = isinstance(x, tuple)
    n = w.shape[1]
    prologue = "none" if norm_g is None else ("mod" if sh_c is not None else "norm")
    epilogue = "none" if resid is None else "resid"
    grp = lambda i: _group_of_tile(i, tm)
    if split:
        x_ctx, x_lat = x
        m = x_ctx.shape[0] + x_lat.shape[0]
        k = x_ctx.shape[1]
        n_ctx = x_ctx.shape[0] // tm
        args = [x_ctx, x_lat]
        in_specs = [
            pl.BlockSpec((tm, k), lambda i: (jnp.minimum(i, n_ctx - 1), 0)),
            pl.BlockSpec((tm, k), lambda i: (jnp.maximum(i - n_ctx, 0), 0)),
        ]
    else:
        m = x.shape[0]
        k = x.shape[1] if k is None else k
        args = [x]
        in_specs = [pl.BlockSpec((tm, k), lambda i: (i, x_col))]
    args.append(w)
    in_specs.append(pl.BlockSpec((k, n), lambda i: (0, 0)))
    if prologue != "none":
        args.append(norm_g.reshape(1, k))
        in_specs.append(pl.BlockSpec((1, k), lambda i: (0, 0)))
    if prologue == "mod":
        args += [mods, mods]
        in_specs += [
            pl.BlockSpec((None, None, 1, k), lambda i: (grp(i), sh_c, 0, 0)),
            pl.BlockSpec((None, None, 1, k), lambda i: (grp(i), sc_c, 0, 0)),
        ]
    if epilogue == "resid":
        args += [resid, mods]
        in_specs += [
            pl.BlockSpec((tm, n), lambda i: (i, 0)),
            pl.BlockSpec((None, None, 1, n), lambda i: (grp(i), gate_c, 0, 0)),
        ]
    if planes:
        out_spec = pl.BlockSpec((n // tn, tm, tn), lambda i: (0, i, 0))
        out_shape = jax.ShapeDtypeStruct((n // tn, m, tn), out_dtype)
    else:
        out_spec = pl.BlockSpec((tm, n), lambda i: (i, 0))
        out_shape = jax.ShapeDtypeStruct((m, n), out_dtype)
    return pl.pallas_call(
        functools.partial(_linear_kernel, prologue=prologue, eps=eps, epilogue=epilogue, split=split, tm=tm,
                          tn=tn, planes=planes),
        grid=(m // tm,),
        in_specs=in_specs,
        out_specs=out_spec,
        out_shape=out_shape,
        scratch_shapes=[pltpu.VMEM((k, n), BF16)],
        compiler_params=_cparams(("arbitrary",)),
        name=name,
    )(*args)


POOL_ROWS = 2048


def _pool_kernel(h_ref, xc_ref, xl_ref, w_ref, ps_ref, gate_ref, o_ref):
    i = pl.program_id(0)
    g = pl.program_id(1)
    seq = jnp.where(i < M_CTX // POOL_ROWS, T_CTX, T_LAT)
    t = lax.broadcasted_iota(jnp.int32, (POOL_ROWS, 1), 0) & (seq - 1)

    def ahead(a, k):
        return jnp.where(t + k < seq, pltpu.roll(a, POOL_ROWS - k, 0), 0.0)

    def behind(a, k):
        return jnp.where(t - k >= 0, pltpu.roll(a, k, 0), 0.0)

    for gi, win in enumerate(POOL_WINDOWS):
        @pl.when(g == gi)
        def _(win=win):
            h = h_ref[...]
            fwd, bwd, k = h, h, 1
            while k < win // 2:
                fwd, bwd, k = fwd + ahead(fwd, k), bwd + behind(bwd, k), 2 * k
            acc = fwd + behind(bwd, 1)
            cnt = jnp.minimum(t + (win - win // 2), seq) - jnp.maximum(t - win // 2, 0)
            pooled = acc / cnt.astype(F32)
            y = _dot((pooled - h).astype(BF16), w_ref[...].astype(BF16)) * ps_ref[...]
            o_ref[...] = _token_tile(xc_ref, xl_ref, i, POOL_ROWS) + gate_ref[...] * y


def _pool(h, x_ctx, x_lat, pool_w, pool_scale, mods, gate_c):
    gw = D // len(POOL_WINDOWS)
    grp = lambda i: _group_of_tile(i, POOL_ROWS)
    return pl.pallas_call(
        _pool_kernel,
        grid=(M_TOK // POOL_ROWS, len(POOL_WINDOWS)),
        in_specs=[
            pl.BlockSpec((POOL_ROWS, gw), lambda i, g: (i, g)),
            *_split_specs((POOL_ROWS, gw), POOL_ROWS, lambda g: g),
            pl.BlockSpec((None, gw, gw), lambda i, g: (g, 0, 0)),
            pl.BlockSpec((1, gw), lambda i, g: (0, g)),
            pl.BlockSpec((None, None, 1, gw), lambda i, g: (grp(i), gate_c, 0, g)),
        ],
        out_specs=pl.BlockSpec((POOL_ROWS, gw), lambda i, g: (i, g)),
        out_shape=jax.ShapeDtypeStruct((M_TOK, D), F32),
        compiler_params=_cparams(("parallel", "parallel")),
        name="pool",
    )(h, x_ctx, x_lat, pool_w, pool_scale.reshape(1, D), mods)


NJ = FFN_HIDDEN // TH
TN2 = 256
NN = D // TN2
STRAIGHT_COUNTS = (TMF // SUB, 4, 5)


def _swiglu_kernel(*refs, moe):
    if moe:
        te_ref, tv_ref, x_ref, wg_ref, wu_ref, wd_ref, o_ref, xb_ref, h_ref = refs
    else:
        x_ref, wg_ref, wu_ref, wd_ref, g_ref, sh_ref, sc_ref, res_ref, gate_ref, o_ref, xb_ref, h_ref = refs
    j = pl.program_id(1)
    n_valid = tv_ref[pl.program_id(0)] if moe else TMF

    def per_sub_block(fn, fn_skipped=None):
        sub_blocks = [pl.ds(s * SUB, SUB) for s in range(TMF // SUB)]

        def leading(count):
            def run():
                for rows in sub_blocks[:count]:
                    fn(rows)
                if fn_skipped is not None:
                    for rows in sub_blocks[count:]:
                        fn_skipped(rows)
            return run

        def valid_rows_only():
            for s, rows in enumerate(sub_blocks):
                pl.when(s * SUB < n_valid)(functools.partial(fn, rows))
                if fn_skipped is not None:
                    pl.when(s * SUB >= n_valid)(functools.partial(fn_skipped, rows))

        if moe:
            n_blocks = (n_valid + SUB - 1) // SUB
            for count in STRAIGHT_COUNTS:
                pl.when(n_blocks == count)(leading(count))
            pl.when(functools.reduce(jnp.logical_and, [n_blocks != c for c in STRAIGHT_COUNTS]))(valid_rows_only)
        else:
            leading(len(sub_blocks))()

    @pl.when(j == 0)
    def _():
        def cast(rows):
            x = x_ref[rows, :]
            if not moe:
                x = _rms(x, g_ref[...], NORM_EPS) * (1.0 + sc_ref[...]) + sh_ref[...]
            xb_ref[rows, :] = x.astype(BF16)

        per_sub_block(cast)

    @pl.when(j < NJ)
    def _():
        wg = wg_ref[...].astype(BF16)
        wu = wu_ref[...].astype(BF16)

        def up(rows):
            xb = xb_ref[rows, :]
            a = _dot(xb, wg)
            u = _dot(xb, wu)
            h_ref[j, rows, :] = (a * jax.nn.sigmoid(a) * u).astype(BF16)

        per_sub_block(up)

    @pl.when(j >= NJ)
    def _():
        wd = wd_ref[...].astype(BF16)

        def down(rows):
            hcat = jnp.concatenate([h_ref[jj, rows, :] for jj in range(NJ)], axis=1)
            y = _dot(hcat, wd)
            if moe:
                o_ref[rows, :] = y
            else:
                o_ref[rows, :] = res_ref[rows, :] + gate_ref[...] * y

        def skipped(rows):
            o_ref[rows, :] = jnp.zeros((SUB, TN2), F32)

        per_sub_block(down, skipped)


_SWIGLU_SCRATCH = [pltpu.VMEM((TMF, D), BF16), pltpu.VMEM((NJ, TMF, TH), BF16)]


def _ffn_dense(x, wg, wu, wd, f, norm_g, mods, sh_c, sc_c, gate_c):
    grp = lambda i: _group_of_tile(i, TMF)
    jh = lambda j: jnp.minimum(j, NJ - 1)
    jn = lambda j: jnp.maximum(j - NJ, 0)
    mod_spec = lambda c: pl.BlockSpec((None, None, 1, D), lambda i, j: (grp(i), c, 0, 0))
    return pl.pallas_call(
        functools.partial(_swiglu_kernel, moe=False),
        grid=(M_TOK // TMF, NJ + NN),
        in_specs=[
            pl.BlockSpec((TMF, D), lambda i, j: (i, 0)),
            pl.BlockSpec((None, D, TH), lambda i, j: (f, 0, jh(j))),
            pl.BlockSpec((None, D, TH), lambda i, j: (f, 0, jh(j))),
            pl.BlockSpec((None, FFN_HIDDEN, TN2), lambda i, j: (f, 0, jn(j))),
            pl.BlockSpec((1, D), lambda i, j: (0, 0)),
            mod_spec(sh_c), mod_spec(sc_c),
            pl.BlockSpec((TMF, TN2), lambda i, j: (i, jn(j))),
            pl.BlockSpec((None, None, 1, TN2), lambda i, j: (grp(i), gate_c, 0, jn(j))),
        ],
        out_specs=pl.BlockSpec((TMF, TN2), lambda i, j: (i, jn(j))),
        out_shape=jax.ShapeDtypeStruct((M_TOK, D), F32),
        scratch_shapes=_SWIGLU_SCRATCH,
        compiler_params=_cparams(("parallel", "arbitrary")),
        name="ffn_dense",
    )(x, wg, wu, wd, norm_g.reshape(1, D), mods, mods, x, mods)


def _ffn_moe(xs, wg, wu, wd, f, tile_expert, tile_valid):
    jh = lambda i, j, tv: jnp.where(tv[i] > 0, jnp.minimum(j, NJ - 1), NJ - 1)
    jn = lambda i, j, tv: jnp.where(tv[i] > 0, jnp.maximum(j - NJ, 0), NN - 1)
    return pl.pallas_call(
        functools.partial(_swiglu_kernel, moe=True),
        grid_spec=pltpu.PrefetchScalarGridSpec(
            num_scalar_prefetch=2,
            grid=(MOE_TILES, NJ + NN),
            in_specs=[
                pl.BlockSpec((TMF, D), lambda i, j, te, tv: (i, 0)),
                pl.BlockSpec((None, None, D, TH), lambda i, j, te, tv: (f, te[i], 0, jh(i, j, tv))),
                pl.BlockSpec((None, None, D, TH), lambda i, j, te, tv: (f, te[i], 0, jh(i, j, tv))),
                pl.BlockSpec((None, None, FFN_HIDDEN, TN2), lambda i, j, te, tv: (f, te[i], 0, jn(i, j, tv))),
            ],
            out_specs=pl.BlockSpec((TMF, TN2), lambda i, j, te, tv: (i, jnp.maximum(j - NJ, 0))),
            scratch_shapes=_SWIGLU_SCRATCH,
        ),
        out_shape=jax.ShapeDtypeStruct((MOE_ROWS, D), F32),
        compiler_params=_cparams(("parallel", "arbitrary")),
        name="ffn_moe",
    )(tile_expert, tile_valid, xs, wg, wu, wd)


def _router_kernel(x_ref, g_ref, sh_ref, sc_ref, r_ref, h_ref, idx_ref, gates_ref):
    h = _rms(x_ref[...], g_ref[...], NORM_EPS) * (1.0 + sc_ref[...]) + sh_ref[...]
    h_ref[...] = h
    w = r_ref[...]
    h_hi = h.astype(BF16)
    h_lo = (h - h_hi.astype(F32)).astype(BF16)
    w_hi = w.astype(BF16)
    w_lo = (w - w_hi.astype(F32)).astype(BF16)
    logits = _dot(h_hi, w_hi) + (_dot(h_lo, w_hi) + _dot(h_hi, w_lo))
    col = lax.broadcasted_iota(jnp.int32, logits.shape, 1)
    colf = col.astype(F32)
    lg = jnp.where(col < N_EXPERTS, logits, -jnp.inf)
    m1 = jnp.max(lg, axis=-1, keepdims=True)
    i1 = jnp.min(jnp.where(lg == m1, colf, float(LANE)), axis=-1, keepdims=True)
    lg2 = jnp.where(colf == i1, -jnp.inf, lg)
    m2 = jnp.max(lg2, axis=-1, keepdims=True)
    i2 = jnp.min(jnp.where(lg2 == m2, colf, float(LANE)), axis=-1, keepdims=True)
    i1 = i1.astype(jnp.int32)
    i2 = i2.astype(jnp.int32)
    e = jnp.exp(m2 - m1)
    g1 = 1.0 / (1.0 + e)
    g2 = e / (1.0 + e)
    idx_ref[...] = jnp.where(col == 0, i1, jnp.where(col == 1, i2, 0))
    gates_ref[...] = jnp.where(col == 0, g1, jnp.where(col == 1, g2, 0.0))


def _router(x, norm_g, mods, sh_c, sc_c, router):
    r_pad = jnp.pad(router, ((0, 0), (0, LANE - N_EXPERTS)))
    grp = lambda i: _group_of_tile(i, TM)
    return pl.pallas_call(
        _router_kernel,
        grid=(M_TOK // TM,),
        in_specs=[
            pl.BlockSpec((TM, D), lambda i: (i, 0)),
            pl.BlockSpec((1, D), lambda i: (0, 0)),
            pl.BlockSpec((None, None, 1, D), lambda i: (grp(i), sh_c, 0, 0)),
            pl.BlockSpec((None, None, 1, D), lambda i: (grp(i), sc_c, 0, 0)),
            pl.BlockSpec((D, LANE), lambda i: (0, 0)),
        ],
        out_specs=[
            pl.BlockSpec((TM, D), lambda i: (i, 0)),
            pl.BlockSpec((TM, LANE), lambda i: (i, 0)),
            pl.BlockSpec((TM, LANE), lambda i: (i, 0)),
        ],
        out_shape=[
            jax.ShapeDtypeStruct((M_TOK, D), F32),
            jax.ShapeDtypeStruct((M_TOK, LANE), jnp.int32),
            jax.ShapeDtypeStruct((M_TOK, LANE), F32),
        ],
        compiler_params=_cparams(("parallel",)),
        name="router",
    )(x, norm_g.reshape(1, D), mods, mods, r_pad)


def _route_plan(e1, e2):
    e = jnp.concatenate([e1, e2])
    oh = (e[:, None] == jnp.arange(N_EXPERTS, dtype=jnp.int32)[None, :]).astype(jnp.int32)
    csum = jnp.cumsum(oh, axis=0)
    rank = jnp.sum((csum - oh) * oh, axis=1)
    counts = csum[-1]
    padded = ((counts + TMF - 1) // TMF) * TMF
    ends = jnp.cumsum(padded)
    starts = ends - padded
    pos = jnp.sum(oh * starts[None, :], axis=1) + rank
    tile_start = jnp.arange(MOE_TILES, dtype=jnp.int32) * TMF
    te = jnp.minimum(jnp.sum((tile_start[:, None] >= ends[None, :]).astype(jnp.int32), axis=1),
                     N_EXPERTS - 1)
    used = tile_start < ends[-1]
    valid = jnp.where(used, jnp.clip(counts[te] - (tile_start - starts[te]), 0, TMF), 0)
    last_e = te[jnp.maximum(ends[-1] // TMF - 1, 0)]
    te = jnp.where(used, te, last_e)
    sub_off = jnp.arange(TMF // SUB, dtype=jnp.int32) * SUB
    sub_valid = jnp.clip(valid[:, None] - sub_off[None, :], 0, SUB).reshape(-1)
    return pos.astype(jnp.int32), te.astype(jnp.int32), valid.astype(jnp.int32), sub_valid.astype(jnp.int32)


DMA_UNROLL = 8
DISPATCH_ROWS = 1024
COMBINE_ROWS = 512


def _dispatch_kernel(pos_ref, sv_ref, h_ref, xs_hbm, zero_ref, sem, zsem):
    i = pl.program_id(0)

    @pl.when(i == 0)
    def _():
        zero_ref[...] = jnp.zeros_like(zero_ref)

        def zero_copy(b):
            return pltpu.make_async_copy(zero_ref, xs_hbm.at[pl.ds(pl.multiple_of(b * SUB, SUB), SUB)], zsem)

        def start(b, c):
            pl.when(sv_ref[b] < SUB)(lambda: zero_copy(b).start())
            return c

        def wait(b, c):
            pl.when(sv_ref[b] < SUB)(lambda: zero_copy(b).wait())
            return c

        lax.fori_loop(0, MOE_ROWS // SUB, start, 0)
        lax.fori_loop(0, MOE_ROWS // SUB, wait, 0)

    def issue(r, c):
        p1 = pos_ref[i * DISPATCH_ROWS + r]
        p2 = pos_ref[M_TOK + i * DISPATCH_ROWS + r]
        pltpu.make_async_copy(h_ref.at[pl.ds(r, 1)], xs_hbm.at[pl.ds(p1, 1)], sem).start()
        pltpu.make_async_copy(h_ref.at[pl.ds(r, 1)], xs_hbm.at[pl.ds(p2, 1)], sem).start()
        return c

    lax.fori_loop(0, DISPATCH_ROWS, issue, 0, unroll=DMA_UNROLL)
    pltpu.make_async_copy(h_ref, xs_hbm.at[pl.ds(0, DISPATCH_ROWS)], sem).wait()
    pltpu.make_async_copy(h_ref, xs_hbm.at[pl.ds(0, DISPATCH_ROWS)], sem).wait()


def _moe_dispatch(h, pos, sub_valid):
    return pl.pallas_call(
        _dispatch_kernel,
        grid_spec=pltpu.PrefetchScalarGridSpec(
            num_scalar_prefetch=2,
            grid=(M_TOK // DISPATCH_ROWS,),
            in_specs=[pl.BlockSpec((DISPATCH_ROWS, D), lambda i, p, z: (i, 0))],
            out_specs=pl.BlockSpec(memory_space=pl.ANY),
            scratch_shapes=[pltpu.VMEM((SUB, D), F32), pltpu.SemaphoreType.DMA, pltpu.SemaphoreType.DMA],
        ),
        out_shape=jax.ShapeDtypeStruct((MOE_ROWS, D), F32),
        compiler_params=_cparams(("arbitrary",)),
        name="moe_dispatch",
    )(pos, sub_valid, h)


def _combine_kernel(pos_ref, y_hbm, x_ref, tg_ref, gate_ref, o_ref, y_ref, sem):
    i = pl.program_id(0)
    slot = i % 2

    def picked(k, s):
        return pltpu.make_async_copy(y_hbm.at[pl.ds(0, COMBINE_ROWS)], y_ref.at[s, k], sem.at[s, k])

    def gather_tile(t, s):
        def issue(r, c):
            for k in range(2):
                p = pos_ref[k * M_TOK + t * COMBINE_ROWS + r]
                pltpu.make_async_copy(y_hbm.at[pl.ds(p, 1)], y_ref.at[s, k, pl.ds(r, 1)], sem.at[s, k]).start()
            return c

        lax.fori_loop(0, COMBINE_ROWS, issue, 0, unroll=DMA_UNROLL)

    pl.when(i == 0)(lambda: gather_tile(0, 0))
    pl.when(i + 1 < pl.num_programs(0))(lambda: gather_tile(i + 1, 1 - slot))
    picked(0, slot).wait()
    picked(1, slot).wait()
    tg = tg_ref[...]
    moe_out = tg[:, 0:1] * y_ref[slot, 0] + tg[:, 1:2] * y_ref[slot, 1]
    o_ref[...] = x_ref[...] + gate_ref[...] * moe_out


def _moe_combine(y, pos, x, top_gates, mods, gate_c):
    grp = lambda i: _group_of_tile(i, COMBINE_ROWS)
    return pl.pallas_call(
        _combine_kernel,
        grid_spec=pltpu.PrefetchScalarGridSpec(
            num_scalar_prefetch=1,
            grid=(M_TOK // COMBINE_ROWS,),
            in_specs=[
                pl.BlockSpec(memory_space=pl.ANY),
                pl.BlockSpec((COMBINE_ROWS, D), lambda i, p: (i, 0)),
                pl.BlockSpec((COMBINE_ROWS, LANE), lambda i, p: (i, 0)),
                pl.BlockSpec((None, None, 1, D), lambda i, p: (grp(i), gate_c, 0, 0)),
            ],
            out_specs=pl.BlockSpec((COMBINE_ROWS, D), lambda i, p: (i, 0)),
            scratch_shapes=[pltpu.VMEM((2, 2, COMBINE_ROWS, D), F32), pltpu.SemaphoreType.DMA((2, 2))],
        ),
        out_shape=jax.ShapeDtypeStruct((M_TOK, D), F32),
        compiler_params=_cparams(("arbitrary",)),
        name="moe_combine",
    )(pos, y, x, top_gates, mods)


def _moe(x, norm_g, mods, sh_c, sc_c, gate_c, router, wg, wu, wd, f):
    h, idx, gates = _router(x, norm_g, mods, sh_c, sc_c, router)
    pos, te, tv, sub_valid = _route_plan(idx[:, 0], idx[:, 1])
    xs = _moe_dispatch(h, pos, sub_valid)
    ys = _ffn_moe(xs, wg, wu, wd, f, te, tv)
    return _moe_combine(ys, pos, x, gates, mods, gate_c)


def _rope_tables(width, lo, hi):
    pos = jnp.arange(T_LAT)
    rows = (pos // GRID_W).astype(F32)
    cols = (pos % GRID_W).astype(F32)
    n_freq = 16
    inv = ROPE_BASE ** (-jnp.arange(n_freq, dtype=F32) / n_freq)
    ang = jnp.concatenate([rows[:, None] * inv[None, :], cols[:, None] * inv[None, :]], axis=-1)
    cos, sin = jnp.cos(ang), jnp.sin(ang)
    lane = np.arange(width)
    active = (lane >= lo) & (lane < hi)
    reps = width // 64
    c = jnp.tile(jnp.concatenate([cos, cos], axis=-1), (1, reps))
    s = jnp.tile(jnp.concatenate([-sin, sin], axis=-1), (1, reps))
    return jnp.where(active[None, :], c, 1.0), jnp.where(active[None, :], s, 0.0)


def _rope_kernel(x_ref, c_ref, s_ref, o_ref, *, width):
    lane = lax.broadcasted_iota(jnp.int32, (1, width), 1)
    first = (lane & 63) < 32
    c = c_ref[...]
    s = s_ref[...]
    for j in range(x_ref.shape[-1] // width):
        cols = slice(j * width, (j + 1) * width)
        x = x_ref[:, cols]
        partner = jnp.where(first, pltpu.roll(x, width - 32, 1), pltpu.roll(x, 32, 1))
        o_ref[:, cols] = (x * c + partner * s).astype(o_ref.dtype)


def _rope(x, c, s, *, planes, row0, width, col0=0, ncol=1, tr=512):
    r0 = row0 // tr
    nt = T_LAT // tr
    if x.ndim == 3:
        ncol = x.shape[2] // width
        x_spec = pl.BlockSpec((None, tr, ncol * width), lambda p, i: (p, i + r0, 0))
    else:
        x_spec = pl.BlockSpec((tr, ncol * width), lambda p, i: (i + r0, col0 // ncol))
    return pl.pallas_call(
        functools.partial(_rope_kernel, width=width),
        grid=(planes, M_LAT // tr),
        in_specs=[
            x_spec,
            pl.BlockSpec((tr, width), lambda p, i: (i % nt, 0)),
            pl.BlockSpec((tr, width), lambda p, i: (i % nt, 0)),
        ],
        out_specs=pl.BlockSpec((None, tr, ncol * width), lambda p, i: (p, i, 0)),
        out_shape=jax.ShapeDtypeStruct((planes, M_LAT, ncol * width), BF16),
        compiler_params=_cparams(("parallel", "parallel")),
        name="rope",
    )(x, c, s)


def _attn_kernel(*refs, kind, hb, nseg, scale, lam_init, stack):
    it = iter(refs)
    q_ref = next(it)
    k_refs = [next(it) for _ in range(nseg)]
    kpe_refs = [next(it) for _ in range(nseg)] if kind == "mla" else None
    v_refs = [next(it) for _ in range(nseg)] if kind != "mla" else k_refs
    lam_ref = next(it) if kind == "diff" else None
    subln_ref = next(it) if kind == "diff" else None
    o_ref = next(it)

    lane = lax.broadcasted_iota(jnp.int32, (1, LANE), 1)
    lo_half = lane < 64
    log2_scale = scale * LOG2_E

    def softmax_parts(s_list):
        t_list = [s * log2_scale for s in s_list]
        m = functools.reduce(jnp.maximum, [jnp.max(t, axis=-1, keepdims=True) for t in t_list])
        p_list = [jnp.exp2(t - m) for t in t_list]
        l = functools.reduce(jnp.add, [jnp.sum(p, axis=-1, keepdims=True) for p in p_list])
        return p_list, 1.0 / l

    def attend(s_list, v_list):
        p_list, inv_l = softmax_parts(s_list)
        return functools.reduce(jnp.add, [_dot(p.astype(BF16), v) for p, v in zip(p_list, v_list)]) * inv_l

    for hh in range(hb):
        if kind == "mla":
            q2 = q_ref[:, hh * 256:(hh + 1) * 256].astype(BF16)
            s_list = []
            for kr, pr in zip(k_refs, kpe_refs):
                kcat = jnp.concatenate([kr[:, hh * 256:hh * 256 + 128].astype(BF16), pr[...].astype(BF16)], axis=1)
                s_list.append(_dot_nt(q2, kcat))
            v2 = [vr[:, hh * 256 + 128:hh * 256 + 256].astype(BF16) for vr in v_refs]
            o_ref[:, hh * LANE:(hh + 1) * LANE] = attend(s_list, v2)
            continue

        sl = slice(hh * LANE, (hh + 1) * LANE)
        q2 = q_ref[:, sl]
        k2 = [kr[:, sl].astype(BF16) for kr in k_refs]
        v2 = [vr[:, sl].astype(BF16) for vr in v_refs]
        tq = q2.shape[0]
        q_heads = [jnp.where(lo_half, q2, 0.0), jnp.where(lo_half, 0.0, q2)]
        if stack:
            qs = jnp.concatenate(q_heads, axis=0).astype(BF16)
            o2 = attend([_dot_nt(qs, kk) for kk in k2], v2)
            o_heads = [o2[:tq], o2[tq:]]
            if kind == "diff":
                o_diff = o_heads[0] - lam_ref[...][:, :1] * o_heads[1]
        elif kind == "na":
            o_heads = [attend([_dot_nt(qh.astype(BF16), kk) for kk in k2], v2) for qh in q_heads]
        else:
            (p0, il0), (p1, il1) = [softmax_parts([_dot_nt(qh.astype(BF16), kk) for kk in k2]) for qh in q_heads]
            lam = lam_ref[...][:, :1]
            o_diff = functools.reduce(jnp.add, [
                _dot((a * il0 - lam * (b * il1)).astype(BF16), vv) for a, b, vv in zip(p0, p1, v2)])
        if kind == "na":
            o_ref[:, sl] = jnp.where(lo_half, o_heads[0], o_heads[1])
        else:
            o_ref[:, sl] = _rms(o_diff, subln_ref[...], DIFF_SUBLN_EPS) * (1.0 - lam_init)


def _attention(kind, *, grid, q, ks, vs, kpes=None, extra=(), rows, out_spec, hb, scale, lam_init=0.0,
               stack=False, name="attn"):
    ops = [q] + list(ks) + (list(kpes) if kind == "mla" else []) + (list(vs) if kind != "mla" else [])
    ops += list(extra)
    return pl.pallas_call(
        functools.partial(_attn_kernel, kind=kind, hb=hb, nseg=len(ks), scale=scale, lam_init=lam_init,
                          stack=stack),
        grid=grid,
        in_specs=[s for _, s in ops],
        out_specs=out_spec,
        out_shape=jax.ShapeDtypeStruct((rows, D), F32),
        compiler_params=_cparams(("parallel",) * len(grid)),
        name=name,
    )(*[a for a, _ in ops])


NA_QROWS = 4
NA_KROWS = 12
NA_TQ = NA_QROWS * GRID_W
NA_TK = NA_KROWS * GRID_W


def _na_bias_table(rpb):
    h = rpb.shape[0]
    qc = np.arange(GRID_W)[:, None]
    kc = np.arange(GRID_W)[None, :]
    qs = np.clip(qc - NA_WIN_C // 2, 0, GRID_W - NA_WIN_C)
    col_ok = (kc >= qs) & (kc < qs + NA_WIN_C)
    ic = np.clip(kc - qc + NA_WIN_C - 1, 0, 2 * NA_WIN_C - 2)
    n_ir = 2 * NA_WIN_R - 1
    onehot = (ic.reshape(1, -1) == np.arange(2 * NA_WIN_C - 1)[:, None]).astype(np.float32)
    tab = jnp.einsum("hrm,mx->hrx", rpb.astype(F32) * LOG2_E, jnp.asarray(onehot),
                     precision=lax.Precision.HIGHEST)
    tab = jnp.where(jnp.asarray(col_ok.reshape(1, 1, -1)), tab, NEG_INF).reshape(h, n_ir, GRID_W, GRID_W)
    tab = jnp.concatenate([tab, tab], axis=-1)
    n_rows = T_LAT // GRID_W
    n_g = n_rows // NA_QROWS
    ir_idx = np.zeros((3, NA_QROWS, NA_KROWS), np.int32)
    row_ok = np.zeros((3, NA_QROWS, NA_KROWS), bool)
    for pat, g in enumerate((0, 1, n_g - 1)):
        base = int(np.clip(NA_QROWS * g - NA_QROWS, 0, n_rows - NA_KROWS))
        for qr in range(NA_QROWS):
            r = NA_QROWS * g + qr
            r0 = int(np.clip(r - NA_WIN_R // 2, 0, n_rows - NA_WIN_R))
            for kr in range(NA_KROWS):
                ok = r0 <= base + kr < r0 + NA_WIN_R
                row_ok[pat, qr, kr] = ok
                ir_idx[pat, qr, kr] = (base + kr - r + NA_WIN_R - 1) if ok else 0

    def build(tab_ref, o_ref):
        pat_id = pl.program_id(0)
        lo_half = lax.broadcasted_iota(jnp.int32, (1, LANE), 1) < GRID_W
        masked = jnp.full((GRID_W, LANE), NEG_INF, F32)
        for pat in range(3):
            @pl.when(pat_id == pat)
            def _(pat=pat):
                for hd in range(heads_per_step):
                    for qr in range(NA_QROWS):
                        for kp in range(NA_KROWS // 2):
                            halves = [tab_ref[hd, int(ir_idx[pat, qr, kr])] if row_ok[pat, qr, kr] else masked
                                      for kr in (2 * kp, 2 * kp + 1)]
                            o_ref[hd, qr * GRID_W:(qr + 1) * GRID_W, kp * LANE:(kp + 1) * LANE] = jnp.where(
                                lo_half, halves[0], halves[1])

    heads_per_step = 4
    return pl.pallas_call(
        build,
        grid=(3, h // heads_per_step),
        in_specs=[pl.BlockSpec((heads_per_step, n_ir, GRID_W, LANE), lambda p, hh: (hh, 0, 0, 0))],
        out_specs=pl.BlockSpec((None, heads_per_step, NA_TQ, NA_TK), lambda p, hh: (p, hh, 0, 0)),
        out_shape=jax.ShapeDtypeStruct((3, h, NA_TQ, NA_TK), F32),
        compiler_params=_cparams(("parallel", "parallel")),
        name="na_bias",
    )(tab)


def _na_lat_kernel(q_ref, kc_ref, vc_ref, k_ref, v_ref, b_ref, o_ref, *, scale):
    g = pl.program_id(2)
    n_rows = T_LAT // GRID_W
    base = jnp.clip(NA_QROWS * g - NA_QROWS, 0, n_rows - NA_KROWS) * GRID_W
    base = pl.multiple_of(base, GRID_W)
    lane = lax.broadcasted_iota(jnp.int32, (1, LANE), 1)
    lo_half = lane < 64
    log2_scale = scale * LOG2_E
    for lb in range(LATENT_HB):
        cols = slice(lb * LANE, (lb + 1) * LANE)
        q2 = q_ref[:, cols]
        kc = kc_ref[:, cols].astype(BF16)
        vc = vc_ref[:, cols].astype(BF16)
        kl = k_ref[pl.ds(base, NA_TK), cols].astype(BF16)
        vl = v_ref[pl.ds(base, NA_TK), cols].astype(BF16)
        outs = []
        for n in range(2):
            qm = jnp.where(lo_half if n == 0 else jnp.logical_not(lo_half), q2, 0.0).astype(BF16)
            s_c = _dot_nt(qm, kc) * log2_scale
            s_l = _dot_nt(qm, kl) * log2_scale + b_ref[2 * lb + n]
            m = jnp.maximum(jnp.max(s_c, axis=-1, keepdims=True), jnp.max(s_l, axis=-1, keepdims=True))
            p_c = jnp.exp2(s_c - m)
            p_l = jnp.exp2(s_l - m)
            l = jnp.sum(p_c, axis=-1, keepdims=True) + jnp.sum(p_l, axis=-1, keepdims=True)
            o = _dot(p_c.astype(BF16), vc) + _dot(p_l.astype(BF16), vl)
            outs.append(o * (1.0 / l))
        o_ref[:, cols] = jnp.where(lo_half, outs[0], outs[1])


def _na_lat(qkv, cache_k, cache_v, bias):
    n_g = T_LAT // NA_TQ
    q_blk0 = M_CTX // NA_TQ
    kv_blk0 = M_CTX // T_LAT
    wb = LATENT_HB * LANE
    pat = lambda g: jnp.where(g == 0, 0, jnp.where(g == n_g - 1, 2, 1))
    return pl.pallas_call(
        functools.partial(_na_lat_kernel, scale=0.125),
        grid=(N_LAT_SEQ, D // wb, n_g),
        in_specs=[
            pl.BlockSpec((None, NA_TQ, wb), lambda b, h, g: (0, q_blk0 + b * n_g + g, h)),
            pl.BlockSpec((None, PAST, wb), lambda b, h, g: (b, 0, h)),
            pl.BlockSpec((None, PAST, wb), lambda b, h, g: (b, 0, h)),
            pl.BlockSpec((None, T_LAT, wb), lambda b, h, g: (1, kv_blk0 + b, h)),
            pl.BlockSpec((None, T_LAT, wb), lambda b, h, g: (2, kv_blk0 + b, h)),
            pl.BlockSpec((None, 2 * LATENT_HB, NA_TQ, NA_TK), lambda b, h, g: (pat(g), h, 0, 0)),
        ],
        out_specs=pl.BlockSpec((NA_TQ, wb), lambda b, h, g: (b * n_g + g, h)),
        out_shape=jax.ShapeDtypeStruct((M_LAT, D), F32),
        compiler_params=_cparams(("parallel", "parallel", "parallel")),
        name="na_latent",
    )(qkv, cache_k, cache_v, qkv, qkv, bias)


SEQ_T_PER_STEP = 4


def _seq_t_kernel(x_ref, o_ref):
    for s in range(SEQ_T_PER_STEP):
        o_ref[s] = x_ref[s * T_CTX:(s + 1) * T_CTX, :].T


def _seq_transpose(qkv, plane):
    return pl.pallas_call(
        _seq_t_kernel,
        grid=(N_CTX_SEQ // SEQ_T_PER_STEP,),
        in_specs=[pl.BlockSpec((None, SEQ_T_PER_STEP * T_CTX, D), lambda b: (plane, b, 0))],
        out_specs=pl.BlockSpec((SEQ_T_PER_STEP, D, T_CTX), lambda b: (b, 0, 0)),
        out_shape=jax.ShapeDtypeStruct((N_CTX_SEQ, D, T_CTX), F32),
        compiler_params=_cparams(("parallel",)),
        name="seq_transpose",
    )(qkv)


def kernel(x_prompt, x_sample, c, c_ctx, cache_na_k, cache_na_v, cache_diff_k, cache_diff_v, cache_mla_ckv, cache_mla_kpe, ada_w, ada_b, norm1, norm2, pool_w, pool_scale, na_qkv, na_rpb, na_out, diff_qkv, diff_lambda, diff_subln, diff_out, mla_qa, mla_qnorm, mla_qb, mla_kva, mla_kvnorm, mla_kvb, mla_out, ffn_gate, ffn_up, ffn_down, moe_router, moe_gate, moe_up, moe_down, final_norm):
    depth = ada_w.shape[0]
    x = None
    cond8 = jnp.concatenate([c_ctx[None, :], c, jnp.zeros((5, D), F32)], axis=0)
    mods_all = _ada(cond8, ada_w, ada_b)[:, :3].reshape(depth, 3, 6, 1, D)
    outs = {}

    for l in range(depth):
        kind, li = l % 4, l // 4
        mods = mods_all[l]
        if kind == 0:
            x_pair = (x_prompt.reshape(M_CTX, D), x_sample.reshape(M_LAT, D)) if x is None else (x[:M_CTX], x[M_CTX:])
            h = _modrows(x_pair, norm1[l], mods, 0, 1, rows=M_TOK, row0=0, width=D)
            x = _pool(h, *x_pair, pool_w[li], pool_scale[li], mods, 2)
        elif kind == 1:
            qkv = _linear(x, na_qkv[li], tn=D, norm_g=norm1[l], mods=mods, sh_c=0, sc_c=1, planes=True,
                          name="na_qkv")
            to_cache = lambda a: jnp.transpose(a.reshape(N_CTX_SEQ, 1, 16, 64, T_CTX), (0, 1, 4, 2, 3))
            outs["na_k"] = to_cache(_seq_transpose(qkv, 1))
            outs["na_v"] = to_cache(_seq_transpose(qkv, 2))
            seq_spec = lambda p: pl.BlockSpec((None, T_CTX, D), lambda b, p=p: (p, b, 0))
            o_ctx = _attention(
                "na", grid=(N_CTX_SEQ,), q=(qkv, seq_spec(0)), ks=[(qkv, seq_spec(1))], vs=[(qkv, seq_spec(2))],
                rows=M_CTX, out_spec=pl.BlockSpec((T_CTX, D), lambda b: (b, 0)), hb=D // LANE, scale=0.125,
                stack=True, name="na_ctx")
            bias = _na_bias_table(na_rpb[li])
            o_lat = _na_lat(qkv, cache_na_k[:, li].reshape(N_LAT_SEQ, PAST, D),
                            cache_na_v[:, li].reshape(N_LAT_SEQ, PAST, D), bias)
            x = _linear((o_ctx, o_lat), na_out[li], tn=D, tm=TM, resid=x, mods=mods, gate_c=2, name="na_out")
        elif kind == 2:
            lam_init = 0.8 - 0.6 * math.exp(-0.3 * l)
            lp = diff_lambda[li].astype(F32)
            lam = jnp.exp(jnp.sum(lp[0] * lp[1])) - jnp.exp(jnp.sum(lp[2] * lp[3])) + lam_init
            lam_row = jnp.full((1, LANE), lam, F32)
            subln = diff_subln[li].reshape(1, LANE)
            extra = [(lam_row, pl.BlockSpec((1, LANE), lambda *_: (0, 0))),
                     (subln, pl.BlockSpec((1, LANE), lambda *_: (0, 0)))]
            qkv = _linear(x, diff_qkv[li], tn=D, norm_g=norm1[l], mods=mods, sh_c=0, sc_c=1, planes=True,
                          name="diff_qkv")
            outs["diff_k"] = jnp.transpose(
                _seq_transpose(qkv, 1).reshape(N_CTX_SEQ, 1, 8, 2, 64, T_CTX), (0, 1, 5, 2, 3, 4))
            outs["diff_v"] = qkv[2, :M_CTX].reshape(N_CTX_SEQ, 1, T_CTX, 8, 128)
            seq_spec = lambda p: pl.BlockSpec((None, T_CTX, D), lambda b, p=p: (p, b, 0))
            o_ctx = _attention(
                "diff", grid=(N_CTX_SEQ,), q=(qkv, seq_spec(0)), ks=[(qkv, seq_spec(1))], vs=[(qkv, seq_spec(2))],
                extra=extra, rows=M_CTX, out_spec=pl.BlockSpec((T_CTX, D), lambda b: (b, 0)), hb=D // LANE,
                scale=0.125, lam_init=lam_init, stack=True, name="diff_ctx")
            rc, rs = _rope_tables(LANE, 0, LANE)
            qk_r = _rope(qkv, rc, rs, planes=2, row0=M_CTX, width=LANE)
            tq = LATENT_TQ
            nq = T_LAT // tq
            ck = cache_diff_k[:, li].reshape(N_LAT_SEQ, PAST, D)
            cv = cache_diff_v[:, li].reshape(N_LAT_SEQ, PAST, D)
            hb, wb = LATENT_HB, LATENT_HB * LANE
            o_lat = _attention(
                "diff", grid=(N_LAT_SEQ, D // wb, nq),
                q=(qk_r, pl.BlockSpec((None, tq, wb), lambda b, h, i: (0, b * nq + i, h))),
                ks=[(ck, pl.BlockSpec((None, PAST, wb), lambda b, h, i: (b, 0, h))),
                    (qk_r, pl.BlockSpec((None, T_LAT, wb), lambda b, h, i: (1, b, h)))],
                vs=[(cv, pl.BlockSpec((None, PAST, wb), lambda b, h, i: (b, 0, h))),
                    (qkv, pl.BlockSpec((None, T_LAT, wb), lambda b, h, i: (2, M_CTX // T_LAT + b, h)))],
                extra=extra,
                rows=M_LAT, out_spec=pl.BlockSpec((tq, wb), lambda b, h, i: (b * nq + i, h)),
                hb=hb, scale=0.125, lam_init=lam_init, name="diff_latent")
            x = _linear((o_ctx, o_lat), diff_out[li], tn=D, tm=TM, resid=x, mods=mods, gate_c=2, name="diff_out")
        else:
            hd = MLA_NOPE + MLA_ROPE
            wqb = mla_qb[li].reshape(MLA_Q_LORA, MLA_HEADS, hd)
            wqb = jnp.pad(wqb, ((0, 0), (0, 0), (0, 256 - hd))).reshape(MLA_Q_LORA, MLA_HEADS * 256)
            w_a = jnp.concatenate(
                [mla_qa[li], jnp.pad(mla_kva[li], ((0, 0), (0, 512 - MLA_KV_LORA - MLA_ROPE)))], axis=1)
            a = _linear(x, w_a, tn=D, tm=TM, norm_g=norm1[l], mods=mods, sh_c=0, sc_c=1, name="mla_a")
            q = _linear(a, wqb, tn=D, tm=TM, k=MLA_Q_LORA, x_col=0, norm_g=mla_qnorm[li], name="mla_q")
            zero_mod = jnp.zeros((1, 2, 1, MLA_KV_LORA), F32)
            ckv = _modrows(a, mla_kvnorm[li], zero_mod, 0, 1, rows=M_TOK, row0=0, width=MLA_KV_LORA, col_blk=2,
                           grouped=False)
            outs["mla_ckv"] = ckv[:M_CTX].reshape(N_CTX_SEQ, 1, T_CTX, MLA_KV_LORA)
            outs["mla_kpe"] = a[:M_CTX, 768:768 + MLA_ROPE].reshape(N_CTX_SEQ, 1, T_CTX, MLA_ROPE)
            ckv_all = jnp.concatenate([ckv, cache_mla_ckv[:, li].reshape(N_LAT_SEQ * PAST, MLA_KV_LORA)], axis=0)
            kvb = _linear(ckv_all, mla_kvb[li], tn=D, tm=TM, out_dtype=BF16, name="mla_kvb")
            scale = float(hd) ** -0.5
            o_ctx = _attention(
                "mla", grid=(N_CTX_SEQ,),
                q=(q, pl.BlockSpec((T_CTX, 2048), lambda b: (b, 0))),
                ks=[(kvb, pl.BlockSpec((T_CTX, 2048), lambda b: (b, 0)))],
                kpes=[(a, pl.BlockSpec((T_CTX, LANE), lambda b: (b, 6)))],
                vs=None, rows=M_CTX, out_spec=pl.BlockSpec((T_CTX, D), lambda b: (b, 0)), hb=MLA_HEADS,
                scale=scale, name="mla_ctx")
            rc, rs = _rope_tables(256, 128, 192)
            q_r = _rope(q, rc, rs, planes=1, row0=M_CTX, width=256, ncol=MLA_HEADS)[0]
            rc1, rs1 = _rope_tables(LANE, 0, 64)
            kpe_r = _rope(a, rc1, rs1, planes=1, row0=M_CTX, width=LANE, col0=6)[0]
            kpe_c = jnp.pad(cache_mla_kpe[:, li], ((0, 0), (0, 0), (0, LANE - MLA_ROPE)))
            tq = LATENT_TQ
            nq = T_LAT // tq
            hb = LATENT_HB
            o_lat = _attention(
                "mla", grid=(N_LAT_SEQ, MLA_HEADS // hb, nq),
                q=(q_r, pl.BlockSpec((tq, hb * 256), lambda b, h, i: (b * nq + i, h))),
                ks=[(kvb, pl.BlockSpec((PAST, hb * 256), lambda b, h, i: (M_TOK // PAST + b, h))),
                    (kvb, pl.BlockSpec((T_LAT, hb * 256), lambda b, h, i: (M_CTX // T_LAT + b, h)))],
                kpes=[(kpe_c, pl.BlockSpec((None, PAST, LANE), lambda b, h, i: (b, 0, 0))),
                      (kpe_r, pl.BlockSpec((T_LAT, LANE), lambda b, h, i: (b, 0)))],
                vs=None, rows=M_LAT,
                out_spec=pl.BlockSpec((tq, hb * LANE), lambda b, h, i: (b * nq + i, h)),
                hb=hb, scale=scale, name="mla_latent")
            x = _linear((o_ctx, o_lat), mla_out[li], tn=D, tm=TM, resid=x, mods=mods, gate_c=2, name="mla_out")

        f = l // 2
        if l % 2 == 0:
            x = _ffn_dense(x, ffn_gate, ffn_up, ffn_down, f, norm2[l], mods, 3, 4, 5)
        else:
            x = _moe(x, norm2[l], mods, 3, 4, 5, moe_router[f], moe_gate, moe_up, moe_down, f)

    zero_mod = jnp.zeros((1, 2, 1, D), F32)
    y_prompt = _modrows(x, final_norm, zero_mod, 0, 1, rows=M_CTX, row0=0, width=D, grouped=False)
    y_sample = _modrows(x, final_norm, zero_mod, 0, 1, rows=M_LAT, row0=M_CTX, width=D, grouped=False)
    return (y_prompt.reshape(N_CTX_SEQ, T_CTX, D), y_sample.reshape(N_LAT_SEQ, T_LAT, D),
            outs["na_k"], outs["na_v"], outs["diff_k"], outs["diff_v"], outs["mla_ckv"], outs["mla_kpe"])
```

```python
import functools
import math

import numpy as np
import jax
import jax.numpy as jnp
from jax import lax
from jax.experimental import pallas as pl
from jax.experimental.pallas import tpu as pltpu

F32 = jnp.float32
BF16 = jnp.bfloat16

D = 1024
N_CTX_SEQ = 32
T_CTX = 256
N_LAT_SEQ = 2
T_LAT = 2048
PAST = 512
M_CTX = N_CTX_SEQ * T_CTX
M_LAT = N_LAT_SEQ * T_LAT
M_TOK = M_CTX + M_LAT
GRID_W = 64
NORM_EPS = 1e-6
NEG_INF = -1e30
ROPE_BASE = 10000.0
POOL_WINDOWS = (2, 4, 8, 16)
NA_WIN_R = 8
NA_WIN_C = 16
DIFF_SUBLN_EPS = 1e-5
MLA_HEADS = 8
MLA_NOPE = 128
MLA_ROPE = 64
MLA_KV_LORA = 256
MLA_Q_LORA = 512
FFN_HIDDEN = 2816
N_EXPERTS = 8

LOG2_E = math.log2(math.e)
LANE = 128
TM = 1024
TH = 256
SUB = 256
VMEM_LIMIT = 56 * 1024 * 1024

TMF = 2048
MOE_ROWS = 2 * M_TOK + N_EXPERTS * TMF
MOE_TILES = MOE_ROWS // TMF


def _cparams(sem):
    return pltpu.CompilerParams(dimension_semantics=sem, vmem_limit_bytes=VMEM_LIMIT)


def _group_of_tile(i, tm):
    n_ctx = M_CTX // tm
    return jnp.where(i < n_ctx, 0, 1 + (i - n_ctx) // (T_LAT // tm))


def _dot(a, b):
    return jnp.dot(a, b, preferred_element_type=F32)


def _dot_nt(a, b):
    return lax.dot_general(a, b, (((1,), (1,)), ((), ())), preferred_element_type=F32)


def _rms(x, g, eps):
    return x * lax.rsqrt(jnp.mean(x * x, axis=-1, keepdims=True) + eps) * g


def _ada_kernel(c_ref, w_ref, b_ref, o_ref):
    c = c_ref[...]
    s = (c * jax.nn.sigmoid(c)).astype(BF16)
    o_ref[...] = _dot(s, w_ref[...].astype(BF16)) + b_ref[...]


def _ada(cond8, ada_w, ada_b):
    depth = ada_w.shape[0]
    tn = 1536
    return pl.pallas_call(
        _ada_kernel,
        grid=(depth, 6 * D // tn),
        in_specs=[
            pl.BlockSpec((8, D), lambda l, j: (0, 0)),
            pl.BlockSpec((None, D, tn), lambda l, j: (l, 0, j)),
            pl.BlockSpec((None, 1, tn), lambda l, j: (l, 0, j)),
        ],
        out_specs=pl.BlockSpec((None, 8, tn), lambda l, j: (l, 0, j)),
        out_shape=jax.ShapeDtypeStruct((depth, 8, 6 * D), F32),
        compiler_params=_cparams(("parallel", "parallel")),
        name="ada",
    )(cond8, ada_w, ada_b.reshape(depth, 1, 6 * D))


def _token_tile(xc_ref, xl_ref, tile, tm):
    return jnp.where(tile < M_CTX // tm, xc_ref[...], xl_ref[...])


def _split_specs(block, tm, col):
    n_ctx = M_CTX // tm
    return [pl.BlockSpec(block, lambda i, *_: (jnp.minimum(i, n_ctx - 1), col(*_))),
            pl.BlockSpec(block, lambda i, *_: (jnp.maximum(i - n_ctx, 0), col(*_)))]


def _modrows_kernel(*refs, eps, split, tm):
    if split:
        xc_ref, xl_ref, g_ref, sh_ref, sc_ref, o_ref = refs
        x = _token_tile(xc_ref, xl_ref, pl.program_id(0), tm)
    else:
        x_ref, g_ref, sh_ref, sc_ref, o_ref = refs
        x = x_ref[...]
    o_ref[...] = _rms(x, g_ref[...], eps) * (1.0 + sc_ref[...]) + sh_ref[...]


def _modrows(x, g, mods, sh_c, sc_c, *, rows, row0, width, col_blk=0, tm=TM, eps=NORM_EPS,
             grouped=True):
    r0 = row0 // tm
    if grouped:
        grp = lambda i: _group_of_tile(i + r0, tm)
    else:
        grp = lambda i: 0
    split = isinstance(x, tuple)
    if split:
        xs, x_specs = list(x), _split_specs((tm, width), tm, lambda: 0)
    else:
        xs, x_specs = [x], [pl.BlockSpec((tm, width), lambda i: (i + r0, col_blk))]
    return pl.pallas_call(
        functools.partial(_modrows_kernel, eps=eps, split=split, tm=tm),
        grid=(rows // tm,),
        in_specs=x_specs + [
            pl.BlockSpec((1, width), lambda i: (0, 0)),
            pl.BlockSpec((None, None, 1, width), lambda i: (grp(i), sh_c, 0, 0)),
            pl.BlockSpec((None, None, 1, width), lambda i: (grp(i), sc_c, 0, 0)),
        ],
        out_specs=pl.BlockSpec((tm, width), lambda i: (i, 0)),
        out_shape=jax.ShapeDtypeStruct((rows, width), F32),
        compiler_params=_cparams(("parallel",)),
        name="modrows",
    )(*xs, g.reshape(1, width), mods, mods)


def _linear_kernel(*refs, prologue, eps, epilogue, split, tm, tn, planes, route):
    it = iter(refs)
    x_ref = next(it)
    x_lat_ref = next(it) if split else None
    w_ref = next(it)
    g_ref = next(it) if prologue != "none" else None
    sh_ref = next(it) if prologue == "mod" else None
    sc_ref = next(it) if prologue == "mod" else None
    res_ref = next(it) if epilogue == "resid" else None
    gate_ref = next(it) if epilogue == "resid" else None
    if route:
        g2_ref, sh2_ref, sc2_ref, r_ref = next(it), next(it), next(it), next(it)
    o_ref = next(it)
    if route:
        h_ref, idx_ref, gates_ref = next(it), next(it), next(it)
    wb_ref = next(it)
    n = w_ref.shape[1]

    @pl.when(pl.program_id(0) == 0)
    def _():
        wb_ref[...] = w_ref[...].astype(BF16)

    def run(src_ref):
        x = src_ref[...]
        if prologue != "none":
            x = _rms(x, g_ref[...], eps)
        if prologue == "mod":
            x = x * (1.0 + sc_ref[...]) + sh_ref[...]
        xb = x.astype(BF16)
        for p in range(n // tn):
            cols = slice(p * tn, (p + 1) * tn)
            acc = _dot(xb, wb_ref[:, cols])
            if epilogue == "resid":
                acc = res_ref[:, cols] + gate_ref[:, cols] * acc
            if planes:
                o_ref[p] = acc.astype(o_ref.dtype)
            else:
                o_ref[:, cols] = acc.astype(o_ref.dtype)
            if route:
                h = _rms(acc, g2_ref[...], NORM_EPS) * (1.0 + sc2_ref[...]) + sh2_ref[...]
                h_ref[...] = h
                _route_rows(h, r_ref, idx_ref, gates_ref)

    if split:
        is_ctx = pl.program_id(0) < M_CTX // tm
        pl.when(is_ctx)(lambda: run(x_ref))
        pl.when(jnp.logical_not(is_ctx))(lambda: run(x_lat_ref))
    else:
        run(x_ref)


TML = 512
LATENT_TQ = 256
LATENT_HB = 4


def _linear(x, w, *, tn, k=None, x_col=0, norm_g=None, eps=NORM_EPS, mods=None, sh_c=None, sc_c=None,
            resid=None, gate_c=None, planes=False, tm=TML, out_dtype=F32, route=None, name="linear"):
    split = isinstance(x, tuple)
    n = w.shape[1]
    if route is not None:
        tm = TML
    prologue = "none" if norm_g is None else ("mod" if sh_c is not None else "norm")
    epilogue = "none" if resid is None else "resid"
    grp = lambda i: _group_of_tile(i, tm)
    if split:
        x_ctx, x_lat = x
        m = x_ctx.shape[0] + x_lat.shape[0]
        k = x_ctx.shape[1]
        n_ctx = x_ctx.shape[0] // tm
        args = [x_ctx, x_lat]
        in_specs = [
            pl.BlockSpec((tm, k), lambda i: (jnp.minimum(i, n_ctx - 1), 0)),
            pl.BlockSpec((tm, k), lambda i: (jnp.maximum(i - n_ctx, 0), 0)),
        ]
    else:
        m = x.shape[0]
        k = x.shape[1] if k is None else k
        args = [x]
        in_specs = [pl.BlockSpec((tm, k), lambda i: (i, x_col))]
    args.append(w)
    in_specs.append(pl.BlockSpec((k, n), lambda i: (0, 0)))
    if prologue != "none":
        args.append(norm_g.reshape(1, k))
        in_specs.append(pl.BlockSpec((1, k), lambda i: (0, 0)))
    if prologue == "mod":
        args += [mods, mods]
        in_specs += [
            pl.BlockSpec((None, None, 1, k), lambda i: (grp(i), sh_c, 0, 0)),
            pl.BlockSpec((None, None, 1, k), lambda i: (grp(i), sc_c, 0, 0)),
        ]
    if epilogue == "resid":
        args += [resid, mods]
        in_specs += [
            pl.BlockSpec((tm, n), lambda i: (i, 0)),
            pl.BlockSpec((None, None, 1, n), lambda i: (grp(i), gate_c, 0, 0)),
        ]
    if planes:
        out_spec = pl.BlockSpec((n // tn, tm, tn), lambda i: (0, i, 0))
        out_shape = jax.ShapeDtypeStruct((n // tn, m, tn), out_dtype)
    else:
        out_spec = pl.BlockSpec((tm, n), lambda i: (i, 0))
        out_shape = jax.ShapeDtypeStruct((m, n), out_dtype)
    if route is not None:
        assert epilogue == "resid" and n == tn == D and not planes
        norm2_g, sh2_c, sc2_c, router = route
        args += [norm2_g.reshape(1, D), mods, mods, jnp.pad(router, ((0, 0), (0, LANE - N_EXPERTS)))]
        in_specs += [
            pl.BlockSpec((1, D), lambda i: (0, 0)),
            pl.BlockSpec((None, None, 1, D), lambda i: (grp(i), sh2_c, 0, 0)),
            pl.BlockSpec((None, None, 1, D), lambda i: (grp(i), sc2_c, 0, 0)),
            pl.BlockSpec((D, LANE), lambda i: (0, 0)),
        ]
        out_spec = [out_spec, pl.BlockSpec((tm, D), lambda i: (i, 0)),
                    pl.BlockSpec((tm, LANE), lambda i: (i, 0)), pl.BlockSpec((tm, LANE), lambda i: (i, 0))]
        out_shape = [out_shape, jax.ShapeDtypeStruct((m, D), F32),
                     jax.ShapeDtypeStruct((m, LANE), jnp.int32), jax.ShapeDtypeStruct((m, LANE), F32)]
    return pl.pallas_call(
        functools.partial(_linear_kernel, prologue=prologue, eps=eps, epilogue=epilogue, split=split, tm=tm,
                          tn=tn, planes=planes, route=route is not None),
        grid=(m // tm,),
        in_specs=in_specs,
        out_specs=out_spec,
        out_shape=out_shape,
        scratch_shapes=[pltpu.VMEM((k, n), BF16)],
        compiler_params=_cparams(("arbitrary",)),
        name=name,
    )(*args)


POOL_ROWS = 2048


def _pool_kernel(h_ref, xc_ref, xl_ref, w_ref, ps_ref, gate_ref, o_ref):
    i = pl.program_id(0)
    g = pl.program_id(1)
    seq = jnp.where(i < M_CTX // POOL_ROWS, T_CTX, T_LAT)
    t = lax.broadcasted_iota(jnp.int32, (POOL_ROWS, 1), 0) & (seq - 1)

    def ahead(a, k):
        return jnp.where(t + k < seq, pltpu.roll(a, POOL_ROWS - k, 0), 0.0)

    def behind(a, k):
        return jnp.where(t - k >= 0, pltpu.roll(a, k, 0), 0.0)

    for gi, win in enumerate(POOL_WINDOWS):
        @pl.when(g == gi)
        def _(win=win):
            h = h_ref[...]
            fwd, bwd, k = h, h, 1
            while k < win // 2:
                fwd, bwd, k = fwd + ahead(fwd, k), bwd + behind(bwd, k), 2 * k
            acc = fwd + behind(bwd, 1)
            cnt = jnp.minimum(t + (win - win // 2), seq) - jnp.maximum(t - win // 2, 0)
            pooled = acc / cnt.astype(F32)
            y = _dot((pooled - h).astype(BF16), w_ref[...].astype(BF16)) * ps_ref[...]
            o_ref[...] = _token_tile(xc_ref, xl_ref, i, POOL_ROWS) + gate_ref[...] * y


def _pool(h, x_ctx, x_lat, pool_w, pool_scale, mods, gate_c):
    gw = D // len(POOL_WINDOWS)
    grp = lambda i: _group_of_tile(i, POOL_ROWS)
    return pl.pallas_call(
        _pool_kernel,
        grid=(M_TOK // POOL_ROWS, len(POOL_WINDOWS)),
        in_specs=[
            pl.BlockSpec((POOL_ROWS, gw), lambda i, g: (i, g)),
            *_split_specs((POOL_ROWS, gw), POOL_ROWS, lambda g: g),
            pl.BlockSpec((None, gw, gw), lambda i, g: (g, 0, 0)),
            pl.BlockSpec((1, gw), lambda i, g: (0, g)),
            pl.BlockSpec((None, None, 1, gw), lambda i, g: (grp(i), gate_c, 0, g)),
        ],
        out_specs=pl.BlockSpec((POOL_ROWS, gw), lambda i, g: (i, g)),
        out_shape=jax.ShapeDtypeStruct((M_TOK, D), F32),
        compiler_params=_cparams(("parallel", "parallel")),
        name="pool",
    )(h, x_ctx, x_lat, pool_w, pool_scale.reshape(1, D), mods)


NJ = FFN_HIDDEN // TH
TN2 = 256
NN = D // TN2
STRAIGHT_COUNTS = (TMF // SUB, 4, 5)


def _swiglu_kernel(*refs, moe):
    if moe:
        te_ref, tv_ref, x_ref, wg_ref, wu_ref, wd_ref, o_ref, xb_ref, h_ref = refs
    else:
        x_ref, wg_ref, wu_ref, wd_ref, g_ref, sh_ref, sc_ref, res_ref, gate_ref, o_ref, xb_ref, h_ref = refs
    j = pl.program_id(1)
    n_valid = tv_ref[pl.program_id(0)] if moe else TMF

    def per_sub_block(fn, fn_skipped=None):
        sub_blocks = [pl.ds(s * SUB, SUB) for s in range(TMF // SUB)]

        def leading(count):
            def run():
                for rows in sub_blocks[:count]:
                    fn(rows)
                if fn_skipped is not None:
                    for rows in sub_blocks[count:]:
                        fn_skipped(rows)
            return run

        def valid_rows_only():
            for s, rows in enumerate(sub_blocks):
                pl.when(s * SUB < n_valid)(functools.partial(fn, rows))
                if fn_skipped is not None:
                    pl.when(s * SUB >= n_valid)(functools.partial(fn_skipped, rows))

        if moe:
            n_blocks = (n_valid + SUB - 1) // SUB
            for count in STRAIGHT_COUNTS:
                pl.when(n_blocks == count)(leading(count))
            pl.when(functools.reduce(jnp.logical_and, [n_blocks != c for c in STRAIGHT_COUNTS]))(valid_rows_only)
        else:
            leading(len(sub_blocks))()

    @pl.when(j == 0)
    def _():
        def cast(rows):
            x = x_ref[rows, :]
            if not moe:
                x = _rms(x, g_ref[...], NORM_EPS) * (1.0 + sc_ref[...]) + sh_ref[...]
            xb_ref[rows, :] = x.astype(BF16)

        per_sub_block(cast)

    @pl.when(j < NJ)
    def _():
        wg = wg_ref[...].astype(BF16)
        wu = wu_ref[...].astype(BF16)

        def up(rows):
            xb = xb_ref[rows, :]
            a = _dot(xb, wg)
            u = _dot(xb, wu)
            h_ref[j, rows, :] = (a * jax.nn.sigmoid(a) * u).astype(BF16)

        per_sub_block(up)

    @pl.when(j >= NJ)
    def _():
        wd = wd_ref[...].astype(BF16)

        def down(rows):
            hcat = jnp.concatenate([h_ref[jj, rows, :] for jj in range(NJ)], axis=1)
            y = _dot(hcat, wd)
            if moe:
                o_ref[rows, :] = y
            else:
                o_ref[rows, :] = res_ref[rows, :] + gate_ref[...] * y

        def skipped(rows):
            o_ref[rows, :] = jnp.zeros((SUB, TN2), F32)

        per_sub_block(down, skipped)


_SWIGLU_SCRATCH = [pltpu.VMEM((TMF, D), BF16), pltpu.VMEM((NJ, TMF, TH), BF16)]


def _ffn_dense(x, wg, wu, wd, f, norm_g, mods, sh_c, sc_c, gate_c):
    grp = lambda i: _group_of_tile(i, TMF)
    jh = lambda j: jnp.minimum(j, NJ - 1)
    jn = lambda j: jnp.maximum(j - NJ, 0)
    mod_spec = lambda c: pl.BlockSpec((None, None, 1, D), lambda i, j: (grp(i), c, 0, 0))
    return pl.pallas_call(
        functools.partial(_swiglu_kernel, moe=False),
        grid=(M_TOK // TMF, NJ + NN),
        in_specs=[
            pl.BlockSpec((TMF, D), lambda i, j: (i, 0)),
            pl.BlockSpec((None, D, TH), lambda i, j: (f, 0, jh(j))),
            pl.BlockSpec((None, D, TH), lambda i, j: (f, 0, jh(j))),
            pl.BlockSpec((None, FFN_HIDDEN, TN2), lambda i, j: (f, 0, jn(j))),
            pl.BlockSpec((1, D), lambda i, j: (0, 0)),
            mod_spec(sh_c), mod_spec(sc_c),
            pl.BlockSpec((TMF, TN2), lambda i, j: (i, jn(j))),
            pl.BlockSpec((None, None, 1, TN2), lambda i, j: (grp(i), gate_c, 0, jn(j))),
        ],
        out_specs=pl.BlockSpec((TMF, TN2), lambda i, j: (i, jn(j))),
        out_shape=jax.ShapeDtypeStruct((M_TOK, D), F32),
        scratch_shapes=_SWIGLU_SCRATCH,
        compiler_params=_cparams(("parallel", "arbitrary")),
        name="ffn_dense",
    )(x, wg, wu, wd, norm_g.reshape(1, D), mods, mods, x, mods)


def _ffn_moe(xs, wg, wu, wd, f, tile_expert, tile_valid):
    jh = lambda i, j, tv: jnp.where(tv[i] > 0, jnp.minimum(j, NJ - 1), NJ - 1)
    jn = lambda i, j, tv: jnp.where(tv[i] > 0, jnp.maximum(j - NJ, 0), NN - 1)
    return pl.pallas_call(
        functools.partial(_swiglu_kernel, moe=True),
        grid_spec=pltpu.PrefetchScalarGridSpec(
            num_scalar_prefetch=2,
            grid=(MOE_TILES, NJ + NN),
            in_specs=[
                pl.BlockSpec((TMF, D), lambda i, j, te, tv: (i, 0)),
                pl.BlockSpec((None, None, D, TH), lambda i, j, te, tv: (f, te[i], 0, jh(i, j, tv))),
                pl.BlockSpec((None, None, D, TH), lambda i, j, te, tv: (f, te[i], 0, jh(i, j, tv))),
                pl.BlockSpec((None, None, FFN_HIDDEN, TN2), lambda i, j, te, tv: (f, te[i], 0, jn(i, j, tv))),
            ],
            out_specs=pl.BlockSpec((TMF, TN2), lambda i, j, te, tv: (i, jnp.maximum(j - NJ, 0))),
            scratch_shapes=_SWIGLU_SCRATCH,
        ),
        out_shape=jax.ShapeDtypeStruct((MOE_ROWS, D), F32),
        compiler_params=_cparams(("parallel", "arbitrary")),
        name="ffn_moe",
    )(tile_expert, tile_valid, xs, wg, wu, wd)


def _route_rows(h, r_ref, idx_ref, gates_ref):
    w = r_ref[...]
    h_hi = h.astype(BF16)
    h_lo = (h - h_hi.astype(F32)).astype(BF16)
    w_hi = w.astype(BF16)
    w_lo = (w - w_hi.astype(F32)).astype(BF16)
    logits = _dot(h_hi, w_hi) + (_dot(h_lo, w_hi) + _dot(h_hi, w_lo))
    col = lax.broadcasted_iota(jnp.int32, logits.shape, 1)
    colf = col.astype(F32)
    lg = jnp.where(col < N_EXPERTS, logits, -jnp.inf)
    m1 = jnp.max(lg, axis=-1, keepdims=True)
    i1 = jnp.min(jnp.where(lg == m1, colf, float(LANE)), axis=-1, keepdims=True)
    lg2 = jnp.where(colf == i1, -jnp.inf, lg)
    m2 = jnp.max(lg2, axis=-1, keepdims=True)
    i2 = jnp.min(jnp.where(lg2 == m2, colf, float(LANE)), axis=-1, keepdims=True)
    i1 = i1.astype(jnp.int32)
    i2 = i2.astype(jnp.int32)
    e = jnp.exp(m2 - m1)
    g1 = 1.0 / (1.0 + e)
    g2 = e / (1.0 + e)
    idx_ref[...] = jnp.where(col == 0, i1, jnp.where(col == 1, i2, 0))
    gates_ref[...] = jnp.where(col == 0, g1, jnp.where(col == 1, g2, 0.0))


def _router_kernel(x_ref, g_ref, sh_ref, sc_ref, r_ref, h_ref, idx_ref, gates_ref):
    h = _rms(x_ref[...], g_ref[...], NORM_EPS) * (1.0 + sc_ref[...]) + sh_ref[...]
    h_ref[...] = h
    _route_rows(h, r_ref, idx_ref, gates_ref)


def _router(x, norm_g, mods, sh_c, sc_c, router):
    r_pad = jnp.pad(router, ((0, 0), (0, LANE - N_EXPERTS)))
    grp = lambda i: _group_of_tile(i, TM)
    return pl.pallas_call(
        _router_kernel,
        grid=(M_TOK // TM,),
        in_specs=[
            pl.BlockSpec((TM, D), lambda i: (i, 0)),
            pl.BlockSpec((1, D), lambda i: (0, 0)),
            pl.BlockSpec((None, None, 1, D), lambda i: (grp(i), sh_c, 0, 0)),
            pl.BlockSpec((None, None, 1, D), lambda i: (grp(i), sc_c, 0, 0)),
            pl.BlockSpec((D, LANE), lambda i: (0, 0)),
        ],
        out_specs=[
            pl.BlockSpec((TM, D), lambda i: (i, 0)),
            pl.BlockSpec((TM, LANE), lambda i: (i, 0)),
            pl.BlockSpec((TM, LANE), lambda i: (i, 0)),
        ],
        out_shape=[
            jax.ShapeDtypeStruct((M_TOK, D), F32),
            jax.ShapeDtypeStruct((M_TOK, LANE), jnp.int32),
            jax.ShapeDtypeStruct((M_TOK, LANE), F32),
        ],
        compiler_params=_cparams(("parallel",)),
        name="router",
    )(x, norm_g.reshape(1, D), mods, mods, r_pad)


def _route_plan(e1, e2):
    e = jnp.concatenate([e1, e2])
    oh = (e[:, None] == jnp.arange(N_EXPERTS, dtype=jnp.int32)[None, :]).astype(jnp.int32)
    csum = jnp.cumsum(oh, axis=0)
    rank = jnp.sum((csum - oh) * oh, axis=1)
    counts = csum[-1]
    padded = ((counts + TMF - 1) // TMF) * TMF
    ends = jnp.cumsum(padded)
    starts = ends - padded
    pos = jnp.sum(oh * starts[None, :], axis=1) + rank
    tile_start = jnp.arange(MOE_TILES, dtype=jnp.int32) * TMF
    te = jnp.minimum(jnp.sum((tile_start[:, None] >= ends[None, :]).astype(jnp.int32), axis=1),
                     N_EXPERTS - 1)
    used = tile_start < ends[-1]
    valid = jnp.where(used, jnp.clip(counts[te] - (tile_start - starts[te]), 0, TMF), 0)
    last_e = te[jnp.maximum(ends[-1] // TMF - 1, 0)]
    te = jnp.where(used, te, last_e)
    sub_off = jnp.arange(TMF // SUB, dtype=jnp.int32) * SUB
    sub_valid = jnp.clip(valid[:, None] - sub_off[None, :], 0, SUB).reshape(-1)
    return pos.astype(jnp.int32), te.astype(jnp.int32), valid.astype(jnp.int32), sub_valid.astype(jnp.int32)


DMA_UNROLL = 8
DISPATCH_ROWS = 1024
COMBINE_ROWS = 512


def _dispatch_kernel(pos_ref, sv_ref, h_ref, xs_hbm, zero_ref, sem, zsem):
    i = pl.program_id(0)

    @pl.when(i == 0)
    def _():
        zero_ref[...] = jnp.zeros_like(zero_ref)

        def zero_copy(b):
            return pltpu.make_async_copy(zero_ref, xs_hbm.at[pl.ds(pl.multiple_of(b * SUB, SUB), SUB)], zsem)

        def start(b, c):
            pl.when(sv_ref[b] < SUB)(lambda: zero_copy(b).start())
            return c

        def wait(b, c):
            pl.when(sv_ref[b] < SUB)(lambda: zero_copy(b).wait())
            return c

        lax.fori_loop(0, MOE_ROWS // SUB, start, 0)
        lax.fori_loop(0, MOE_ROWS // SUB, wait, 0)

    def issue(r, c):
        p1 = pos_ref[i * DISPATCH_ROWS + r]
        p2 = pos_ref[M_TOK + i * DISPATCH_ROWS + r]
        pltpu.make_async_copy(h_ref.at[pl.ds(r, 1)], xs_hbm.at[pl.ds(p1, 1)], sem).start()
        pltpu.make_async_copy(h_ref.at[pl.ds(r, 1)], xs_hbm.at[pl.ds(p2, 1)], sem).start()
        return c

    lax.fori_loop(0, DISPATCH_ROWS, issue, 0, unroll=DMA_UNROLL)
    pltpu.make_async_copy(h_ref, xs_hbm.at[pl.ds(0, DISPATCH_ROWS)], sem).wait()
    pltpu.make_async_copy(h_ref, xs_hbm.at[pl.ds(0, DISPATCH_ROWS)], sem).wait()


def _moe_dispatch(h, pos, sub_valid):
    return pl.pallas_call(
        _dispatch_kernel,
        grid_spec=pltpu.PrefetchScalarGridSpec(
            num_scalar_prefetch=2,
            grid=(M_TOK // DISPATCH_ROWS,),
            in_specs=[pl.BlockSpec((DISPATCH_ROWS, D), lambda i, p, z: (i, 0))],
            out_specs=pl.BlockSpec(memory_space=pl.ANY),
            scratch_shapes=[pltpu.VMEM((SUB, D), F32), pltpu.SemaphoreType.DMA, pltpu.SemaphoreType.DMA],
        ),
        out_shape=jax.ShapeDtypeStruct((MOE_ROWS, D), F32),
        compiler_params=_cparams(("arbitrary",)),
        name="moe_dispatch",
    )(pos, sub_valid, h)


def _combine_kernel(pos_ref, y_hbm, x_ref, tg_ref, gate_ref, o_ref, y_ref, sem):
    i = pl.program_id(0)
    slot = i % 2

    def picked(k, s):
        return pltpu.make_async_copy(y_hbm.at[pl.ds(0, COMBINE_ROWS)], y_ref.at[s, k], sem.at[s, k])

    def gather_tile(t, s):
        def issue(r, c):
            for k in range(2):
                p = pos_ref[k * M_TOK + t * COMBINE_ROWS + r]
                pltpu.make_async_copy(y_hbm.at[pl.ds(p, 1)], y_ref.at[s, k, pl.ds(r, 1)], sem.at[s, k]).start()
            return c

        lax.fori_loop(0, COMBINE_ROWS, issue, 0, unroll=DMA_UNROLL)

    pl.when(i == 0)(lambda: gather_tile(0, 0))
    pl.when(i + 1 < pl.num_programs(0))(lambda: gather_tile(i + 1, 1 - slot))
    picked(0, slot).wait()
    picked(1, slot).wait()
    tg = tg_ref[...]
    moe_out = tg[:, 0:1] * y_ref[slot, 0] + tg[:, 1:2] * y_ref[slot, 1]
    o_ref[...] = x_ref[...] + gate_ref[...] * moe_out


def _moe_combine(y, pos, x, top_gates, mods, gate_c):
    grp = lambda i: _group_of_tile(i, COMBINE_ROWS)
    return pl.pallas_call(
        _combine_kernel,
        grid_spec=pltpu.PrefetchScalarGridSpec(
            num_scalar_prefetch=1,
            grid=(M_TOK // COMBINE_ROWS,),
            in_specs=[
                pl.BlockSpec(memory_space=pl.ANY),
                pl.BlockSpec((COMBINE_ROWS, D), lambda i, p: (i, 0)),
                pl.BlockSpec((COMBINE_ROWS, LANE), lambda i, p: (i, 0)),
                pl.BlockSpec((None, None, 1, D), lambda i, p: (grp(i), gate_c, 0, 0)),
            ],
            out_specs=pl.BlockSpec((COMBINE_ROWS, D), lambda i, p: (i, 0)),
            scratch_shapes=[pltpu.VMEM((2, 2, COMBINE_ROWS, D), F32), pltpu.SemaphoreType.DMA((2, 2))],
        ),
        out_shape=jax.ShapeDtypeStruct((M_TOK, D), F32),
        compiler_params=_cparams(("arbitrary",)),
        name="moe_combine",
    )(pos, y, x, top_gates, mods)


def _moe(x, norm_g, mods, sh_c, sc_c, gate_c, router, wg, wu, wd, f, routed=None):
    h, idx, gates = _router(x, norm_g, mods, sh_c, sc_c, router) if routed is None else routed
    pos, te, tv, sub_valid = _route_plan(idx[:, 0], idx[:, 1])
    xs = _moe_dispatch(h, pos, sub_valid)
    ys = _ffn_moe(xs, wg, wu, wd, f, te, tv)
    return _moe_combine(ys, pos, x, gates, mods, gate_c)


def _rope_tables(width, lo, hi):
    pos = jnp.arange(T_LAT)
    rows = (pos // GRID_W).astype(F32)
    cols = (pos % GRID_W).astype(F32)
    n_freq = 16
    inv = ROPE_BASE ** (-jnp.arange(n_freq, dtype=F32) / n_freq)
    ang = jnp.concatenate([rows[:, None] * inv[None, :], cols[:, None] * inv[None, :]], axis=-1)
    cos, sin = jnp.cos(ang), jnp.sin(ang)
    lane = np.arange(width)
    active = (lane >= lo) & (lane < hi)
    reps = width // 64
    c = jnp.tile(jnp.concatenate([cos, cos], axis=-1), (1, reps))
    s = jnp.tile(jnp.concatenate([-sin, sin], axis=-1), (1, reps))
    return jnp.where(active[None, :], c, 1.0), jnp.where(active[None, :], s, 0.0)


def _rope_kernel(x_ref, c_ref, s_ref, o_ref, *, width):
    lane = lax.broadcasted_iota(jnp.int32, (1, width), 1)
    first = (lane & 63) < 32
    c = c_ref[...]
    s = s_ref[...]
    for j in range(x_ref.shape[-1] // width):
        cols = slice(j * width, (j + 1) * width)
        x = x_ref[:, cols]
        partner = jnp.where(first, pltpu.roll(x, width - 32, 1), pltpu.roll(x, 32, 1))
        o_ref[:, cols] = (x * c + partner * s).astype(o_ref.dtype)


def _rope(x, c, s, *, planes, row0, width, col0=0, ncol=1, tr=512):
    r0 = row0 // tr
    nt = T_LAT // tr
    if x.ndim == 3:
        ncol = x.shape[2] // width
        x_spec = pl.BlockSpec((None, tr, ncol * width), lambda p, i: (p, i + r0, 0))
    else:
        x_spec = pl.BlockSpec((tr, ncol * width), lambda p, i: (i + r0, col0 // ncol))
    return pl.pallas_call(
        functools.partial(_rope_kernel, width=width),
        grid=(planes, M_LAT // tr),
        in_specs=[
            x_spec,
            pl.BlockSpec((tr, width), lambda p, i: (i % nt, 0)),
            pl.BlockSpec((tr, width), lambda p, i: (i % nt, 0)),
        ],
        out_specs=pl.BlockSpec((None, tr, ncol * width), lambda p, i: (p, i, 0)),
        out_shape=jax.ShapeDtypeStruct((planes, M_LAT, ncol * width), BF16),
        compiler_params=_cparams(("parallel", "parallel")),
        name="rope",
    )(x, c, s)


def _attn_kernel(*refs, kind, hb, nseg, scale, lam_init, stack):
    it = iter(refs)
    q_ref = next(it)
    k_refs = [next(it) for _ in range(nseg)]
    kpe_refs = [next(it) for _ in range(nseg)] if kind == "mla" else None
    v_refs = [next(it) for _ in range(nseg)] if kind != "mla" else k_refs
    lam_ref = next(it) if kind == "diff" else None
    subln_ref = next(it) if kind == "diff" else None
    o_ref = next(it)

    lane = lax.broadcasted_iota(jnp.int32, (1, LANE), 1)
    lo_half = lane < 64
    log2_scale = scale * LOG2_E

    def softmax_parts(s_list):
        t_list = [s * log2_scale for s in s_list]
        m = functools.reduce(jnp.maximum, [jnp.max(t, axis=-1, keepdims=True) for t in t_list])
        p_list = [jnp.exp2(t - m) for t in t_list]
        l = functools.reduce(jnp.add, [jnp.sum(p, axis=-1, keepdims=True) for p in p_list])
        return p_list, 1.0 / l

    def attend(s_list, v_list):
        p_list, inv_l = softmax_parts(s_list)
        return functools.reduce(jnp.add, [_dot(p.astype(BF16), v) for p, v in zip(p_list, v_list)]) * inv_l

    for hh in range(hb):
        if kind == "mla":
            q2 = q_ref[:, hh * 256:(hh + 1) * 256].astype(BF16)
            s_list = []
            for kr, pr in zip(k_refs, kpe_refs):
                kcat = jnp.concatenate([kr[:, hh * 256:hh * 256 + 128].astype(BF16), pr[...].astype(BF16)], axis=1)
                s_list.append(_dot_nt(q2, kcat))
            v2 = [vr[:, hh * 256 + 128:hh * 256 + 256].astype(BF16) for vr in v_refs]
            o_ref[:, hh * LANE:(hh + 1) * LANE] = attend(s_list, v2)
            continue

        sl = slice(hh * LANE, (hh + 1) * LANE)
        q2 = q_ref[:, sl]
        k2 = [kr[:, sl].astype(BF16) for kr in k_refs]
        v2 = [vr[:, sl].astype(BF16) for vr in v_refs]
        tq = q2.shape[0]
        q_heads = [jnp.where(lo_half, q2, 0.0), jnp.where(lo_half, 0.0, q2)]
        if stack:
            qs = jnp.concatenate(q_heads, axis=0).astype(BF16)
            o2 = attend([_dot_nt(qs, kk) for kk in k2], v2)
            o_heads = [o2[:tq], o2[tq:]]
            if kind == "diff":
                o_diff = o_heads[0] - lam_ref[...][:, :1] * o_heads[1]
        elif kind == "na":
            o_heads = [attend([_dot_nt(qh.astype(BF16), kk) for kk in k2], v2) for qh in q_heads]
        else:
            (p0, il0), (p1, il1) = [softmax_parts([_dot_nt(qh.astype(BF16), kk) for kk in k2]) for qh in q_heads]
            lam = lam_ref[...][:, :1]
            o_diff = functools.reduce(jnp.add, [
                _dot((a * il0 - lam * (b * il1)).astype(BF16), vv) for a, b, vv in zip(p0, p1, v2)])
        if kind == "na":
            o_ref[:, sl] = jnp.where(lo_half, o_heads[0], o_heads[1])
        else:
            o_ref[:, sl] = _rms(o_diff, subln_ref[...], DIFF_SUBLN_EPS) * (1.0 - lam_init)


def _attention(kind, *, grid, q, ks, vs, kpes=None, extra=(), rows, out_spec, hb, scale, lam_init=0.0,
               stack=False, name="attn"):
    ops = [q] + list(ks) + (list(kpes) if kind == "mla" else []) + (list(vs) if kind != "mla" else [])
    ops += list(extra)
    return pl.pallas_call(
        functools.partial(_attn_kernel, kind=kind, hb=hb, nseg=len(ks), scale=scale, lam_init=lam_init,
                          stack=stack),
        grid=grid,
        in_specs=[s for _, s in ops],
        out_specs=out_spec,
        out_shape=jax.ShapeDtypeStruct((rows, D), F32),
        compiler_params=_cparams(("parallel",) * len(grid)),
        name=name,
    )(*[a for a, _ in ops])


NA_QROWS = 4
NA_KROWS = 12
NA_TQ = NA_QROWS * GRID_W
NA_TK = NA_KROWS * GRID_W


def _na_bias_table(rpb):
    h = rpb.shape[0]
    qc = np.arange(GRID_W)[:, None]
    kc = np.arange(GRID_W)[None, :]
    qs = np.clip(qc - NA_WIN_C // 2, 0, GRID_W - NA_WIN_C)
    col_ok = (kc >= qs) & (kc < qs + NA_WIN_C)
    ic = np.clip(kc - qc + NA_WIN_C - 1, 0, 2 * NA_WIN_C - 2)
    n_ir = 2 * NA_WIN_R - 1
    onehot = (ic.reshape(1, -1) == np.arange(2 * NA_WIN_C - 1)[:, None]).astype(np.float32)
    tab = jnp.einsum("hrm,mx->hrx", rpb.astype(F32) * LOG2_E, jnp.asarray(onehot),
                     precision=lax.Precision.HIGHEST)
    tab = jnp.where(jnp.asarray(col_ok.reshape(1, 1, -1)), tab, NEG_INF).reshape(h, n_ir, GRID_W, GRID_W)
    tab = jnp.concatenate([tab, tab], axis=-1)
    n_rows = T_LAT // GRID_W
    n_g = n_rows // NA_QROWS
    ir_idx = np.zeros((3, NA_QROWS, NA_KROWS), np.int32)
    row_ok = np.zeros((3, NA_QROWS, NA_KROWS), bool)
    for pat, g in enumerate((0, 1, n_g - 1)):
        base = int(np.clip(NA_QROWS * g - NA_QROWS, 0, n_rows - NA_KROWS))
        for qr in range(NA_QROWS):
            r = NA_QROWS * g + qr
            r0 = int(np.clip(r - NA_WIN_R // 2, 0, n_rows - NA_WIN_R))
            for kr in range(NA_KROWS):
                ok = r0 <= base + kr < r0 + NA_WIN_R
                row_ok[pat, qr, kr] = ok
                ir_idx[pat, qr, kr] = (base + kr - r + NA_WIN_R - 1) if ok else 0

    def build(tab_ref, o_ref):
        pat_id = pl.program_id(0)
        lo_half = lax.broadcasted_iota(jnp.int32, (1, LANE), 1) < GRID_W
        masked = jnp.full((GRID_W, LANE), NEG_INF, F32)
        for pat in range(3):
            @pl.when(pat_id == pat)
            def _(pat=pat):
                for hd in range(heads_per_step):
                    for qr in range(NA_QROWS):
                        for kp in range(NA_KROWS // 2):
                            halves = [tab_ref[hd, int(ir_idx[pat, qr, kr])] if row_ok[pat, qr, kr] else masked
                                      for kr in (2 * kp, 2 * kp + 1)]
                            o_ref[hd, qr * GRID_W:(qr + 1) * GRID_W, kp * LANE:(kp + 1) * LANE] = jnp.where(
                                lo_half, halves[0], halves[1])

    heads_per_step = 4
    return pl.pallas_call(
        build,
        grid=(3, h // heads_per_step),
        in_specs=[pl.BlockSpec((heads_per_step, n_ir, GRID_W, LANE), lambda p, hh: (hh, 0, 0, 0))],
        out_specs=pl.BlockSpec((None, heads_per_step, NA_TQ, NA_TK), lambda p, hh: (p, hh, 0, 0)),
        out_shape=jax.ShapeDtypeStruct((3, h, NA_TQ, NA_TK), F32),
        compiler_params=_cparams(("parallel", "parallel")),
        name="na_bias",
    )(tab)


def _na_lat_kernel(q_ref, kc_ref, vc_ref, k_ref, v_ref, b_ref, o_ref, *, scale):
    g = pl.program_id(2)
    n_rows = T_LAT // GRID_W
    base = jnp.clip(NA_QROWS * g - NA_QROWS, 0, n_rows - NA_KROWS) * GRID_W
    base = pl.multiple_of(base, GRID_W)
    lane = lax.broadcasted_iota(jnp.int32, (1, LANE), 1)
    lo_half = lane < 64
    log2_scale = scale * LOG2_E
    for lb in range(LATENT_HB):
        cols = slice(lb * LANE, (lb + 1) * LANE)
        q2 = q_ref[:, cols]
        kc = kc_ref[:, cols].astype(BF16)
        vc = vc_ref[:, cols].astype(BF16)
        kl = k_ref[pl.ds(base, NA_TK), cols].astype(BF16)
        vl = v_ref[pl.ds(base, NA_TK), cols].astype(BF16)
        outs = []
        for n in range(2):
            qm = jnp.where(lo_half if n == 0 else jnp.logical_not(lo_half), q2, 0.0).astype(BF16)
            s_c = _dot_nt(qm, kc) * log2_scale
            s_l = _dot_nt(qm, kl) * log2_scale + b_ref[2 * lb + n]
            m = jnp.maximum(jnp.max(s_c, axis=-1, keepdims=True), jnp.max(s_l, axis=-1, keepdims=True))
            p_c = jnp.exp2(s_c - m)
            p_l = jnp.exp2(s_l - m)
            l = jnp.sum(p_c, axis=-1, keepdims=True) + jnp.sum(p_l, axis=-1, keepdims=True)
            o = _dot(p_c.astype(BF16), vc) + _dot(p_l.astype(BF16), vl)
            outs.append(o * (1.0 / l))
        o_ref[:, cols] = jnp.where(lo_half, outs[0], outs[1])


def _na_lat(qkv, cache_k, cache_v, bias):
    n_g = T_LAT // NA_TQ
    q_blk0 = M_CTX // NA_TQ
    kv_blk0 = M_CTX // T_LAT
    wb = LATENT_HB * LANE
    pat = lambda g: jnp.where(g == 0, 0, jnp.where(g == n_g - 1, 2, 1))
    return pl.pallas_call(
        functools.partial(_na_lat_kernel, scale=0.125),
        grid=(N_LAT_SEQ, D // wb, n_g),
        in_specs=[
            pl.BlockSpec((None, NA_TQ, wb), lambda b, h, g: (0, q_blk0 + b * n_g + g, h)),
            pl.BlockSpec((None, PAST, wb), lambda b, h, g: (b, 0, h)),
            pl.BlockSpec((None, PAST, wb), lambda b, h, g: (b, 0, h)),
            pl.BlockSpec((None, T_LAT, wb), lambda b, h, g: (1, kv_blk0 + b, h)),
            pl.BlockSpec((None, T_LAT, wb), lambda b, h, g: (2, kv_blk0 + b, h)),
            pl.BlockSpec((None, 2 * LATENT_HB, NA_TQ, NA_TK), lambda b, h, g: (pat(g), h, 0, 0)),
        ],
        out_specs=pl.BlockSpec((NA_TQ, wb), lambda b, h, g: (b * n_g + g, h)),
        out_shape=jax.ShapeDtypeStruct((M_LAT, D), F32),
        compiler_params=_cparams(("parallel", "parallel", "parallel")),
        name="na_latent",
    )(qkv, cache_k, cache_v, qkv, qkv, bias)


SEQ_T_PER_STEP = 4


def _seq_t_kernel(x_ref, o_ref):
    for s in range(SEQ_T_PER_STEP):
        o_ref[s] = x_ref[s * T_CTX:(s + 1) * T_CTX, :].T


def _seq_transpose(qkv, plane):
    return pl.pallas_call(
        _seq_t_kernel,
        grid=(N_CTX_SEQ // SEQ_T_PER_STEP,),
        in_specs=[pl.BlockSpec((None, SEQ_T_PER_STEP * T_CTX, D), lambda b: (plane, b, 0))],
        out_specs=pl.BlockSpec((SEQ_T_PER_STEP, D, T_CTX), lambda b: (b, 0, 0)),
        out_shape=jax.ShapeDtypeStruct((N_CTX_SEQ, D, T_CTX), F32),
        compiler_params=_cparams(("parallel",)),
        name="seq_transpose",
    )(qkv)


def kernel(x_prompt, x_sample, c, c_ctx, cache_na_k, cache_na_v, cache_diff_k, cache_diff_v, cache_mla_ckv, cache_mla_kpe, ada_w, ada_b, norm1, norm2, pool_w, pool_scale, na_qkv, na_rpb, na_out, diff_qkv, diff_lambda, diff_subln, diff_out, mla_qa, mla_qnorm, mla_qb, mla_kva, mla_kvnorm, mla_kvb, mla_out, ffn_gate, ffn_up, ffn_down, moe_router, moe_gate, moe_up, moe_down, final_norm):
    depth = ada_w.shape[0]
    x = None
    cond8 = jnp.concatenate([c_ctx[None, :], c, jnp.zeros((5, D), F32)], axis=0)
    mods_all = _ada(cond8, ada_w, ada_b)[:, :3].reshape(depth, 3, 6, 1, D)
    outs = {}

    for l in range(depth):
        kind, li, f = l % 4, l // 4, l // 2
        mods = mods_all[l]
        route = (norm2[l], 3, 4, moe_router[f]) if l % 2 == 1 else None
        if kind == 0:
            x_pair = (x_prompt.reshape(M_CTX, D), x_sample.reshape(M_LAT, D)) if x is None else (x[:M_CTX], x[M_CTX:])
            h = _modrows(x_pair, norm1[l], mods, 0, 1, rows=M_TOK, row0=0, width=D)
            x = _pool(h, *x_pair, pool_w[li], pool_scale[li], mods, 2)
        elif kind == 1:
            qkv = _linear(x, na_qkv[li], tn=D, norm_g=norm1[l], mods=mods, sh_c=0, sc_c=1, planes=True,
                          name="na_qkv")
            to_cache = lambda a: jnp.transpose(a.reshape(N_CTX_SEQ, 1, 16, 64, T_CTX), (0, 1, 4, 2, 3))
            outs["na_k"] = to_cache(_seq_transpose(qkv, 1))
            outs["na_v"] = to_cache(_seq_transpose(qkv, 2))
            seq_spec = lambda p: pl.BlockSpec((None, T_CTX, D), lambda b, p=p: (p, b, 0))
            o_ctx = _attention(
                "na", grid=(N_CTX_SEQ,), q=(qkv, seq_spec(0)), ks=[(qkv, seq_spec(1))], vs=[(qkv, seq_spec(2))],
                rows=M_CTX, out_spec=pl.BlockSpec((T_CTX, D), lambda b: (b, 0)), hb=D // LANE, scale=0.125,
                stack=True, name="na_ctx")
            bias = _na_bias_table(na_rpb[li])
            o_lat = _na_lat(qkv, cache_na_k[:, li].reshape(N_LAT_SEQ, PAST, D),
                            cache_na_v[:, li].reshape(N_LAT_SEQ, PAST, D), bias)
            x = _linear((o_ctx, o_lat), na_out[li], tn=D, tm=TM, resid=x, mods=mods, gate_c=2, route=route,
                        name="na_out")
        elif kind == 2:
            lam_init = 0.8 - 0.6 * math.exp(-0.3 * l)
            lp = diff_lambda[li].astype(F32)
            lam = jnp.exp(jnp.sum(lp[0] * lp[1])) - jnp.exp(jnp.sum(lp[2] * lp[3])) + lam_init
            lam_row = jnp.full((1, LANE), lam, F32)
            subln = diff_subln[li].reshape(1, LANE)
            extra = [(lam_row, pl.BlockSpec((1, LANE), lambda *_: (0, 0))),
                     (subln, pl.BlockSpec((1, LANE), lambda *_: (0, 0)))]
            qkv = _linear(x, diff_qkv[li], tn=D, norm_g=norm1[l], mods=mods, sh_c=0, sc_c=1, planes=True,
                          name="diff_qkv")
            outs["diff_k"] = jnp.transpose(
                _seq_transpose(qkv, 1).reshape(N_CTX_SEQ, 1, 8, 2, 64, T_CTX), (0, 1, 5, 2, 3, 4))
            outs["diff_v"] = qkv[2, :M_CTX].reshape(N_CTX_SEQ, 1, T_CTX, 8, 128)
            seq_spec = lambda p: pl.BlockSpec((None, T_CTX, D), lambda b, p=p: (p, b, 0))
            o_ctx = _attention(
                "diff", grid=(N_CTX_SEQ,), q=(qkv, seq_spec(0)), ks=[(qkv, seq_spec(1))], vs=[(qkv, seq_spec(2))],
                extra=extra, rows=M_CTX, out_spec=pl.BlockSpec((T_CTX, D), lambda b: (b, 0)), hb=D // LANE,
                scale=0.125, lam_init=lam_init, stack=True, name="diff_ctx")
            rc, rs = _rope_tables(LANE, 0, LANE)
            qk_r = _rope(qkv, rc, rs, planes=2, row0=M_CTX, width=LANE)
            tq = LATENT_TQ
            nq = T_LAT // tq
            ck = cache_diff_k[:, li].reshape(N_LAT_SEQ, PAST, D)
            cv = cache_diff_v[:, li].reshape(N_LAT_SEQ, PAST, D)
            hb, wb = LATENT_HB, LATENT_HB * LANE
            o_lat = _attention(
                "diff", grid=(N_LAT_SEQ, D // wb, nq),
                q=(qk_r, pl.BlockSpec((None, tq, wb), lambda b, h, i: (0, b * nq + i, h))),
                ks=[(ck, pl.BlockSpec((None, PAST, wb), lambda b, h, i: (b, 0, h))),
                    (qk_r, pl.BlockSpec((None, T_LAT, wb), lambda b, h, i: (1, b, h)))],
                vs=[(cv, pl.BlockSpec((None, PAST, wb), lambda b, h, i: (b, 0, h))),
                    (qkv, pl.BlockSpec((None, T_LAT, wb), lambda b, h, i: (2, M_CTX // T_LAT + b, h)))],
                extra=extra,
                rows=M_LAT, out_spec=pl.BlockSpec((tq, wb), lambda b, h, i: (b * nq + i, h)),
                hb=hb, scale=0.125, lam_init=lam_init, name="diff_latent")
            x = _linear((o_ctx, o_lat), diff_out[li], tn=D, tm=TM, resid=x, mods=mods, gate_c=2, route=route,
                        name="diff_out")
        else:
            hd = MLA_NOPE + MLA_ROPE
            wqb = mla_qb[li].reshape(MLA_Q_LORA, MLA_HEADS, hd)
            wqb = jnp.pad(wqb, ((0, 0), (0, 0), (0, 256 - hd))).reshape(MLA_Q_LORA, MLA_HEADS * 256)
            w_a = jnp.concatenate(
                [mla_qa[li], jnp.pad(mla_kva[li], ((0, 0), (0, 512 - MLA_KV_LORA - MLA_ROPE)))], axis=1)
            a = _linear(x, w_a, tn=D, tm=TM, norm_g=norm1[l], mods=mods, sh_c=0, sc_c=1, name="mla_a")
            q = _linear(a, wqb, tn=D, tm=TM, k=MLA_Q_LORA, x_col=0, norm_g=mla_qnorm[li], name="mla_q")
            zero_mod = jnp.zeros((1, 2, 1, MLA_KV_LORA), F32)
            ckv = _modrows(a, mla_kvnorm[li], zero_mod, 0, 1, rows=M_TOK, row0=0, width=MLA_KV_LORA, col_blk=2,
                           grouped=False)
            outs["mla_ckv"] = ckv[:M_CTX].reshape(N_CTX_SEQ, 1, T_CTX, MLA_KV_LORA)
            outs["mla_kpe"] = a[:M_CTX, 768:768 + MLA_ROPE].reshape(N_CTX_SEQ, 1, T_CTX, MLA_ROPE)
            ckv_all = jnp.concatenate([ckv, cache_mla_ckv[:, li].reshape(N_LAT_SEQ * PAST, MLA_KV_LORA)], axis=0)
            kvb = _linear(ckv_all, mla_kvb[li], tn=D, tm=TM, out_dtype=BF16, name="mla_kvb")
            scale = float(hd) ** -0.5
            o_ctx = _attention(
                "mla", grid=(N_CTX_SEQ,),
                q=(q, pl.BlockSpec((T_CTX, 2048), lambda b: (b, 0))),
                ks=[(kvb, pl.BlockSpec((T_CTX, 2048), lambda b: (b, 0)))],
                kpes=[(a, pl.BlockSpec((T_CTX, LANE), lambda b: (b, 6)))],
                vs=None, rows=M_CTX, out_spec=pl.BlockSpec((T_CTX, D), lambda b: (b, 0)), hb=MLA_HEADS,
                scale=scale, name="mla_ctx")
            rc, rs = _rope_tables(256, 128, 192)
            q_r = _rope(q, rc, rs, planes=1, row0=M_CTX, width=256, ncol=MLA_HEADS)[0]
            rc1, rs1 = _rope_tables(LANE, 0, 64)
            kpe_r = _rope(a, rc1, rs1, planes=1, row0=M_CTX, width=LANE, col0=6)[0]
            kpe_c = jnp.pad(cache_mla_kpe[:, li], ((0, 0), (0, 0), (0, LANE - MLA_ROPE)))
            tq = LATENT_TQ
            nq = T_LAT // tq
            hb = LATENT_HB
            o_lat = _attention(
                "mla", grid=(N_LAT_SEQ, MLA_HEADS // hb, nq),
                q=(q_r, pl.BlockSpec((tq, hb * 256), lambda b, h, i: (b * nq + i, h))),
                ks=[(kvb, pl.BlockSpec((PAST, hb * 256), lambda b, h, i: (M_TOK // PAST + b, h))),
                    (kvb, pl.BlockSpec((T_LAT, hb * 256), lambda b, h, i: (M_CTX // T_LAT + b, h)))],
                kpes=[(kpe_c, pl.BlockSpec((None, PAST, LANE), lambda b, h, i: (b, 0, 0))),
                      (kpe_r, pl.BlockSpec((T_LAT, LANE), lambda b, h, i: (b, 0)))],
                vs=None, rows=M_LAT,
                out_spec=pl.BlockSpec((tq, hb * LANE), lambda b, h, i: (b * nq + i, h)),
                hb=hb, scale=scale, name="mla_latent")
            x = _linear((o_ctx, o_lat), mla_out[li], tn=D, tm=TM, resid=x, mods=mods, gate_c=2, route=route,
                        name="mla_out")

        if l % 2 == 0:
            x = _ffn_dense(x, ffn_gate, ffn_up, ffn_down, f, norm2[l], mods, 3, 4, 5)
        else:
            routed = None
            if isinstance(x, (tuple, list)):
                x, *routed = x
            x = _moe(x, norm2[l], mods, 3, 4, 5, moe_router[f], moe_gate, moe_up, moe_down, f, routed=routed)

    zero_mod = jnp.zeros((1, 2, 1, D), F32)
    y_prompt = _modrows(x, final_norm, zero_mod, 0, 1, rows=M_CTX, row0=0, width=D, grouped=False)
    y_sample = _modrows(x, final_norm, zero_mod, 0, 1, rows=M_LAT, row0=M_CTX, width=D, grouped=False)
    return (y_prompt.reshape(N_CTX_SEQ, T_CTX, D), y_sample.reshape(N_LAT_SEQ, T_LAT, D),
            outs["na_k"], outs["na_v"], outs["diff_k"], outs["diff_v"], outs["mla_ckv"], outs["mla_kpe"])
```

```python
import functools
import math

import numpy as np
import jax
import jax.numpy as jnp
from jax import lax
from jax.experimental import pallas as pl
from jax.experimental.pallas import tpu as pltpu

F32 = jnp.float32
BF16 = jnp.bfloat16

D = 1024
N_CTX_SEQ = 32
T_CTX = 256
N_LAT_SEQ = 2
T_LAT = 2048
PAST = 512
M_CTX = N_CTX_SEQ * T_CTX
M_LAT = N_LAT_SEQ * T_LAT
M_TOK = M_CTX + M_LAT
GRID_W = 64
NORM_EPS = 1e-6
NEG_INF = -1e30
ROPE_BASE = 10000.0
POOL_WINDOWS = (2, 4, 8, 16)
NA_WIN_R = 8
NA_WIN_C = 16
DIFF_SUBLN_EPS = 1e-5
MLA_HEADS = 8
MLA_NOPE = 128
MLA_ROPE = 64
MLA_KV_LORA = 256
MLA_Q_LORA = 512
FFN_HIDDEN = 2816
N_EXPERTS = 8

LOG2_E = math.log2(math.e)
LANE = 128
TM = 1024
TH = 256
SUB = 256
VMEM_LIMIT = 56 * 1024 * 1024

TMF = 2048
MOE_ROWS = 2 * M_TOK + N_EXPERTS * TMF
MOE_TILES = MOE_ROWS // TMF


def _cparams(sem):
    return pltpu.CompilerParams(dimension_semantics=sem, vmem_limit_bytes=VMEM_LIMIT)


def _group_of_tile(i, tm):
    n_ctx = M_CTX // tm
    return jnp.where(i < n_ctx, 0, 1 + (i - n_ctx) // (T_LAT // tm))


def _dot(a, b):
    return jnp.dot(a, b, preferred_element_type=F32)


def _dot_nt(a, b):
    return lax.dot_general(a, b, (((1,), (1,)), ((), ())), preferred_element_type=F32)


def _rms(x, g, eps):
    return x * lax.rsqrt(jnp.mean(x * x, axis=-1, keepdims=True) + eps) * g


def _ada_kernel(c_ref, w_ref, b_ref, o_ref):
    c = c_ref[...]
    s = (c * jax.nn.sigmoid(c)).astype(BF16)
    o_ref[...] = _dot(s, w_ref[...].astype(BF16)) + b_ref[...]


def _ada(cond8, ada_w, ada_b):
    depth = ada_w.shape[0]
    tn = 1536
    return pl.pallas_call(
        _ada_kernel,
        grid=(depth, 6 * D // tn),
        in_specs=[
            pl.BlockSpec((8, D), lambda l, j: (0, 0)),
            pl.BlockSpec((None, D, tn), lambda l, j: (l, 0, j)),
            pl.BlockSpec((None, 1, tn), lambda l, j: (l, 0, j)),
        ],
        out_specs=pl.BlockSpec((None, 8, tn), lambda l, j: (l, 0, j)),
        out_shape=jax.ShapeDtypeStruct((depth, 8, 6 * D), F32),
        compiler_params=_cparams(("parallel", "parallel")),
        name="ada",
    )(cond8, ada_w, ada_b.reshape(depth, 1, 6 * D))


def _token_tile(xc_ref, xl_ref, tile, tm):
    return jnp.where(tile < M_CTX // tm, xc_ref[...], xl_ref[...])


def _split_specs(block, tm, col):
    n_ctx = M_CTX // tm
    return [pl.BlockSpec(block, lambda i, *_: (jnp.minimum(i, n_ctx - 1), col(*_))),
            pl.BlockSpec(block, lambda i, *_: (jnp.maximum(i - n_ctx, 0), col(*_)))]


def _modrows_kernel(*refs, eps, split, tm):
    if split:
        xc_ref, xl_ref, g_ref, sh_ref, sc_ref, o_ref = refs
        x = _token_tile(xc_ref, xl_ref, pl.program_id(0), tm)
    else:
        x_ref, g_ref, sh_ref, sc_ref, o_ref = refs
        x = x_ref[...]
    o_ref[...] = _rms(x, g_ref[...], eps) * (1.0 + sc_ref[...]) + sh_ref[...]


def _modrows(x, g, mods, sh_c, sc_c, *, rows, row0, width, col_blk=0, tm=TM, eps=NORM_EPS,
             grouped=True):
    r0 = row0 // tm
    if grouped:
        grp = lambda i: _group_of_tile(i + r0, tm)
    else:
        grp = lambda i: 0
    split = isinstance(x, tuple)
    if split:
        xs, x_specs = list(x), _split_specs((tm, width), tm, lambda: 0)
    else:
        xs, x_specs = [x], [pl.BlockSpec((tm, width), lambda i: (i + r0, col_blk))]
    return pl.pallas_call(
        functools.partial(_modrows_kernel, eps=eps, split=split, tm=tm),
        grid=(rows // tm,),
        in_specs=x_specs + [
            pl.BlockSpec((1, width), lambda i: (0, 0)),
            pl.BlockSpec((None, None, 1, width), lambda i: (grp(i), sh_c, 0, 0)),
            pl.BlockSpec((None, None, 1, width), lambda i: (grp(i), sc_c, 0, 0)),
        ],
        out_specs=pl.BlockSpec((tm, width), lambda i: (i, 0)),
        out_shape=jax.ShapeDtypeStruct((rows, width), F32),
        compiler_params=_cparams(("parallel",)),
        name="modrows",
    )(*xs, g.reshape(1, width), mods, mods)


def _linear_kernel(*refs, prologue, eps, epilogue, split, tm, tn, planes):
    it = iter(refs)
    x_ref = next(it)
    x_lat_ref = next(it) if split else None
    w_ref = next(it)
    g_ref = next(it) if prologue != "none" else None
    sh_ref = next(it) if prologue == "mod" else None
    sc_ref = next(it) if prologue == "mod" else None
    res_ref = next(it) if epilogue == "resid" else None
    gate_ref = next(it) if epilogue == "resid" else None
    o_ref = next(it)
    wb_ref = next(it)
    n = w_ref.shape[1]

    @pl.when(pl.program_id(0) == 0)
    def _():
        wb_ref[...] = w_ref[...].astype(BF16)

    def run(src_ref):
        x = src_ref[...]
        if prologue != "none":
            x = _rms(x, g_ref[...], eps)
        if prologue == "mod":
            x = x * (1.0 + sc_ref[...]) + sh_ref[...]
        xb = x.astype(BF16)
        for p in range(n // tn):
            cols = slice(p * tn, (p + 1) * tn)
            acc = _dot(xb, wb_ref[:, cols])
            if epilogue == "resid":
                acc = res_ref[:, cols] + gate_ref[:, cols] * acc
            if planes:
                o_ref[p] = acc.astype(o_ref.dtype)
            else:
                o_ref[:, cols] = acc.astype(o_ref.dtype)

    if split:
        is_ctx = pl.program_id(0) < M_CTX // tm
        pl.when(is_ctx)(lambda: run(x_ref))
        pl.when(jnp.logical_not(is_ctx))(lambda: run(x_lat_ref))
    else:
        run(x_ref)


TML = 512
LATENT_TQ = 256
LATENT_HB = 4


def _linear(x, w, *, tn, k=None, x_col=0, norm_g=None, eps=NORM_EPS, mods=None, sh_c=None, sc_c=None,
            resid=None, gate_c=None, planes=False, tm=TML, out_dtype=F32, name="linear"):
    split = isinstance(x, tuple)
    n = w.shape[1]
    prologue = "none" if norm_g is None else ("mod" if sh_c is not None else "norm")
    epilogue = "none" if resid is None else "resid"
    grp = lambda i: _group_of_tile(i, tm)
    if split:
        x_ctx, x_lat = x
        m = x_ctx.shape[0] + x_lat.shape[0]
        k = x_ctx.shape[1]
        n_ctx = x_ctx.shape[0] // tm
        args = [x_ctx, x_lat]
        in_specs = [
            pl.BlockSpec((tm, k), lambda i: (jnp.minimum(i, n_ctx - 1), 0)),
            pl.BlockSpec((tm, k), lambda i: (jnp.maximum(i - n_ctx, 0), 0)),
        ]
    else:
        m = x.shape[0]
        k = x.shape[1] if k is None else k
        args = [x]
        in_specs = [pl.BlockSpec((tm, k), lambda i: (i, x_col))]
    args.append(w)
    in_specs.append(pl.BlockSpec((k, n), lambda i: (0, 0)))
    if prologue != "none":
        args.append(norm_g.reshape(1, k))
        in_specs.append(pl.BlockSpec((1, k), lambda i: (0, 0)))
    if prologue == "mod":
        args += [mods, mods]
        in_specs += [
            pl.BlockSpec((None, None, 1, k), lambda i: (grp(i), sh_c, 0, 0)),
            pl.BlockSpec((None, None, 1, k), lambda i: (grp(i), sc_c, 0, 0)),
        ]
    if epilogue == "resid":
        args += [resid, mods]
        in_specs += [
            pl.BlockSpec((tm, n), lambda i: (i, 0)),
            pl.BlockSpec((None, None, 1, n), lambda i: (grp(i), gate_c, 0, 0)),
        ]
    if planes:
        out_spec = pl.BlockSpec((n // tn, tm, tn), lambda i: (0, i, 0))
        out_shape = jax.ShapeDtypeStruct((n // tn, m, tn), out_dtype)
    else:
        out_spec = pl.BlockSpec((tm, n), lambda i: (i, 0))
        out_shape = jax.ShapeDtypeStruct((m, n), out_dtype)
    return pl.pallas_call(
        functools.partial(_linear_kernel, prologue=prologue, eps=eps, epilogue=epilogue, split=split, tm=tm,
                          tn=tn, planes=planes),
        grid=(m // tm,),
        in_specs=in_specs,
        out_specs=out_spec,
        out_shape=out_shape,
        scratch_shapes=[pltpu.VMEM((k, n), BF16)],
        compiler_params=_cparams(("arbitrary",)),
        name=name,
    )(*args)


POOL_ROWS = 2048


def _pool_kernel(h_ref, xc_ref, xl_ref, w_ref, ps_ref, gate_ref, o_ref):
    i = pl.program_id(0)
    g = pl.program_id(1)
    seq = jnp.where(i < M_CTX // POOL_ROWS, T_CTX, T_LAT)
    t = lax.broadcasted_iota(jnp.int32, (POOL_ROWS, 1), 0) & (seq - 1)

    def ahead(a, k):
        return jnp.where(t + k < seq, pltpu.roll(a, POOL_ROWS - k, 0), 0.0)

    def behind(a, k):
        return jnp.where(t - k >= 0, pltpu.roll(a, k, 0), 0.0)

    for gi, win in enumerate(POOL_WINDOWS):
        @pl.when(g == gi)
        def _(win=win):
            h = h_ref[...]
            fwd, bwd, k = h, h, 1
            while k < win // 2:
                fwd, bwd, k = fwd + ahead(fwd, k), bwd + behind(bwd, k), 2 * k
            acc = fwd + behind(bwd, 1)
            cnt = jnp.minimum(t + (win - win // 2), seq) - jnp.maximum(t - win // 2, 0)
            pooled = acc / cnt.astype(F32)
            y = _dot((pooled - h).astype(BF16), w_ref[...].astype(BF16)) * ps_ref[...]
            o_ref[...] = _token_tile(xc_ref, xl_ref, i, POOL_ROWS) + gate_ref[...] * y


def _pool(h, x_ctx, x_lat, pool_w, pool_scale, mods, gate_c):
    gw = D // len(POOL_WINDOWS)
    grp = lambda i: _group_of_tile(i, POOL_ROWS)
    return pl.pallas_call(
        _pool_kernel,
        grid=(M_TOK // POOL_ROWS, len(POOL_WINDOWS)),
        in_specs=[
            pl.BlockSpec((POOL_ROWS, gw), lambda i, g: (i, g)),
            *_split_specs((POOL_ROWS, gw), POOL_ROWS, lambda g: g),
            pl.BlockSpec((None, gw, gw), lambda i, g: (g, 0, 0)),
            pl.BlockSpec((1, gw), lambda i, g: (0, g)),
            pl.BlockSpec((None, None, 1, gw), lambda i, g: (grp(i), gate_c, 0, g)),
        ],
        out_specs=pl.BlockSpec((POOL_ROWS, gw), lambda i, g: (i, g)),
        out_shape=jax.ShapeDtypeStruct((M_TOK, D), F32),
        compiler_params=_cparams(("parallel", "parallel")),
        name="pool",
    )(h, x_ctx, x_lat, pool_w, pool_scale.reshape(1, D), mods)


NJ = FFN_HIDDEN // TH
TN2 = 256
NN = D // TN2
STRAIGHT_COUNTS = (TMF // SUB, 4, 5)


def _swiglu_kernel(*refs, moe):
    if moe:
        te_ref, tv_ref, x_ref, wg_ref, wu_ref, wd_ref, o_ref, xb_ref, h_ref = refs
    else:
        x_ref, wg_ref, wu_ref, wd_ref, g_ref, sh_ref, sc_ref, res_ref, gate_ref, o_ref, xb_ref, h_ref = refs
    j = pl.program_id(1)
    n_valid = tv_ref[pl.program_id(0)] if moe else TMF

    def per_sub_block(fn, fn_skipped=None):
        sub_blocks = [pl.ds(s * SUB, SUB) for s in range(TMF // SUB)]

        def leading(count):
            def run():
                for rows in sub_blocks[:count]:
                    fn(rows)
                if fn_skipped is not None:
                    for rows in sub_blocks[count:]:
                        fn_skipped(rows)
            return run

        def valid_rows_only():
            for s, rows in enumerate(sub_blocks):
                pl.when(s * SUB < n_valid)(functools.partial(fn, rows))
                if fn_skipped is not None:
                    pl.when(s * SUB >= n_valid)(functools.partial(fn_skipped, rows))

        if moe:
            n_blocks = (n_valid + SUB - 1) // SUB
            for count in STRAIGHT_COUNTS:
                pl.when(n_blocks == count)(leading(count))
            pl.when(functools.reduce(jnp.logical_and, [n_blocks != c for c in STRAIGHT_COUNTS]))(valid_rows_only)
        else:
            leading(len(sub_blocks))()

    @pl.when(j == 0)
    def _():
        def cast(rows):
            x = x_ref[rows, :]
            if not moe:
                x = _rms(x, g_ref[...], NORM_EPS) * (1.0 + sc_ref[...]) + sh_ref[...]
            xb_ref[rows, :] = x.astype(BF16)

        per_sub_block(cast)

    @pl.when(j < NJ)
    def _():
        wg = wg_ref[...].astype(BF16)
        wu = wu_ref[...].astype(BF16)

        def up(rows):
            xb = xb_ref[rows, :]
            a = _dot(xb, wg)
            u = _dot(xb, wu)
            h_ref[j, rows, :] = (a * jax.nn.sigmoid(a) * u).astype(BF16)

        per_sub_block(up)

    @pl.when(j >= NJ)
    def _():
        wd = wd_ref[...].astype(BF16)

        def down(rows):
            hcat = jnp.concatenate([h_ref[jj, rows, :] for jj in range(NJ)], axis=1)
            y = _dot(hcat, wd)
            if moe:
                o_ref[rows, :] = y
            else:
                o_ref[rows, :] = res_ref[rows, :] + gate_ref[...] * y

        def skipped(rows):
            o_ref[rows, :] = jnp.zeros((SUB, TN2), F32)

        per_sub_block(down, skipped)


_SWIGLU_SCRATCH = [pltpu.VMEM((TMF, D), BF16), pltpu.VMEM((NJ, TMF, TH), BF16)]


def _ffn_dense(x, wg, wu, wd, f, norm_g, mods, sh_c, sc_c, gate_c):
    grp = lambda i: _group_of_tile(i, TMF)
    jh = lambda j: jnp.minimum(j, NJ - 1)
    jn = lambda j: jnp.maximum(j - NJ, 0)
    mod_spec = lambda c: pl.BlockSpec((None, None, 1, D), lambda i, j: (grp(i), c, 0, 0))
    return pl.pallas_call(
        functools.partial(_swiglu_kernel, moe=False),
        grid=(M_TOK // TMF, NJ + NN),
        in_specs=[
            pl.BlockSpec((TMF, D), lambda i, j: (i, 0)),
            pl.BlockSpec((None, D, TH), lambda i, j: (f, 0, jh(j))),
            pl.BlockSpec((None, D, TH), lambda i, j: (f, 0, jh(j))),
            pl.BlockSpec((None, FFN_HIDDEN, TN2), lambda i, j: (f, 0, jn(j))),
            pl.BlockSpec((1, D), lambda i, j: (0, 0)),
            mod_spec(sh_c), mod_spec(sc_c),
            pl.BlockSpec((TMF, TN2), lambda i, j: (i, jn(j))),
            pl.BlockSpec((None, None, 1, TN2), lambda i, j: (grp(i), gate_c, 0, jn(j))),
        ],
        out_specs=pl.BlockSpec((TMF, TN2), lambda i, j: (i, jn(j))),
        out_shape=jax.ShapeDtypeStruct((M_TOK, D), F32),
        scratch_shapes=_SWIGLU_SCRATCH,
        compiler_params=_cparams(("parallel", "arbitrary")),
        name="ffn_dense",
    )(x, wg, wu, wd, norm_g.reshape(1, D), mods, mods, x, mods)


def _ffn_moe(xs, wg, wu, wd, f, tile_expert, tile_valid):
    jh = lambda i, j, tv: jnp.where(tv[i] > 0, jnp.minimum(j, NJ - 1), NJ - 1)
    jn = lambda i, j, tv: jnp.where(tv[i] > 0, jnp.maximum(j - NJ, 0), NN - 1)
    return pl.pallas_call(
        functools.partial(_swiglu_kernel, moe=True),
        grid_spec=pltpu.PrefetchScalarGridSpec(
            num_scalar_prefetch=2,
            grid=(MOE_TILES, NJ + NN),
            in_specs=[
                pl.BlockSpec((TMF, D), lambda i, j, te, tv: (i, 0)),
                pl.BlockSpec((None, None, D, TH), lambda i, j, te, tv: (f, te[i], 0, jh(i, j, tv))),
                pl.BlockSpec((None, None, D, TH), lambda i, j, te, tv: (f, te[i], 0, jh(i, j, tv))),
                pl.BlockSpec((None, None, FFN_HIDDEN, TN2), lambda i, j, te, tv: (f, te[i], 0, jn(i, j, tv))),
            ],
            out_specs=pl.BlockSpec((TMF, TN2), lambda i, j, te, tv: (i, jnp.maximum(j - NJ, 0))),
            scratch_shapes=_SWIGLU_SCRATCH,
        ),
        out_shape=jax.ShapeDtypeStruct((MOE_ROWS, D), F32),
        compiler_params=_cparams(("parallel", "arbitrary")),
        name="ffn_moe",
    )(tile_expert, tile_valid, xs, wg, wu, wd)


def _router_kernel(x_ref, g_ref, sh_ref, sc_ref, r_ref, h_ref, idx_ref, gates_ref):
    h = _rms(x_ref[...], g_ref[...], NORM_EPS) * (1.0 + sc_ref[...]) + sh_ref[...]
    h_ref[...] = h
    w = r_ref[...]
    h_hi = h.astype(BF16)
    h_lo = (h - h_hi.astype(F32)).astype(BF16)
    w_hi = w.astype(BF16)
    w_lo = (w - w_hi.astype(F32)).astype(BF16)
    logits = _dot(h_hi, w_hi) + (_dot(h_lo, w_hi) + _dot(h_hi, w_lo))
    col = lax.broadcasted_iota(jnp.int32, logits.shape, 1)
    colf = col.astype(F32)
    lg = jnp.where(col < N_EXPERTS, logits, -jnp.inf)
    m1 = jnp.max(lg, axis=-1, keepdims=True)
    i1 = jnp.min(jnp.where(lg == m1, colf, float(LANE)), axis=-1, keepdims=True)
    lg2 = jnp.where(colf == i1, -jnp.inf, lg)
    m2 = jnp.max(lg2, axis=-1, keepdims=True)
    i2 = jnp.min(jnp.where(lg2 == m2, colf, float(LANE)), axis=-1, keepdims=True)
    i1 = i1.astype(jnp.int32)
    i2 = i2.astype(jnp.int32)
    e = jnp.exp(m2 - m1)
    g1 = 1.0 / (1.0 + e)
    g2 = e / (1.0 + e)
    idx_ref[...] = jnp.where(col == 0, i1, jnp.where(col == 1, i2, 0))
    gates_ref[...] = jnp.where(col == 0, g1, jnp.where(col == 1, g2, 0.0))


def _router(x, norm_g, mods, sh_c, sc_c, router):
    r_pad = jnp.pad(router, ((0, 0), (0, LANE - N_EXPERTS)))
    grp = lambda i: _group_of_tile(i, TM)
    return pl.pallas_call(
        _router_kernel,
        grid=(M_TOK // TM,),
        in_specs=[
            pl.BlockSpec((TM, D), lambda i: (i, 0)),
            pl.BlockSpec((1, D), lambda i: (0, 0)),
            pl.BlockSpec((None, None, 1, D), lambda i: (grp(i), sh_c, 0, 0)),
            pl.BlockSpec((None, None, 1, D), lambda i: (grp(i), sc_c, 0, 0)),
            pl.BlockSpec((D, LANE), lambda i: (0, 0)),
        ],
        out_specs=[
            pl.BlockSpec((TM, D), lambda i: (i, 0)),
            pl.BlockSpec((TM, LANE), lambda i: (i, 0)),
            pl.BlockSpec((TM, LANE), lambda i: (i, 0)),
        ],
        out_shape=[
            jax.ShapeDtypeStruct((M_TOK, D), F32),
            jax.ShapeDtypeStruct((M_TOK, LANE), jnp.int32),
            jax.ShapeDtypeStruct((M_TOK, LANE), F32),
        ],
        compiler_params=_cparams(("parallel",)),
        name="router",
    )(x, norm_g.reshape(1, D), mods, mods, r_pad)


def _route_plan(e1, e2):
    e = jnp.concatenate([e1, e2])
    oh = (e[:, None] == jnp.arange(N_EXPERTS, dtype=jnp.int32)[None, :]).astype(jnp.int32)
    csum = jnp.cumsum(oh, axis=0)
    rank = jnp.sum((csum - oh) * oh, axis=1)
    counts = csum[-1]
    padded = ((counts + TMF - 1) // TMF) * TMF
    ends = jnp.cumsum(padded)
    starts = ends - padded
    pos = jnp.sum(oh * starts[None, :], axis=1) + rank
    tile_start = jnp.arange(MOE_TILES, dtype=jnp.int32) * TMF
    te = jnp.minimum(jnp.sum((tile_start[:, None] >= ends[None, :]).astype(jnp.int32), axis=1),
                     N_EXPERTS - 1)
    used = tile_start < ends[-1]
    valid = jnp.where(used, jnp.clip(counts[te] - (tile_start - starts[te]), 0, TMF), 0)
    last_e = te[jnp.maximum(ends[-1] // TMF - 1, 0)]
    te = jnp.where(used, te, last_e)
    sub_off = jnp.arange(TMF // SUB, dtype=jnp.int32) * SUB
    sub_valid = jnp.clip(valid[:, None] - sub_off[None, :], 0, SUB).reshape(-1)
    return pos.astype(jnp.int32), te.astype(jnp.int32), valid.astype(jnp.int32), sub_valid.astype(jnp.int32)


DMA_UNROLL = 8
DISPATCH_ROWS = 2048
COMBINE_ROWS = 1024


def _dispatch_kernel(pos_ref, sv_ref, h_ref, xs_hbm, zero_ref, sem, zsem):
    i = pl.program_id(0)

    @pl.when(i == 0)
    def _():
        zero_ref[...] = jnp.zeros_like(zero_ref)

        def zero_copy(b):
            return pltpu.make_async_copy(zero_ref, xs_hbm.at[pl.ds(pl.multiple_of(b * SUB, SUB), SUB)], zsem)

        def start(b, c):
            pl.when(sv_ref[b] < SUB)(lambda: zero_copy(b).start())
            return c

        def wait(b, c):
            pl.when(sv_ref[b] < SUB)(lambda: zero_copy(b).wait())
            return c

        lax.fori_loop(0, MOE_ROWS // SUB, start, 0)
        lax.fori_loop(0, MOE_ROWS // SUB, wait, 0)

    def issue(r, c):
        p1 = pos_ref[i * DISPATCH_ROWS + r]
        p2 = pos_ref[M_TOK + i * DISPATCH_ROWS + r]
        pltpu.make_async_copy(h_ref.at[pl.ds(r, 1)], xs_hbm.at[pl.ds(p1, 1)], sem).start()
        pltpu.make_async_copy(h_ref.at[pl.ds(r, 1)], xs_hbm.at[pl.ds(p2, 1)], sem).start()
        return c

    lax.fori_loop(0, DISPATCH_ROWS, issue, 0, unroll=DMA_UNROLL)
    pltpu.make_async_copy(h_ref, xs_hbm.at[pl.ds(0, DISPATCH_ROWS)], sem).wait()
    pltpu.make_async_copy(h_ref, xs_hbm.at[pl.ds(0, DISPATCH_ROWS)], sem).wait()


def _moe_dispatch(h, pos, sub_valid):
    return pl.pallas_call(
        _dispatch_kernel,
        grid_spec=pltpu.PrefetchScalarGridSpec(
            num_scalar_prefetch=2,
            grid=(M_TOK // DISPATCH_ROWS,),
            in_specs=[pl.BlockSpec((DISPATCH_ROWS, D), lambda i, p, z: (i, 0))],
            out_specs=pl.BlockSpec(memory_space=pl.ANY),
            scratch_shapes=[pltpu.VMEM((SUB, D), F32), pltpu.SemaphoreType.DMA, pltpu.SemaphoreType.DMA],
        ),
        out_shape=jax.ShapeDtypeStruct((MOE_ROWS, D), F32),
        compiler_params=_cparams(("arbitrary",)),
        name="moe_dispatch",
    )(pos, sub_valid, h)


def _combine_kernel(pos_ref, y_hbm, x_ref, tg_ref, gate_ref, o_ref, y_ref, sem):
    i = pl.program_id(0)
    slot = i % 2

    def picked(k, s):
        return pltpu.make_async_copy(y_hbm.at[pl.ds(0, COMBINE_ROWS)], y_ref.at[s, k], sem.at[s, k])

    def gather_tile(t, s):
        def issue(r, c):
            for k in range(2):
                p = pos_ref[k * M_TOK + t * COMBINE_ROWS + r]
                pltpu.make_async_copy(y_hbm.at[pl.ds(p, 1)], y_ref.at[s, k, pl.ds(r, 1)], sem.at[s, k]).start()
            return c

        lax.fori_loop(0, COMBINE_ROWS, issue, 0, unroll=DMA_UNROLL)

    pl.when(i == 0)(lambda: gather_tile(0, 0))
    pl.when(i + 1 < pl.num_programs(0))(lambda: gather_tile(i + 1, 1 - slot))
    picked(0, slot).wait()
    picked(1, slot).wait()
    tg = tg_ref[...]
    moe_out = tg[:, 0:1] * y_ref[slot, 0] + tg[:, 1:2] * y_ref[slot, 1]
    o_ref[...] = x_ref[...] + gate_ref[...] * moe_out


def _moe_combine(y, pos, x, top_gates, mods, gate_c):
    grp = lambda i: _group_of_tile(i, COMBINE_ROWS)
    return pl.pallas_call(
        _combine_kernel,
        grid_spec=pltpu.PrefetchScalarGridSpec(
            num_scalar_prefetch=1,
            grid=(M_TOK // COMBINE_ROWS,),
            in_specs=[
                pl.BlockSpec(memory_space=pl.ANY),
                pl.BlockSpec((COMBINE_ROWS, D), lambda i, p: (i, 0)),
                pl.BlockSpec((COMBINE_ROWS, LANE), lambda i, p: (i, 0)),
                pl.BlockSpec((None, None, 1, D), lambda i, p: (grp(i), gate_c, 0, 0)),
            ],
            out_specs=pl.BlockSpec((COMBINE_ROWS, D), lambda i, p: (i, 0)),
            scratch_shapes=[pltpu.VMEM((2, 2, COMBINE_ROWS, D), F32), pltpu.SemaphoreType.DMA((2, 2))],
        ),
        out_shape=jax.ShapeDtypeStruct((M_TOK, D), F32),
        compiler_params=_cparams(("arbitrary",)),
        name="moe_combine",
    )(pos, y, x, top_gates, mods)


def _moe(x, norm_g, mods, sh_c, sc_c, gate_c, router, wg, wu, wd, f):
    h, idx, gates = _router(x, norm_g, mods, sh_c, sc_c, router)
    pos, te, tv, sub_valid = _route_plan(idx[:, 0], idx[:, 1])
    xs = _moe_dispatch(h, pos, sub_valid)
    ys = _ffn_moe(xs, wg, wu, wd, f, te, tv)
    return _moe_combine(ys, pos, x, gates, mods, gate_c)


def _rope_tables(width, lo, hi):
    pos = jnp.arange(T_LAT)
    rows = (pos // GRID_W).astype(F32)
    cols = (pos % GRID_W).astype(F32)
    n_freq = 16
    inv = ROPE_BASE ** (-jnp.arange(n_freq, dtype=F32) / n_freq)
    ang = jnp.concatenate([rows[:, None] * inv[None, :], cols[:, None] * inv[None, :]], axis=-1)
    cos, sin = jnp.cos(ang), jnp.sin(ang)
    lane = np.arange(width)
    active = (lane >= lo) & (lane < hi)
    reps = width // 64
    c = jnp.tile(jnp.concatenate([cos, cos], axis=-1), (1, reps))
    s = jnp.tile(jnp.concatenate([-sin, sin], axis=-1), (1, reps))
    return jnp.where(active[None, :], c, 1.0), jnp.where(active[None, :], s, 0.0)


def _rope_kernel(x_ref, c_ref, s_ref, o_ref, *, width):
    lane = lax.broadcasted_iota(jnp.int32, (1, width), 1)
    first = (lane & 63) < 32
    c = c_ref[...]
    s = s_ref[...]
    for j in range(x_ref.shape[-1] // width):
        cols = slice(j * width, (j + 1) * width)
        x = x_ref[:, cols]
        partner = jnp.where(first, pltpu.roll(x, width - 32, 1), pltpu.roll(x, 32, 1))
        o_ref[:, cols] = (x * c + partner * s).astype(o_ref.dtype)


def _rope(x, c, s, *, planes, row0, width, col0=0, ncol=1, tr=512):
    r0 = row0 // tr
    nt = T_LAT // tr
    if x.ndim == 3:
        ncol = x.shape[2] // width
        x_spec = pl.BlockSpec((None, tr, ncol * width), lambda p, i: (p, i + r0, 0))
    else:
        x_spec = pl.BlockSpec((tr, ncol * width), lambda p, i: (i + r0, col0 // ncol))
    return pl.pallas_call(
        functools.partial(_rope_kernel, width=width),
        grid=(planes, M_LAT // tr),
        in_specs=[
            x_spec,
            pl.BlockSpec((tr, width), lambda p, i: (i % nt, 0)),
            pl.BlockSpec((tr, width), lambda p, i: (i % nt, 0)),
        ],
        out_specs=pl.BlockSpec((None, tr, ncol * width), lambda p, i: (p, i, 0)),
        out_shape=jax.ShapeDtypeStruct((planes, M_LAT, ncol * width), BF16),
        compiler_params=_cparams(("parallel", "parallel")),
        name="rope",
    )(x, c, s)


def _attn_kernel(*refs, kind, hb, nseg, scale, lam_init, stack):
    it = iter(refs)
    q_ref = next(it)
    k_refs = [next(it) for _ in range(nseg)]
    kpe_refs = [next(it) for _ in range(nseg)] if kind == "mla" else None
    v_refs = [next(it) for _ in range(nseg)] if kind != "mla" else k_refs
    lam_ref = next(it) if kind == "diff" else None
    subln_ref = next(it) if kind == "diff" else None
    o_ref = next(it)

    lane = lax.broadcasted_iota(jnp.int32, (1, LANE), 1)
    lo_half = lane < 64
    log2_scale = scale * LOG2_E

    def softmax_parts(s_list):
        t_list = [s * log2_scale for s in s_list]
        m = functools.reduce(jnp.maximum, [jnp.max(t, axis=-1, keepdims=True) for t in t_list])
        p_list = [jnp.exp2(t - m) for t in t_list]
        l = functools.reduce(jnp.add, [jnp.sum(p, axis=-1, keepdims=True) for p in p_list])
        return p_list, 1.0 / l

    def attend(s_list, v_list):
        p_list, inv_l = softmax_parts(s_list)
        return functools.reduce(jnp.add, [_dot(p.astype(BF16), v) for p, v in zip(p_list, v_list)]) * inv_l

    for hh in range(hb):
        if kind == "mla":
            q2 = q_ref[:, hh * 256:(hh + 1) * 256].astype(BF16)
            s_list = []
            for kr, pr in zip(k_refs, kpe_refs):
                kcat = jnp.concatenate([kr[:, hh * 256:hh * 256 + 128].astype(BF16), pr[...].astype(BF16)], axis=1)
                s_list.append(_dot_nt(q2, kcat))
            v2 = [vr[:, hh * 256 + 128:hh * 256 + 256].astype(BF16) for vr in v_refs]
            o_ref[:, hh * LANE:(hh + 1) * LANE] = attend(s_list, v2)
            continue

        sl = slice(hh * LANE, (hh + 1) * LANE)
        q2 = q_ref[:, sl]
        k2 = [kr[:, sl].astype(BF16) for kr in k_refs]
        v2 = [vr[:, sl].astype(BF16) for vr in v_refs]
        tq = q2.shape[0]
        q_heads = [jnp.where(lo_half, q2, 0.0), jnp.where(lo_half, 0.0, q2)]
        if stack:
            qs = jnp.concatenate(q_heads, axis=0).astype(BF16)
            o2 = attend([_dot_nt(qs, kk) for kk in k2], v2)
            o_heads = [o2[:tq], o2[tq:]]
            if kind == "diff":
                o_diff = o_heads[0] - lam_ref[...][:, :1] * o_heads[1]
        elif kind == "na":
            o_heads = [attend([_dot_nt(qh.astype(BF16), kk) for kk in k2], v2) for qh in q_heads]
        else:
            (p0, il0), (p1, il1) = [softmax_parts([_dot_nt(qh.astype(BF16), kk) for kk in k2]) for qh in q_heads]
            lam = lam_ref[...][:, :1]
            o_diff = functools.reduce(jnp.add, [
                _dot((a * il0 - lam * (b * il1)).astype(BF16), vv) for a, b, vv in zip(p0, p1, v2)])
        if kind == "na":
            o_ref[:, sl] = jnp.where(lo_half, o_heads[0], o_heads[1])
        else:
            o_ref[:, sl] = _rms(o_diff, subln_ref[...], DIFF_SUBLN_EPS) * (1.0 - lam_init)


def _attention(kind, *, grid, q, ks, vs, kpes=None, extra=(), rows, out_spec, hb, scale, lam_init=0.0,
               stack=False, name="attn"):
    ops = [q] + list(ks) + (list(kpes) if kind == "mla" else []) + (list(vs) if kind != "mla" else [])
    ops += list(extra)
    return pl.pallas_call(
        functools.partial(_attn_kernel, kind=kind, hb=hb, nseg=len(ks), scale=scale, lam_init=lam_init,
                          stack=stack),
        grid=grid,
        in_specs=[s for _, s in ops],
        out_specs=out_spec,
        out_shape=jax.ShapeDtypeStruct((rows, D), F32),
        compiler_params=_cparams(("parallel",) * len(grid)),
        name=name,
    )(*[a for a, _ in ops])


NA_QROWS = 4
NA_KROWS = 12
NA_TQ = NA_QROWS * GRID_W
NA_TK = NA_KROWS * GRID_W


def _na_bias_table(rpb):
    h = rpb.shape[0]
    qc = np.arange(GRID_W)[:, None]
    kc = np.arange(GRID_W)[None, :]
    qs = np.clip(qc - NA_WIN_C // 2, 0, GRID_W - NA_WIN_C)
    col_ok = (kc >= qs) & (kc < qs + NA_WIN_C)
    ic = np.clip(kc - qc + NA_WIN_C - 1, 0, 2 * NA_WIN_C - 2)
    n_ir = 2 * NA_WIN_R - 1
    onehot = (ic.reshape(1, -1) == np.arange(2 * NA_WIN_C - 1)[:, None]).astype(np.float32)
    tab = jnp.einsum("hrm,mx->hrx", rpb.astype(F32) * LOG2_E, jnp.asarray(onehot),
                     precision=lax.Precision.HIGHEST)
    tab = jnp.where(jnp.asarray(col_ok.reshape(1, 1, -1)), tab, NEG_INF).reshape(h, n_ir, GRID_W, GRID_W)
    tab = jnp.concatenate([tab, tab], axis=-1)
    n_rows = T_LAT // GRID_W
    n_g = n_rows // NA_QROWS
    ir_idx = np.zeros((3, NA_QROWS, NA_KROWS), np.int32)
    row_ok = np.zeros((3, NA_QROWS, NA_KROWS), bool)
    for pat, g in enumerate((0, 1, n_g - 1)):
        base = int(np.clip(NA_QROWS * g - NA_QROWS, 0, n_rows - NA_KROWS))
        for qr in range(NA_QROWS):
            r = NA_QROWS * g + qr
            r0 = int(np.clip(r - NA_WIN_R // 2, 0, n_rows - NA_WIN_R))
            for kr in range(NA_KROWS):
                ok = r0 <= base + kr < r0 + NA_WIN_R
                row_ok[pat, qr, kr] = ok
                ir_idx[pat, qr, kr] = (base + kr - r + NA_WIN_R - 1) if ok else 0

    def build(tab_ref, o_ref):
        pat_id = pl.program_id(0)
        lo_half = lax.broadcasted_iota(jnp.int32, (1, LANE), 1) < GRID_W
        masked = jnp.full((GRID_W, LANE), NEG_INF, F32)
        for pat in range(3):
            @pl.when(pat_id == pat)
            def _(pat=pat):
                for hd in range(heads_per_step):
                    for qr in range(NA_QROWS):
                        for kp in range(NA_KROWS // 2):
                            halves = [tab_ref[hd, int(ir_idx[pat, qr, kr])] if row_ok[pat, qr, kr] else masked
                                      for kr in (2 * kp, 2 * kp + 1)]
                            o_ref[hd, qr * GRID_W:(qr + 1) * GRID_W, kp * LANE:(kp + 1) * LANE] = jnp.where(
                                lo_half, halves[0], halves[1])

    heads_per_step = 4
    return pl.pallas_call(
        build,
        grid=(3, h // heads_per_step),
        in_specs=[pl.BlockSpec((heads_per_step, n_ir, GRID_W, LANE), lambda p, hh: (hh, 0, 0, 0))],
        out_specs=pl.BlockSpec((None, heads_per_step, NA_TQ, NA_TK), lambda p, hh: (p, hh, 0, 0)),
        out_shape=jax.ShapeDtypeStruct((3, h, NA_TQ, NA_TK), F32),
        compiler_params=_cparams(("parallel", "parallel")),
        name="na_bias",
    )(tab)


def _na_lat_kernel(q_ref, kc_ref, vc_ref, k_ref, v_ref, b_ref, o_ref, *, scale):
    g = pl.program_id(2)
    n_rows = T_LAT // GRID_W
    base = jnp.clip(NA_QROWS * g - NA_QROWS, 0, n_rows - NA_KROWS) * GRID_W
    base = pl.multiple_of(base, GRID_W)
    lane = lax.broadcasted_iota(jnp.int32, (1, LANE), 1)
    lo_half = lane < 64
    log2_scale = scale * LOG2_E
    for lb in range(LATENT_HB):
        cols = slice(lb * LANE, (lb + 1) * LANE)
        q2 = q_ref[:, cols]
        kc = kc_ref[:, cols].astype(BF16)
        vc = vc_ref[:, cols].astype(BF16)
        kl = k_ref[pl.ds(base, NA_TK), cols].astype(BF16)
        vl = v_ref[pl.ds(base, NA_TK), cols].astype(BF16)
        outs = []
        for n in range(2):
            qm = jnp.where(lo_half if n == 0 else jnp.logical_not(lo_half), q2, 0.0).astype(BF16)
            s_c = _dot_nt(qm, kc) * log2_scale
            s_l = _dot_nt(qm, kl) * log2_scale + b_ref[2 * lb + n]
            m = jnp.maximum(jnp.max(s_c, axis=-1, keepdims=True), jnp.max(s_l, axis=-1, keepdims=True))
            p_c = jnp.exp2(s_c - m)
            p_l = jnp.exp2(s_l - m)
            l = jnp.sum(p_c, axis=-1, keepdims=True) + jnp.sum(p_l, axis=-1, keepdims=True)
            o = _dot(p_c.astype(BF16), vc) + _dot(p_l.astype(BF16), vl)
            outs.append(o * (1.0 / l))
        o_ref[:, cols] = jnp.where(lo_half, outs[0], outs[1])


def _na_lat(qkv, cache_k, cache_v, bias):
    n_g = T_LAT // NA_TQ
    q_blk0 = M_CTX // NA_TQ
    kv_blk0 = M_CTX // T_LAT
    wb = LATENT_HB * LANE
    pat = lambda g: jnp.where(g == 0, 0, jnp.where(g == n_g - 1, 2, 1))
    return pl.pallas_call(
        functools.partial(_na_lat_kernel, scale=0.125),
        grid=(N_LAT_SEQ, D // wb, n_g),
        in_specs=[
            pl.BlockSpec((None, NA_TQ, wb), lambda b, h, g: (0, q_blk0 + b * n_g + g, h)),
            pl.BlockSpec((None, PAST, wb), lambda b, h, g: (b, 0, h)),
            pl.BlockSpec((None, PAST, wb), lambda b, h, g: (b, 0, h)),
            pl.BlockSpec((None, T_LAT, wb), lambda b, h, g: (1, kv_blk0 + b, h)),
            pl.BlockSpec((None, T_LAT, wb), lambda b, h, g: (2, kv_blk0 + b, h)),
            pl.BlockSpec((None, 2 * LATENT_HB, NA_TQ, NA_TK), lambda b, h, g: (pat(g), h, 0, 0)),
        ],
        out_specs=pl.BlockSpec((NA_TQ, wb), lambda b, h, g: (b * n_g + g, h)),
        out_shape=jax.ShapeDtypeStruct((M_LAT, D), F32),
        compiler_params=_cparams(("parallel", "parallel", "parallel")),
        name="na_latent",
    )(qkv, cache_k, cache_v, qkv, qkv, bias)


SEQ_T_PER_STEP = 8


def _seq_t_kernel(x_ref, o_ref):
    for s in range(SEQ_T_PER_STEP):
        o_ref[s] = x_ref[s * T_CTX:(s + 1) * T_CTX, :].T


def _seq_transpose(qkv, plane):
    return pl.pallas_call(
        _seq_t_kernel,
        grid=(N_CTX_SEQ // SEQ_T_PER_STEP,),
        in_specs=[pl.BlockSpec((None, SEQ_T_PER_STEP * T_CTX, D), lambda b: (plane, b, 0))],
        out_specs=pl.BlockSpec((SEQ_T_PER_STEP, D, T_CTX), lambda b: (b, 0, 0)),
        out_shape=jax.ShapeDtypeStruct((N_CTX_SEQ, D, T_CTX), F32),
        compiler_params=_cparams(("parallel",)),
        name="seq_transpose",
    )(qkv)


def kernel(x_prompt, x_sample, c, c_ctx, cache_na_k, cache_na_v, cache_diff_k, cache_diff_v, cache_mla_ckv, cache_mla_kpe, ada_w, ada_b, norm1, norm2, pool_w, pool_scale, na_qkv, na_rpb, na_out, diff_qkv, diff_lambda, diff_subln, diff_out, mla_qa, mla_qnorm, mla_qb, mla_kva, mla_kvnorm, mla_kvb, mla_out, ffn_gate, ffn_up, ffn_down, moe_router, moe_gate, moe_up, moe_down, final_norm):
    depth = ada_w.shape[0]
    x = None
    cond8 = jnp.concatenate([c_ctx[None, :], c, jnp.zeros((5, D), F32)], axis=0)
    mods_all = _ada(cond8, ada_w, ada_b)[:, :3].reshape(depth, 3, 6, 1, D)
    outs = {}

    for l in range(depth):
        kind, li = l % 4, l // 4
        mods = mods_all[l]
        if kind == 0:
            x_pair = (x_prompt.reshape(M_CTX, D), x_sample.reshape(M_LAT, D)) if x is None else (x[:M_CTX], x[M_CTX:])
            h = _modrows(x_pair, norm1[l], mods, 0, 1, rows=M_TOK, row0=0, width=D)
            x = _pool(h, *x_pair, pool_w[li], pool_scale[li], mods, 2)
        elif kind == 1:
            qkv = _linear(x, na_qkv[li], tn=D, norm_g=norm1[l], mods=mods, sh_c=0, sc_c=1, planes=True,
                          name="na_qkv")
            to_cache = lambda a: jnp.transpose(a.reshape(N_CTX_SEQ, 1, 16, 64, T_CTX), (0, 1, 4, 2, 3))
            outs["na_k"] = to_cache(_seq_transpose(qkv, 1))
            outs["na_v"] = to_cache(_seq_transpose(qkv, 2))
            seq_spec = lambda p: pl.BlockSpec((None, T_CTX, D), lambda b, p=p: (p, b, 0))
            o_ctx = _attention(
                "na", grid=(N_CTX_SEQ,), q=(qkv, seq_spec(0)), ks=[(qkv, seq_spec(1))], vs=[(qkv, seq_spec(2))],
                rows=M_CTX, out_spec=pl.BlockSpec((T_CTX, D), lambda b: (b, 0)), hb=D // LANE, scale=0.125,
                stack=True, name="na_ctx")
            bias = _na_bias_table(na_rpb[li])
            o_lat = _na_lat(qkv, cache_na_k[:, li].reshape(N_LAT_SEQ, PAST, D),
                            cache_na_v[:, li].reshape(N_LAT_SEQ, PAST, D), bias)
            x = _linear((o_ctx, o_lat), na_out[li], tn=D, tm=TM, resid=x, mods=mods, gate_c=2, name="na_out")
        elif kind == 2:
            lam_init = 0.8 - 0.6 * math.exp(-0.3 * l)
            lp = diff_lambda[li].astype(F32)
            lam = jnp.exp(jnp.sum(lp[0] * lp[1])) - jnp.exp(jnp.sum(lp[2] * lp[3])) + lam_init
            lam_row = jnp.full((1, LANE), lam, F32)
            subln = diff_subln[li].reshape(1, LANE)
            extra = [(lam_row, pl.BlockSpec((1, LANE), lambda *_: (0, 0))),
                     (subln, pl.BlockSpec((1, LANE), lambda *_: (0, 0)))]
            qkv = _linear(x, diff_qkv[li], tn=D, norm_g=norm1[l], mods=mods, sh_c=0, sc_c=1, planes=True,
                          name="diff_qkv")
            outs["diff_k"] = jnp.transpose(
                _seq_transpose(qkv, 1).reshape(N_CTX_SEQ, 1, 8, 2, 64, T_CTX), (0, 1, 5, 2, 3, 4))
            outs["diff_v"] = qkv[2, :M_CTX].reshape(N_CTX_SEQ, 1, T_CTX, 8, 128)
            seq_spec = lambda p: pl.BlockSpec((None, T_CTX, D), lambda b, p=p: (p, b, 0))
            o_ctx = _attention(
                "diff", grid=(N_CTX_SEQ,), q=(qkv, seq_spec(0)), ks=[(qkv, seq_spec(1))], vs=[(qkv, seq_spec(2))],
                extra=extra, rows=M_CTX, out_spec=pl.BlockSpec((T_CTX, D), lambda b: (b, 0)), hb=D // LANE,
                scale=0.125, lam_init=lam_init, stack=True, name="diff_ctx")
            rc, rs = _rope_tables(LANE, 0, LANE)
            qk_r = _rope(qkv, rc, rs, planes=2, row0=M_CTX, width=LANE)
            tq = LATENT_TQ
            nq = T_LAT // tq
            ck = cache_diff_k[:, li].reshape(N_LAT_SEQ, PAST, D)
            cv = cache_diff_v[:, li].reshape(N_LAT_SEQ, PAST, D)
            hb, wb = LATENT_HB, LATENT_HB * LANE
            o_lat = _attention(
                "diff", grid=(N_LAT_SEQ, D // wb, nq),
                q=(qk_r, pl.BlockSpec((None, tq, wb), lambda b, h, i: (0, b * nq + i, h))),
                ks=[(ck, pl.BlockSpec((None, PAST, wb), lambda b, h, i: (b, 0, h))),
                    (qk_r, pl.BlockSpec((None, T_LAT, wb), lambda b, h, i: (1, b, h)))],
                vs=[(cv, pl.BlockSpec((None, PAST, wb), lambda b, h, i: (b, 0, h))),
                    (qkv, pl.BlockSpec((None, T_LAT, wb), lambda b, h, i: (2, M_CTX // T_LAT + b, h)))],
                extra=extra,
                rows=M_LAT, out_spec=pl.BlockSpec((tq, wb), lambda b, h, i: (b * nq + i, h)),
                hb=hb, scale=0.125, lam_init=lam_init, name="diff_latent")
            x = _linear((o_ctx, o_lat), diff_out[li], tn=D, tm=TM, resid=x, mods=mods, gate_c=2, name="diff_out")
        else:
            hd = MLA_NOPE + MLA_ROPE
            wqb = mla_qb[li].reshape(MLA_Q_LORA, MLA_HEADS, hd)
            wqb = jnp.pad(wqb, ((0, 0), (0, 0), (0, 256 - hd))).reshape(MLA_Q_LORA, MLA_HEADS * 256)
            w_a = jnp.concatenate(
                [mla_qa[li], jnp.pad(mla_kva[li], ((0, 0), (0, 512 - MLA_KV_LORA - MLA_ROPE)))], axis=1)
            a = _linear(x, w_a, tn=D, tm=TM, norm_g=norm1[l], mods=mods, sh_c=0, sc_c=1, name="mla_a")
            q = _linear(a, wqb, tn=D, tm=TM, k=MLA_Q_LORA, x_col=0, norm_g=mla_qnorm[li], name="mla_q")
            zero_mod = jnp.zeros((1, 2, 1, MLA_KV_LORA), F32)
            ckv = _modrows(a, mla_kvnorm[li], zero_mod, 0, 1, rows=M_TOK, row0=0, width=MLA_KV_LORA, col_blk=2,
                           grouped=False)
            outs["mla_ckv"] = ckv[:M_CTX].reshape(N_CTX_SEQ, 1, T_CTX, MLA_KV_LORA)
            outs["mla_kpe"] = a[:M_CTX, 768:768 + MLA_ROPE].reshape(N_CTX_SEQ, 1, T_CTX, MLA_ROPE)
            ckv_all = jnp.concatenate([ckv, cache_mla_ckv[:, li].reshape(N_LAT_SEQ * PAST, MLA_KV_LORA)], axis=0)
            kvb = _linear(ckv_all, mla_kvb[li], tn=D, tm=TM, out_dtype=BF16, name="mla_kvb")
            scale = float(hd) ** -0.5
            o_ctx = _attention(
                "mla", grid=(N_CTX_SEQ,),
                q=(q, pl.BlockSpec((T_CTX, 2048), lambda b: (b, 0))),
                ks=[(kvb, pl.BlockSpec((T_CTX, 2048), lambda b: (b, 0)))],
                kpes=[(a, pl.BlockSpec((T_CTX, LANE), lambda b: (b, 6)))],
                vs=None, rows=M_CTX, out_spec=pl.BlockSpec((T_CTX, D), lambda b: (b, 0)), hb=MLA_HEADS,
                scale=scale, name="mla_ctx")
            rc, rs = _rope_tables(256, 128, 192)
            q_r = _rope(q, rc, rs, planes=1, row0=M_CTX, width=256, ncol=MLA_HEADS)[0]
            rc1, rs1 = _rope_tables(LANE, 0, 64)
            kpe_r = _rope(a, rc1, rs1, planes=1, row0=M_CTX, width=LANE, col0=6)[0]
            kpe_c = jnp.pad(cache_mla_kpe[:, li], ((0, 0), (0, 0), (0, LANE - MLA_ROPE)))
            tq = LATENT_TQ
            nq = T_LAT // tq
            hb = LATENT_HB
            o_lat = _attention(
                "mla", grid=(N_LAT_SEQ, MLA_HEADS // hb, nq),
                q=(q_r, pl.BlockSpec((tq, hb * 256), lambda b, h, i: (b * nq + i, h))),
                ks=[(kvb, pl.BlockSpec((PAST, hb * 256), lambda b, h, i: (M_TOK // PAST + b, h))),
                    (kvb, pl.BlockSpec((T_LAT, hb * 256), lambda b, h, i: (M_CTX // T_LAT + b, h)))],
                kpes=[(kpe_c, pl.BlockSpec((None, PAST, LANE), lambda b, h, i: (b, 0, 0))),
                      (kpe_r, pl.BlockSpec((T_LAT, LANE), lambda b, h, i: (b, 0)))],
                vs=None, rows=M_LAT,
                out_spec=pl.BlockSpec((tq, hb * LANE), lambda b, h, i: (b * nq + i, h)),
                hb=hb, scale=scale, name="mla_latent")
            x = _linear((o_ctx, o_lat), mla_out[li], tn=D, tm=TM, resid=x, mods=mods, gate_c=2, name="mla_out")

        f = l // 2
        if l % 2 == 0:
            x = _ffn_dense(x, ffn_gate, ffn_up, ffn_down, f, norm2[l], mods, 3, 4, 5)
        else:
            x = _moe(x, norm2[l], mods, 3, 4, 5, moe_router[f], moe_gate, moe_up, moe_down, f)

    zero_mod = jnp.zeros((1, 2, 1, D), F32)
    y_prompt = _modrows(x, final_norm, zero_mod, 0, 1, rows=M_CTX, row0=0, width=D, grouped=False)
    y_sample = _modrows(x, final_norm, zero_mod, 0, 1, rows=M_LAT, row0=M_CTX, width=D, grouped=False)
    return (y_prompt.reshape(N_CTX_SEQ, T_CTX, D), y_sample.reshape(N_LAT_SEQ, T_LAT, D),
            outs["na_k"], outs["na_v"], outs["diff_k"], outs["diff_v"], outs["mla_ckv"], outs["mla_kpe"])
```

```python
import functools
import math

import numpy as np
import jax
import jax.numpy as jnp
from jax import lax
from jax.experimental import pallas as pl
from jax.experimental.pallas import tpu as pltpu

F32 = jnp.float32
BF16 = jnp.bfloat16

D = 1024
N_CTX_SEQ = 32
T_CTX = 256
N_LAT_SEQ = 2
T_LAT = 2048
PAST = 512
M_CTX = N_CTX_SEQ * T_CTX
M_LAT = N_LAT_SEQ * T_LAT
M_TOK = M_CTX + M_LAT
GRID_W = 64
NORM_EPS = 1e-6
NEG_INF = -1e30
ROPE_BASE = 10000.0
POOL_WINDOWS = (2, 4, 8, 16)
NA_WIN_R = 8
NA_WIN_C = 16
DIFF_SUBLN_EPS = 1e-5
MLA_HEADS = 8
MLA_NOPE = 128
MLA_ROPE = 64
MLA_KV_LORA = 256
MLA_Q_LORA = 512
FFN_HIDDEN = 2816
N_EXPERTS = 8

LOG2_E = math.log2(math.e)
LANE = 128
TM = 1024
TH = 256
SUB = 256
VMEM_LIMIT = 56 * 1024 * 1024

TMF = 2048
MOE_ROWS = 2 * M_TOK + N_EXPERTS * TMF
MOE_TILES = MOE_ROWS // TMF


def _cparams(sem):
    return pltpu.CompilerParams(dimension_semantics=sem, vmem_limit_bytes=VMEM_LIMIT)


def _group_of_tile(i, tm):
    n_ctx = M_CTX // tm
    return jnp.where(i < n_ctx, 0, 1 + (i - n_ctx) // (T_LAT // tm))


def _dot(a, b):
    return jnp.dot(a, b, preferred_element_type=F32)


def _dot_nt(a, b):
    return lax.dot_general(a, b, (((1,), (1,)), ((), ())), preferred_element_type=F32)


def _rms(x, g, eps):
    return x * lax.rsqrt(jnp.mean(x * x, axis=-1, keepdims=True) + eps) * g


def _ada_kernel(c_ref, w_ref, b_ref, o_ref):
    c = c_ref[...]
    s = (c * jax.nn.sigmoid(c)).astype(BF16)
    o_ref[...] = _dot(s, w_ref[...].astype(BF16)) + b_ref[...]


def _ada(cond8, ada_w, ada_b):
    depth = ada_w.shape[0]
    tn = 1536
    return pl.pallas_call(
        _ada_kernel,
        grid=(depth, 6 * D // tn),
        in_specs=[
            pl.BlockSpec((8, D), lambda l, j: (0, 0)),
            pl.BlockSpec((None, D, tn), lambda l, j: (l, 0, j)),
            pl.BlockSpec((None, 1, tn), lambda l, j: (l, 0, j)),
        ],
        out_specs=pl.BlockSpec((None, 8, tn), lambda l, j: (l, 0, j)),
        out_shape=jax.ShapeDtypeStruct((depth, 8, 6 * D), F32),
        compiler_params=_cparams(("parallel", "parallel")),
        name="ada",
    )(cond8, ada_w, ada_b.reshape(depth, 1, 6 * D))


def _token_tile(xc_ref, xl_ref, tile, tm):
    return jnp.where(tile < M_CTX // tm, xc_ref[...], xl_ref[...])


def _split_specs(block, tm, col):
    n_ctx = M_CTX // tm
    return [pl.BlockSpec(block, lambda i, *_: (jnp.minimum(i, n_ctx - 1), col(*_))),
            pl.BlockSpec(block, lambda i, *_: (jnp.maximum(i - n_ctx, 0), col(*_)))]


def _modrows_kernel(*refs, eps, split, tm):
    if split:
        xc_ref, xl_ref, g_ref, sh_ref, sc_ref, o_ref = refs
        x = _token_tile(xc_ref, xl_ref, pl.program_id(0), tm)
    else:
        x_ref, g_ref, sh_ref, sc_ref, o_ref = refs
        x = x_ref[...]
    o_ref[...] = _rms(x, g_ref[...], eps) * (1.0 + sc_ref[...]) + sh_ref[...]


def _modrows(x, g, mods, sh_c, sc_c, *, rows, row0, width, col_blk=0, tm=TM, eps=NORM_EPS,
             grouped=True):
    r0 = row0 // tm
    if grouped:
        grp = lambda i: _group_of_tile(i + r0, tm)
    else:
        grp = lambda i: 0
    split = isinstance(x, tuple)
    if split:
        xs, x_specs = list(x), _split_specs((tm, width), tm, lambda: 0)
    else:
        xs, x_specs = [x], [pl.BlockSpec((tm, width), lambda i: (i + r0, col_blk))]
    return pl.pallas_call(
        functools.partial(_modrows_kernel, eps=eps, split=split, tm=tm),
        grid=(rows // tm,),
        in_specs=x_specs + [
            pl.BlockSpec((1, width), lambda i: (0, 0)),
            pl.BlockSpec((None, None, 1, width), lambda i: (grp(i), sh_c, 0, 0)),
            pl.BlockSpec((None, None, 1, width), lambda i: (grp(i), sc_c, 0, 0)),
        ],
        out_specs=pl.BlockSpec((tm, width), lambda i: (i, 0)),
        out_shape=jax.ShapeDtypeStruct((rows, width), F32),
        compiler_params=_cparams(("parallel",)),
        name="modrows",
    )(*xs, g.reshape(1, width), mods, mods)


def _linear_kernel(*refs, prologue, eps, epilogue, split, tm, tn, planes):
    it = iter(refs)
    x_ref = next(it)
    x_lat_ref = next(it) if split else None
    w_ref = next(it)
    g_ref = next(it) if prologue != "none" else None
    sh_ref = next(it) if prologue == "mod" else None
    sc_ref = next(it) if prologue == "mod" else None
    res_ref = next(it) if epilogue == "resid" else None
    gate_ref = next(it) if epilogue == "resid" else None
    o_ref = next(it)
    wb_ref = next(it)
    n = w_ref.shape[1]

    @pl.when(pl.program_id(0) == 0)
    def _():
        wb_ref[...] = w_ref[...].astype(BF16)

    def run(src_ref):
        x = src_ref[...]
        if prologue != "none":
            x = _rms(x, g_ref[...], eps)
        if prologue == "mod":
            x = x * (1.0 + sc_ref[...]) + sh_ref[...]
        xb = x.astype(BF16)
        for p in range(n // tn):
            cols = slice(p * tn, (p + 1) * tn)
            acc = _dot(xb, wb_ref[:, cols])
            if epilogue == "resid":
                acc = res_ref[:, cols] + gate_ref[:, cols] * acc
            if planes:
                o_ref[p] = acc.astype(o_ref.dtype)
            else:
                o_ref[:, cols] = acc.astype(o_ref.dtype)

    if split:
        is_ctx = pl.program_id(0) < M_CTX // tm
        pl.when(is_ctx)(lambda: run(x_ref))
        pl.when(jnp.logical_not(is_ctx))(lambda: run(x_lat_ref))
    else:
        run(x_ref)


TML = 512
LATENT_TQ = 256
LATENT_HB = 4


def _linear(x, w, *, tn, k=None, x_col=0, norm_g=None, eps=NORM_EPS, mods=None, sh_c=None, sc_c=None,
            resid=None, gate_c=None, planes=False, tm=TML, out_dtype=F32, name="linear"):
    split = isinstance(x, tuple)
    n = w.shape[1]
    prologue = "none" if norm_g is None else ("mod" if sh_c is not None else "norm")
    epilogue = "none" if resid is None else "resid"
    grp = lambda i: _group_of_tile(i, tm)
    if split:
        x_ctx, x_lat = x
        m = x_ctx.shape[0] + x_lat.shape[0]
        k = x_ctx.shape[1]
        n_ctx = x_ctx.shape[0] // tm
        args = [x_ctx, x_lat]
        in_specs = [
            pl.BlockSpec((tm, k), lambda i: (jnp.minimum(i, n_ctx - 1), 0)),
            pl.BlockSpec((tm, k), lambda i: (jnp.maximum(i - n_ctx, 0), 0)),
        ]
    else:
        m = x.shape[0]
        k = x.shape[1] if k is None else k
        args = [x]
        in_specs = [pl.BlockSpec((tm, k), lambda i: (i, x_col))]
    args.append(w)
    in_specs.append(pl.BlockSpec((k, n), lambda i: (0, 0)))
    if prologue != "none":
        args.append(norm_g.reshape(1, k))
        in_specs.append(pl.BlockSpec((1, k), lambda i: (0, 0)))
    if prologue == "mod":
        args += [mods, mods]
        in_specs += [
            pl.BlockSpec((None, None, 1, k), lambda i: (grp(i), sh_c, 0, 0)),
            pl.BlockSpec((None, None, 1, k), lambda i: (grp(i), sc_c, 0, 0)),
        ]
    if epilogue == "resid":
        args += [resid, mods]
        in_specs += [
            pl.BlockSpec((tm, n), lambda i: (i, 0)),
            pl.BlockSpec((None, None, 1, n), lambda i: (grp(i), gate_c, 0, 0)),
        ]
    if planes:
        out_spec = pl.BlockSpec((n // tn, tm, tn), lambda i: (0, i, 0))
        out_shape = jax.ShapeDtypeStruct((n // tn, m, tn), out_dtype)
    else:
        out_spec = pl.BlockSpec((tm, n), lambda i: (i, 0))
        out_shape = jax.ShapeDtypeStruct((m, n), out_dtype)
    return pl.pallas_call(
        functools.partial(_linear_kernel, prologue=prologue, eps=eps, epilogue=epilogue, split=split, tm=tm,
                          tn=tn, planes=planes),
        grid=(m // tm,),
        in_specs=in_specs,
        out_specs=out_spec,
        out_shape=out_shape,
        scratch_shapes=[pltpu.VMEM((k, n), BF16)],
        compiler_params=_cparams(("arbitrary",)),
        name=name,
    )(*args)


POOL_ROWS = 2048


def _pool_kernel(h_ref, xc_ref, xl_ref, w_ref, ps_ref, gate_ref, o_ref):
    i = pl.program_id(0)
    g = pl.program_id(1)
    seq = jnp.where(i < M_CTX // POOL_ROWS, T_CTX, T_LAT)
    t = lax.broadcasted_iota(jnp.int32, (POOL_ROWS, 1), 0) & (seq - 1)

    def ahead(a, k):
        return jnp.where(t + k < seq, pltpu.roll(a, POOL_ROWS - k, 0), 0.0)

    def behind(a, k):
        return jnp.where(t - k >= 0, pltpu.roll(a, k, 0), 0.0)

    for gi, win in enumerate(POOL_WINDOWS):
        @pl.when(g == gi)
        def _(win=win):
            h = h_ref[...]
            fwd, bwd, k = h, h, 1
            while k < win // 2:
                fwd, bwd, k = fwd + ahead(fwd, k), bwd + behind(bwd, k), 2 * k
            acc = fwd + behind(bwd, 1)
            cnt = jnp.minimum(t + (win - win // 2), seq) - jnp.maximum(t - win // 2, 0)
            pooled = acc / cnt.astype(F32)
            y = _dot((pooled - h).astype(BF16), w_ref[...].astype(BF16)) * ps_ref[...]
            o_ref[...] = _token_tile(xc_ref, xl_ref, i, POOL_ROWS) + gate_ref[...] * y


def _pool(h, x_ctx, x_lat, pool_w, pool_scale, mods, gate_c):
    gw = D // len(POOL_WINDOWS)
    grp = lambda i: _group_of_tile(i, POOL_ROWS)
    return pl.pallas_call(
        _pool_kernel,
        grid=(M_TOK // POOL_ROWS, len(POOL_WINDOWS)),
        in_specs=[
            pl.BlockSpec((POOL_ROWS, gw), lambda i, g: (i, g)),
            *_split_specs((POOL_ROWS, gw), POOL_ROWS, lambda g: g),
            pl.BlockSpec((None, gw, gw), lambda i, g: (g, 0, 0)),
            pl.BlockSpec((1, gw), lambda i, g: (0, g)),
            pl.BlockSpec((None, None, 1, gw), lambda i, g: (grp(i), gate_c, 0, g)),
        ],
        out_specs=pl.BlockSpec((POOL_ROWS, gw), lambda i, g: (i, g)),
        out_shape=jax.ShapeDtypeStruct((M_TOK, D), F32),
        compiler_params=_cparams(("parallel", "parallel")),
        name="pool",
    )(h, x_ctx, x_lat, pool_w, pool_scale.reshape(1, D), mods)


NJ = FFN_HIDDEN // TH
TN2 = 256
NN = D // TN2
STRAIGHT_COUNTS = (TMF // SUB, 4, 5)


def _swiglu_kernel(*refs, moe):
    if moe:
        te_ref, tv_ref, x_ref, wg_ref, wu_ref, wd_ref, o_ref, xb_ref, h_ref = refs
    else:
        x_ref, wg_ref, wu_ref, wd_ref, g_ref, sh_ref, sc_ref, res_ref, gate_ref, o_ref, xb_ref, h_ref = refs
    j = pl.program_id(1)
    n_valid = tv_ref[pl.program_id(0)] if moe else TMF

    def per_sub_block(fn, fn_skipped=None):
        sub_blocks = [pl.ds(s * SUB, SUB) for s in range(TMF // SUB)]

        def leading(count):
            def run():
                for rows in sub_blocks[:count]:
                    fn(rows)
                if fn_skipped is not None:
                    for rows in sub_blocks[count:]:
                        fn_skipped(rows)
            return run

        def valid_rows_only():
            for s, rows in enumerate(sub_blocks):
                pl.when(s * SUB < n_valid)(functools.partial(fn, rows))
                if fn_skipped is not None:
                    pl.when(s * SUB >= n_valid)(functools.partial(fn_skipped, rows))

        if moe:
            n_blocks = (n_valid + SUB - 1) // SUB
            for count in STRAIGHT_COUNTS:
                pl.when(n_blocks == count)(leading(count))
            pl.when(functools.reduce(jnp.logical_and, [n_blocks != c for c in STRAIGHT_COUNTS]))(valid_rows_only)
        else:
            leading(len(sub_blocks))()

    @pl.when(j == 0)
    def _():
        def cast(rows):
            x = x_ref[rows, :]
            if not moe:
                x = _rms(x, g_ref[...], NORM_EPS) * (1.0 + sc_ref[...]) + sh_ref[...]
            xb_ref[rows, :] = x.astype(BF16)

        per_sub_block(cast)

    @pl.when(j < NJ)
    def _():
        wg = wg_ref[...].astype(BF16)
        wu = wu_ref[...].astype(BF16)

        def up(rows):
            xb = xb_ref[rows, :]
            a = _dot(xb, wg)
            u = _dot(xb, wu)
            h_ref[j, rows, :] = (a * jax.nn.sigmoid(a) * u).astype(BF16)

        per_sub_block(up)

    @pl.when(j >= NJ)
    def _():
        wd = wd_ref[...].astype(BF16)

        def down(rows):
            hcat = jnp.concatenate([h_ref[jj, rows, :] for jj in range(NJ)], axis=1)
            y = _dot(hcat, wd)
            if moe:
                o_ref[rows, :] = y
            else:
                o_ref[rows, :] = res_ref[rows, :] + gate_ref[...] * y

        def skipped(rows):
            o_ref[rows, :] = jnp.zeros((SUB, TN2), F32)

        per_sub_block(down, skipped)


_SWIGLU_SCRATCH = [pltpu.VMEM((TMF, D), BF16), pltpu.VMEM((NJ, TMF, TH), BF16)]


def _ffn_dense(x, wg, wu, wd, f, norm_g, mods, sh_c, sc_c, gate_c):
    grp = lambda i: _group_of_tile(i, TMF)
    jh = lambda j: jnp.minimum(j, NJ - 1)
    jn = lambda j: jnp.maximum(j - NJ, 0)
    mod_spec = lambda c: pl.BlockSpec((None, None, 1, D), lambda i, j: (grp(i), c, 0, 0))
    return pl.pallas_call(
        functools.partial(_swiglu_kernel, moe=False),
        grid=(M_TOK // TMF, NJ + NN),
        in_specs=[
            pl.BlockSpec((TMF, D), lambda i, j: (i, 0)),
            pl.BlockSpec((None, D, TH), lambda i, j: (f, 0, jh(j))),
            pl.BlockSpec((None, D, TH), lambda i, j: (f, 0, jh(j))),
            pl.BlockSpec((None, FFN_HIDDEN, TN2), lambda i, j: (f, 0, jn(j))),
            pl.BlockSpec((1, D), lambda i, j: (0, 0)),
            mod_spec(sh_c), mod_spec(sc_c),
            pl.BlockSpec((TMF, TN2), lambda i, j: (i, jn(j))),
            pl.BlockSpec((None, None, 1, TN2), lambda i, j: (grp(i), gate_c, 0, jn(j))),
        ],
        out_specs=pl.BlockSpec((TMF, TN2), lambda i, j: (i, jn(j))),
        out_shape=jax.ShapeDtypeStruct((M_TOK, D), F32),
        scratch_shapes=_SWIGLU_SCRATCH,
        compiler_params=_cparams(("parallel", "arbitrary")),
        name="ffn_dense",
    )(x, wg, wu, wd, norm_g.reshape(1, D), mods, mods, x, mods)


def _ffn_moe(xs, wg, wu, wd, f, tile_expert, tile_valid):
    jh = lambda i, j, tv: jnp.where(tv[i] > 0, jnp.minimum(j, NJ - 1), NJ - 1)
    jn = lambda i, j, tv: jnp.where(tv[i] > 0, jnp.maximum(j - NJ, 0), NN - 1)
    return pl.pallas_call(
        functools.partial(_swiglu_kernel, moe=True),
        grid_spec=pltpu.PrefetchScalarGridSpec(
            num_scalar_prefetch=2,
            grid=(MOE_TILES, NJ + NN),
            in_specs=[
                pl.BlockSpec((TMF, D), lambda i, j, te, tv: (i, 0)),
                pl.BlockSpec((None, None, D, TH), lambda i, j, te, tv: (f, te[i], 0, jh(i, j, tv))),
                pl.BlockSpec((None, None, D, TH), lambda i, j, te, tv: (f, te[i], 0, jh(i, j, tv))),
                pl.BlockSpec((None, None, FFN_HIDDEN, TN2), lambda i, j, te, tv: (f, te[i], 0, jn(i, j, tv))),
            ],
            out_specs=pl.BlockSpec((TMF, TN2), lambda i, j, te, tv: (i, jnp.maximum(j - NJ, 0))),
            scratch_shapes=_SWIGLU_SCRATCH,
        ),
        out_shape=jax.ShapeDtypeStruct((MOE_ROWS, D), F32),
        compiler_params=_cparams(("parallel", "arbitrary")),
        name="ffn_moe",
    )(tile_expert, tile_valid, xs, wg, wu, wd)


def _router_kernel(x_ref, g_ref, sh_ref, sc_ref, r_ref, h_ref, idx_ref, gates_ref):
    h = _rms(x_ref[...], g_ref[...], NORM_EPS) * (1.0 + sc_ref[...]) + sh_ref[...]
    h_ref[...] = h
    w = r_ref[...]
    h_hi = h.astype(BF16)
    h_lo = (h - h_hi.astype(F32)).astype(BF16)
    w_hi = w.astype(BF16)
    w_lo = (w - w_hi.astype(F32)).astype(BF16)
    logits = _dot(h_hi, w_hi) + (_dot(h_lo, w_hi) + _dot(h_hi, w_lo))
    col = lax.broadcasted_iota(jnp.int32, logits.shape, 1)
    colf = col.astype(F32)
    lg = jnp.where(col < N_EXPERTS, logits, -jnp.inf)
    m1 = jnp.max(lg, axis=-1, keepdims=True)
    i1 = jnp.min(jnp.where(lg == m1, colf, float(LANE)), axis=-1, keepdims=True)
    lg2 = jnp.where(colf == i1, -jnp.inf, lg)
    m2 = jnp.max(lg2, axis=-1, keepdims=True)
    i2 = jnp.min(jnp.where(lg2 == m2, colf, float(LANE)), axis=-1, keepdims=True)
    i1 = i1.astype(jnp.int32)
    i2 = i2.astype(jnp.int32)
    e = jnp.exp(m2 - m1)
    g1 = 1.0 / (1.0 + e)
    g2 = e / (1.0 + e)
    idx_ref[...] = jnp.where(col == 0, i1, jnp.where(col == 1, i2, 0))
    gates_ref[...] = jnp.where(col == 0, g1, jnp.where(col == 1, g2, 0.0))


def _router(x, norm_g, mods, sh_c, sc_c, router):
    r_pad = jnp.pad(router, ((0, 0), (0, LANE - N_EXPERTS)))
    grp = lambda i: _group_of_tile(i, TM)
    return pl.pallas_call(
        _router_kernel,
        grid=(M_TOK // TM,),
        in_specs=[
            pl.BlockSpec((TM, D), lambda i: (i, 0)),
            pl.BlockSpec((1, D), lambda i: (0, 0)),
            pl.BlockSpec((None, None, 1, D), lambda i: (grp(i), sh_c, 0, 0)),
            pl.BlockSpec((None, None, 1, D), lambda i: (grp(i), sc_c, 0, 0)),
            pl.BlockSpec((D, LANE), lambda i: (0, 0)),
        ],
        out_specs=[
            pl.BlockSpec((TM, D), lambda i: (i, 0)),
            pl.BlockSpec((TM, LANE), lambda i: (i, 0)),
            pl.BlockSpec((TM, LANE), lambda i: (i, 0)),
        ],
        out_shape=[
            jax.ShapeDtypeStruct((M_TOK, D), F32),
            jax.ShapeDtypeStruct((M_TOK, LANE), jnp.int32),
            jax.ShapeDtypeStruct((M_TOK, LANE), F32),
        ],
        compiler_params=_cparams(("parallel",)),
        name="router",
    )(x, norm_g.reshape(1, D), mods, mods, r_pad)


def _route_plan(e1, e2):
    e = jnp.concatenate([e1, e2])
    oh = (e[:, None] == jnp.arange(N_EXPERTS, dtype=jnp.int32)[None, :]).astype(jnp.int32)
    csum = jnp.cumsum(oh, axis=0)
    rank = jnp.sum((csum - oh) * oh, axis=1)
    counts = csum[-1]
    padded = ((counts + TMF - 1) // TMF) * TMF
    ends = jnp.cumsum(padded)
    starts = ends - padded
    pos = jnp.sum(oh * starts[None, :], axis=1) + rank
    tile_start = jnp.arange(MOE_TILES, dtype=jnp.int32) * TMF
    te = jnp.minimum(jnp.sum((tile_start[:, None] >= ends[None, :]).astype(jnp.int32), axis=1),
                     N_EXPERTS - 1)
    used = tile_start < ends[-1]
    valid = jnp.where(used, jnp.clip(counts[te] - (tile_start - starts[te]), 0, TMF), 0)
    last_e = te[jnp.maximum(ends[-1] // TMF - 1, 0)]
    te = jnp.where(used, te, last_e)
    sub_off = jnp.arange(TMF // SUB, dtype=jnp.int32) * SUB
    sub_valid = jnp.clip(valid[:, None] - sub_off[None, :], 0, SUB).reshape(-1)
    return pos.astype(jnp.int32), te.astype(jnp.int32), valid.astype(jnp.int32), sub_valid.astype(jnp.int32)


DMA_UNROLL = 8
DISPATCH_ROWS = 2048
COMBINE_ROWS = 1024


def _dispatch_kernel(pos_ref, sv_ref, h_ref, xs_hbm, zero_ref, sem, zsem):
    i = pl.program_id(0)

    @pl.when(i == 0)
    def _():
        zero_ref[...] = jnp.zeros_like(zero_ref)

        def zero_copy(b):
            return pltpu.make_async_copy(zero_ref, xs_hbm.at[pl.ds(pl.multiple_of(b * SUB, SUB), SUB)], zsem)

        def start(b, c):
            pl.when(sv_ref[b] < SUB)(lambda: zero_copy(b).start())
            return c

        def wait(b, c):
            pl.when(sv_ref[b] < SUB)(lambda: zero_copy(b).wait())
            return c

        lax.fori_loop(0, MOE_ROWS // SUB, start, 0)
        lax.fori_loop(0, MOE_ROWS // SUB, wait, 0)

    def issue(r, c):
        p1 = pos_ref[i * DISPATCH_ROWS + r]
        p2 = pos_ref[M_TOK + i * DISPATCH_ROWS + r]
        pltpu.make_async_copy(h_ref.at[pl.ds(r, 1)], xs_hbm.at[pl.ds(p1, 1)], sem).start(priority=0)
        pltpu.make_async_copy(h_ref.at[pl.ds(r, 1)], xs_hbm.at[pl.ds(p2, 1)], sem).start(priority=1)
        return c

    lax.fori_loop(0, DISPATCH_ROWS, issue, 0, unroll=DMA_UNROLL)
    pltpu.make_async_copy(h_ref, xs_hbm.at[pl.ds(0, DISPATCH_ROWS)], sem).wait()
    pltpu.make_async_copy(h_ref, xs_hbm.at[pl.ds(0, DISPATCH_ROWS)], sem).wait()


def _moe_dispatch(h, pos, sub_valid):
    return pl.pallas_call(
        _dispatch_kernel,
        grid_spec=pltpu.PrefetchScalarGridSpec(
            num_scalar_prefetch=2,
            grid=(M_TOK // DISPATCH_ROWS,),
            in_specs=[pl.BlockSpec((DISPATCH_ROWS, D), lambda i, p, z: (i, 0))],
            out_specs=pl.BlockSpec(memory_space=pl.ANY),
            scratch_shapes=[pltpu.VMEM((SUB, D), F32), pltpu.SemaphoreType.DMA, pltpu.SemaphoreType.DMA],
        ),
        out_shape=jax.ShapeDtypeStruct((MOE_ROWS, D), F32),
        compiler_params=_cparams(("arbitrary",)),
        name="moe_dispatch",
    )(pos, sub_valid, h)


def _combine_kernel(pos_ref, y_hbm, x_ref, tg_ref, gate_ref, o_ref, y_ref, sem):
    i = pl.program_id(0)
    slot = i % 2

    def picked(k, s):
        return pltpu.make_async_copy(y_hbm.at[pl.ds(0, COMBINE_ROWS)], y_ref.at[s, k], sem.at[s, k])

    def gather_tile(t, s):
        def issue(r, c):
            for k in range(2):
                p = pos_ref[k * M_TOK + t * COMBINE_ROWS + r]
                pltpu.make_async_copy(y_hbm.at[pl.ds(p, 1)], y_ref.at[s, k, pl.ds(r, 1)],
                                      sem.at[s, k]).start(priority=k)
            return c

        lax.fori_loop(0, COMBINE_ROWS, issue, 0, unroll=DMA_UNROLL)

    pl.when(i == 0)(lambda: gather_tile(0, 0))
    pl.when(i + 1 < pl.num_programs(0))(lambda: gather_tile(i + 1, 1 - slot))
    picked(0, slot).wait()
    picked(1, slot).wait()
    tg = tg_ref[...]
    moe_out = tg[:, 0:1] * y_ref[slot, 0] + tg[:, 1:2] * y_ref[slot, 1]
    o_ref[...] = x_ref[...] + gate_ref[...] * moe_out


def _moe_combine(y, pos, x, top_gates, mods, gate_c):
    grp = lambda i: _group_of_tile(i, COMBINE_ROWS)
    return pl.pallas_call(
        _combine_kernel,
        grid_spec=pltpu.PrefetchScalarGridSpec(
            num_scalar_prefetch=1,
            grid=(M_TOK // COMBINE_ROWS,),
            in_specs=[
                pl.BlockSpec(memory_space=pl.ANY),
                pl.BlockSpec((COMBINE_ROWS, D), lambda i, p: (i, 0)),
                pl.BlockSpec((COMBINE_ROWS, LANE), lambda i, p: (i, 0)),
                pl.BlockSpec((None, None, 1, D), lambda i, p: (grp(i), gate_c, 0, 0)),
            ],
            out_specs=pl.BlockSpec((COMBINE_ROWS, D), lambda i, p: (i, 0)),
            scratch_shapes=[pltpu.VMEM((2, 2, COMBINE_ROWS, D), F32), pltpu.SemaphoreType.DMA((2, 2))],
        ),
        out_shape=jax.ShapeDtypeStruct((M_TOK, D), F32),
        compiler_params=_cparams(("arbitrary",)),
        name="moe_combine",
    )(pos, y, x, top_gates, mods)


def _moe(x, norm_g, mods, sh_c, sc_c, gate_c, router, wg, wu, wd, f):
    h, idx, gates = _router(x, norm_g, mods, sh_c, sc_c, router)
    pos, te, tv, sub_valid = _route_plan(idx[:, 0], idx[:, 1])
    xs = _moe_dispatch(h, pos, sub_valid)
    ys = _ffn_moe(xs, wg, wu, wd, f, te, tv)
    return _moe_combine(ys, pos, x, gates, mods, gate_c)


def _rope_tables(width, lo, hi):
    pos = jnp.arange(T_LAT)
    rows = (pos // GRID_W).astype(F32)
    cols = (pos % GRID_W).astype(F32)
    n_freq = 16
    inv = ROPE_BASE ** (-jnp.arange(n_freq, dtype=F32) / n_freq)
    ang = jnp.concatenate([rows[:, None] * inv[None, :], cols[:, None] * inv[None, :]], axis=-1)
    cos, sin = jnp.cos(ang), jnp.sin(ang)
    lane = np.arange(width)
    active = (lane >= lo) & (lane < hi)
    reps = width // 64
    c = jnp.tile(jnp.concatenate([cos, cos], axis=-1), (1, reps))
    s = jnp.tile(jnp.concatenate([-sin, sin], axis=-1), (1, reps))
    return jnp.where(active[None, :], c, 1.0), jnp.where(active[None, :], s, 0.0)


def _rope_kernel(x_ref, c_ref, s_ref, o_ref, *, width):
    lane = lax.broadcasted_iota(jnp.int32, (1, width), 1)
    first = (lane & 63) < 32
    c = c_ref[...]
    s = s_ref[...]
    for j in range(x_ref.shape[-1] // width):
        cols = slice(j * width, (j + 1) * width)
        x = x_ref[:, cols]
        partner = jnp.where(first, pltpu.roll(x, width - 32, 1), pltpu.roll(x, 32, 1))
        o_ref[:, cols] = (x * c + partner * s).astype(o_ref.dtype)


def _rope(x, c, s, *, planes, row0, width, col0=0, ncol=1, tr=512):
    r0 = row0 // tr
    nt = T_LAT // tr
    if x.ndim == 3:
        ncol = x.shape[2] // width
        x_spec = pl.BlockSpec((None, tr, ncol * width), lambda p, i: (p, i + r0, 0))
    else:
        x_spec = pl.BlockSpec((tr, ncol * width), lambda p, i: (i + r0, col0 // ncol))
    return pl.pallas_call(
        functools.partial(_rope_kernel, width=width),
        grid=(planes, M_LAT // tr),
        in_specs=[
            x_spec,
            pl.BlockSpec((tr, width), lambda p, i: (i % nt, 0)),
            pl.BlockSpec((tr, width), lambda p, i: (i % nt, 0)),
        ],
        out_specs=pl.BlockSpec((None, tr, ncol * width), lambda p, i: (p, i, 0)),
        out_shape=jax.ShapeDtypeStruct((planes, M_LAT, ncol * width), BF16),
        compiler_params=_cparams(("parallel", "parallel")),
        name="rope",
    )(x, c, s)


def _attn_kernel(*refs, kind, hb, nseg, scale, lam_init, stack):
    it = iter(refs)
    q_ref = next(it)
    k_refs = [next(it) for _ in range(nseg)]
    kpe_refs = [next(it) for _ in range(nseg)] if kind == "mla" else None
    v_refs = [next(it) for _ in range(nseg)] if kind != "mla" else k_refs
    lam_ref = next(it) if kind == "diff" else None
    subln_ref = next(it) if kind == "diff" else None
    o_ref = next(it)

    lane = lax.broadcasted_iota(jnp.int32, (1, LANE), 1)
    lo_half = lane < 64
    log2_scale = scale * LOG2_E

    def softmax_parts(s_list):
        t_list = [s * log2_scale for s in s_list]
        m = functools.reduce(jnp.maximum, [jnp.max(t, axis=-1, keepdims=True) for t in t_list])
        p_list = [jnp.exp2(t - m) for t in t_list]
        l = functools.reduce(jnp.add, [jnp.sum(p, axis=-1, keepdims=True) for p in p_list])
        return p_list, 1.0 / l

    def attend(s_list, v_list):
        p_list, inv_l = softmax_parts(s_list)
        return functools.reduce(jnp.add, [_dot(p.astype(BF16), v) for p, v in zip(p_list, v_list)]) * inv_l

    for hh in range(hb):
        if kind == "mla":
            q2 = q_ref[:, hh * 256:(hh + 1) * 256].astype(BF16)
            s_list = []
            for kr, pr in zip(k_refs, kpe_refs):
                kcat = jnp.concatenate([kr[:, hh * 256:hh * 256 + 128].astype(BF16), pr[...].astype(BF16)], axis=1)
                s_list.append(_dot_nt(q2, kcat))
            v2 = [vr[:, hh * 256 + 128:hh * 256 + 256].astype(BF16) for vr in v_refs]
            o_ref[:, hh * LANE:(hh + 1) * LANE] = attend(s_list, v2)
            continue

        sl = slice(hh * LANE, (hh + 1) * LANE)
        q2 = q_ref[:, sl]
        k2 = [kr[:, sl].astype(BF16) for kr in k_refs]
        v2 = [vr[:, sl].astype(BF16) for vr in v_refs]
        tq = q2.shape[0]
        q_heads = [jnp.where(lo_half, q2, 0.0), jnp.where(lo_half, 0.0, q2)]
        if stack:
            qs = jnp.concatenate(q_heads, axis=0).astype(BF16)
            o2 = attend([_dot_nt(qs, kk) for kk in k2], v2)
            o_heads = [o2[:tq], o2[tq:]]
            if kind == "diff":
                o_diff = o_heads[0] - lam_ref[...][:, :1] * o_heads[1]
        elif kind == "na":
            o_heads = [attend([_dot_nt(qh.astype(BF16), kk) for kk in k2], v2) for qh in q_heads]
        else:
            (p0, il0), (p1, il1) = [softmax_parts([_dot_nt(qh.astype(BF16), kk) for kk in k2]) for qh in q_heads]
            lam = lam_ref[...][:, :1]
            o_diff = functools.reduce(jnp.add, [
                _dot((a * il0 - lam * (b * il1)).astype(BF16), vv) for a, b, vv in zip(p0, p1, v2)])
        if kind == "na":
            o_ref[:, sl] = jnp.where(lo_half, o_heads[0], o_heads[1])
        else:
            o_ref[:, sl] = _rms(o_diff, subln_ref[...], DIFF_SUBLN_EPS) * (1.0 - lam_init)


def _attention(kind, *, grid, q, ks, vs, kpes=None, extra=(), rows, out_spec, hb, scale, lam_init=0.0,
               stack=False, name="attn"):
    ops = [q] + list(ks) + (list(kpes) if kind == "mla" else []) + (list(vs) if kind != "mla" else [])
    ops += list(extra)
    return pl.pallas_call(
        functools.partial(_attn_kernel, kind=kind, hb=hb, nseg=len(ks), scale=scale, lam_init=lam_init,
                          stack=stack),
        grid=grid,
        in_specs=[s for _, s in ops],
        out_specs=out_spec,
        out_shape=jax.ShapeDtypeStruct((rows, D), F32),
        compiler_params=_cparams(("parallel",) * len(grid)),
        name=name,
    )(*[a for a, _ in ops])


NA_QROWS = 4
NA_KROWS = 12
NA_TQ = NA_QROWS * GRID_W
NA_TK = NA_KROWS * GRID_W


def _na_bias_table(rpb):
    h = rpb.shape[0]
    qc = np.arange(GRID_W)[:, None]
    kc = np.arange(GRID_W)[None, :]
    qs = np.clip(qc - NA_WIN_C // 2, 0, GRID_W - NA_WIN_C)
    col_ok = (kc >= qs) & (kc < qs + NA_WIN_C)
    ic = np.clip(kc - qc + NA_WIN_C - 1, 0, 2 * NA_WIN_C - 2)
    n_ir = 2 * NA_WIN_R - 1
    onehot = (ic.reshape(1, -1) == np.arange(2 * NA_WIN_C - 1)[:, None]).astype(np.float32)
    tab = jnp.einsum("hrm,mx->hrx", rpb.astype(F32) * LOG2_E, jnp.asarray(onehot),
                     precision=lax.Precision.HIGHEST)
    tab = jnp.where(jnp.asarray(col_ok.reshape(1, 1, -1)), tab, NEG_INF).reshape(h, n_ir, GRID_W, GRID_W)
    tab = jnp.concatenate([tab, tab], axis=-1)
    n_rows = T_LAT // GRID_W
    n_g = n_rows // NA_QROWS
    ir_idx = np.zeros((3, NA_QROWS, NA_KROWS), np.int32)
    row_ok = np.zeros((3, NA_QROWS, NA_KROWS), bool)
    for pat, g in enumerate((0, 1, n_g - 1)):
        base = int(np.clip(NA_QROWS * g - NA_QROWS, 0, n_rows - NA_KROWS))
        for qr in range(NA_QROWS):
            r = NA_QROWS * g + qr
            r0 = int(np.clip(r - NA_WIN_R // 2, 0, n_rows - NA_WIN_R))
            for kr in range(NA_KROWS):
                ok = r0 <= base + kr < r0 + NA_WIN_R
                row_ok[pat, qr, kr] = ok
                ir_idx[pat, qr, kr] = (base + kr - r + NA_WIN_R - 1) if ok else 0

    def build(tab_ref, o_ref):
        pat_id = pl.program_id(0)
        lo_half = lax.broadcasted_iota(jnp.int32, (1, LANE), 1) < GRID_W
        masked = jnp.full((GRID_W, LANE), NEG_INF, F32)
        for pat in range(3):
            @pl.when(pat_id == pat)
            def _(pat=pat):
                for hd in range(heads_per_step):
                    for qr in range(NA_QROWS):
                        for kp in range(NA_KROWS // 2):
                            halves = [tab_ref[hd, int(ir_idx[pat, qr, kr])] if row_ok[pat, qr, kr] else masked
                                      for kr in (2 * kp, 2 * kp + 1)]
                            o_ref[hd, qr * GRID_W:(qr + 1) * GRID_W, kp * LANE:(kp + 1) * LANE] = jnp.where(
                                lo_half, halves[0], halves[1])

    heads_per_step = 4
    return pl.pallas_call(
        build,
        grid=(3, h // heads_per_step),
        in_specs=[pl.BlockSpec((heads_per_step, n_ir, GRID_W, LANE), lambda p, hh: (hh, 0, 0, 0))],
        out_specs=pl.BlockSpec((None, heads_per_step, NA_TQ, NA_TK), lambda p, hh: (p, hh, 0, 0)),
        out_shape=jax.ShapeDtypeStruct((3, h, NA_TQ, NA_TK), F32),
        compiler_params=_cparams(("parallel", "parallel")),
        name="na_bias",
    )(tab)


def _na_lat_kernel(q_ref, kc_ref, vc_ref, k_ref, v_ref, b_ref, o_ref, *, scale):
    g = pl.program_id(2)
    n_rows = T_LAT // GRID_W
    base = jnp.clip(NA_QROWS * g - NA_QROWS, 0, n_rows - NA_KROWS) * GRID_W
    base = pl.multiple_of(base, GRID_W)
    lane = lax.broadcasted_iota(jnp.int32, (1, LANE), 1)
    lo_half = lane < 64
    log2_scale = scale * LOG2_E
    for lb in range(LATENT_HB):
        cols = slice(lb * LANE, (lb + 1) * LANE)
        q2 = q_ref[:, cols]
        kc = kc_ref[:, cols].astype(BF16)
        vc = vc_ref[:, cols].astype(BF16)
        kl = k_ref[pl.ds(base, NA_TK), cols].astype(BF16)
        vl = v_ref[pl.ds(base, NA_TK), cols].astype(BF16)
        outs = []
        for n in range(2):
            qm = jnp.where(lo_half if n == 0 else jnp.logical_not(lo_half), q2, 0.0).astype(BF16)
            s_c = _dot_nt(qm, kc) * log2_scale
            s_l = _dot_nt(qm, kl) * log2_scale + b_ref[2 * lb + n]
            m = jnp.maximum(jnp.max(s_c, axis=-1, keepdims=True), jnp.max(s_l, axis=-1, keepdims=True))
            p_c = jnp.exp2(s_c - m)
            p_l = jnp.exp2(s_l - m)
            l = jnp.sum(p_c, axis=-1, keepdims=True) + jnp.sum(p_l, axis=-1, keepdims=True)
            o = _dot(p_c.astype(BF16), vc) + _dot(p_l.astype(BF16), vl)
            outs.append(o * (1.0 / l))
        o_ref[:, cols] = jnp.where(lo_half, outs[0], outs[1])


def _na_lat(qkv, cache_k, cache_v, bias):
    n_g = T_LAT // NA_TQ
    q_blk0 = M_CTX // NA_TQ
    kv_blk0 = M_CTX // T_LAT
    wb = LATENT_HB * LANE
    pat = lambda g: jnp.where(g == 0, 0, jnp.where(g == n_g - 1, 2, 1))
    return pl.pallas_call(
        functools.partial(_na_lat_kernel, scale=0.125),
        grid=(N_LAT_SEQ, D // wb, n_g),
        in_specs=[
            pl.BlockSpec((None, NA_TQ, wb), lambda b, h, g: (0, q_blk0 + b * n_g + g, h)),
            pl.BlockSpec((None, PAST, wb), lambda b, h, g: (b, 0, h)),
            pl.BlockSpec((None, PAST, wb), lambda b, h, g: (b, 0, h)),
            pl.BlockSpec((None, T_LAT, wb), lambda b, h, g: (1, kv_blk0 + b, h)),
            pl.BlockSpec((None, T_LAT, wb), lambda b, h, g: (2, kv_blk0 + b, h)),
            pl.BlockSpec((None, 2 * LATENT_HB, NA_TQ, NA_TK), lambda b, h, g: (pat(g), h, 0, 0)),
        ],
        out_specs=pl.BlockSpec((NA_TQ, wb), lambda b, h, g: (b * n_g + g, h)),
        out_shape=jax.ShapeDtypeStruct((M_LAT, D), F32),
        compiler_params=_cparams(("parallel", "parallel", "parallel")),
        name="na_latent",
    )(qkv, cache_k, cache_v, qkv, qkv, bias)


SEQ_T_PER_STEP = 8


def _seq_t_kernel(x_ref, o_ref):
    for s in range(SEQ_T_PER_STEP):
        o_ref[s] = x_ref[s * T_CTX:(s + 1) * T_CTX, :].T


def _seq_transpose(qkv, plane):
    return pl.pallas_call(
        _seq_t_kernel,
        grid=(N_CTX_SEQ // SEQ_T_PER_STEP,),
        in_specs=[pl.BlockSpec((None, SEQ_T_PER_STEP * T_CTX, D), lambda b: (plane, b, 0))],
        out_specs=pl.BlockSpec((SEQ_T_PER_STEP, D, T_CTX), lambda b: (b, 0, 0)),
        out_shape=jax.ShapeDtypeStruct((N_CTX_SEQ, D, T_CTX), F32),
        compiler_params=_cparams(("parallel",)),
        name="seq_transpose",
    )(qkv)


def kernel(x_prompt, x_sample, c, c_ctx, cache_na_k, cache_na_v, cache_diff_k, cache_diff_v, cache_mla_ckv, cache_mla_kpe, ada_w, ada_b, norm1, norm2, pool_w, pool_scale, na_qkv, na_rpb, na_out, diff_qkv, diff_lambda, diff_subln, diff_out, mla_qa, mla_qnorm, mla_qb, mla_kva, mla_kvnorm, mla_kvb, mla_out, ffn_gate, ffn_up, ffn_down, moe_router, moe_gate, moe_up, moe_down, final_norm):
    depth = ada_w.shape[0]
    x = None
    cond8 = jnp.concatenate([c_ctx[None, :], c, jnp.zeros((5, D), F32)], axis=0)
    mods_all = _ada(cond8, ada_w, ada_b)[:, :3].reshape(depth, 3, 6, 1, D)
    outs = {}

    for l in range(depth):
        kind, li = l % 4, l // 4
        mods = mods_all[l]
        if kind == 0:
            x_pair = (x_prompt.reshape(M_CTX, D), x_sample.reshape(M_LAT, D)) if x is None else (x[:M_CTX], x[M_CTX:])
            h = _modrows(x_pair, norm1[l], mods, 0, 1, rows=M_TOK, row0=0, width=D)
            x = _pool(h, *x_pair, pool_w[li], pool_scale[li], mods, 2)
        elif kind == 1:
            qkv = _linear(x, na_qkv[li], tn=D, norm_g=norm1[l], mods=mods, sh_c=0, sc_c=1, planes=True,
                          name="na_qkv")
            to_cache = lambda a: jnp.transpose(a.reshape(N_CTX_SEQ, 1, 16, 64, T_CTX), (0, 1, 4, 2, 3))
            outs["na_k"] = to_cache(_seq_transpose(qkv, 1))
            outs["na_v"] = to_cache(_seq_transpose(qkv, 2))
            seq_spec = lambda p: pl.BlockSpec((None, T_CTX, D), lambda b, p=p: (p, b, 0))
            o_ctx = _attention(
                "na", grid=(N_CTX_SEQ,), q=(qkv, seq_spec(0)), ks=[(qkv, seq_spec(1))], vs=[(qkv, seq_spec(2))],
                rows=M_CTX, out_spec=pl.BlockSpec((T_CTX, D), lambda b: (b, 0)), hb=D // LANE, scale=0.125,
                stack=True, name="na_ctx")
            bias = _na_bias_table(na_rpb[li])
            o_lat = _na_lat(qkv, cache_na_k[:, li].reshape(N_LAT_SEQ, PAST, D),
                            cache_na_v[:, li].reshape(N_LAT_SEQ, PAST, D), bias)
            x = _linear((o_ctx, o_lat), na_out[li], tn=D, tm=TM, resid=x, mods=mods, gate_c=2, name="na_out")
        elif kind == 2:
            lam_init = 0.8 - 0.6 * math.exp(-0.3 * l)
            lp = diff_lambda[li].astype(F32)
            lam = jnp.exp(jnp.sum(lp[0] * lp[1])) - jnp.exp(jnp.sum(lp[2] * lp[3])) + lam_init
            lam_row = jnp.full((1, LANE), lam, F32)
            subln = diff_subln[li].reshape(1, LANE)
            extra = [(lam_row, pl.BlockSpec((1, LANE), lambda *_: (0, 0))),
                     (subln, pl.BlockSpec((1, LANE), lambda *_: (0, 0)))]
            qkv = _linear(x, diff_qkv[li], tn=D, norm_g=norm1[l], mods=mods, sh_c=0, sc_c=1, planes=True,
                          name="diff_qkv")
            outs["diff_k"] = jnp.transpose(
                _seq_transpose(qkv, 1).reshape(N_CTX_SEQ, 1, 8, 2, 64, T_CTX), (0, 1, 5, 2, 3, 4))
            outs["diff_v"] = qkv[2, :M_CTX].reshape(N_CTX_SEQ, 1, T_CTX, 8, 128)
            seq_spec = lambda p: pl.BlockSpec((None, T_CTX, D), lambda b, p=p: (p, b, 0))
            o_ctx = _attention(
                "diff", grid=(N_CTX_SEQ,), q=(qkv, seq_spec(0)), ks=[(qkv, seq_spec(1))], vs=[(qkv, seq_spec(2))],
                extra=extra, rows=M_CTX, out_spec=pl.BlockSpec((T_CTX, D), lambda b: (b, 0)), hb=D // LANE,
                scale=0.125, lam_init=lam_init, stack=True, name="diff_ctx")
            rc, rs = _rope_tables(LANE, 0, LANE)
            qk_r = _rope(qkv, rc, rs, planes=2, row0=M_CTX, width=LANE)
            tq = LATENT_TQ
            nq = T_LAT // tq
            ck = cache_diff_k[:, li].reshape(N_LAT_SEQ, PAST, D)
            cv = cache_diff_v[:, li].reshape(N_LAT_SEQ, PAST, D)
            hb, wb = LATENT_HB, LATENT_HB * LANE
            o_lat = _attention(
                "diff", grid=(N_LAT_SEQ, D // wb, nq),
                q=(qk_r, pl.BlockSpec((None, tq, wb), lambda b, h, i: (0, b * nq + i, h))),
                ks=[(ck, pl.BlockSpec((None, PAST, wb), lambda b, h, i: (b, 0, h))),
                    (qk_r, pl.BlockSpec((None, T_LAT, wb), lambda b, h, i: (1, b, h)))],
                vs=[(cv, pl.BlockSpec((None, PAST, wb), lambda b, h, i: (b, 0, h))),
                    (qkv, pl.BlockSpec((None, T_LAT, wb), lambda b, h, i: (2, M_CTX // T_LAT + b, h)))],
                extra=extra,
                rows=M_LAT, out_spec=pl.BlockSpec((tq, wb), lambda b, h, i: (b * nq + i, h)),
                hb=hb, scale=0.125, lam_init=lam_init, name="diff_latent")
            x = _linear((o_ctx, o_lat), diff_out[li], tn=D, tm=TM, resid=x, mods=mods, gate_c=2, name="diff_out")
        else:
            hd = MLA_NOPE + MLA_ROPE
            wqb = mla_qb[li].reshape(MLA_Q_LORA, MLA_HEADS, hd)
            wqb = jnp.pad(wqb, ((0, 0), (0, 0), (0, 256 - hd))).reshape(MLA_Q_LORA, MLA_HEADS * 256)
            w_a = jnp.concatenate(
                [mla_qa[li], jnp.pad(mla_kva[li], ((0, 0), (0, 512 - MLA_KV_LORA - MLA_ROPE)))], axis=1)
            a = _linear(x, w_a, tn=D, tm=TM, norm_g=norm1[l], mods=mods, sh_c=0, sc_c=1, name="mla_a")
            q = _linear(a, wqb, tn=D, tm=TM, k=MLA_Q_LORA, x_col=0, norm_g=mla_qnorm[li], name="mla_q")
            zero_mod = jnp.zeros((1, 2, 1, MLA_KV_LORA), F32)
            ckv = _modrows(a, mla_kvnorm[li], zero_mod, 0, 1, rows=M_TOK, row0=0, width=MLA_KV_LORA, col_blk=2,
                           grouped=False)
            outs["mla_ckv"] = ckv[:M_CTX].reshape(N_CTX_SEQ, 1, T_CTX, MLA_KV_LORA)
            outs["mla_kpe"] = a[:M_CTX, 768:768 + MLA_ROPE].reshape(N_CTX_SEQ, 1, T_CTX, MLA_ROPE)
            ckv_all = jnp.concatenate([ckv, cache_mla_ckv[:, li].reshape(N_LAT_SEQ * PAST, MLA_KV_LORA)], axis=0)
            kvb = _linear(ckv_all, mla_kvb[li], tn=D, tm=TM, out_dtype=BF16, name="mla_kvb")
            scale = float(hd) ** -0.5
            o_ctx = _attention(
                "mla", grid=(N_CTX_SEQ,),
                q=(q, pl.BlockSpec((T_CTX, 2048), lambda b: (b, 0))),
                ks=[(kvb, pl.BlockSpec((T_CTX, 2048), lambda b: (b, 0)))],
                kpes=[(a, pl.BlockSpec((T_CTX, LANE), lambda b: (b, 6)))],
                vs=None, rows=M_CTX, out_spec=pl.BlockSpec((T_CTX, D), lambda b: (b, 0)), hb=MLA_HEADS,
                scale=scale, name="mla_ctx")
            rc, rs = _rope_tables(256, 128, 192)
            q_r = _rope(q, rc, rs, planes=1, row0=M_CTX, width=256, ncol=MLA_HEADS)[0]
            rc1, rs1 = _rope_tables(LANE, 0, 64)
            kpe_r = _rope(a, rc1, rs1, planes=1, row0=M_CTX, width=LANE, col0=6)[0]
            kpe_c = jnp.pad(cache_mla_kpe[:, li], ((0, 0), (0, 0), (0, LANE - MLA_ROPE)))
            tq = LATENT_TQ
            nq = T_LAT // tq
            hb = LATENT_HB
            o_lat = _attention(
                "mla", grid=(N_LAT_SEQ, MLA_HEADS // hb, nq),
                q=(q_r, pl.BlockSpec((tq, hb * 256), lambda b, h, i: (b * nq + i, h))),
                ks=[(kvb, pl.BlockSpec((PAST, hb * 256), lambda b, h, i: (M_TOK // PAST + b, h))),
                    (kvb, pl.BlockSpec((T_LAT, hb * 256), lambda b, h, i: (M_CTX // T_LAT + b, h)))],
                kpes=[(kpe_c, pl.BlockSpec((None, PAST, LANE), lambda b, h, i: (b, 0, 0))),
                      (kpe_r, pl.BlockSpec((T_LAT, LANE), lambda b, h, i: (b, 0)))],
                vs=None, rows=M_LAT,
                out_spec=pl.BlockSpec((tq, hb * LANE), lambda b, h, i: (b * nq + i, h)),
                hb=hb, scale=scale, name="mla_latent")
            x = _linear((o_ctx, o_lat), mla_out[li], tn=D, tm=TM, resid=x, mods=mods, gate_c=2, name="mla_out")

        f = l // 2
        if l % 2 == 0:
            x = _ffn_dense(x, ffn_gate, ffn_up, ffn_down, f, norm2[l], mods, 3, 4, 5)
        else:
            x = _moe(x, norm2[l], mods, 3, 4, 5, moe_router[f], moe_gate, moe_up, moe_down, f)

    zero_mod = jnp.zeros((1, 2, 1, D), F32)
    y_prompt = _modrows(x, final_norm, zero_mod, 0, 1, rows=M_CTX, row0=0, width=D, grouped=False)
    y_sample = _modrows(x, final_norm, zero_mod, 0, 1, rows=M_LAT, row0=M_CTX, width=D, grouped=False)
    return (y_prompt.reshape(N_CTX_SEQ, T_CTX, D), y_sample.reshape(N_LAT_SEQ, T_LAT, D),
            outs["na_k"], outs["na_v"], outs["diff_k"], outs["diff_v"], outs["mla_ckv"], outs["mla_kpe"])
```
